```python
import math
import jax, jax.numpy as jnp
from jax import lax
import numpy as np

D_MODEL = 2048
BATCH = 2
SEQ = 4096
DEPTH = 1

PLE_DIM = 256
EPS = 1e-6
POOL_WINDOWS = (2, 4, 8, 16)
POOL_CH = 256
D_POOL = POOL_CH * len(POOL_WINDOWS)
N_HEADS = 8
Q_LORA = 512
KV_LORA = 512
QK_NOPE = 128
QK_ROPE = 64
QK_DIM = QK_NOPE + QK_ROPE
V_DIM = 128
D_ATTN = N_HEADS * V_DIM
ROPE_THETA = 10000.0
Q_BLOCK = 128
ATTN_SCALE = 1.0 / math.sqrt(QK_DIM)
D_IN = D_POOL + Q_LORA + KV_LORA + QK_ROPE
D_MIX = D_POOL + D_ATTN
N_GROUPS = 8
EXPERTS_PER_GROUP = 8
N_EXPERTS = N_GROUPS * EXPERTS_PER_GROUP
TOP_K = 2
D_EXPERT = 512
MOE_BLOCK = 128

kernel_name = "hymba_pool_mla_hiermoe_ple"


def rms_norm(x, gain):
    x32 = x.astype(jnp.float32)
    y = x32 * lax.rsqrt(jnp.mean(x32 * x32, axis=-1, keepdims=True) + EPS)
    return (y * gain.astype(jnp.float32)).astype(x.dtype)


def rope(x, positions):
    half = x.shape[-1] // 2
    inv_freq = ROPE_THETA ** (-jnp.arange(half, dtype=jnp.float32) / half)
    ang = positions.astype(jnp.float32)[..., None] * inv_freq
    ang = ang.reshape(ang.shape[:2] + (1,) * (x.ndim - 3) + (half,))
    cos, sin = jnp.cos(ang), jnp.sin(ang)
    x32 = x.astype(jnp.float32)
    x1, x2 = x32[..., :half], x32[..., half:]
    return jnp.concatenate([x1 * cos - x2 * sin, x2 * cos + x1 * sin], axis=-1).astype(x.dtype)


def pool_mixer(u, w_pool, pool_scale):
    B, S, _ = u.shape
    u32 = u.astype(jnp.float32)
    cs = jnp.cumsum(u32, axis=1)
    count = jnp.arange(1, S + 1, dtype=jnp.float32)[None, :, None]
    diffs = []
    for g, w in enumerate(POOL_WINDOWS):
        sl = slice(g * POOL_CH, (g + 1) * POOL_CH)
        c = cs[..., sl]
        lower = jnp.pad(c[:, :S - w], ((0, 0), (w, 0), (0, 0)))
        mean = (c - lower) / jnp.minimum(count, float(w))
        diffs.append(mean - u32[..., sl])
    d = jnp.stack(diffs, axis=2).astype(u.dtype)
    y = jnp.einsum('bsgc,gcd->bsgd', d, w_pool).reshape(B, S, D_POOL)
    return y * pool_scale


def mla_attention(q_lat, kv_lat, positions, q_a_gain, w_q_b, kv_a_gain, w_kv_b,
                  q_norm_gain, k_norm_gain):
    B, S, _ = q_lat.shape
    q = (rms_norm(q_lat, q_a_gain) @ w_q_b).reshape(B, S, N_HEADS, QK_DIM)
    c_kv = rms_norm(kv_lat[..., :KV_LORA], kv_a_gain)
    k_rope = kv_lat[..., KV_LORA:]
    kv = (c_kv @ w_kv_b).reshape(B, S, N_HEADS, QK_NOPE + V_DIM)
    k_nope, v = kv[..., :QK_NOPE], kv[..., QK_NOPE:]
    k = jnp.concatenate(
        [k_nope, jnp.broadcast_to(k_rope[:, :, None, :], (B, S, N_HEADS, QK_ROPE))], axis=-1)
    q = rms_norm(q, q_norm_gain)
    k = rms_norm(k, k_norm_gain)
    q = jnp.concatenate([q[..., :QK_NOPE], rope(q[..., QK_NOPE:], positions)], axis=-1)
    k = jnp.concatenate([k[..., :QK_NOPE], rope(k[..., QK_NOPE:], positions)], axis=-1)

    n_blk = S // Q_BLOCK
    q_blocks = q.reshape(B, n_blk, Q_BLOCK, N_HEADS, QK_DIM).transpose(1, 0, 2, 3, 4)
    k_pos = jnp.arange(S)
    neg = jnp.finfo(jnp.float32).min

    def attend(args):
        qi, blk = args
        s = jnp.einsum('bqhd,bkhd->bhqk', qi, k).astype(jnp.float32) * ATTN_SCALE
        q_pos = blk * Q_BLOCK + jnp.arange(Q_BLOCK)
        mask = k_pos[None, :] <= q_pos[:, None]
        prob = jax.nn.softmax(jnp.where(mask, s, neg), axis=-1)
        return jnp.einsum('bhqk,bkhd->bqhd', prob.astype(v.dtype), v)

    o = lax.map(attend, (q_blocks, jnp.arange(n_blk)))
    return o.transpose(1, 0, 2, 3, 4).reshape(B, S, D_ATTN)


def hier_moe(h, w_router_group, b_router_group, w_router_expert, b_router_expert,
             w_exp_gate, w_exp_up, w_exp_down):
    B, S, D = h.shape
    T = B * S
    xf = h.reshape(T, D)
    g_logits = (xf @ w_router_group).astype(jnp.float32) + b_router_group.astype(jnp.float32)
    g_prob = jax.nn.softmax(g_logits, axis=-1)
    g_sel = jnp.argmax(g_logits, axis=-1)
    g_w = jnp.take_along_axis(g_prob, g_sel[:, None], axis=1)[:, 0]
    e_logits = ((xf @ w_router_expert).astype(jnp.float32)
                + b_router_expert.astype(jnp.float32)).reshape(T, N_GROUPS, EXPERTS_PER_GROUP)
    e_logits = jnp.take_along_axis(e_logits, g_sel[:, None, None], axis=1)[:, 0]
    top_v, top_i = lax.top_k(e_logits, TOP_K)
    top_w = jax.nn.softmax(top_v, axis=-1) * g_w[:, None]
    expert_id = g_sel[:, None] * EXPERTS_PER_GROUP + top_i

    n_assign = T * TOP_K
    flat_e = expert_id.reshape(-1).astype(jnp.int32)
    flat_w = top_w.reshape(-1)
    flat_t = jnp.repeat(jnp.arange(T, dtype=jnp.int32), TOP_K)
    order = jnp.argsort(flat_e, stable=True)
    se = flat_e[order]
    counts = jnp.bincount(flat_e, length=N_EXPERTS)
    starts = jnp.cumsum(counts) - counts
    padded = (counts + MOE_BLOCK - 1) // MOE_BLOCK * MOE_BLOCK
    pad_ends = jnp.cumsum(padded)
    pad_starts = pad_ends - padded
    dest = pad_starts[se] + (jnp.arange(n_assign) - starts[se])
    n_blocks = (n_assign + N_EXPERTS * (MOE_BLOCK - 1) + MOE_BLOCK - 1) // MOE_BLOCK
    n_slots = n_blocks * MOE_BLOCK
    slot_tok = jnp.full((n_slots,), T, dtype=jnp.int32).at[dest].set(flat_t[order])
    slot_w = jnp.zeros((n_slots,), jnp.float32).at[dest].set(flat_w[order])
    block_e = jnp.minimum(
        jnp.searchsorted(pad_ends, jnp.arange(n_blocks) * MOE_BLOCK, side='right'),
        N_EXPERTS - 1)
    x_pad = jnp.concatenate([xf, jnp.zeros((1, D), xf.dtype)], axis=0)

    def expert_block(args):
        tok, e = args
        xb = x_pad[tok]
        a = xb @ w_exp_gate[e]
        u = xb @ w_exp_up[e]
        return (jax.nn.silu(a) * u) @ w_exp_down[e]

    y = lax.map(expert_block, (slot_tok.reshape(n_blocks, MOE_BLOCK), block_e))
    y = y.reshape(n_slots, D) * slot_w[:, None].astype(y.dtype)
    out = jnp.zeros((T + 1, D), y.dtype).at[slot_tok].add(y)[:T]
    return out.reshape(B, S, D)


def setup_inputs(seed: int = 0) -> dict:
    key = jax.random.key(seed)
    ks = iter(jax.random.split(key, 32))

    def nrm(shape, scale):
        return jax.random.normal(next(ks), shape, jnp.float32) * scale

    def gain(shape):
        return 1.0 + 0.1 * jax.random.normal(next(ks), shape, jnp.float32)

    L = DEPTH
    x = nrm((BATCH, SEQ, D_MODEL), 1.0)
    p = nrm((DEPTH, BATCH, SEQ, PLE_DIM), 1.0)
    offset = jax.random.randint(next(ks), (BATCH, 1), 0, 4096, dtype=jnp.int32)
    positions = offset + jnp.arange(SEQ, dtype=jnp.int32)[None, :]
    return {
        "x": x,
        "p": p,
        "positions": positions,
        "norm1_gain": gain((L, D_MODEL)),
        "w_in": nrm((L, D_MODEL, D_IN), D_MODEL ** -0.5),
        "q_a_gain": gain((L, Q_LORA)),
        "w_q_b": nrm((L, Q_LORA, N_HEADS * QK_DIM), Q_LORA ** -0.5),
        "kv_a_gain": gain((L, KV_LORA)),
        "w_kv_b": nrm((L, KV_LORA, N_HEADS * (QK_NOPE + V_DIM)), KV_LORA ** -0.5),
        "q_norm_gain": gain((L, QK_DIM)),
        "k_norm_gain": gain((L, QK_DIM)),
        "w_pool": nrm((L, len(POOL_WINDOWS), POOL_CH, POOL_CH), POOL_CH ** -0.5),
        "pool_scale": gain((L, D_POOL)),
        "w_out": nrm((L, D_MIX, D_MODEL), D_MIX ** -0.5),
        "norm2_gain": gain((L, D_MODEL)),
        "w_router_group": nrm((L, D_MODEL, N_GROUPS), D_MODEL ** -0.5),
        "b_router_group": nrm((L, N_GROUPS), 0.01),
        "w_router_expert": nrm((L, D_MODEL, N_EXPERTS), D_MODEL ** -0.5),
        "b_router_expert": nrm((L, N_EXPERTS), 0.01),
        "w_exp_gate": nrm((L, N_EXPERTS, D_MODEL, D_EXPERT), D_MODEL ** -0.5),
        "w_exp_up": nrm((L, N_EXPERTS, D_MODEL, D_EXPERT), D_MODEL ** -0.5),
        "w_exp_down": nrm((L, N_EXPERTS, D_EXPERT, D_MODEL), D_EXPERT ** -0.5),
        "norm3_gain": gain((L, D_MODEL)),
        "w_ple_gate": nrm((L, D_MODEL, D_MODEL), D_MODEL ** -0.5),
        "w_ple_proj": nrm((L, PLE_DIM, D_MODEL), PLE_DIM ** -0.5),
        "ple_norm_gain": gain((L, D_MODEL)),
    }


def reference(x, p, positions, norm1_gain, w_in, q_a_gain, w_q_b, kv_a_gain, w_kv_b,
              q_norm_gain, k_norm_gain, w_pool, pool_scale, w_out, norm2_gain,
              w_router_group, b_router_group, w_router_expert, b_router_expert,
              w_exp_gate, w_exp_up, w_exp_down, norm3_gain, w_ple_gate, w_ple_proj,
              ple_norm_gain):
    h = x
    for i in range(DEPTH):
        hn = rms_norm(h, norm1_gain[i])
        z = hn @ w_in[i]
        u_pool = z[..., :D_POOL]
        q_lat = z[..., D_POOL:D_POOL + Q_LORA]
        kv_lat = z[..., D_POOL + Q_LORA:]
        y_pool = pool_mixer(u_pool, w_pool[i], pool_scale[i])
        y_attn = mla_attention(q_lat, kv_lat, positions, q_a_gain[i], w_q_b[i],
                               kv_a_gain[i], w_kv_b[i], q_norm_gain[i], k_norm_gain[i])
        h = h + jnp.concatenate([y_pool, y_attn], axis=-1) @ w_out[i]
        h = h + hier_moe(rms_norm(h, norm2_gain[i]), w_router_group[i], b_router_group[i],
                         w_router_expert[i], b_router_expert[i],
                         w_exp_gate[i], w_exp_up[i], w_exp_down[i])
        e = rms_norm(p[i] @ w_ple_proj[i], ple_norm_gain[i])
        gate = jax.nn.sigmoid(rms_norm(h, norm3_gain[i]) @ w_ple_gate[i])
        h = h + gate * e
    return h
```

```python
import functools
import math

import jax
import jax.numpy as jnp
from jax import lax
from jax.experimental import pallas as pl
from jax.experimental.pallas import tpu as pltpu

F32 = jnp.float32
BF16 = jnp.bfloat16
I32 = jnp.int32

D_MODEL = 2048
PLE_DIM = 256
EPS = 1e-6
POOL_WINDOWS = (2, 4, 8, 16)
POOL_CH = 256
D_POOL = POOL_CH * len(POOL_WINDOWS)
N_HEADS = 8
Q_LORA = 512
KV_LORA = 512
QK_NOPE = 128
QK_ROPE = 64
QK_DIM = QK_NOPE + QK_ROPE
V_DIM = 128
D_ATTN = N_HEADS * V_DIM
ROPE_THETA = 10000.0
ATTN_SCALE = 1.0 / math.sqrt(QK_DIM)
N_GROUPS = 8
EXPERTS_PER_GROUP = 8
N_EXPERTS = N_GROUPS * EXPERTS_PER_GROUP
TOP_K = 2
D_EXPERT = 512

LANES = 128
MXU_DIM = 256
VMEM_LIMIT = 56 * 1024 * 1024

HEAD_W = 2 * LANES
ROPE_HALF = QK_ROPE // 2
POOL_HALO = 16
TM_PRE = 256
TM_POST = 256
TQ = 512
TM_ROUTE = 256
MOE_BLK = 256
NEG_BIG = -1e30


def _const_spec(shape):
    nd = len(shape)
    return pl.BlockSpec(shape, lambda *_: (0,) * nd, pipeline_mode=pl.Buffered(1))


def _rms(x, gain):
    return x * lax.rsqrt(jnp.mean(x * x, axis=-1, keepdims=True) + EPS) * gain


def _pre_kernel(x_ref, pos_ref, g1_ref, win_ref, qag_ref, wq_ref, kvag_ref, wkv_ref,
                qng_ref, kng_ref, wpool_ref, pscale_ref, invf_ref, sgn_ref,
                ypool_ref, q_ref, k_ref, v_ref, carry_ref):
    st = pl.program_id(1)
    tm = x_ref.shape[0]

    hn = _rms(x_ref[...], g1_ref[...])
    z = jnp.dot(hn.astype(BF16), win_ref[...], preferred_element_type=F32)

    @pl.when(st == 0)
    def _():
        carry_ref[...] = jnp.zeros_like(carry_ref)

    u = z[:, :D_POOL]
    ext = jnp.concatenate([carry_ref[...], u], axis=0)
    carry_ref[...] = u[tm - POOL_HALO:, :]
    row = lax.broadcasted_iota(I32, (tm, 1), 0) + st * tm
    level = ext
    shift = 1
    for g, w in enumerate(POOL_WINDOWS):
        sl = slice(g * POOL_CH, (g + 1) * POOL_CH)
        while shift < w:
            level = level + pltpu.roll(level, shift, 0)
            shift *= 2
        win_sum = level[POOL_HALO:, sl]
        cnt = jnp.minimum(row + 1, w).astype(F32)
        d = win_sum / cnt - u[:, sl]
        y = jnp.dot(d.astype(BF16), wpool_ref[g], preferred_element_type=F32)
        ypool_ref[:, sl] = (y * pscale_ref[:, sl]).astype(BF16)

    q_lat = z[:, D_POOL:D_POOL + Q_LORA]
    kv_lat = z[:, D_POOL + Q_LORA:D_POOL + Q_LORA + KV_LORA]
    k_rope = z[:, D_POOL + Q_LORA + KV_LORA:]
    qa = jnp.dot(_rms(q_lat, qag_ref[...]).astype(BF16), wq_ref[...],
                 preferred_element_type=F32)
    kv = jnp.dot(_rms(kv_lat, kvag_ref[...]).astype(BF16), wkv_ref[...],
                 preferred_element_type=F32)

    ang = pos_ref[...].astype(F32) * invf_ref[...]
    cos = jnp.cos(ang)
    sin = jnp.sin(ang) * sgn_ref[...]

    def rot(t):
        return t * cos + pltpu.roll(t, LANES // 2, 1) * sin

    qng = qng_ref[...]
    kng = kng_ref[...]
    kr_rot = rot(k_rope * kng[:, LANES:])
    kr_ssq = jnp.sum(k_rope * k_rope, axis=-1, keepdims=True)
    for h in range(N_HEADS):
        qh = qa[:, h * HEAD_W:(h + 1) * HEAD_W]
        rq = lax.rsqrt(jnp.sum(qh * qh, axis=-1, keepdims=True) / QK_DIM + EPS)
        qn = qh * rq * qng
        q_ref[0, h, :, :LANES] = qn[:, :LANES].astype(BF16)
        q_ref[0, h, :, LANES:] = rot(qn[:, LANES:]).astype(BF16)
        kh = kv[:, h * QK_NOPE:(h + 1) * QK_NOPE]
        rk = lax.rsqrt((jnp.sum(kh * kh, axis=-1, keepdims=True) + kr_ssq) / QK_DIM + EPS)
        k_ref[0, h, :, :LANES] = (kh * rk * kng[:, :LANES]).astype(BF16)
        k_ref[0, h, :, LANES:] = (kr_rot * rk).astype(BF16)
        v_ref[0, h, :, :] = kv[:, N_HEADS * QK_NOPE + h * V_DIM:
                               N_HEADS * QK_NOPE + (h + 1) * V_DIM].astype(BF16)


def _pre_call(x, pos, g1, win, qag, wq, kvag, wkv, qng, kng, wpool, pscale, invf, sgn):
    B, S, D = x.shape
    tm = TM_PRE
    grid = (B, S // tm)
    row_spec = lambda w: pl.BlockSpec((None, tm, w), lambda b, s: (b, s, 0))
    head_spec = lambda w: pl.BlockSpec((1, N_HEADS, tm, w), lambda b, s: (b, 0, s, 0))
    consts = [g1, win, qag, wq, kvag, wkv, qng, kng, wpool, pscale, invf, sgn]
    return pl.pallas_call(
        _pre_kernel,
        grid=grid,
        in_specs=[row_spec(D), row_spec(1)] + [_const_spec(c.shape) for c in consts],
        out_specs=[row_spec(D_POOL), head_spec(HEAD_W), head_spec(HEAD_W), head_spec(V_DIM)],
        out_shape=[jax.ShapeDtypeStruct((B, S, D_POOL), BF16),
                   jax.ShapeDtypeStruct((B, N_HEADS, S, HEAD_W), BF16),
                   jax.ShapeDtypeStruct((B, N_HEADS, S, HEAD_W), BF16),
                   jax.ShapeDtypeStruct((B, N_HEADS, S, V_DIM), BF16)],
        scratch_shapes=[pltpu.VMEM((POOL_HALO, D_POOL), F32)],
        compiler_params=pltpu.CompilerParams(
            dimension_semantics=("arbitrary", "arbitrary"), vmem_limit_bytes=VMEM_LIMIT),
        name="pre",
    )(x, pos, *consts)


def _attn_kernel(q_ref, k_ref, v_ref, o_ref):
    S = q_ref.shape[2]
    nq = S // TQ
    rows = lax.broadcasted_iota(I32, (TQ, TQ), 0)
    cols = lax.broadcasted_iota(I32, (TQ, TQ), 1)
    causal = cols <= rows

    def q_tile(qi, _):
        q = q_ref[0, 0, pl.ds(pl.multiple_of(qi * TQ, TQ), TQ), :]

        def step(kj, carry, masked):
            m, l, acc = carry
            start = pl.multiple_of(kj * TQ, TQ)
            k = k_ref[0, 0, pl.ds(start, TQ), :]
            v = v_ref[0, 0, pl.ds(start, TQ), :]
            s = lax.dot_general(q, k, (((1,), (1,)), ((), ())),
                                preferred_element_type=F32) * ATTN_SCALE
            if masked:
                s = jnp.where(causal, s, NEG_BIG)
            m_new = jnp.maximum(m, jnp.max(s, axis=-1, keepdims=True))
            alpha = jnp.exp(m - m_new)
            p = jnp.exp(s - m_new)
            l = alpha * l + jnp.sum(p, axis=-1, keepdims=True)
            acc = alpha * acc + jnp.dot(p.astype(BF16), v, preferred_element_type=F32)
            return m_new, l, acc

        init = (jnp.full((TQ, 1), NEG_BIG, F32), jnp.zeros((TQ, 1), F32),
                jnp.zeros((TQ, V_DIM), F32))
        carry = lax.fori_loop(0, qi, lambda kj, c: step(kj, c, False), init)
        _, l, acc = step(qi, carry, True)
        o_ref[0, pl.ds(pl.multiple_of(qi * TQ, TQ), TQ), :] = (acc / l).astype(BF16)
        return 0

    lax.fori_loop(0, nq, q_tile, 0)


def _attn_call(q, k, v):
    B, H, S, _ = q.shape
    head = lambda w: pl.BlockSpec((1, 1, S, w), lambda b, h: (b, h, 0, 0))
    return pl.pallas_call(
        _attn_kernel,
        grid=(B, H),
        in_specs=[head(HEAD_W), head(HEAD_W), head(V_DIM)],
        out_specs=pl.BlockSpec((1, S, V_DIM), lambda b, h: (b, 0, h)),
        out_shape=jax.ShapeDtypeStruct((B, S, H * V_DIM), BF16),
        compiler_params=pltpu.CompilerParams(
            dimension_semantics=("arbitrary", "arbitrary"), vmem_limit_bytes=VMEM_LIMIT),
        name="attn",
    )(q, k, v)


def _post_kernel(x_ref, yp_ref, ya_ref, wo_ref, g2_ref, wr_ref, br_ref,
                 h1_ref, xn_ref, lg_ref):
    h1 = (x_ref[...]
          + jnp.dot(yp_ref[...], wo_ref[:D_POOL, :], preferred_element_type=F32)
          + jnp.dot(ya_ref[...], wo_ref[D_POOL:, :], preferred_element_type=F32))
    h1_ref[...] = h1
    xn = _rms(h1, g2_ref[...])
    xn_ref[...] = xn
    lg_ref[...] = jnp.dot(xn, wr_ref[...], preferred_element_type=F32,
                          precision=lax.Precision.HIGHEST) + br_ref[...]


def _post_call(x, yp, ya, wo, g2, wr, br):
    T, D = x.shape
    tm = TM_POST
    row = lambda w: pl.BlockSpec((tm, w), lambda i: (i, 0))
    return pl.pallas_call(
        _post_kernel,
        grid=(T // tm,),
        in_specs=[row(D), row(D_POOL), row(D_ATTN), _const_spec(wo.shape),
                  _const_spec(g2.shape), _const_spec(wr.shape), _const_spec(br.shape)],
        out_specs=[row(D), row(D), row(LANES)],
        out_shape=[jax.ShapeDtypeStruct((T, D), F32), jax.ShapeDtypeStruct((T, D), F32),
                   jax.ShapeDtypeStruct((T, LANES), F32)],
        compiler_params=pltpu.CompilerParams(
            dimension_semantics=("arbitrary",), vmem_limit_bytes=VMEM_LIMIT),
        name="post",
    )(x, yp, ya, wo, g2, wr, br)


def _route_kernel(lg_ref, code_ref, wts_ref, cnt_ref, carry_ref):
    i = pl.program_id(0)
    tm = lg_ref.shape[0]

    @pl.when(i == 0)
    def _():
        carry_ref[...] = jnp.zeros_like(carry_ref)

    lg = lg_ref[...]
    lane = lax.broadcasted_iota(I32, (tm, LANES), 1)
    lane_f = lane.astype(F32)
    neg_inf = -jnp.inf

    def first_argmax(vals):
        mx = jnp.max(vals, axis=-1, keepdims=True)
        idx = jnp.min(jnp.where(vals == mx, lane_f, float(LANES)), axis=-1, keepdims=True)
        return mx, idx.astype(I32)

    is_group = lane < N_GROUPS
    g_logits = jnp.where(is_group, lg, neg_inf)
    g_max, g_sel = first_argmax(g_logits)
    g_w = 1.0 / jnp.sum(jnp.where(is_group, jnp.exp(lg - g_max), 0.0), axis=-1, keepdims=True)

    lo = N_GROUPS + g_sel * EXPERTS_PER_GROUP
    in_group = (lane >= lo) & (lane < lo + EXPERTS_PER_GROUP)
    e_logits = jnp.where(in_group, lg, neg_inf)
    v1, i1 = first_argmax(e_logits)
    v2, i2 = first_argmax(jnp.where(lane == i1, neg_inf, e_logits))
    e1 = i1 - N_GROUPS
    e2 = i2 - N_GROUPS
    t = jnp.exp(v2 - v1)
    w1 = g_w / (1.0 + t)
    w2 = g_w * t / (1.0 + t)
    wts_ref[...] = jnp.where(lane == 0, w1, jnp.where(lane == 1, w2, 0.0))

    hit1 = lane == e1
    hit2 = lane == e2
    onehot = jnp.where(hit1 | hit2, 1.0, 0.0).astype(BF16)
    r = lax.broadcasted_iota(I32, (tm, tm), 0)
    c = lax.broadcasted_iota(I32, (tm, tm), 1)
    lower = jnp.where(c < r, 1.0, 0.0).astype(BF16)
    before = jnp.dot(lower, onehot, preferred_element_type=F32) + carry_ref[0:1, :]
    carry_new = carry_ref[0:1, :] + jnp.sum(onehot.astype(F32), axis=0, keepdims=True)
    carry_ref[...] = jnp.broadcast_to(carry_new, carry_ref.shape)
    pos1 = jnp.sum(jnp.where(hit1, before, 0.0), axis=-1, keepdims=True).astype(I32)
    pos2 = jnp.sum(jnp.where(hit2, before, 0.0), axis=-1, keepdims=True).astype(I32)
    code1 = e1 * 65536 + pos1
    code2 = e2 * 65536 + pos2
    code_ref[...] = jnp.where(lane == 0, code1, jnp.where(lane == 1, code2, 0))
    cnt_ref[...] = carry_ref[...].astype(I32)


def _route_call(logits):
    T = logits.shape[0]
    tm = TM_ROUTE
    row = pl.BlockSpec((tm, LANES), lambda i: (i, 0))
    return pl.pallas_call(
        _route_kernel,
        grid=(T // tm,),
        in_specs=[row],
        out_specs=[row, row, pl.BlockSpec((8, LANES), lambda i: (0, 0))],
        out_shape=[jax.ShapeDtypeStruct((T, LANES), I32), jax.ShapeDtypeStruct((T, LANES), F32),
                   jax.ShapeDtypeStruct((8, LANES), I32)],
        scratch_shapes=[pltpu.VMEM((8, LANES), F32)],
        compiler_params=pltpu.CompilerParams(
            dimension_semantics=("arbitrary",), vmem_limit_bytes=VMEM_LIMIT),
        name="route",
    )(logits)


def _plan_kernel(code_ref, cnt_ref, slot_ref, blke_ref, blkn_ref, start_ref):
    n_assign = code_ref.shape[0]
    n_slots = slot_ref.shape[0]
    n_blocks = blke_ref.shape[0]

    def fill(s, _):
        slot_ref[s] = 0
        return 0
    lax.fori_loop(0, n_slots, fill, 0)

    def per_expert(e, nb_done):
        cnt = cnt_ref[e]
        nb = lax.shift_right_logical(cnt + (MOE_BLK - 1), MOE_BLK.bit_length() - 1)
        start_ref[e] = nb_done * MOE_BLK

        def per_block(j, _):
            blke_ref[nb_done + j] = e
            blkn_ref[nb_done + j] = jnp.minimum(cnt - j * MOE_BLK, MOE_BLK)
            return 0
        lax.fori_loop(0, nb, per_block, 0)
        return nb_done + nb
    used = lax.fori_loop(0, N_EXPERTS, per_expert, 0)

    last_e = blke_ref[jnp.maximum(used - 1, 0)]

    def tail(b, _):
        blke_ref[b] = last_e
        blkn_ref[b] = 0
        return 0
    lax.fori_loop(used, n_blocks, tail, 0)

    def place(a, _):
        code = code_ref[a]
        e = lax.shift_right_logical(code, 16)
        slot_ref[start_ref[e] + (code & 0xFFFF)] = a
        return 0
    lax.fori_loop(0, n_assign, place, 0)


def _plan_call(code_flat, counts, n_blocks):
    smem = pl.BlockSpec(memory_space=pltpu.SMEM)
    return pl.pallas_call(
        _plan_kernel,
        in_specs=[smem, smem],
        out_specs=[smem, smem, smem],
        out_shape=[jax.ShapeDtypeStruct((n_blocks * MOE_BLK,), I32),
                   jax.ShapeDtypeStruct((n_blocks,), I32),
                   jax.ShapeDtypeStruct((n_blocks,), I32)],
        scratch_shapes=[pltpu.SMEM((N_EXPERTS,), I32)],
        name="plan",
    )(code_flat, counts)


def _expert_kernel(blke_ref, blkn_ref, slot_ref, xn_hbm, wg_ref, wu_ref, wd_ref, y_hbm,
                   xbuf, ybuf, gsem, ssem):
    i = pl.program_id(0)
    nsteps = pl.num_programs(0)
    n = blkn_ref[i]

    def wait_rows(src, dst, sem, cnt):
        full = pl.multiple_of(cnt & ~7, 8)

        @pl.when(full > 0)
        def _():
            pltpu.make_async_copy(src.at[pl.ds(0, full)], dst.at[pl.ds(0, full)], sem).wait()

        def one(r, _):
            pltpu.make_async_copy(src.at[pl.ds(0, 1)], dst.at[pl.ds(0, 1)], sem).wait()
            return 0
        lax.fori_loop(0, cnt & 7, one, 0)

    def scatter_wait(cnt):
        wait_rows(ybuf, y_hbm, ssem, cnt)

    @pl.when(i == 0)
    def _():
        xbuf[...] = jnp.zeros_like(xbuf)

    @pl.when(i > 0)
    def _():
        scatter_wait(blkn_ref[jnp.maximum(i - 1, 0)])

    @pl.when(n > 0)
    def _():
        def gather(r, _):
            tok = lax.shift_right_logical(slot_ref[0, 0, r], 1)
            pltpu.make_async_copy(xn_hbm.at[pl.ds(tok, 1)], xbuf.at[pl.ds(r, 1)], gsem).start()
            return 0
        lax.fori_loop(0, n, gather, 0)
        wait_rows(xn_hbm, xbuf, gsem, n)

        xb = xbuf[...].astype(BF16)
        a = jnp.dot(xb, wg_ref[0].astype(BF16), preferred_element_type=F32)
        u = jnp.dot(xb, wu_ref[0].astype(BF16), preferred_element_type=F32)
        hmid = (a * jax.nn.sigmoid(a) * u).astype(BF16)
        ybuf[...] = jnp.dot(hmid, wd_ref[0].astype(BF16), preferred_element_type=F32)

        def scatter(r, _):
            pltpu.make_async_copy(ybuf.at[pl.ds(r, 1)],
                                  y_hbm.at[pl.ds(slot_ref[0, 0, r], 1)], ssem).start()
            return 0
        lax.fori_loop(0, n, scatter, 0)

    @pl.when(i == nsteps - 1)
    def _():
        scatter_wait(n)


def _expert_call(blk_e, blk_n, slots, xn, wg, wu, wd):
    T, D = xn.shape
    n_blocks = blk_e.shape[0]
    F = wg.shape[-1]
    grid_spec = pltpu.PrefetchScalarGridSpec(
        num_scalar_prefetch=2,
        grid=(n_blocks,),
        in_specs=[
            pl.BlockSpec((1, 1, MOE_BLK), lambda i, be, bn: (i, 0, 0), memory_space=pltpu.SMEM),
            pl.BlockSpec(memory_space=pl.ANY),
            pl.BlockSpec((1, D, F), lambda i, be, bn: (be[i], 0, 0)),
            pl.BlockSpec((1, D, F), lambda i, be, bn: (be[i], 0, 0)),
            pl.BlockSpec((1, F, D), lambda i, be, bn: (be[i], 0, 0)),
        ],
        out_specs=pl.BlockSpec(memory_space=pl.ANY),
        scratch_shapes=[pltpu.VMEM((MOE_BLK, D), F32), pltpu.VMEM((MOE_BLK, D), F32),
                        pltpu.SemaphoreType.DMA, pltpu.SemaphoreType.DMA],
    )
    return pl.pallas_call(
        _expert_kernel,
        grid_spec=grid_spec,
        out_shape=jax.ShapeDtypeStruct((T * TOP_K, D), F32),
        compiler_params=pltpu.CompilerParams(
            dimension_semantics=("arbitrary",), vmem_limit_bytes=VMEM_LIMIT),
        name="experts",
    )(blk_e, blk_n, slots.reshape(n_blocks, 1, MOE_BLK), xn, wg, wu, wd)


def _final_kernel(h1_ref, y_ref, wts_ref, p_ref, wpp_ref, pg_ref, g3_ref, wpg_ref, o_ref):
    D = h1_ref.shape[1]
    w = wts_ref[...]
    h2 = h1_ref[...] + w[:, 0:1] * y_ref[:, :D] + w[:, 1:2] * y_ref[:, D:]
    e = _rms(jnp.dot(p_ref[...].astype(BF16), wpp_ref[...], preferred_element_type=F32),
             pg_ref[...])
    gate = jax.nn.sigmoid(jnp.dot(_rms(h2, g3_ref[...]).astype(BF16), wpg_ref[...],
                                  preferred_element_type=F32))
    o_ref[...] = h2 + gate * e


def _final_call(h1, y2, wts, p, wpp, pg, g3, wpg):
    T, D = h1.shape
    tm = TM_POST
    row = lambda w: pl.BlockSpec((tm, w), lambda i: (i, 0))
    return pl.pallas_call(
        _final_kernel,
        grid=(T // tm,),
        in_specs=[row(D), row(TOP_K * D), row(LANES), row(PLE_DIM), _const_spec(wpp.shape),
                  _const_spec(pg.shape), _const_spec(g3.shape), _const_spec(wpg.shape)],
        out_specs=row(D),
        out_shape=jax.ShapeDtypeStruct((T, D), F32),
        compiler_params=pltpu.CompilerParams(
            dimension_semantics=("arbitrary",), vmem_limit_bytes=VMEM_LIMIT),
        name="final",
    )(h1, y2, wts, p, wpp, pg, g3, wpg)


def _rope_lanes(a):
    z = jnp.zeros(a.shape[:-1] + (ROPE_HALF,), a.dtype)
    return jnp.concatenate([a[..., :ROPE_HALF], z, a[..., ROPE_HALF:], z], axis=-1)


def _head_lanes(a):
    return jnp.concatenate([a[..., :QK_NOPE], _rope_lanes(a[..., QK_NOPE:])], axis=-1)


def kernel(x, p, positions, norm1_gain, w_in, q_a_gain, w_q_b, kv_a_gain, w_kv_b, q_norm_gain, k_norm_gain, w_pool, pool_scale, w_out, norm2_gain, w_router_group, b_router_group, w_router_expert, b_router_expert, w_exp_gate, w_exp_up, w_exp_down, norm3_gain, w_ple_gate, w_ple_proj, ple_norm_gain):
    B, S, D = x.shape
    T = B * S
    assert x.shape[2] == D_MODEL and S % TQ == 0 and S % TM_PRE == 0 and T % TM_POST == 0
    layer = 0
    row = lambda a: a[layer].reshape(1, -1)

    n_lat = D_POOL + Q_LORA + KV_LORA
    win = jnp.concatenate([w_in[layer][:, :n_lat], _rope_lanes(w_in[layer][:, n_lat:])],
                          axis=1).astype(BF16)
    wq = _head_lanes(w_q_b[layer].reshape(Q_LORA, N_HEADS, QK_DIM)
                     ).reshape(Q_LORA, N_HEADS * HEAD_W).astype(BF16)
    wkv3 = w_kv_b[layer].reshape(KV_LORA, N_HEADS, QK_NOPE + V_DIM)
    wkv = jnp.concatenate([wkv3[..., :QK_NOPE].reshape(KV_LORA, -1),
                           wkv3[..., QK_NOPE:].reshape(KV_LORA, -1)], axis=1).astype(BF16)
    qng = _head_lanes(q_norm_gain[layer]).reshape(1, HEAD_W)
    kng = _head_lanes(k_norm_gain[layer]).reshape(1, HEAD_W)
    inv_freq = ROPE_THETA ** (-jnp.arange(ROPE_HALF, dtype=F32) / ROPE_HALF)
    invf = _rope_lanes(jnp.concatenate([inv_freq, inv_freq])).reshape(1, LANES)
    sgn = _rope_lanes(jnp.concatenate([-jnp.ones((ROPE_HALF,), F32),
                                       jnp.ones((ROPE_HALF,), F32)])).reshape(1, LANES)
    wr = jnp.zeros((D, LANES), F32)
    wr = wr.at[:, :N_GROUPS].set(w_router_group[layer])
    wr = wr.at[:, N_GROUPS:N_GROUPS + N_EXPERTS].set(w_router_expert[layer])
    br = jnp.zeros((1, LANES), F32)
    br = br.at[0, :N_GROUPS].set(b_router_group[layer])
    br = br.at[0, N_GROUPS:N_GROUPS + N_EXPERTS].set(b_router_expert[layer])

    ypool, q, k, v = _pre_call(
        x, positions.reshape(B, S, 1), row(norm1_gain), win, row(q_a_gain), wq, row(kv_a_gain),
        wkv, qng, kng, w_pool[layer].astype(BF16), row(pool_scale), invf, sgn)
    yattn = _attn_call(q, k, v)
    h1, xn2, logits = _post_call(
        x.reshape(T, D), ypool.reshape(T, D_POOL), yattn.reshape(T, D_ATTN),
        w_out[layer].astype(BF16), row(norm2_gain), wr, br)

    n_assign = T * TOP_K
    n_blocks = (n_assign + N_EXPERTS * (MOE_BLK - 1)) // MOE_BLK
    code, wts, counts = _route_call(logits)
    slots, blk_e, blk_n = _plan_call(code[:, :TOP_K].reshape(n_assign), counts[0], n_blocks)
    y = _expert_call(blk_e, blk_n, slots, xn2,
                     w_exp_gate[layer], w_exp_up[layer], w_exp_down[layer])

    out = _final_call(h1, y.reshape(T, TOP_K * D), wts, p[layer].reshape(T, PLE_DIM),
                      w_ple_proj[layer].astype(BF16), row(ple_norm_gain), row(norm3_gain),
                      w_ple_gate[layer].astype(BF16))
    return out.reshape(B, S, D)
```

```python
import functools
import math

import jax
import jax.numpy as jnp
from jax import lax
from jax.experimental import pallas as pl
from jax.experimental.pallas import tpu as pltpu

F32 = jnp.float32
BF16 = jnp.bfloat16
I32 = jnp.int32

D_MODEL = 2048
PLE_DIM = 256
EPS = 1e-6
POOL_WINDOWS = (2, 4, 8, 16)
POOL_CH = 256
D_POOL = POOL_CH * len(POOL_WINDOWS)
N_HEADS = 8
Q_LORA = 512
KV_LORA = 512
QK_NOPE = 128
QK_ROPE = 64
QK_DIM = QK_NOPE + QK_ROPE
V_DIM = 128
D_ATTN = N_HEADS * V_DIM
ROPE_THETA = 10000.0
ATTN_SCALE = 1.0 / math.sqrt(QK_DIM)
Q_PRESCALE = ATTN_SCALE * math.log2(math.e)
N_GROUPS = 8
EXPERTS_PER_GROUP = 8
N_EXPERTS = N_GROUPS * EXPERTS_PER_GROUP
TOP_K = 2
D_EXPERT = 512

LANES = 128
MXU_DIM = 256
VMEM_LIMIT = 56 * 1024 * 1024

HEAD_W = 2 * LANES
ROPE_HALF = QK_ROPE // 2
POOL_HALO = 16
TM_PRE = 256
TM_POST = 512
TM_FINAL = 256
TQ = 512
TK = TM_PRE
TM_ROUTE = 256
MOE_BLK = 256
MOE_DRAIN_STEPS = 3
PLAN_UNROLL = 16
NEG_BIG = -1e30


def _const_spec(shape):
    nd = len(shape)
    return pl.BlockSpec(shape, lambda *_: (0,) * nd, pipeline_mode=pl.Buffered(1))


def _rms(x, gain):
    return x * lax.rsqrt(jnp.mean(x * x, axis=-1, keepdims=True) + EPS) * gain


def _pre_kernel(x_ref, pos_ref, g1_ref, win_ref, qag_ref, wq_ref, kvag_ref, wkv_ref,
                qng_ref, kng_ref, wpool_ref, pscale_ref, invf_ref, sgn_ref,
                ypool_ref, q_ref, k_ref, vt_ref, carry_ref):
    st = pl.program_id(1)
    tm = x_ref.shape[0]

    hn = _rms(x_ref[...], g1_ref[...])
    z = jnp.dot(hn.astype(BF16), win_ref[...], preferred_element_type=F32)

    @pl.when(st == 0)
    def _():
        carry_ref[...] = jnp.zeros_like(carry_ref)

    u = z[:, :D_POOL]
    ext = jnp.concatenate([carry_ref[...], u], axis=0)
    carry_ref[...] = u[tm - POOL_HALO:, :]
    row = lax.broadcasted_iota(I32, (tm, 1), 0) + st * tm
    level = ext
    shift = 1
    for g, w in enumerate(POOL_WINDOWS):
        sl = slice(g * POOL_CH, (g + 1) * POOL_CH)
        while shift < w:
            level = level + pltpu.roll(level, shift, 0)
            shift *= 2
        win_sum = level[POOL_HALO:, sl]
        cnt = jnp.minimum(row + 1, w).astype(F32)
        d = win_sum / cnt - u[:, sl]
        y = jnp.dot(d.astype(BF16), wpool_ref[g], preferred_element_type=F32)
        ypool_ref[:, sl] = (y * pscale_ref[:, sl]).astype(BF16)

    q_lat = z[:, D_POOL:D_POOL + Q_LORA]
    kv_lat = z[:, D_POOL + Q_LORA:D_POOL + Q_LORA + KV_LORA]
    k_rope = z[:, D_POOL + Q_LORA + KV_LORA:]
    qa = jnp.dot(_rms(q_lat, qag_ref[...]).astype(BF16), wq_ref[...],
                 preferred_element_type=F32)
    kv = jnp.dot(_rms(kv_lat, kvag_ref[...]).astype(BF16), wkv_ref[...],
                 preferred_element_type=F32)

    ang = pos_ref[...].astype(F32) * invf_ref[...]
    cos = jnp.cos(ang)
    sin = jnp.sin(ang) * sgn_ref[...]

    def rot(t):
        return t * cos + pltpu.roll(t, LANES // 2, 1) * sin

    qng = qng_ref[...]
    kng = kng_ref[...]
    kr_rot = rot(k_rope * kng[:, LANES:])
    kr_ssq = jnp.sum(k_rope * k_rope, axis=-1, keepdims=True)
    for h in range(N_HEADS):
        qh = qa[:, h * HEAD_W:(h + 1) * HEAD_W]
        rq = lax.rsqrt(jnp.sum(qh * qh, axis=-1, keepdims=True) / QK_DIM + EPS) * Q_PRESCALE
        qn = qh * rq * qng
        q_ref[0, h, :, :LANES] = qn[:, :LANES].astype(BF16)
        q_ref[0, h, :, LANES:] = rot(qn[:, LANES:]).astype(BF16)
        kh = kv[:, h * QK_NOPE:(h + 1) * QK_NOPE]
        rk = lax.rsqrt((jnp.sum(kh * kh, axis=-1, keepdims=True) + kr_ssq) / QK_DIM + EPS)
        k_ref[0, h, :, :LANES] = (kh * rk * kng[:, :LANES]).astype(BF16)
        k_ref[0, h, :, LANES:] = (kr_rot * rk).astype(BF16)
        vh = kv[:, N_HEADS * QK_NOPE + h * V_DIM:N_HEADS * QK_NOPE + (h + 1) * V_DIM]
        vt_ref[0, h, 0, :, :] = vh.T.astype(BF16)


def _pre_call(x, pos, g1, win, qag, wq, kvag, wkv, qng, kng, wpool, pscale, invf, sgn):
    B, S, D = x.shape
    tm = TM_PRE
    grid = (B, S // tm)
    row_spec = lambda w: pl.BlockSpec((None, tm, w), lambda b, s: (b, s, 0))
    head_spec = lambda w: pl.BlockSpec((1, N_HEADS, tm, w), lambda b, s: (b, 0, s, 0))
    consts = [g1, win, qag, wq, kvag, wkv, qng, kng, wpool, pscale, invf, sgn]
    return pl.pallas_call(
        _pre_kernel,
        grid=grid,
        in_specs=[row_spec(D), row_spec(1)] + [_const_spec(c.shape) for c in consts],
        out_specs=[row_spec(D_POOL), head_spec(HEAD_W), head_spec(HEAD_W),
                   pl.BlockSpec((1, N_HEADS, 1, V_DIM, tm), lambda b, s: (b, 0, s, 0, 0))],
        out_shape=[jax.ShapeDtypeStruct((B, S, D_POOL), BF16),
                   jax.ShapeDtypeStruct((B, N_HEADS, S, HEAD_W), BF16),
                   jax.ShapeDtypeStruct((B, N_HEADS, S, HEAD_W), BF16),
                   jax.ShapeDtypeStruct((B, N_HEADS, S // tm, V_DIM, tm), BF16)],
        scratch_shapes=[pltpu.VMEM((POOL_HALO, D_POOL), F32)],
        compiler_params=pltpu.CompilerParams(
            dimension_semantics=("arbitrary", "arbitrary"), vmem_limit_bytes=VMEM_LIMIT),
        name="pre",
    )(x, pos, *consts)


def _attn_kernel(q_ref, k_ref, vt_ref, o_ref, *chain_scratch):
    S = q_ref.shape[2]
    nq = S // TQ
    per_q = TQ // TK
    assert per_q == 2, "two online-softmax chains take the even / odd key tiles"
    chains = (chain_scratch[0:3], chain_scratch[3:6])
    key_idx = lax.broadcasted_iota(I32, (TK, TQ), 0)
    qry_idx = lax.broadcasted_iota(I32, (TK, TQ), 1)

    def q_tile(qi, _):
        q = q_ref[0, 0, pl.ds(pl.multiple_of(qi * TQ, TQ), TQ), :]
        for m_ref, l_ref, acc_ref in chains:
            m_ref[...] = jnp.full(m_ref.shape, NEG_BIG, F32)
            l_ref[...] = jnp.zeros(l_ref.shape, F32)
            acc_ref[...] = jnp.zeros(acc_ref.shape, F32)

        def scores(kt):
            k = k_ref[0, 0, pl.ds(pl.multiple_of(kt * TK, TK), TK), :]
            return lax.dot_general(k, q, (((1,), (1,)), ((), ())), preferred_element_type=F32)

        def fold(chain, st, kt, mask):
            m_ref, l_ref, acc_ref = chain
            if mask is not None:
                st = jnp.where(mask, st, NEG_BIG)
            m = m_ref[...]
            m_new = jnp.maximum(m, jnp.max(st, axis=0, keepdims=True))
            alpha = jnp.exp2(m - m_new)
            p = jnp.exp2(st - m_new)
            l_ref[...] = alpha * l_ref[...] + jnp.sum(p, axis=0, keepdims=True)
            m_ref[...] = m_new
            acc_ref[...] = alpha * acc_ref[...] + jnp.dot(
                vt_ref[0, 0, kt], p.astype(BF16), preferred_element_type=F32)

        def pair_scores(j):
            return scores(per_q * j), scores(per_q * j + 1)

        def fold_pair(j, st, masks):
            fold(chains[0], st[0], per_q * j, masks[0])
            fold(chains[1], st[1], per_q * j + 1, masks[1])

        def off_diagonal(j, st):
            st_next = pair_scores(j + 1)
            fold_pair(j, st, (None, None))
            return st_next
        st = lax.fori_loop(0, qi, off_diagonal, pair_scores(0))
        fold_pair(qi, st, (key_idx <= qry_idx, key_idx + TK <= qry_idx))

        (m0, l0, acc0), (m1, l1, acc1) = chains
        m = jnp.maximum(m0[...], m1[...])
        w0 = jnp.exp2(m0[...] - m)
        w1 = jnp.exp2(m1[...] - m)
        l = w0 * l0[...] + w1 * l1[...]
        acc = w0 * acc0[...] + w1 * acc1[...]
        o_ref[0, pl.ds(pl.multiple_of(qi * TQ, TQ), TQ), :] = (acc / l).T.astype(BF16)
        return 0

    lax.fori_loop(0, nq, q_tile, 0)


def _attn_call(q, k, vt):
    B, H, S, _ = q.shape
    head = lambda w: pl.BlockSpec((1, 1, S, w), lambda b, h: (b, h, 0, 0))
    return pl.pallas_call(
        _attn_kernel,
        grid=(B, H),
        in_specs=[head(HEAD_W), head(HEAD_W),
                  pl.BlockSpec((1, 1) + vt.shape[2:], lambda b, h: (b, h, 0, 0, 0))],
        out_specs=pl.BlockSpec((1, S, V_DIM), lambda b, h: (b, 0, h)),
        out_shape=jax.ShapeDtypeStruct((B, S, H * V_DIM), BF16),
        scratch_shapes=[pltpu.VMEM((1, TQ), F32), pltpu.VMEM((1, TQ), F32),
                        pltpu.VMEM((V_DIM, TQ), F32)] * 2,
        compiler_params=pltpu.CompilerParams(
            dimension_semantics=("arbitrary", "arbitrary"), vmem_limit_bytes=VMEM_LIMIT),
        name="attn",
    )(q, k, vt)


def _post_kernel(x_ref, yp_ref, ya_ref, wo_ref, g2_ref, wr_ref, br_ref,
                 h1_ref, xn_ref, lg_ref):
    h1 = (x_ref[...]
          + jnp.dot(yp_ref[...], wo_ref[:D_POOL, :], preferred_element_type=F32)
          + jnp.dot(ya_ref[...], wo_ref[D_POOL:, :], preferred_element_type=F32))
    h1_ref[...] = h1
    xn = _rms(h1, g2_ref[...])
    xn_ref[...] = xn
    xn_hi = xn.astype(BF16)
    xn_lo = (xn - xn_hi.astype(F32)).astype(BF16)
    both = (jnp.dot(xn_hi, wr_ref[...], preferred_element_type=F32)
            + jnp.dot(xn_lo, wr_ref[...], preferred_element_type=F32))
    lg_ref[...] = both[:, :LANES] + both[:, LANES:] + br_ref[...]


def _post_call(x, yp, ya, wo, g2, wr, br):
    T, D = x.shape
    tm = TM_POST
    row = lambda w: pl.BlockSpec((tm, w), lambda i: (i, 0))
    return pl.pallas_call(
        _post_kernel,
        grid=(T // tm,),
        in_specs=[row(D), row(D_POOL), row(D_ATTN), _const_spec(wo.shape),
                  _const_spec(g2.shape), _const_spec(wr.shape), _const_spec(br.shape)],
        out_specs=[row(D), row(D), row(LANES)],
        out_shape=[jax.ShapeDtypeStruct((T, D), F32), jax.ShapeDtypeStruct((T, D), F32),
                   jax.ShapeDtypeStruct((T, LANES), F32)],
        compiler_params=pltpu.CompilerParams(
            dimension_semantics=("arbitrary",), vmem_limit_bytes=VMEM_LIMIT),
        name="post",
    )(x, yp, ya, wo, g2, wr, br)


def _route_kernel(lg_ref, code_ref, wts_ref, cnt_ref, carry_ref):
    i = pl.program_id(0)
    tm = lg_ref.shape[0]

    @pl.when(i == 0)
    def _():
        carry_ref[...] = jnp.zeros_like(carry_ref)

    lg = lg_ref[...]
    lane = lax.broadcasted_iota(I32, (tm, LANES), 1)
    lane_f = lane.astype(F32)
    neg_inf = -jnp.inf

    def first_argmax(vals):
        mx = jnp.max(vals, axis=-1, keepdims=True)
        idx = jnp.min(jnp.where(vals == mx, lane_f, float(LANES)), axis=-1, keepdims=True)
        return mx, idx.astype(I32)

    is_group = lane < N_GROUPS
    g_logits = jnp.where(is_group, lg, neg_inf)
    g_max, g_sel = first_argmax(g_logits)
    g_w = 1.0 / jnp.sum(jnp.where(is_group, jnp.exp(lg - g_max), 0.0), axis=-1, keepdims=True)

    lo = N_GROUPS + g_sel * EXPERTS_PER_GROUP
    in_group = (lane >= lo) & (lane < lo + EXPERTS_PER_GROUP)
    e_logits = jnp.where(in_group, lg, neg_inf)
    v1, i1 = first_argmax(e_logits)
    v2, i2 = first_argmax(jnp.where(lane == i1, neg_inf, e_logits))
    e1 = i1 - N_GROUPS
    e2 = i2 - N_GROUPS
    t = jnp.exp(v2 - v1)
    w1 = g_w / (1.0 + t)
    w2 = g_w * t / (1.0 + t)
    wts_ref[...] = jnp.where(lane == 0, w1, jnp.where(lane == 1, w2, 0.0))

    hit1 = lane == e1
    hit2 = lane == e2
    onehot = jnp.where(hit1 | hit2, 1.0, 0.0).astype(BF16)
    r = lax.broadcasted_iota(I32, (tm, tm), 0)
    c = lax.broadcasted_iota(I32, (tm, tm), 1)
    lower = jnp.where(c < r, 1.0, 0.0).astype(BF16)
    before = jnp.dot(lower, onehot, preferred_element_type=F32) + carry_ref[0:1, :]
    carry_new = carry_ref[0:1, :] + jnp.sum(onehot.astype(F32), axis=0, keepdims=True)
    carry_ref[...] = jnp.broadcast_to(carry_new, carry_ref.shape)
    pos1 = jnp.sum(jnp.where(hit1, before, 0.0), axis=-1, keepdims=True).astype(I32)
    pos2 = jnp.sum(jnp.where(hit2, before, 0.0), axis=-1, keepdims=True).astype(I32)
    code1 = e1 * 65536 + pos1
    code2 = e2 * 65536 + pos2
    code_ref[...] = jnp.where(lane == 0, code1, jnp.where(lane == 1, code2, 0))
    cnt_ref[...] = carry_ref[...].astype(I32)


def _route_call(logits):
    T = logits.shape[0]
    tm = TM_ROUTE
    row = pl.BlockSpec((tm, LANES), lambda i: (i, 0))
    return pl.pallas_call(
        _route_kernel,
        grid=(T // tm,),
        in_specs=[row],
        out_specs=[row, row, pl.BlockSpec((8, LANES), lambda i: (0, 0))],
        out_shape=[jax.ShapeDtypeStruct((T, LANES), I32), jax.ShapeDtypeStruct((T, LANES), F32),
                   jax.ShapeDtypeStruct((8, LANES), I32)],
        scratch_shapes=[pltpu.VMEM((8, LANES), F32)],
        compiler_params=pltpu.CompilerParams(
            dimension_semantics=("arbitrary",), vmem_limit_bytes=VMEM_LIMIT),
        name="route",
    )(logits)


def _plan_kernel(code_ref, cnt_ref, slot_ref, blke_ref, blkn_ref, start_ref):
    n_assign = code_ref.shape[0]
    n_slots = slot_ref.shape[0]
    n_blocks = blke_ref.shape[0]

    def fill(s, _):
        slot_ref[s] = n_assign + (s & (MOE_BLK - 1))
        return 0
    lax.fori_loop(0, n_slots, fill, 0, unroll=PLAN_UNROLL)

    def per_expert(e, nb_done):
        cnt = cnt_ref[e]
        nb = lax.shift_right_logical(cnt + (MOE_BLK - 1), MOE_BLK.bit_length() - 1)
        start_ref[e] = nb_done * MOE_BLK

        def per_block(j, _):
            blke_ref[nb_done + j] = e
            blkn_ref[nb_done + j] = jnp.minimum(cnt - j * MOE_BLK, MOE_BLK)
            return 0
        lax.fori_loop(0, nb, per_block, 0)
        return nb_done + nb
    used = lax.fori_loop(0, N_EXPERTS, per_expert, 0)

    last_e = blke_ref[jnp.maximum(used - 1, 0)]

    def tail(b, _):
        blke_ref[b] = last_e
        blkn_ref[b] = 0
        return 0
    lax.fori_loop(used, n_blocks, tail, 0)

    def place(a, _):
        code = code_ref[a]
        e = lax.shift_right_logical(code, 16)
        slot_ref[start_ref[e] + (code & 0xFFFF)] = a
        return 0
    lax.fori_loop(0, n_assign, place, 0, unroll=PLAN_UNROLL)


def _plan_call(code_flat, counts, n_tab):
    smem = pl.BlockSpec(memory_space=pltpu.SMEM)
    return pl.pallas_call(
        _plan_kernel,
        in_specs=[smem, smem],
        out_specs=[smem, smem, smem],
        out_shape=[jax.ShapeDtypeStruct((n_tab * MOE_BLK,), I32),
                   jax.ShapeDtypeStruct((n_tab,), I32),
                   jax.ShapeDtypeStruct((n_tab,), I32)],
        scratch_shapes=[pltpu.SMEM((N_EXPERTS,), I32)],
        name="plan",
    )(code_flat, counts)


def _expert_kernel(blke_ref, blkn_ref, slot_g_ref, slot_s_ref, xn_hbm, wg_ref, wu_ref, wd_ref,
                   y_hbm, xbuf, ybuf, gsem, ssem):
    s = pl.program_id(0)
    n_tok = xn_hbm.shape[0]
    nxt = s & 1
    cur = 1 - nxt

    def active(j):
        return (j >= 0) & (blkn_ref[jnp.maximum(j, 0)] > 0)
    act1, act2, act3 = active(s - 1), active(s - 2), active(s - 3)

    def gather_copy(r, tok):
        return pltpu.make_async_copy(xn_hbm.at[pl.ds(tok, 1)], xbuf.at[nxt, pl.ds(r, 1)],
                                     gsem.at[nxt])

    def scatter_copy(r, dst):
        return pltpu.make_async_copy(ybuf.at[nxt, pl.ds(r, 1)], y_hbm.at[pl.ds(dst, 1)],
                                     ssem.at[nxt])

    def gather_issue():
        for r in range(MOE_BLK):
            gather_copy(r, slot_g_ref[0, 0, r] & (n_tok - 1)).start()

    def scatter_issue():
        for r in range(MOE_BLK):
            scatter_copy(r, slot_s_ref[0, 0, r]).start()

    def compute():
        xb = xbuf[cur].astype(BF16)
        a = jnp.dot(xb, wg_ref[0].astype(BF16), preferred_element_type=F32)
        u = jnp.dot(xb, wu_ref[0].astype(BF16), preferred_element_type=F32)
        hmid = (a * jax.nn.sigmoid(a) * u).astype(BF16)
        ybuf[cur] = jnp.dot(hmid, wd_ref[0].astype(BF16), preferred_element_type=F32)

    @pl.when((s == 1) | (s == 2) | ((s >= 3) & act2))
    def _():
        pltpu.make_async_copy(xn_hbm.at[pl.ds(0, MOE_BLK)], xbuf.at[cur], gsem.at[cur]).wait()

    @pl.when((s >= 3) & act3)
    def _():
        pltpu.make_async_copy(ybuf.at[cur], y_hbm.at[pl.ds(0, MOE_BLK)], ssem.at[cur]).wait()

    @pl.when(s == 0)
    def _():
        ybuf[0] = jnp.zeros((MOE_BLK, ybuf.shape[2]), F32)
        dump = pltpu.make_async_copy(ybuf.at[0], y_hbm.at[pl.ds(TOP_K * n_tok, MOE_BLK)],
                                     ssem.at[0])
        dump.start()
        dump.wait()
        gather_issue()

    @pl.when(s == 1)
    def _():
        gather_issue()
        compute()

    @pl.when((s >= 2) & act1)
    def _():
        gather_issue()
        scatter_issue()
        compute()

    @pl.when((s >= 2) & act2 & jnp.logical_not(act1))
    def _():
        scatter_issue()


def _expert_call(blk_e, blk_n, slots, xn, wg, wu, wd):
    T, D = xn.shape
    assert T & (T - 1) == 0, "token id is recovered from the assignment id with a mask"
    n_tab = blk_e.shape[0]
    F = wg.shape[-1]
    prev = lambda j: jnp.maximum(j, 0)
    grid_spec = pltpu.PrefetchScalarGridSpec(
        num_scalar_prefetch=2,
        grid=(n_tab,),
        in_specs=[
            pl.BlockSpec((1, 1, MOE_BLK), lambda s, be, bn: (s, 0, 0), memory_space=pltpu.SMEM),
            pl.BlockSpec((1, 1, MOE_BLK), lambda s, be, bn: (prev(s - 2), 0, 0),
                         memory_space=pltpu.SMEM),
            pl.BlockSpec(memory_space=pl.ANY),
            pl.BlockSpec((1, D, F), lambda s, be, bn: (be[prev(s - 1)], 0, 0)),
            pl.BlockSpec((1, D, F), lambda s, be, bn: (be[prev(s - 1)], 0, 0)),
            pl.BlockSpec((1, F, D), lambda s, be, bn: (be[prev(s - 1)], 0, 0)),
        ],
        out_specs=pl.BlockSpec(memory_space=pl.ANY),
        scratch_shapes=[pltpu.VMEM((2, MOE_BLK, D), F32), pltpu.VMEM((2, MOE_BLK, D), F32),
                        pltpu.SemaphoreType.DMA((2,)), pltpu.SemaphoreType.DMA((2,))],
    )
    slots3 = slots.reshape(n_tab, 1, MOE_BLK)
    return pl.pallas_call(
        _expert_kernel,
        grid_spec=grid_spec,
        out_shape=jax.ShapeDtypeStruct((T * TOP_K + MOE_BLK, D), F32),
        compiler_params=pltpu.CompilerParams(
            dimension_semantics=("arbitrary",), vmem_limit_bytes=VMEM_LIMIT,
            disable_bounds_checks=True),
        name="experts",
    )(blk_e, blk_n, slots3, slots3, xn, wg, wu, wd)


def _final_kernel(h1_ref, y1_ref, y2_ref, wts_ref, p_ref, wpp_ref, pg_ref, g3_ref, wpg_ref,
                  o_ref):
    w = wts_ref[...]
    h2 = h1_ref[...] + w[:, 0:1] * y1_ref[...] + w[:, 1:2] * y2_ref[...]
    e = _rms(jnp.dot(p_ref[...].astype(BF16), wpp_ref[...], preferred_element_type=F32),
             pg_ref[...])
    gate = jax.nn.sigmoid(jnp.dot(_rms(h2, g3_ref[...]).astype(BF16), wpg_ref[...],
                                  preferred_element_type=F32))
    o_ref[...] = h2 + gate * e


def _final_call(h1, y, wts, p, wpp, pg, g3, wpg):
    T, D = h1.shape
    tm = TM_FINAL
    row = lambda w: pl.BlockSpec((tm, w), lambda i: (i, 0))
    second = pl.BlockSpec((tm, D), lambda i: (i + T // tm, 0))
    return pl.pallas_call(
        _final_kernel,
        grid=(T // tm,),
        in_specs=[row(D), row(D), second, row(LANES), row(PLE_DIM), _const_spec(wpp.shape),
                  _const_spec(pg.shape), _const_spec(g3.shape), _const_spec(wpg.shape)],
        out_specs=row(D),
        out_shape=jax.ShapeDtypeStruct((T, D), F32),
        compiler_params=pltpu.CompilerParams(
            dimension_semantics=("arbitrary",), vmem_limit_bytes=VMEM_LIMIT),
        name="final",
    )(h1, y, y, wts, p, wpp, pg, g3, wpg)


def _rope_lanes(a):
    z = jnp.zeros(a.shape[:-1] + (ROPE_HALF,), a.dtype)
    return jnp.concatenate([a[..., :ROPE_HALF], z, a[..., ROPE_HALF:], z], axis=-1)


def _head_lanes(a):
    return jnp.concatenate([a[..., :QK_NOPE], _rope_lanes(a[..., QK_NOPE:])], axis=-1)


def kernel(x, p, positions, norm1_gain, w_in, q_a_gain, w_q_b, kv_a_gain, w_kv_b, q_norm_gain, k_norm_gain, w_pool, pool_scale, w_out, norm2_gain, w_router_group, b_router_group, w_router_expert, b_router_expert, w_exp_gate, w_exp_up, w_exp_down, norm3_gain, w_ple_gate, w_ple_proj, ple_norm_gain):
    B, S, D = x.shape
    T = B * S
    assert x.shape[2] == D_MODEL and S % TQ == 0 and S % TM_PRE == 0
    assert T % TM_POST == 0 and T % TM_FINAL == 0 and T % TM_ROUTE == 0
    layer = 0
    row = lambda a: a[layer].reshape(1, -1)

    n_lat = D_POOL + Q_LORA + KV_LORA
    win = jnp.concatenate([w_in[layer][:, :n_lat], _rope_lanes(w_in[layer][:, n_lat:])],
                          axis=1).astype(BF16)
    wq = _head_lanes(w_q_b[layer].reshape(Q_LORA, N_HEADS, QK_DIM)
                     ).reshape(Q_LORA, N_HEADS * HEAD_W).astype(BF16)
    wkv3 = w_kv_b[layer].reshape(KV_LORA, N_HEADS, QK_NOPE + V_DIM)
    wkv = jnp.concatenate([wkv3[..., :QK_NOPE].reshape(KV_LORA, -1),
                           wkv3[..., QK_NOPE:].reshape(KV_LORA, -1)], axis=1).astype(BF16)
    qng = _head_lanes(q_norm_gain[layer]).reshape(1, HEAD_W)
    kng = _head_lanes(k_norm_gain[layer]).reshape(1, HEAD_W)
    inv_freq = ROPE_THETA ** (-jnp.arange(ROPE_HALF, dtype=F32) / ROPE_HALF)
    invf = _rope_lanes(jnp.concatenate([inv_freq, inv_freq])).reshape(1, LANES)
    sgn = _rope_lanes(jnp.concatenate([-jnp.ones((ROPE_HALF,), F32),
                                       jnp.ones((ROPE_HALF,), F32)])).reshape(1, LANES)
    wr = jnp.zeros((D, LANES), F32)
    wr = wr.at[:, :N_GROUPS].set(w_router_group[layer])
    wr = wr.at[:, N_GROUPS:N_GROUPS + N_EXPERTS].set(w_router_expert[layer])
    wr_hi = wr.astype(BF16)
    wr = jnp.concatenate([wr_hi, (wr - wr_hi.astype(F32)).astype(BF16)], axis=1)
    br = jnp.zeros((1, LANES), F32)
    br = br.at[0, :N_GROUPS].set(b_router_group[layer])
    br = br.at[0, N_GROUPS:N_GROUPS + N_EXPERTS].set(b_router_expert[layer])

    ypool, q, k, vt = _pre_call(
        x, positions.reshape(B, S, 1), row(norm1_gain), win, row(q_a_gain), wq, row(kv_a_gain),
        wkv, qng, kng, w_pool[layer].astype(BF16), row(pool_scale), invf, sgn)
    yattn = _attn_call(q, k, vt)
    h1, xn2, logits = _post_call(
        x.reshape(T, D), ypool.reshape(T, D_POOL), yattn.reshape(T, D_ATTN),
        w_out[layer].astype(BF16), row(norm2_gain), wr, br)

    n_assign = T * TOP_K
    n_blocks = (n_assign + N_EXPERTS * (MOE_BLK - 1)) // MOE_BLK
    code, wts, counts = _route_call(logits)
    slots, blk_e, blk_n = _plan_call(code[:, :TOP_K].T.reshape(n_assign), counts[0],
                                     n_blocks + MOE_DRAIN_STEPS)
    y = _expert_call(blk_e, blk_n, slots, xn2,
                     w_exp_gate[layer], w_exp_up[layer], w_exp_down[layer])

    out = _final_call(h1, y, wts, p[layer].reshape(T, PLE_DIM),
                      w_ple_proj[layer].astype(BF16), row(ple_norm_gain), row(norm3_gain),
                      w_ple_gate[layer].astype(BF16))
    return out.reshape(B, S, D)
```

```python
import functools
import math

import jax
import jax.numpy as jnp
from jax import lax
from jax.experimental import pallas as pl
from jax.experimental.pallas import tpu as pltpu

F32 = jnp.float32
BF16 = jnp.bfloat16
I32 = jnp.int32

D_MODEL = 2048
PLE_DIM = 256
EPS = 1e-6
POOL_WINDOWS = (2, 4, 8, 16)
POOL_CH = 256
D_POOL = POOL_CH * len(POOL_WINDOWS)
N_HEADS = 8
Q_LORA = 512
KV_LORA = 512
QK_NOPE = 128
QK_ROPE = 64
QK_DIM = QK_NOPE + QK_ROPE
V_DIM = 128
D_ATTN = N_HEADS * V_DIM
ROPE_THETA = 10000.0
ATTN_SCALE = 1.0 / math.sqrt(QK_DIM)
Q_PRESCALE = ATTN_SCALE * math.log2(math.e)
N_GROUPS = 8
EXPERTS_PER_GROUP = 8
N_EXPERTS = N_GROUPS * EXPERTS_PER_GROUP
TOP_K = 2
D_EXPERT = 512

LANES = 128
MXU_DIM = 256
VMEM_LIMIT = 56 * 1024 * 1024

HEAD_W = 2 * LANES
ROPE_HALF = QK_ROPE // 2
POOL_HALO = 16
TM_PRE = 256
TM_POST = 512
TM_FINAL = 256
TQ = 512
TK = TM_PRE
TM_ROUTE = 256
MOE_BLK = 256
MOE_DRAIN_STEPS = 3
PLAN_UNROLL = 16
NEG_BIG = -1e30


def _const_spec(shape):
    nd = len(shape)
    return pl.BlockSpec(shape, lambda *_: (0,) * nd, pipeline_mode=pl.Buffered(1))


def _rms(x, gain):
    return x * lax.rsqrt(jnp.mean(x * x, axis=-1, keepdims=True) + EPS) * gain


def _pre_kernel(x_ref, pos_ref, g1_ref, win_ref, qag_ref, wq_ref, kvag_ref, wkv_ref,
                qng_ref, kng_ref, wpool_ref, pscale_ref, invf_ref, sgn_ref,
                ypool_ref, q_ref, k_ref, vt_ref, carry_ref):
    st = pl.program_id(1)
    tm = x_ref.shape[0]

    hn = _rms(x_ref[...], g1_ref[...])
    z = jnp.dot(hn.astype(BF16), win_ref[...], preferred_element_type=F32)

    @pl.when(st == 0)
    def _():
        carry_ref[...] = jnp.zeros_like(carry_ref)

    u = z[:, :D_POOL]
    ext = jnp.concatenate([carry_ref[...], u], axis=0)
    carry_ref[...] = u[tm - POOL_HALO:, :]
    row = lax.broadcasted_iota(I32, (tm, 1), 0) + st * tm
    level = ext
    shift = 1
    for g, w in enumerate(POOL_WINDOWS):
        sl = slice(g * POOL_CH, (g + 1) * POOL_CH)
        while shift < w:
            level = level + pltpu.roll(level, shift, 0)
            shift *= 2
        win_sum = level[POOL_HALO:, sl]
        cnt = jnp.minimum(row + 1, w).astype(F32)
        d = win_sum / cnt - u[:, sl]
        y = jnp.dot(d.astype(BF16), wpool_ref[g], preferred_element_type=F32)
        ypool_ref[:, sl] = (y * pscale_ref[:, sl]).astype(BF16)

    q_lat = z[:, D_POOL:D_POOL + Q_LORA]
    kv_lat = z[:, D_POOL + Q_LORA:D_POOL + Q_LORA + KV_LORA]
    k_rope = z[:, D_POOL + Q_LORA + KV_LORA:]
    qa = jnp.dot(_rms(q_lat, qag_ref[...]).astype(BF16), wq_ref[...],
                 preferred_element_type=F32)
    kv = jnp.dot(_rms(kv_lat, kvag_ref[...]).astype(BF16), wkv_ref[...],
                 preferred_element_type=F32)

    ang = pos_ref[...].astype(F32) * invf_ref[...]
    cos = jnp.cos(ang)
    sin = jnp.sin(ang) * sgn_ref[...]

    def rot(t):
        return t * cos + pltpu.roll(t, LANES // 2, 1) * sin

    qng = qng_ref[...]
    kng = kng_ref[...]
    kr_rot = rot(k_rope * kng[:, LANES:])
    kr_ssq = jnp.sum(k_rope * k_rope, axis=-1, keepdims=True)
    for h in range(N_HEADS):
        qh = qa[:, h * HEAD_W:(h + 1) * HEAD_W]
        rq = lax.rsqrt(jnp.sum(qh * qh, axis=-1, keepdims=True) / QK_DIM + EPS) * Q_PRESCALE
        qn = qh * rq * qng
        q_ref[0, h, :, :LANES] = qn[:, :LANES].astype(BF16)
        q_ref[0, h, :, LANES:] = rot(qn[:, LANES:]).astype(BF16)
        kh = kv[:, h * QK_NOPE:(h + 1) * QK_NOPE]
        rk = lax.rsqrt((jnp.sum(kh * kh, axis=-1, keepdims=True) + kr_ssq) / QK_DIM + EPS)
        k_ref[0, h, :, :LANES] = (kh * rk * kng[:, :LANES]).astype(BF16)
        k_ref[0, h, :, LANES:] = (kr_rot * rk).astype(BF16)
        vh = kv[:, N_HEADS * QK_NOPE + h * V_DIM:N_HEADS * QK_NOPE + (h + 1) * V_DIM]
        vt_ref[0, h, 0, :, :] = vh.T.astype(BF16)


def _pre_call(x, pos, g1, win, qag, wq, kvag, wkv, qng, kng, wpool, pscale, invf, sgn):
    B, S, D = x.shape
    tm = TM_PRE
    grid = (B, S // tm)
    row_spec = lambda w: pl.BlockSpec((None, tm, w), lambda b, s: (b, s, 0))
    head_spec = lambda w: pl.BlockSpec((1, N_HEADS, tm, w), lambda b, s: (b, 0, s, 0))
    consts = [g1, win, qag, wq, kvag, wkv, qng, kng, wpool, pscale, invf, sgn]
    return pl.pallas_call(
        _pre_kernel,
        grid=grid,
        in_specs=[row_spec(D), row_spec(1)] + [_const_spec(c.shape) for c in consts],
        out_specs=[row_spec(D_POOL), head_spec(HEAD_W), head_spec(HEAD_W),
                   pl.BlockSpec((1, N_HEADS, 1, V_DIM, tm), lambda b, s: (b, 0, s, 0, 0))],
        out_shape=[jax.ShapeDtypeStruct((B, S, D_POOL), BF16),
                   jax.ShapeDtypeStruct((B, N_HEADS, S, HEAD_W), BF16),
                   jax.ShapeDtypeStruct((B, N_HEADS, S, HEAD_W), BF16),
                   jax.ShapeDtypeStruct((B, N_HEADS, S // tm, V_DIM, tm), BF16)],
        scratch_shapes=[pltpu.VMEM((POOL_HALO, D_POOL), F32)],
        compiler_params=pltpu.CompilerParams(
            dimension_semantics=("arbitrary", "arbitrary"), vmem_limit_bytes=VMEM_LIMIT),
        name="pre",
    )(x, pos, *consts)


def _attn_kernel(q_ref, k_ref, vt_ref, o_ref, *chain_scratch):
    S = q_ref.shape[2]
    nq = S // TQ
    per_q = TQ // TK
    assert per_q == 2, "two online-softmax chains take the even / odd key tiles"
    chains = (chain_scratch[0:3], chain_scratch[3:6])
    score_bufs = (chain_scratch[6:8], chain_scratch[8:10])
    key_idx = lax.broadcasted_iota(I32, (TK, TQ), 0)
    qry_idx = lax.broadcasted_iota(I32, (TK, TQ), 1)

    def q_tile(qi, _):
        q = q_ref[0, 0, pl.ds(pl.multiple_of(qi * TQ, TQ), TQ), :]
        for m_ref, l_ref, acc_ref in chains:
            m_ref[...] = jnp.full(m_ref.shape, NEG_BIG, F32)
            l_ref[...] = jnp.zeros(l_ref.shape, F32)
            acc_ref[...] = jnp.zeros(acc_ref.shape, F32)

        def scores(kt):
            k = k_ref[0, 0, pl.ds(pl.multiple_of(kt * TK, TK), TK), :]
            return lax.dot_general(k, q, (((1,), (1,)), ((), ())), preferred_element_type=F32)

        def fold(chain, st, kt, mask):
            m_ref, l_ref, acc_ref = chain
            if mask is not None:
                st = jnp.where(mask, st, NEG_BIG)
            m = m_ref[...]
            m_new = jnp.maximum(m, jnp.max(st, axis=0, keepdims=True))
            alpha = jnp.exp2(m - m_new)
            p = jnp.exp2(st - m_new)
            l_ref[...] = alpha * l_ref[...] + jnp.sum(p, axis=0, keepdims=True)
            m_ref[...] = m_new
            acc_ref[...] = alpha * acc_ref[...] + jnp.dot(
                vt_ref[0, 0, kt], p.astype(BF16), preferred_element_type=F32)

        def pair_scores(j, buf):
            for c in range(per_q):
                buf[c][...] = scores(per_q * j + c)

        def fold_pair(j, buf, masks=(None, None)):
            for c in range(per_q):
                fold(chains[c], buf[c][...], per_q * j + c, masks[c])

        diag_masks = (key_idx <= qry_idx, key_idx + TK <= qry_idx)
        pair_scores(0, score_bufs[0])

        def two_pairs(i, _):
            pair_scores(2 * i + 1, score_bufs[1])
            fold_pair(2 * i, score_bufs[0])
            pair_scores(2 * i + 2, score_bufs[0])
            fold_pair(2 * i + 1, score_bufs[1])
            return 0
        lax.fori_loop(0, qi // 2, two_pairs, 0)

        @pl.when(qi % 2 == 0)
        def _():
            fold_pair(qi, score_bufs[0], diag_masks)

        @pl.when(qi % 2 == 1)
        def _():
            pair_scores(qi, score_bufs[1])
            fold_pair(qi - 1, score_bufs[0])
            fold_pair(qi, score_bufs[1], diag_masks)

        (m0, l0, acc0), (m1, l1, acc1) = chains
        m = jnp.maximum(m0[...], m1[...])
        w0 = jnp.exp2(m0[...] - m)
        w1 = jnp.exp2(m1[...] - m)
        l = w0 * l0[...] + w1 * l1[...]
        acc = w0 * acc0[...] + w1 * acc1[...]
        o_ref[0, pl.ds(pl.multiple_of(qi * TQ, TQ), TQ), :] = (acc / l).T.astype(BF16)
        return 0

    lax.fori_loop(0, nq, q_tile, 0)


def _attn_call(q, k, vt):
    B, H, S, _ = q.shape
    head = lambda w: pl.BlockSpec((1, 1, S, w), lambda b, h: (b, h, 0, 0))
    return pl.pallas_call(
        _attn_kernel,
        grid=(B, H),
        in_specs=[head(HEAD_W), head(HEAD_W),
                  pl.BlockSpec((1, 1) + vt.shape[2:], lambda b, h: (b, h, 0, 0, 0))],
        out_specs=pl.BlockSpec((1, S, V_DIM), lambda b, h: (b, 0, h)),
        out_shape=jax.ShapeDtypeStruct((B, S, H * V_DIM), BF16),
        scratch_shapes=[pltpu.VMEM((1, TQ), F32), pltpu.VMEM((1, TQ), F32),
                        pltpu.VMEM((V_DIM, TQ), F32)] * 2 + [pltpu.VMEM((TK, TQ), F32)] * 4,
        compiler_params=pltpu.CompilerParams(
            dimension_semantics=("arbitrary", "arbitrary"), vmem_limit_bytes=VMEM_LIMIT),
        name="attn",
    )(q, k, vt)


def _post_kernel(x_ref, yp_ref, ya_ref, wo_ref, g2_ref, wr_ref, br_ref,
                 h1_ref, xn_ref, lg_ref):
    h1 = (x_ref[...]
          + jnp.dot(yp_ref[...], wo_ref[:D_POOL, :], preferred_element_type=F32)
          + jnp.dot(ya_ref[...], wo_ref[D_POOL:, :], preferred_element_type=F32))
    h1_ref[...] = h1
    xn = _rms(h1, g2_ref[...])
    xn_ref[...] = xn
    xn_hi = xn.astype(BF16)
    xn_lo = (xn - xn_hi.astype(F32)).astype(BF16)
    both = (jnp.dot(xn_hi, wr_ref[...], preferred_element_type=F32)
            + jnp.dot(xn_lo, wr_ref[...], preferred_element_type=F32))
    lg_ref[...] = both[:, :LANES] + both[:, LANES:] + br_ref[...]


def _post_call(x, yp, ya, wo, g2, wr, br):
    T, D = x.shape
    tm = TM_POST
    row = lambda w: pl.BlockSpec((tm, w), lambda i: (i, 0))
    return pl.pallas_call(
        _post_kernel,
        grid=(T // tm,),
        in_specs=[row(D), row(D_POOL), row(D_ATTN), _const_spec(wo.shape),
                  _const_spec(g2.shape), _const_spec(wr.shape), _const_spec(br.shape)],
        out_specs=[row(D), row(D), row(LANES)],
        out_shape=[jax.ShapeDtypeStruct((T, D), F32), jax.ShapeDtypeStruct((T, D), F32),
                   jax.ShapeDtypeStruct((T, LANES), F32)],
        compiler_params=pltpu.CompilerParams(
            dimension_semantics=("arbitrary",), vmem_limit_bytes=VMEM_LIMIT),
        name="post",
    )(x, yp, ya, wo, g2, wr, br)


def _route_kernel(lg_ref, code_ref, wts_ref, cnt_ref, carry_ref):
    i = pl.program_id(0)
    tm = lg_ref.shape[0]

    @pl.when(i == 0)
    def _():
        carry_ref[...] = jnp.zeros_like(carry_ref)

    lg = lg_ref[...]
    lane = lax.broadcasted_iota(I32, (tm, LANES), 1)
    lane_f = lane.astype(F32)
    neg_inf = -jnp.inf

    def first_argmax(vals):
        mx = jnp.max(vals, axis=-1, keepdims=True)
        idx = jnp.min(jnp.where(vals == mx, lane_f, float(LANES)), axis=-1, keepdims=True)
        return mx, idx.astype(I32)

    is_group = lane < N_GROUPS
    g_logits = jnp.where(is_group, lg, neg_inf)
    g_max, g_sel = first_argmax(g_logits)
    g_w = 1.0 / jnp.sum(jnp.where(is_group, jnp.exp(lg - g_max), 0.0), axis=-1, keepdims=True)

    lo = N_GROUPS + g_sel * EXPERTS_PER_GROUP
    in_group = (lane >= lo) & (lane < lo + EXPERTS_PER_GROUP)
    e_logits = jnp.where(in_group, lg, neg_inf)
    v1, i1 = first_argmax(e_logits)
    v2, i2 = first_argmax(jnp.where(lane == i1, neg_inf, e_logits))
    e1 = i1 - N_GROUPS
    e2 = i2 - N_GROUPS
    t = jnp.exp(v2 - v1)
    w1 = g_w / (1.0 + t)
    w2 = g_w * t / (1.0 + t)
    wts_ref[...] = jnp.where(lane == 0, w1, jnp.where(lane == 1, w2, 0.0))

    hit1 = lane == e1
    hit2 = lane == e2
    onehot = jnp.where(hit1 | hit2, 1.0, 0.0).astype(BF16)
    r = lax.broadcasted_iota(I32, (tm, tm), 0)
    c = lax.broadcasted_iota(I32, (tm, tm), 1)
    lower = jnp.where(c < r, 1.0, 0.0).astype(BF16)
    before = jnp.dot(lower, onehot, preferred_element_type=F32) + carry_ref[0:1, :]
    carry_new = carry_ref[0:1, :] + jnp.sum(onehot.astype(F32), axis=0, keepdims=True)
    carry_ref[...] = jnp.broadcast_to(carry_new, carry_ref.shape)
    pos1 = jnp.sum(jnp.where(hit1, before, 0.0), axis=-1, keepdims=True).astype(I32)
    pos2 = jnp.sum(jnp.where(hit2, before, 0.0), axis=-1, keepdims=True).astype(I32)
    code1 = e1 * 65536 + pos1
    code2 = e2 * 65536 + pos2
    code_ref[...] = jnp.where(lane == 0, code1, jnp.where(lane == 1, code2, 0))
    cnt_ref[...] = carry_ref[...].astype(I32)


def _route_call(logits):
    T = logits.shape[0]
    tm = TM_ROUTE
    row = pl.BlockSpec((tm, LANES), lambda i: (i, 0))
    return pl.pallas_call(
        _route_kernel,
        grid=(T // tm,),
        in_specs=[row],
        out_specs=[row, row, pl.BlockSpec((8, LANES), lambda i: (0, 0))],
        out_shape=[jax.ShapeDtypeStruct((T, LANES), I32), jax.ShapeDtypeStruct((T, LANES), F32),
                   jax.ShapeDtypeStruct((8, LANES), I32)],
        scratch_shapes=[pltpu.VMEM((8, LANES), F32)],
        compiler_params=pltpu.CompilerParams(
            dimension_semantics=("arbitrary",), vmem_limit_bytes=VMEM_LIMIT),
        name="route",
    )(logits)


def _plan_kernel(code_ref, cnt_ref, unused_hbm, slots_hbm, blke_ref, blkn_ref,
                 slot_ref, start_ref, sem):
    n_assign = code_ref.shape[0]
    n_blocks = blke_ref.shape[0]
    load = pltpu.make_async_copy(unused_hbm, slot_ref, sem)
    load.start()

    def per_expert(e, nb_done):
        cnt = cnt_ref[e]
        nb = lax.shift_right_logical(cnt + (MOE_BLK - 1), MOE_BLK.bit_length() - 1)
        start_ref[e] = nb_done * MOE_BLK

        def per_block(j, _):
            blke_ref[nb_done + j] = e
            blkn_ref[nb_done + j] = jnp.minimum(cnt - j * MOE_BLK, MOE_BLK)
            return 0
        lax.fori_loop(0, nb, per_block, 0)
        return nb_done + nb
    used = lax.fori_loop(0, N_EXPERTS, per_expert, 0)

    last_e = blke_ref[jnp.maximum(used - 1, 0)]

    def tail(b, _):
        blke_ref[b] = last_e
        blkn_ref[b] = 0
        return 0
    lax.fori_loop(used, n_blocks, tail, 0)

    load.wait()

    def place(a, _):
        code = code_ref[a]
        e = lax.shift_right_logical(code, 16)
        slot_ref[start_ref[e] + (code & 0xFFFF)] = a
        return 0
    lax.fori_loop(0, n_assign, place, 0, unroll=PLAN_UNROLL)

    store = pltpu.make_async_copy(slot_ref, slots_hbm, sem)
    store.start()
    store.wait()


def _plan_call(code_flat, counts, n_tab):
    n_assign = code_flat.shape[0]
    n_slots = n_tab * MOE_BLK
    unused = n_assign + (jnp.arange(n_slots, dtype=I32) & (MOE_BLK - 1))
    smem = pl.BlockSpec(memory_space=pltpu.SMEM)
    hbm = pl.BlockSpec(memory_space=pl.ANY)
    return pl.pallas_call(
        _plan_kernel,
        in_specs=[smem, smem, hbm],
        out_specs=[hbm, smem, smem],
        out_shape=[jax.ShapeDtypeStruct((n_slots,), I32),
                   jax.ShapeDtypeStruct((n_tab,), I32),
                   jax.ShapeDtypeStruct((n_tab,), I32)],
        scratch_shapes=[pltpu.SMEM((n_slots,), I32), pltpu.SMEM((N_EXPERTS,), I32),
                        pltpu.SemaphoreType.DMA],
        name="plan",
    )(code_flat, counts, unused)


def _expert_kernel(blke_ref, blkn_ref, slot_g_ref, slot_s_ref, xn_hbm, wg_ref, wu_ref, wd_ref,
                   y_hbm, x0, x1, y0, y1, gsem, ssem):
    xbufs, ybufs = (x0, x1), (y0, y1)
    s = pl.program_id(0)
    n_tok = xn_hbm.shape[0]

    def active(j):
        return (j >= 0) & (blkn_ref[jnp.maximum(j, 0)] > 0)
    act1, act2, act3 = active(s - 1), active(s - 2), active(s - 3)

    def stage(nxt):
        cur = 1 - nxt
        x_nxt, x_cur, y_nxt, y_cur = xbufs[nxt], xbufs[cur], ybufs[nxt], ybufs[cur]

        def gather_issue():
            for r in range(MOE_BLK):
                tok = slot_g_ref[0, 0, r] & (n_tok - 1)
                pltpu.make_async_copy(xn_hbm.at[pl.ds(tok, 1)], x_nxt.at[pl.ds(r, 1)],
                                      gsem.at[nxt]).start()

        def scatter_issue():
            for r in range(MOE_BLK):
                pltpu.make_async_copy(y_nxt.at[pl.ds(r, 1)],
                                      y_hbm.at[pl.ds(slot_s_ref[0, 0, r], 1)],
                                      ssem.at[nxt]).start()

        def compute():
            xb = x_cur[...].astype(BF16)
            a = jnp.dot(xb, wg_ref[0].astype(BF16), preferred_element_type=F32)
            u = jnp.dot(xb, wu_ref[0].astype(BF16), preferred_element_type=F32)
            hmid = (a * jax.nn.sigmoid(a) * u).astype(BF16)
            y_cur[...] = jnp.dot(hmid, wd_ref[0].astype(BF16), preferred_element_type=F32)

        @pl.when((s == 1) | (s == 2) | ((s >= 3) & act2))
        def _():
            pltpu.make_async_copy(xn_hbm.at[pl.ds(0, MOE_BLK)], x_cur, gsem.at[cur]).wait()

        @pl.when((s >= 3) & act3)
        def _():
            pltpu.make_async_copy(y_cur, y_hbm.at[pl.ds(0, MOE_BLK)], ssem.at[cur]).wait()

        if nxt == 0:
            @pl.when(s == 0)
            def _():
                y_nxt[...] = jnp.zeros(y_nxt.shape, F32)
                dump = pltpu.make_async_copy(
                    y_nxt, y_hbm.at[pl.ds(TOP_K * n_tok, MOE_BLK)], ssem.at[nxt])
                dump.start()
                dump.wait()
                gather_issue()
        else:
            @pl.when(s == 1)
            def _():
                gather_issue()
                compute()

        @pl.when((s >= 2) & act1)
        def _():
            gather_issue()
            scatter_issue()
            compute()

        @pl.when((s >= 2) & act2 & jnp.logical_not(act1))
        def _():
            scatter_issue()

    for parity in range(2):
        pl.when((s & 1) == parity)(functools.partial(stage, parity))


def _expert_call(blk_e, blk_n, slots, xn, wg, wu, wd):
    T, D = xn.shape
    assert T & (T - 1) == 0, "token id is recovered from the assignment id with a mask"
    n_tab = blk_e.shape[0]
    F = wg.shape[-1]
    prev = lambda j: jnp.maximum(j, 0)
    grid_spec = pltpu.PrefetchScalarGridSpec(
        num_scalar_prefetch=2,
        grid=(n_tab,),
        in_specs=[
            pl.BlockSpec((1, 1, MOE_BLK), lambda s, be, bn: (s, 0, 0), memory_space=pltpu.SMEM),
            pl.BlockSpec((1, 1, MOE_BLK), lambda s, be, bn: (prev(s - 2), 0, 0),
                         memory_space=pltpu.SMEM),
            pl.BlockSpec(memory_space=pl.ANY),
            pl.BlockSpec((1, D, F), lambda s, be, bn: (be[prev(s - 1)], 0, 0)),
            pl.BlockSpec((1, D, F), lambda s, be, bn: (be[prev(s - 1)], 0, 0)),
            pl.BlockSpec((1, F, D), lambda s, be, bn: (be[prev(s - 1)], 0, 0)),
        ],
        out_specs=pl.BlockSpec(memory_space=pl.ANY),
        scratch_shapes=[pltpu.VMEM((MOE_BLK, D), F32)] * 4
        + [pltpu.SemaphoreType.DMA((2,)), pltpu.SemaphoreType.DMA((2,))],
    )
    slots3 = slots.reshape(n_tab, 1, MOE_BLK)
    return pl.pallas_call(
        _expert_kernel,
        grid_spec=grid_spec,
        out_shape=jax.ShapeDtypeStruct((T * TOP_K + MOE_BLK, D), F32),
        compiler_params=pltpu.CompilerParams(
            dimension_semantics=("arbitrary",), vmem_limit_bytes=VMEM_LIMIT,
            disable_bounds_checks=True),
        name="experts",
    )(blk_e, blk_n, slots3, slots3, xn, wg, wu, wd)


def _final_kernel(h1_ref, y1_ref, y2_ref, wts_ref, p_ref, wpp_ref, pg_ref, g3_ref, wpg_ref,
                  o_ref):
    w = wts_ref[...]
    h2 = h1_ref[...] + w[:, 0:1] * y1_ref[...] + w[:, 1:2] * y2_ref[...]
    e = _rms(jnp.dot(p_ref[...].astype(BF16), wpp_ref[...], preferred_element_type=F32),
             pg_ref[...])
    gate = jax.nn.sigmoid(jnp.dot(_rms(h2, g3_ref[...]).astype(BF16), wpg_ref[...],
                                  preferred_element_type=F32))
    o_ref[...] = h2 + gate * e


def _final_call(h1, y, wts, p, wpp, pg, g3, wpg):
    T, D = h1.shape
    tm = TM_FINAL
    row = lambda w: pl.BlockSpec((tm, w), lambda i: (i, 0))
    second = pl.BlockSpec((tm, D), lambda i: (i + T // tm, 0))
    return pl.pallas_call(
        _final_kernel,
        grid=(T // tm,),
        in_specs=[row(D), row(D), second, row(LANES), row(PLE_DIM), _const_spec(wpp.shape),
                  _const_spec(pg.shape), _const_spec(g3.shape), _const_spec(wpg.shape)],
        out_specs=row(D),
        out_shape=jax.ShapeDtypeStruct((T, D), F32),
        compiler_params=pltpu.CompilerParams(
            dimension_semantics=("arbitrary",), vmem_limit_bytes=VMEM_LIMIT),
        name="final",
    )(h1, y, y, wts, p, wpp, pg, g3, wpg)


def _rope_lanes(a):
    z = jnp.zeros(a.shape[:-1] + (ROPE_HALF,), a.dtype)
    return jnp.concatenate([a[..., :ROPE_HALF], z, a[..., ROPE_HALF:], z], axis=-1)


def _head_lanes(a):
    return jnp.concatenate([a[..., :QK_NOPE], _rope_lanes(a[..., QK_NOPE:])], axis=-1)


def kernel(x, p, positions, norm1_gain, w_in, q_a_gain, w_q_b, kv_a_gain, w_kv_b, q_norm_gain, k_norm_gain, w_pool, pool_scale, w_out, norm2_gain, w_router_group, b_router_group, w_router_expert, b_router_expert, w_exp_gate, w_exp_up, w_exp_down, norm3_gain, w_ple_gate, w_ple_proj, ple_norm_gain):
    B, S, D = x.shape
    T = B * S
    assert x.shape[2] == D_MODEL and S % TQ == 0 and S % TM_PRE == 0
    assert T % TM_POST == 0 and T % TM_FINAL == 0 and T % TM_ROUTE == 0
    layer = 0
    row = lambda a: a[layer].reshape(1, -1)

    n_lat = D_POOL + Q_LORA + KV_LORA
    win = jnp.concatenate([w_in[layer][:, :n_lat], _rope_lanes(w_in[layer][:, n_lat:])],
                          axis=1).astype(BF16)
    wq = _head_lanes(w_q_b[layer].reshape(Q_LORA, N_HEADS, QK_DIM)
                     ).reshape(Q_LORA, N_HEADS * HEAD_W).astype(BF16)
    wkv3 = w_kv_b[layer].reshape(KV_LORA, N_HEADS, QK_NOPE + V_DIM)
    wkv = jnp.concatenate([wkv3[..., :QK_NOPE].reshape(KV_LORA, -1),
                           wkv3[..., QK_NOPE:].reshape(KV_LORA, -1)], axis=1).astype(BF16)
    qng = _head_lanes(q_norm_gain[layer]).reshape(1, HEAD_W)
    kng = _head_lanes(k_norm_gain[layer]).reshape(1, HEAD_W)
    inv_freq = ROPE_THETA ** (-jnp.arange(ROPE_HALF, dtype=F32) / ROPE_HALF)
    invf = _rope_lanes(jnp.concatenate([inv_freq, inv_freq])).reshape(1, LANES)
    sgn = _rope_lanes(jnp.concatenate([-jnp.ones((ROPE_HALF,), F32),
                                       jnp.ones((ROPE_HALF,), F32)])).reshape(1, LANES)
    wr = jnp.zeros((D, LANES), F32)
    wr = wr.at[:, :N_GROUPS].set(w_router_group[layer])
    wr = wr.at[:, N_GROUPS:N_GROUPS + N_EXPERTS].set(w_router_expert[layer])
    wr_hi = wr.astype(BF16)
    wr = jnp.concatenate([wr_hi, (wr - wr_hi.astype(F32)).astype(BF16)], axis=1)
    br = jnp.zeros((1, LANES), F32)
    br = br.at[0, :N_GROUPS].set(b_router_group[layer])
    br = br.at[0, N_GROUPS:N_GROUPS + N_EXPERTS].set(b_router_expert[layer])

    ypool, q, k, vt = _pre_call(
        x, positions.reshape(B, S, 1), row(norm1_gain), win, row(q_a_gain), wq, row(kv_a_gain),
        wkv, qng, kng, w_pool[layer].astype(BF16), row(pool_scale), invf, sgn)
    yattn = _attn_call(q, k, vt)
    h1, xn2, logits = _post_call(
        x.reshape(T, D), ypool.reshape(T, D_POOL), yattn.reshape(T, D_ATTN),
        w_out[layer].astype(BF16), row(norm2_gain), wr, br)

    n_assign = T * TOP_K
    n_blocks = (n_assign + N_EXPERTS * (MOE_BLK - 1)) // MOE_BLK
    code, wts, counts = _route_call(logits)
    slots, blk_e, blk_n = _plan_call(code[:, :TOP_K].T.reshape(n_assign), counts[0],
                                     n_blocks + MOE_DRAIN_STEPS)
    y = _expert_call(blk_e, blk_n, slots, xn2,
                     w_exp_gate[layer], w_exp_up[layer], w_exp_down[layer])

    out = _final_call(h1, y, wts, p[layer].reshape(T, PLE_DIM),
                      w_ple_proj[layer].astype(BF16), row(ple_norm_gain), row(norm3_gain),
                      w_ple_gate[layer].astype(BF16))
    return out.reshape(B, S, D)
```

```python
import functools
import math

import jax
import jax.numpy as jnp
from jax import lax
from jax.experimental import pallas as pl
from jax.experimental.pallas import tpu as pltpu

F32 = jnp.float32
BF16 = jnp.bfloat16
I32 = jnp.int32

D_MODEL = 2048
PLE_DIM = 256
EPS = 1e-6
POOL_WINDOWS = (2, 4, 8, 16)
POOL_CH = 256
D_POOL = POOL_CH * len(POOL_WINDOWS)
N_HEADS = 8
Q_LORA = 512
KV_LORA = 512
QK_NOPE = 128
QK_ROPE = 64
QK_DIM = QK_NOPE + QK_ROPE
V_DIM = 128
D_ATTN = N_HEADS * V_DIM
ROPE_THETA = 10000.0
ATTN_SCALE = 1.0 / math.sqrt(QK_DIM)
Q_PRESCALE = ATTN_SCALE * math.log2(math.e)
N_GROUPS = 8
EXPERTS_PER_GROUP = 8
N_EXPERTS = N_GROUPS * EXPERTS_PER_GROUP
TOP_K = 2
D_EXPERT = 512

LANES = 128
MXU_DIM = 256
VMEM_LIMIT = 56 * 1024 * 1024

HEAD_W = 2 * LANES
ROPE_HALF = QK_ROPE // 2
POOL_HALO = 16
TM_PRE = 512
PRE_SUB = 256
TM_POST = 512
POST_SUB = 256
TM_FINAL = 256
TQ = 512
TK = PRE_SUB
TM_ROUTE = 256
MOE_BLK = 256
MOE_DRAIN_STEPS = 3
ROW_GROUP = 32
PLAN_UNROLL = 16
NEG_BIG = -1e30


def _const_spec(shape):
    nd = len(shape)
    return pl.BlockSpec(shape, lambda *_: (0,) * nd, pipeline_mode=pl.Buffered(1))


def _rms(x, gain):
    return x * lax.rsqrt(jnp.mean(x * x, axis=-1, keepdims=True) + EPS) * gain


def _pre_kernel(x_ref, pos_ref, g1_ref, win_ref, qag_ref, wq_ref, kvag_ref, wkv_ref,
                qng_ref, kng_ref, wpool_ref, pscale_ref, invf_ref, sgn_ref,
                ypool_ref, q_ref, k_ref, vt_ref, carry_ref):
    st = pl.program_id(1)

    @pl.when(st == 0)
    def _():
        carry_ref[...] = jnp.zeros_like(carry_ref)

    halo = carry_ref[...]
    for i in range(x_ref.shape[0] // PRE_SUB):
        halo = _pre_rows(i, st * x_ref.shape[0] + i * PRE_SUB, halo,
                         x_ref, pos_ref, g1_ref, win_ref, qag_ref, wq_ref, kvag_ref, wkv_ref,
                         qng_ref, kng_ref, wpool_ref, pscale_ref, invf_ref, sgn_ref,
                         ypool_ref, q_ref, k_ref, vt_ref)
    carry_ref[...] = halo


def _pre_rows(i, first_pos, halo, x_ref, pos_ref, g1_ref, win_ref, qag_ref, wq_ref, kvag_ref,
              wkv_ref, qng_ref, kng_ref, wpool_ref, pscale_ref, invf_ref, sgn_ref,
              ypool_ref, q_ref, k_ref, vt_ref):
    tm = PRE_SUB
    rs = pl.ds(i * tm, tm)

    hn = _rms(x_ref[rs, :], g1_ref[...])
    z = jnp.dot(hn.astype(BF16), win_ref[...], preferred_element_type=F32)

    u = z[:, :D_POOL]
    ext = jnp.concatenate([halo, u], axis=0)
    row = lax.broadcasted_iota(I32, (tm, 1), 0) + first_pos
    level = ext
    shift = 1
    for g, w in enumerate(POOL_WINDOWS):
        sl = slice(g * POOL_CH, (g + 1) * POOL_CH)
        while shift < w:
            level = level + pltpu.roll(level, shift, 0)
            shift *= 2
        win_sum = level[POOL_HALO:, sl]
        cnt = jnp.minimum(row + 1, w).astype(F32)
        d = win_sum / cnt - u[:, sl]
        y = jnp.dot(d.astype(BF16), wpool_ref[g], preferred_element_type=F32)
        ypool_ref[rs, sl] = (y * pscale_ref[:, sl]).astype(BF16)

    q_lat = z[:, D_POOL:D_POOL + Q_LORA]
    kv_lat = z[:, D_POOL + Q_LORA:D_POOL + Q_LORA + KV_LORA]
    k_rope = z[:, D_POOL + Q_LORA + KV_LORA:]
    qa = jnp.dot(_rms(q_lat, qag_ref[...]).astype(BF16), wq_ref[...],
                 preferred_element_type=F32)
    kv = jnp.dot(_rms(kv_lat, kvag_ref[...]).astype(BF16), wkv_ref[...],
                 preferred_element_type=F32)

    ang = pos_ref[rs, :].astype(F32) * invf_ref[...]
    cos = jnp.cos(ang)
    sin = jnp.sin(ang) * sgn_ref[...]

    def rot(t):
        return t * cos + pltpu.roll(t, LANES // 2, 1) * sin

    qng = qng_ref[...]
    kng = kng_ref[...]
    kr_rot = rot(k_rope * kng[:, LANES:])
    kr_ssq = jnp.sum(k_rope * k_rope, axis=-1, keepdims=True)
    for h in range(N_HEADS):
        qh = qa[:, h * HEAD_W:(h + 1) * HEAD_W]
        rq = lax.rsqrt(jnp.sum(qh * qh, axis=-1, keepdims=True) / QK_DIM + EPS) * Q_PRESCALE
        qn = qh * rq * qng
        q_ref[0, h, rs, :LANES] = qn[:, :LANES].astype(BF16)
        q_ref[0, h, rs, LANES:] = rot(qn[:, LANES:]).astype(BF16)
        kh = kv[:, h * QK_NOPE:(h + 1) * QK_NOPE]
        rk = lax.rsqrt((jnp.sum(kh * kh, axis=-1, keepdims=True) + kr_ssq) / QK_DIM + EPS)
        k_ref[0, h, rs, :LANES] = (kh * rk * kng[:, :LANES]).astype(BF16)
        k_ref[0, h, rs, LANES:] = (kr_rot * rk).astype(BF16)
        vh = kv[:, N_HEADS * QK_NOPE + h * V_DIM:N_HEADS * QK_NOPE + (h + 1) * V_DIM]
        vt_ref[0, h, i, :, :] = vh.T.astype(BF16)
    return u[tm - POOL_HALO:, :]


def _pre_call(x, pos, g1, win, qag, wq, kvag, wkv, qng, kng, wpool, pscale, invf, sgn):
    B, S, D = x.shape
    tm = TM_PRE
    grid = (B, S // tm)
    row_spec = lambda w: pl.BlockSpec((None, tm, w), lambda b, s: (b, s, 0))
    head_spec = lambda w: pl.BlockSpec((1, N_HEADS, tm, w), lambda b, s: (b, 0, s, 0))
    consts = [g1, win, qag, wq, kvag, wkv, qng, kng, wpool, pscale, invf, sgn]
    return pl.pallas_call(
        _pre_kernel,
        grid=grid,
        in_specs=[row_spec(D), row_spec(1)] + [_const_spec(c.shape) for c in consts],
        out_specs=[row_spec(D_POOL), head_spec(HEAD_W), head_spec(HEAD_W),
                   pl.BlockSpec((1, N_HEADS, tm // PRE_SUB, V_DIM, PRE_SUB),
                                lambda b, s: (b, 0, s, 0, 0))],
        out_shape=[jax.ShapeDtypeStruct((B, S, D_POOL), BF16),
                   jax.ShapeDtypeStruct((B, N_HEADS, S, HEAD_W), BF16),
                   jax.ShapeDtypeStruct((B, N_HEADS, S, HEAD_W), BF16),
                   jax.ShapeDtypeStruct((B, N_HEADS, S // PRE_SUB, V_DIM, PRE_SUB), BF16)],
        scratch_shapes=[pltpu.VMEM((POOL_HALO, D_POOL), F32)],
        compiler_params=pltpu.CompilerParams(
            dimension_semantics=("arbitrary", "arbitrary"), vmem_limit_bytes=VMEM_LIMIT),
        name="pre",
    )(x, pos, *consts)


def _attn_kernel(q_ref, k_ref, vt_ref, o_ref, *chain_scratch):
    S = q_ref.shape[2]
    nq = S // TQ
    per_q = TQ // TK
    assert per_q == 2, "two online-softmax chains take the even / odd key tiles"
    chains = (chain_scratch[0:3], chain_scratch[3:6])
    score_bufs = (chain_scratch[6:8], chain_scratch[8:10])
    key_idx = lax.broadcasted_iota(I32, (TK, TQ), 0)
    qry_idx = lax.broadcasted_iota(I32, (TK, TQ), 1)

    def q_tile(qi, _):
        q = q_ref[0, 0, pl.ds(pl.multiple_of(qi * TQ, TQ), TQ), :]
        for m_ref, l_ref, acc_ref in chains:
            m_ref[...] = jnp.full(m_ref.shape, NEG_BIG, F32)
            l_ref[...] = jnp.zeros(l_ref.shape, F32)
            acc_ref[...] = jnp.zeros(acc_ref.shape, F32)

        def scores(kt):
            k = k_ref[0, 0, pl.ds(pl.multiple_of(kt * TK, TK), TK), :]
            return lax.dot_general(k, q, (((1,), (1,)), ((), ())), preferred_element_type=F32)

        def fold(chain, st, kt, mask):
            m_ref, l_ref, acc_ref = chain
            if mask is not None:
                st = jnp.where(mask, st, NEG_BIG)
            m = m_ref[...]
            m_new = jnp.maximum(m, jnp.max(st, axis=0, keepdims=True))
            alpha = jnp.exp2(m - m_new)
            p = jnp.exp2(st - m_new)
            l_ref[...] = alpha * l_ref[...] + jnp.sum(p, axis=0, keepdims=True)
            m_ref[...] = m_new
            acc_ref[...] = alpha * acc_ref[...] + jnp.dot(
                vt_ref[0, 0, kt], p.astype(BF16), preferred_element_type=F32)

        def pair_scores(j, buf):
            for c in range(per_q):
                buf[c][...] = scores(per_q * j + c)

        def fold_pair(j, buf, masks=(None, None)):
            for c in range(per_q):
                fold(chains[c], buf[c][...], per_q * j + c, masks[c])

        diag_masks = (key_idx <= qry_idx, key_idx + TK <= qry_idx)
        pair_scores(0, score_bufs[0])

        def two_pairs(i, _):
            pair_scores(2 * i + 1, score_bufs[1])
            fold_pair(2 * i, score_bufs[0])
            pair_scores(2 * i + 2, score_bufs[0])
            fold_pair(2 * i + 1, score_bufs[1])
            return 0
        lax.fori_loop(0, qi // 2, two_pairs, 0)

        @pl.when(qi % 2 == 0)
        def _():
            fold_pair(qi, score_bufs[0], diag_masks)

        @pl.when(qi % 2 == 1)
        def _():
            pair_scores(qi, score_bufs[1])
            fold_pair(qi - 1, score_bufs[0])
            fold_pair(qi, score_bufs[1], diag_masks)

        (m0, l0, acc0), (m1, l1, acc1) = chains
        m = jnp.maximum(m0[...], m1[...])
        w0 = jnp.exp2(m0[...] - m)
        w1 = jnp.exp2(m1[...] - m)
        l = w0 * l0[...] + w1 * l1[...]
        acc = w0 * acc0[...] + w1 * acc1[...]
        o_ref[0, pl.ds(pl.multiple_of(qi * TQ, TQ), TQ), :] = (acc / l).T.astype(BF16)
        return 0

    lax.fori_loop(0, nq, q_tile, 0)


def _attn_call(q, k, vt):
    B, H, S, _ = q.shape
    head = lambda w: pl.BlockSpec((1, 1, S, w), lambda b, h: (b, h, 0, 0))
    return pl.pallas_call(
        _attn_kernel,
        grid=(B, H),
        in_specs=[head(HEAD_W), head(HEAD_W),
                  pl.BlockSpec((1, 1) + vt.shape[2:], lambda b, h: (b, h, 0, 0, 0))],
        out_specs=pl.BlockSpec((1, S, V_DIM), lambda b, h: (b, 0, h)),
        out_shape=jax.ShapeDtypeStruct((B, S, H * V_DIM), BF16),
        scratch_shapes=[pltpu.VMEM((1, TQ), F32), pltpu.VMEM((1, TQ), F32),
                        pltpu.VMEM((V_DIM, TQ), F32)] * 2 + [pltpu.VMEM((TK, TQ), F32)] * 4,
        compiler_params=pltpu.CompilerParams(
            dimension_semantics=("arbitrary", "arbitrary"), vmem_limit_bytes=VMEM_LIMIT),
        name="attn",
    )(q, k, vt)


def _post_kernel(x_ref, yp_ref, ya_ref, wo_ref, g2_ref, wr_ref, br_ref,
                 h1_ref, xn_ref, lg_ref):
    for i in range(x_ref.shape[0] // POST_SUB):
        rs = pl.ds(i * POST_SUB, POST_SUB)
        h1 = (x_ref[rs, :]
              + jnp.dot(yp_ref[rs, :], wo_ref[:D_POOL, :], preferred_element_type=F32)
              + jnp.dot(ya_ref[rs, :], wo_ref[D_POOL:, :], preferred_element_type=F32))
        h1_ref[rs, :] = h1
        xn = _rms(h1, g2_ref[...])
        xn_ref[rs, :] = xn
        xn_hi = xn.astype(BF16)
        xn_lo = (xn - xn_hi.astype(F32)).astype(BF16)
        both = (jnp.dot(xn_hi, wr_ref[...], preferred_element_type=F32)
                + jnp.dot(xn_lo, wr_ref[...], preferred_element_type=F32))
        lg_ref[rs, :] = both[:, :LANES] + both[:, LANES:] + br_ref[...]


def _post_call(x, yp, ya, wo, g2, wr, br):
    T, D = x.shape
    tm = TM_POST
    row = lambda w: pl.BlockSpec((tm, w), lambda i: (i, 0))
    return pl.pallas_call(
        _post_kernel,
        grid=(T // tm,),
        in_specs=[row(D), row(D_POOL), row(D_ATTN), _const_spec(wo.shape),
                  _const_spec(g2.shape), _const_spec(wr.shape), _const_spec(br.shape)],
        out_specs=[row(D), row(D), row(LANES)],
        out_shape=[jax.ShapeDtypeStruct((T, D), F32), jax.ShapeDtypeStruct((T, D), F32),
                   jax.ShapeDtypeStruct((T, LANES), F32)],
        compiler_params=pltpu.CompilerParams(
            dimension_semantics=("arbitrary",), vmem_limit_bytes=VMEM_LIMIT),
        name="post",
    )(x, yp, ya, wo, g2, wr, br)


def _route_kernel(lg_ref, code_ref, wts_ref, cnt_ref, carry_ref):
    i = pl.program_id(0)
    tm = lg_ref.shape[0]

    @pl.when(i == 0)
    def _():
        carry_ref[...] = jnp.zeros_like(carry_ref)

    lg = lg_ref[...]
    lane = lax.broadcasted_iota(I32, (tm, LANES), 1)
    lane_f = lane.astype(F32)
    neg_inf = -jnp.inf

    def first_argmax(vals):
        mx = jnp.max(vals, axis=-1, keepdims=True)
        idx = jnp.min(jnp.where(vals == mx, lane_f, float(LANES)), axis=-1, keepdims=True)
        return mx, idx.astype(I32)

    is_group = lane < N_GROUPS
    g_logits = jnp.where(is_group, lg, neg_inf)
    g_max, g_sel = first_argmax(g_logits)
    g_w = 1.0 / jnp.sum(jnp.where(is_group, jnp.exp(lg - g_max), 0.0), axis=-1, keepdims=True)

    lo = N_GROUPS + g_sel * EXPERTS_PER_GROUP
    in_group = (lane >= lo) & (lane < lo + EXPERTS_PER_GROUP)
    e_logits = jnp.where(in_group, lg, neg_inf)
    v1, i1 = first_argmax(e_logits)
    v2, i2 = first_argmax(jnp.where(lane == i1, neg_inf, e_logits))
    e1 = i1 - N_GROUPS
    e2 = i2 - N_GROUPS
    t = jnp.exp(v2 - v1)
    w1 = g_w / (1.0 + t)
    w2 = g_w * t / (1.0 + t)
    wts_ref[...] = jnp.where(lane == 0, w1, jnp.where(lane == 1, w2, 0.0))

    hit1 = lane == e1
    hit2 = lane == e2
    onehot = jnp.where(hit1 | hit2, 1.0, 0.0).astype(BF16)
    r = lax.broadcasted_iota(I32, (tm, tm), 0)
    c = lax.broadcasted_iota(I32, (tm, tm), 1)
    lower = jnp.where(c < r, 1.0, 0.0).astype(BF16)
    before = jnp.dot(lower, onehot, preferred_element_type=F32) + carry_ref[0:1, :]
    carry_new = carry_ref[0:1, :] + jnp.sum(onehot.astype(F32), axis=0, keepdims=True)
    carry_ref[...] = jnp.broadcast_to(carry_new, carry_ref.shape)
    pos1 = jnp.sum(jnp.where(hit1, before, 0.0), axis=-1, keepdims=True).astype(I32)
    pos2 = jnp.sum(jnp.where(hit2, before, 0.0), axis=-1, keepdims=True).astype(I32)
    code1 = e1 * 65536 + pos1
    code2 = e2 * 65536 + pos2
    code_ref[...] = jnp.where(lane == 0, code1, jnp.where(lane == 1, code2, 0))
    cnt_ref[...] = carry_ref[...].astype(I32)


def _route_call(logits):
    T = logits.shape[0]
    tm = TM_ROUTE
    row = pl.BlockSpec((tm, LANES), lambda i: (i, 0))
    return pl.pallas_call(
        _route_kernel,
        grid=(T // tm,),
        in_specs=[row],
        out_specs=[row, row, pl.BlockSpec((8, LANES), lambda i: (0, 0))],
        out_shape=[jax.ShapeDtypeStruct((T, LANES), I32), jax.ShapeDtypeStruct((T, LANES), F32),
                   jax.ShapeDtypeStruct((8, LANES), I32)],
        scratch_shapes=[pltpu.VMEM((8, LANES), F32)],
        compiler_params=pltpu.CompilerParams(
            dimension_semantics=("arbitrary",), vmem_limit_bytes=VMEM_LIMIT),
        name="route",
    )(logits)


def _plan_kernel(code_ref, cnt_ref, unused_hbm, slots_hbm, blke_ref, blkn_ref,
                 slot_ref, start_ref, sem):
    n_assign = code_ref.shape[0]
    n_blocks = blke_ref.shape[0]
    load = pltpu.make_async_copy(unused_hbm, slot_ref, sem)
    load.start()

    def per_expert(e, nb_done):
        cnt = cnt_ref[e]
        nb = lax.shift_right_logical(cnt + (MOE_BLK - 1), MOE_BLK.bit_length() - 1)
        start_ref[e] = nb_done * MOE_BLK

        def per_block(j, _):
            blke_ref[nb_done + j] = e
            blkn_ref[nb_done + j] = jnp.minimum(cnt - j * MOE_BLK, MOE_BLK)
            return 0
        lax.fori_loop(0, nb, per_block, 0)
        return nb_done + nb
    used = lax.fori_loop(0, N_EXPERTS, per_expert, 0)

    last_e = blke_ref[jnp.maximum(used - 1, 0)]

    def tail(b, _):
        blke_ref[b] = last_e
        blkn_ref[b] = 0
        return 0
    lax.fori_loop(used, n_blocks, tail, 0)

    load.wait()

    def place(a, _):
        code = code_ref[a]
        e = lax.shift_right_logical(code, 16)
        slot_ref[start_ref[e] + (code & 0xFFFF)] = a
        return 0
    lax.fori_loop(0, n_assign, place, 0, unroll=PLAN_UNROLL)

    store = pltpu.make_async_copy(slot_ref, slots_hbm, sem)
    store.start()
    store.wait()


def _plan_call(code_flat, counts, n_tab):
    n_assign = code_flat.shape[0]
    n_slots = n_tab * MOE_BLK
    unused = n_assign + (jnp.arange(n_slots, dtype=I32) & (MOE_BLK - 1))
    smem = pl.BlockSpec(memory_space=pltpu.SMEM)
    hbm = pl.BlockSpec(memory_space=pl.ANY)
    return pl.pallas_call(
        _plan_kernel,
        in_specs=[smem, smem, hbm],
        out_specs=[hbm, smem, smem],
        out_shape=[jax.ShapeDtypeStruct((n_slots,), I32),
                   jax.ShapeDtypeStruct((n_tab,), I32),
                   jax.ShapeDtypeStruct((n_tab,), I32)],
        scratch_shapes=[pltpu.SMEM((n_slots,), I32), pltpu.SMEM((N_EXPERTS,), I32),
                        pltpu.SemaphoreType.DMA],
        name="plan",
    )(code_flat, counts, unused)


def _expert_kernel(blke_ref, blkn_ref, slot_g_ref, slot_s_ref, xn_hbm, wg_ref, wu_ref, wd_ref,
                   y_hbm, x0, x1, y0, y1, gsem, ssem):
    xbufs, ybufs = (x0, x1), (y0, y1)
    s = pl.program_id(0)
    n_tok = xn_hbm.shape[0]

    def rows(j):
        n = blkn_ref[jnp.clip(j, 0, blkn_ref.shape[0] - 1)]
        return jnp.where(j >= 0, (n + (ROW_GROUP - 1)) & ~(ROW_GROUP - 1), 0)
    rows0, rows1, rows2, rows3 = rows(s), rows(s - 1), rows(s - 2), rows(s - 3)

    def stage(nxt):
        cur = 1 - nxt
        x_nxt, x_cur, y_nxt, y_cur = xbufs[nxt], xbufs[cur], ybufs[nxt], ybufs[cur]

        def per_group(n_rows, issue_row):
            for g in range(MOE_BLK // ROW_GROUP):
                @pl.when(g * ROW_GROUP < n_rows)
                def _():
                    for r in range(g * ROW_GROUP, (g + 1) * ROW_GROUP):
                        issue_row(r)

        def gather_row(r):
            tok = slot_g_ref[0, 0, r] & (n_tok - 1)
            pltpu.make_async_copy(xn_hbm.at[pl.ds(tok, 1)], x_nxt.at[pl.ds(r, 1)],
                                  gsem.at[nxt]).start()

        def scatter_row(r):
            pltpu.make_async_copy(y_nxt.at[pl.ds(r, 1)], y_hbm.at[pl.ds(slot_s_ref[0, 0, r], 1)],
                                  ssem.at[nxt]).start()

        def wait_rows(src, dst, sem, n_rows):
            n_rows = pl.multiple_of(n_rows, ROW_GROUP)

            @pl.when(n_rows > 0)
            def _():
                pltpu.make_async_copy(src.at[pl.ds(0, n_rows)], dst.at[pl.ds(0, n_rows)],
                                      sem).wait()

        def compute():
            xb = x_cur[...].astype(BF16)
            a = jnp.dot(xb, wg_ref[0].astype(BF16), preferred_element_type=F32)
            u = jnp.dot(xb, wu_ref[0].astype(BF16), preferred_element_type=F32)
            hmid = (a * jax.nn.sigmoid(a) * u).astype(BF16)
            y_cur[...] = jnp.dot(hmid, wd_ref[0].astype(BF16), preferred_element_type=F32)

        wait_rows(xn_hbm, x_cur, gsem.at[cur], rows1)
        wait_rows(y_cur, y_hbm, ssem.at[cur], rows3)

        if nxt == 0:
            @pl.when(s == 0)
            def _():
                x_nxt[...] = jnp.zeros(x_nxt.shape, F32)
                x_cur[...] = jnp.zeros(x_cur.shape, F32)
                y_nxt[...] = jnp.zeros(y_nxt.shape, F32)
                dump = pltpu.make_async_copy(
                    y_nxt, y_hbm.at[pl.ds(TOP_K * n_tok, MOE_BLK)], ssem.at[nxt])
                dump.start()
                dump.wait()

        per_group(rows0, gather_row)
        per_group(rows2, scatter_row)

        @pl.when(rows1 > 0)
        def _():
            compute()

    for parity in range(2):
        pl.when((s & 1) == parity)(functools.partial(stage, parity))


def _expert_call(blk_e, blk_n, slots, xn, wg, wu, wd):
    T, D = xn.shape
    assert T & (T - 1) == 0, "token id is recovered from the assignment id with a mask"
    n_tab = blk_e.shape[0]
    F = wg.shape[-1]
    prev = lambda j: jnp.maximum(j, 0)
    grid_spec = pltpu.PrefetchScalarGridSpec(
        num_scalar_prefetch=2,
        grid=(n_tab,),
        in_specs=[
            pl.BlockSpec((1, 1, MOE_BLK), lambda s, be, bn: (s, 0, 0), memory_space=pltpu.SMEM),
            pl.BlockSpec((1, 1, MOE_BLK), lambda s, be, bn: (prev(s - 2), 0, 0),
                         memory_space=pltpu.SMEM),
            pl.BlockSpec(memory_space=pl.ANY),
            pl.BlockSpec((1, D, F), lambda s, be, bn: (be[prev(s - 1)], 0, 0)),
            pl.BlockSpec((1, D, F), lambda s, be, bn: (be[prev(s - 1)], 0, 0)),
            pl.BlockSpec((1, F, D), lambda s, be, bn: (be[prev(s - 1)], 0, 0)),
        ],
        out_specs=pl.BlockSpec(memory_space=pl.ANY),
        scratch_shapes=[pltpu.VMEM((MOE_BLK, D), F32)] * 4
        + [pltpu.SemaphoreType.DMA((2,)), pltpu.SemaphoreType.DMA((2,))],
    )
    slots3 = slots.reshape(n_tab, 1, MOE_BLK)
    return pl.pallas_call(
        _expert_kernel,
        grid_spec=grid_spec,
        out_shape=jax.ShapeDtypeStruct((T * TOP_K + MOE_BLK, D), F32),
        compiler_params=pltpu.CompilerParams(
            dimension_semantics=("arbitrary",), vmem_limit_bytes=VMEM_LIMIT,
            disable_bounds_checks=True),
        name="experts",
    )(blk_e, blk_n, slots3, slots3, xn, wg, wu, wd)


def _final_kernel(h1_ref, y1_ref, y2_ref, wts_ref, p_ref, wpp_ref, pg_ref, g3_ref, wpg_ref,
                  o_ref):
    w = wts_ref[...]
    h2 = h1_ref[...] + w[:, 0:1] * y1_ref[...] + w[:, 1:2] * y2_ref[...]
    e = _rms(jnp.dot(p_ref[...].astype(BF16), wpp_ref[...], preferred_element_type=F32),
             pg_ref[...])
    gate = jax.nn.sigmoid(jnp.dot(_rms(h2, g3_ref[...]).astype(BF16), wpg_ref[...],
                                  preferred_element_type=F32))
    o_ref[...] = h2 + gate * e


def _final_call(h1, y, wts, p, wpp, pg, g3, wpg):
    T, D = h1.shape
    tm = TM_FINAL
    row = lambda w: pl.BlockSpec((tm, w), lambda i: (i, 0))
    second = pl.BlockSpec((tm, D), lambda i: (i + T // tm, 0))
    return pl.pallas_call(
        _final_kernel,
        grid=(T // tm,),
        in_specs=[row(D), row(D), second, row(LANES), row(PLE_DIM), _const_spec(wpp.shape),
                  _const_spec(pg.shape), _const_spec(g3.shape), _const_spec(wpg.shape)],
        out_specs=row(D),
        out_shape=jax.ShapeDtypeStruct((T, D), F32),
        compiler_params=pltpu.CompilerParams(
            dimension_semantics=("arbitrary",), vmem_limit_bytes=VMEM_LIMIT),
        name="final",
    )(h1, y, y, wts, p, wpp, pg, g3, wpg)


def _rope_lanes(a):
    z = jnp.zeros(a.shape[:-1] + (ROPE_HALF,), a.dtype)
    return jnp.concatenate([a[..., :ROPE_HALF], z, a[..., ROPE_HALF:], z], axis=-1)


def _head_lanes(a):
    return jnp.concatenate([a[..., :QK_NOPE], _rope_lanes(a[..., QK_NOPE:])], axis=-1)


def kernel(x, p, positions, norm1_gain, w_in, q_a_gain, w_q_b, kv_a_gain, w_kv_b, q_norm_gain, k_norm_gain, w_pool, pool_scale, w_out, norm2_gain, w_router_group, b_router_group, w_router_expert, b_router_expert, w_exp_gate, w_exp_up, w_exp_down, norm3_gain, w_ple_gate, w_ple_proj, ple_norm_gain):
    B, S, D = x.shape
    T = B * S
    assert x.shape[2] == D_MODEL and S % TQ == 0 and S % TM_PRE == 0
    assert T % TM_POST == 0 and T % TM_FINAL == 0 and T % TM_ROUTE == 0
    layer = 0
    row = lambda a: a[layer].reshape(1, -1)

    n_lat = D_POOL + Q_LORA + KV_LORA
    win = jnp.concatenate([w_in[layer][:, :n_lat], _rope_lanes(w_in[layer][:, n_lat:])],
                          axis=1).astype(BF16)
    wq = _head_lanes(w_q_b[layer].reshape(Q_LORA, N_HEADS, QK_DIM)
                     ).reshape(Q_LORA, N_HEADS * HEAD_W).astype(BF16)
    wkv3 = w_kv_b[layer].reshape(KV_LORA, N_HEADS, QK_NOPE + V_DIM)
    wkv = jnp.concatenate([wkv3[..., :QK_NOPE].reshape(KV_LORA, -1),
                           wkv3[..., QK_NOPE:].reshape(KV_LORA, -1)], axis=1).astype(BF16)
    qng = _head_lanes(q_norm_gain[layer]).reshape(1, HEAD_W)
    kng = _head_lanes(k_norm_gain[layer]).reshape(1, HEAD_W)
    inv_freq = ROPE_THETA ** (-jnp.arange(ROPE_HALF, dtype=F32) / ROPE_HALF)
    invf = _rope_lanes(jnp.concatenate([inv_freq, inv_freq])).reshape(1, LANES)
    sgn = _rope_lanes(jnp.concatenate([-jnp.ones((ROPE_HALF,), F32),
                                       jnp.ones((ROPE_HALF,), F32)])).reshape(1, LANES)
    wr = jnp.zeros((D, LANES), F32)
    wr = wr.at[:, :N_GROUPS].set(w_router_group[layer])
    wr = wr.at[:, N_GROUPS:N_GROUPS + N_EXPERTS].set(w_router_expert[layer])
    wr_hi = wr.astype(BF16)
    wr = jnp.concatenate([wr_hi, (wr - wr_hi.astype(F32)).astype(BF16)], axis=1)
    br = jnp.zeros((1, LANES), F32)
    br = br.at[0, :N_GROUPS].set(b_router_group[layer])
    br = br.at[0, N_GROUPS:N_GROUPS + N_EXPERTS].set(b_router_expert[layer])

    ypool, q, k, vt = _pre_call(
        x, positions.reshape(B, S, 1), row(norm1_gain), win, row(q_a_gain), wq, row(kv_a_gain),
        wkv, qng, kng, w_pool[layer].astype(BF16), row(pool_scale), invf, sgn)
    yattn = _attn_call(q, k, vt)
    h1, xn2, logits = _post_call(
        x.reshape(T, D), ypool.reshape(T, D_POOL), yattn.reshape(T, D_ATTN),
        w_out[layer].astype(BF16), row(norm2_gain), wr, br)

    n_assign = T * TOP_K
    n_blocks = (n_assign + N_EXPERTS * (MOE_BLK - 1)) // MOE_BLK
    code, wts, counts = _route_call(logits)
    slots, blk_e, blk_n = _plan_call(code[:, :TOP_K].T.reshape(n_assign), counts[0],
                                     n_blocks + MOE_DRAIN_STEPS)
    y = _expert_call(blk_e, blk_n, slots, xn2,
                     w_exp_gate[layer], w_exp_up[layer], w_exp_down[layer])

    out = _final_call(h1, y, wts, p[layer].reshape(T, PLE_DIM),
                      w_ple_proj[layer].astype(BF16), row(ple_norm_gain), row(norm3_gain),
                      w_ple_gate[layer].astype(BF16))
    return out.reshape(B, S, D)
```

```python
import functools
import math

import jax
import jax.numpy as jnp
from jax import lax
from jax.experimental import pallas as pl
from jax.experimental.pallas import tpu as pltpu

F32 = jnp.float32
BF16 = jnp.bfloat16
I32 = jnp.int32

D_MODEL = 2048
PLE_DIM = 256
EPS = 1e-6
POOL_WINDOWS = (2, 4, 8, 16)
POOL_CH = 256
D_POOL = POOL_CH * len(POOL_WINDOWS)
N_HEADS = 8
Q_LORA = 512
KV_LORA = 512
QK_NOPE = 128
QK_ROPE = 64
QK_DIM = QK_NOPE + QK_ROPE
V_DIM = 128
D_ATTN = N_HEADS * V_DIM
ROPE_THETA = 10000.0
ATTN_SCALE = 1.0 / math.sqrt(QK_DIM)
Q_PRESCALE = ATTN_SCALE * math.log2(math.e)
N_GROUPS = 8
EXPERTS_PER_GROUP = 8
N_EXPERTS = N_GROUPS * EXPERTS_PER_GROUP
TOP_K = 2
D_EXPERT = 512

LANES = 128
MXU_DIM = 256
VMEM_LIMIT = 56 * 1024 * 1024

HEAD_W = 2 * LANES
ROPE_HALF = QK_ROPE // 2
POOL_HALO = 16
TM_PRE = 512
PRE_SUB = 256
TM_POST = 512
POST_SUB = 256
TM_FINAL = 256
TQ = 512
TK = PRE_SUB
TM_ROUTE = 256
MOE_BLK = 256
MOE_DRAIN_STEPS = 3
ROW_GROUP = 32
PLAN_UNROLL = 16
NEG_BIG = -1e30
V_ROWS = V_DIM + 16
SPARE_ROPE_LANE = ROPE_HALF
SCORE_BOUND_LIMIT = 50.0


def _const_spec(shape):
    nd = len(shape)
    return pl.BlockSpec(shape, lambda *_: (0,) * nd, pipeline_mode=pl.Buffered(1))


def _rms(x, gain):
    return x * lax.rsqrt(jnp.mean(x * x, axis=-1, keepdims=True) + EPS) * gain


def _pre_kernel(x_ref, pos_ref, g1_ref, win_ref, qag_ref, wq_ref, kvag_ref, wkv_ref,
                qng_ref, kng_ref, wpool_ref, pscale_ref, invf_ref, sgn_ref, qpad_ref, kpad_ref,
                ypool_ref, q_ref, k_ref, vt_ref, carry_ref):
    st = pl.program_id(1)

    @pl.when(st == 0)
    def _():
        carry_ref[...] = jnp.zeros_like(carry_ref)

    halo = carry_ref[...]
    for i in range(x_ref.shape[0] // PRE_SUB):
        halo = _pre_rows(i, st * x_ref.shape[0] + i * PRE_SUB, halo,
                         x_ref, pos_ref, g1_ref, win_ref, qag_ref, wq_ref, kvag_ref, wkv_ref,
                         qng_ref, kng_ref, wpool_ref, pscale_ref, invf_ref, sgn_ref, qpad_ref, kpad_ref,
                         ypool_ref, q_ref, k_ref, vt_ref)
    carry_ref[...] = halo


def _pre_rows(i, first_pos, halo, x_ref, pos_ref, g1_ref, win_ref, qag_ref, wq_ref, kvag_ref,
              wkv_ref, qng_ref, kng_ref, wpool_ref, pscale_ref, invf_ref, sgn_ref, qpad_ref, kpad_ref,
              ypool_ref, q_ref, k_ref, vt_ref):
    tm = PRE_SUB
    rs = pl.ds(i * tm, tm)

    hn = _rms(x_ref[rs, :], g1_ref[...])
    z = jnp.dot(hn.astype(BF16), win_ref[...], preferred_element_type=F32)

    u = z[:, :D_POOL]
    ext = jnp.concatenate([halo, u], axis=0)
    row = lax.broadcasted_iota(I32, (tm, 1), 0) + first_pos
    level = ext
    shift = 1
    for g, w in enumerate(POOL_WINDOWS):
        sl = slice(g * POOL_CH, (g + 1) * POOL_CH)
        while shift < w:
            level = level + pltpu.roll(level, shift, 0)
            shift *= 2
        win_sum = level[POOL_HALO:, sl]
        cnt = jnp.minimum(row + 1, w).astype(F32)
        d = win_sum / cnt - u[:, sl]
        y = jnp.dot(d.astype(BF16), wpool_ref[g], preferred_element_type=F32)
        ypool_ref[rs, sl] = (y * pscale_ref[:, sl]).astype(BF16)

    q_lat = z[:, D_POOL:D_POOL + Q_LORA]
    kv_lat = z[:, D_POOL + Q_LORA:D_POOL + Q_LORA + KV_LORA]
    k_rope = z[:, D_POOL + Q_LORA + KV_LORA:]
    qa = jnp.dot(_rms(q_lat, qag_ref[...]).astype(BF16), wq_ref[...],
                 preferred_element_type=F32)
    kv = jnp.dot(_rms(kv_lat, kvag_ref[...]).astype(BF16), wkv_ref[...],
                 preferred_element_type=F32)

    ang = pos_ref[rs, :].astype(F32) * invf_ref[...]
    cos = jnp.cos(ang)
    sin = jnp.sin(ang) * sgn_ref[...]

    def rot(t):
        return t * cos + pltpu.roll(t, LANES // 2, 1) * sin

    qng = qng_ref[...]
    kng = kng_ref[...]
    kr_rot = rot(k_rope * kng[:, LANES:])
    kr_ssq = jnp.sum(k_rope * k_rope, axis=-1, keepdims=True)
    for h in range(N_HEADS):
        qh = qa[:, h * HEAD_W:(h + 1) * HEAD_W]
        rq = lax.rsqrt(jnp.sum(qh * qh, axis=-1, keepdims=True) / QK_DIM + EPS) * Q_PRESCALE
        qn = qh * rq * qng
        q_ref[0, h, rs, :LANES] = qn[:, :LANES].astype(BF16)
        q_ref[0, h, rs, LANES:] = (rot(qn[:, LANES:]) + qpad_ref[...]).astype(BF16)
        kh = kv[:, h * QK_NOPE:(h + 1) * QK_NOPE]
        rk = lax.rsqrt((jnp.sum(kh * kh, axis=-1, keepdims=True) + kr_ssq) / QK_DIM + EPS)
        k_ref[0, h, rs, :LANES] = (kh * rk * kng[:, :LANES]).astype(BF16)
        k_ref[0, h, rs, LANES:] = (kr_rot * rk + kpad_ref[...]).astype(BF16)
        vh = kv[:, N_HEADS * QK_NOPE + h * V_DIM:N_HEADS * QK_NOPE + (h + 1) * V_DIM]
        vt_ref[0, h, i, :V_DIM, :] = vh.T.astype(BF16)
        vt_ref[0, h, i, V_DIM:, :] = jnp.ones((V_ROWS - V_DIM, tm), BF16)
    return u[tm - POOL_HALO:, :]


def _pre_call(x, pos, g1, win, qag, wq, kvag, wkv, qng, kng, wpool, pscale, invf, sgn, qpad,
              kpad):
    B, S, D = x.shape
    tm = TM_PRE
    grid = (B, S // tm)
    row_spec = lambda w: pl.BlockSpec((None, tm, w), lambda b, s: (b, s, 0))
    head_spec = lambda w: pl.BlockSpec((1, N_HEADS, tm, w), lambda b, s: (b, 0, s, 0))
    consts = [g1, win, qag, wq, kvag, wkv, qng, kng, wpool, pscale, invf, sgn, qpad, kpad]
    return pl.pallas_call(
        _pre_kernel,
        grid=grid,
        in_specs=[row_spec(D), row_spec(1)] + [_const_spec(c.shape) for c in consts],
        out_specs=[row_spec(D_POOL), head_spec(HEAD_W), head_spec(HEAD_W),
                   pl.BlockSpec((1, N_HEADS, tm // PRE_SUB, V_ROWS, PRE_SUB),
                                lambda b, s: (b, 0, s, 0, 0))],
        out_shape=[jax.ShapeDtypeStruct((B, S, D_POOL), BF16),
                   jax.ShapeDtypeStruct((B, N_HEADS, S, HEAD_W), BF16),
                   jax.ShapeDtypeStruct((B, N_HEADS, S, HEAD_W), BF16),
                   jax.ShapeDtypeStruct((B, N_HEADS, S // PRE_SUB, V_ROWS, PRE_SUB), BF16)],
        scratch_shapes=[pltpu.VMEM((POOL_HALO, D_POOL), F32)],
        compiler_params=pltpu.CompilerParams(
            dimension_semantics=("arbitrary", "arbitrary"), vmem_limit_bytes=VMEM_LIMIT),
        name="pre",
    )(x, pos, *consts)


def _attn_kernel(bounded_ref, q_ref, k_ref, vt_ref, o_ref, *chain_scratch):
    S = q_ref.shape[2]
    nq = S // TQ
    per_q = TQ // TK
    assert per_q == 2, "two online-softmax chains take the even / odd key tiles"
    chains = (chain_scratch[0:2], chain_scratch[2:4])
    score_bufs = (chain_scratch[4:6], chain_scratch[6:8])
    key_idx = lax.broadcasted_iota(I32, (TK, TQ), 0)
    qry_idx = lax.broadcasted_iota(I32, (TK, TQ), 1)

    def q_tile(bounded, qi, _):
        q = q_ref[0, 0, pl.ds(pl.multiple_of(qi * TQ, TQ), TQ), :]
        for m_ref, acc_ref in chains:
            m_ref[...] = jnp.full(m_ref.shape, NEG_BIG, F32)
            acc_ref[...] = jnp.zeros(acc_ref.shape, F32)

        def scores(kt):
            k = k_ref[0, 0, pl.ds(pl.multiple_of(kt * TK, TK), TK), :]
            return lax.dot_general(k, q, (((1,), (1,)), ((), ())), preferred_element_type=F32)

        def weighted_values(kt, p):
            return jnp.dot(vt_ref[0, 0, kt], p.astype(BF16), preferred_element_type=F32)

        def fold(chain, st, kt, mask):
            m_ref, acc_ref = chain
            if mask is not None:
                st = jnp.where(mask, st, NEG_BIG)
            m = m_ref[...]
            m_new = jnp.maximum(m, jnp.max(st, axis=0, keepdims=True))
            m_ref[...] = m_new
            acc_ref[...] = (jnp.exp2(m - m_new) * acc_ref[...]
                            + weighted_values(kt, jnp.exp2(st - m_new)))

        def pair_scores(j, buf):
            for c in range(per_q):
                buf[c][...] = scores(per_q * j + c)

        def fold_pair(j, buf, masks=(None, None)):
            if bounded:
                total = None
                for c in range(per_q):
                    st = buf[c][...]
                    if masks[c] is not None:
                        st = jnp.where(masks[c], st, NEG_BIG)
                    pv = weighted_values(per_q * j + c, jnp.exp2(st))
                    total = pv if total is None else total + pv
                chains[0][1][...] += total
            else:
                for c in range(per_q):
                    fold(chains[c], buf[c][...], per_q * j + c, masks[c])

        diag_masks = (key_idx <= qry_idx, key_idx + TK <= qry_idx)
        pair_scores(0, score_bufs[0])

        def two_pairs(i, _):
            pair_scores(2 * i + 1, score_bufs[1])
            fold_pair(2 * i, score_bufs[0])
            pair_scores(2 * i + 2, score_bufs[0])
            fold_pair(2 * i + 1, score_bufs[1])
            return 0
        lax.fori_loop(0, qi // 2, two_pairs, 0)

        @pl.when(qi % 2 == 0)
        def _():
            fold_pair(qi, score_bufs[0], diag_masks)

        @pl.when(qi % 2 == 1)
        def _():
            pair_scores(qi, score_bufs[1])
            fold_pair(qi - 1, score_bufs[0])
            fold_pair(qi, score_bufs[1], diag_masks)

        (m0, acc0), (m1, acc1) = chains
        if bounded:
            acc = acc0[...]
        else:
            m = jnp.maximum(m0[...], m1[...])
            acc = jnp.exp2(m0[...] - m) * acc0[...] + jnp.exp2(m1[...] - m) * acc1[...]
        out = acc[:V_DIM, :] / acc[V_DIM:V_DIM + 1, :]
        o_ref[0, pl.ds(pl.multiple_of(qi * TQ, TQ), TQ), :] = out.T.astype(BF16)
        return 0

    for bounded in (True, False):
        @pl.when((bounded_ref[0] != 0) == bounded)
        def _():
            lax.fori_loop(0, nq, functools.partial(q_tile, bounded), 0)


def _attn_call(bounded, q, k, vt):
    B, H, S, _ = q.shape
    head = lambda w: pl.BlockSpec((1, 1, S, w), lambda b, h, flag: (b, h, 0, 0))
    grid_spec = pltpu.PrefetchScalarGridSpec(
        num_scalar_prefetch=1,
        grid=(B, H),
        in_specs=[head(HEAD_W), head(HEAD_W),
                  pl.BlockSpec((1, 1) + vt.shape[2:], lambda b, h, flag: (b, h, 0, 0, 0))],
        out_specs=pl.BlockSpec((1, S, V_DIM), lambda b, h, flag: (b, 0, h)),
        scratch_shapes=[pltpu.VMEM((1, TQ), F32), pltpu.VMEM((V_ROWS, TQ), F32)] * 2
        + [pltpu.VMEM((TK, TQ), F32)] * 4,
    )
    return pl.pallas_call(
        _attn_kernel,
        grid_spec=grid_spec,
        out_shape=jax.ShapeDtypeStruct((B, S, H * V_DIM), BF16),
        compiler_params=pltpu.CompilerParams(
            dimension_semantics=("arbitrary", "arbitrary"), vmem_limit_bytes=VMEM_LIMIT),
        name="attn",
    )(bounded, q, k, vt)


def _post_kernel(x_ref, yp_ref, ya_ref, wo_ref, g2_ref, wr_ref, br_ref,
                 h1_ref, xn_ref, lg_ref):
    for i in range(x_ref.shape[0] // POST_SUB):
        rs = pl.ds(i * POST_SUB, POST_SUB)
        h1 = (x_ref[rs, :]
              + jnp.dot(yp_ref[rs, :], wo_ref[:D_POOL, :], preferred_element_type=F32)
              + jnp.dot(ya_ref[rs, :], wo_ref[D_POOL:, :], preferred_element_type=F32))
        h1_ref[rs, :] = h1
        xn = _rms(h1, g2_ref[...])
        xn_ref[rs, :] = xn
        xn_hi = xn.astype(BF16)
        xn_lo = (xn - xn_hi.astype(F32)).astype(BF16)
        both = (jnp.dot(xn_hi, wr_ref[...], preferred_element_type=F32)
                + jnp.dot(xn_lo, wr_ref[...], preferred_element_type=F32))
        lg_ref[rs, :] = both[:, :LANES] + both[:, LANES:] + br_ref[...]


def _post_call(x, yp, ya, wo, g2, wr, br):
    T, D = x.shape
    tm = TM_POST
    row = lambda w: pl.BlockSpec((tm, w), lambda i: (i, 0))
    return pl.pallas_call(
        _post_kernel,
        grid=(T // tm,),
        in_specs=[row(D), row(D_POOL), row(D_ATTN), _const_spec(wo.shape),
                  _const_spec(g2.shape), _const_spec(wr.shape), _const_spec(br.shape)],
        out_specs=[row(D), row(D), row(LANES)],
        out_shape=[jax.ShapeDtypeStruct((T, D), F32), jax.ShapeDtypeStruct((T, D), F32),
                   jax.ShapeDtypeStruct((T, LANES), F32)],
        compiler_params=pltpu.CompilerParams(
            dimension_semantics=("arbitrary",), vmem_limit_bytes=VMEM_LIMIT),
        name="post",
    )(x, yp, ya, wo, g2, wr, br)


def _route_kernel(lg_ref, code_ref, wts_ref, cnt_ref, carry_ref):
    i = pl.program_id(0)
    tm = lg_ref.shape[0]

    @pl.when(i == 0)
    def _():
        carry_ref[...] = jnp.zeros_like(carry_ref)

    lg = lg_ref[...]
    lane = lax.broadcasted_iota(I32, (tm, LANES), 1)
    lane_f = lane.astype(F32)
    neg_inf = -jnp.inf

    def first_argmax(vals):
        mx = jnp.max(vals, axis=-1, keepdims=True)
        idx = jnp.min(jnp.where(vals == mx, lane_f, float(LANES)), axis=-1, keepdims=True)
        return mx, idx.astype(I32)

    is_group = lane < N_GROUPS
    g_logits = jnp.where(is_group, lg, neg_inf)
    g_max, g_sel = first_argmax(g_logits)
    g_w = 1.0 / jnp.sum(jnp.where(is_group, jnp.exp(lg - g_max), 0.0), axis=-1, keepdims=True)

    lo = N_GROUPS + g_sel * EXPERTS_PER_GROUP
    in_group = (lane >= lo) & (lane < lo + EXPERTS_PER_GROUP)
    e_logits = jnp.where(in_group, lg, neg_inf)
    v1, i1 = first_argmax(e_logits)
    v2, i2 = first_argmax(jnp.where(lane == i1, neg_inf, e_logits))
    e1 = i1 - N_GROUPS
    e2 = i2 - N_GROUPS
    t = jnp.exp(v2 - v1)
    w1 = g_w / (1.0 + t)
    w2 = g_w * t / (1.0 + t)
    wts_ref[...] = jnp.where(lane == 0, w1, jnp.where(lane == 1, w2, 0.0))

    hit1 = lane == e1
    hit2 = lane == e2
    onehot = jnp.where(hit1 | hit2, 1.0, 0.0).astype(BF16)
    r = lax.broadcasted_iota(I32, (tm, tm), 0)
    c = lax.broadcasted_iota(I32, (tm, tm), 1)
    lower = jnp.where(c < r, 1.0, 0.0).astype(BF16)
    before = jnp.dot(lower, onehot, preferred_element_type=F32) + carry_ref[0:1, :]
    carry_new = carry_ref[0:1, :] + jnp.sum(onehot.astype(F32), axis=0, keepdims=True)
    carry_ref[...] = jnp.broadcast_to(carry_new, carry_ref.shape)
    pos1 = jnp.sum(jnp.where(hit1, before, 0.0), axis=-1, keepdims=True).astype(I32)
    pos2 = jnp.sum(jnp.where(hit2, before, 0.0), axis=-1, keepdims=True).astype(I32)
    code1 = e1 * 65536 + pos1
    code2 = e2 * 65536 + pos2
    code_ref[...] = jnp.where(lane == 0, code1, jnp.where(lane == 1, code2, 0))
    cnt_ref[...] = carry_ref[...].astype(I32)


def _route_call(logits):
    T = logits.shape[0]
    tm = TM_ROUTE
    row = pl.BlockSpec((tm, LANES), lambda i: (i, 0))
    return pl.pallas_call(
        _route_kernel,
        grid=(T // tm,),
        in_specs=[row],
        out_specs=[row, row, pl.BlockSpec((8, LANES), lambda i: (0, 0))],
        out_shape=[jax.ShapeDtypeStruct((T, LANES), I32), jax.ShapeDtypeStruct((T, LANES), F32),
                   jax.ShapeDtypeStruct((8, LANES), I32)],
        scratch_shapes=[pltpu.VMEM((8, LANES), F32)],
        compiler_params=pltpu.CompilerParams(
            dimension_semantics=("arbitrary",), vmem_limit_bytes=VMEM_LIMIT),
        name="route",
    )(logits)


def _plan_kernel(code_ref, cnt_ref, unused_hbm, slots_hbm, blke_ref, blkn_ref,
                 slot_ref, start_ref, sem):
    n_assign = code_ref.shape[0]
    n_blocks = blke_ref.shape[0]
    load = pltpu.make_async_copy(unused_hbm, slot_ref, sem)
    load.start()

    def per_expert(e, nb_done):
        cnt = cnt_ref[e]
        nb = lax.shift_right_logical(cnt + (MOE_BLK - 1), MOE_BLK.bit_length() - 1)
        start_ref[e] = nb_done * MOE_BLK

        def per_block(j, _):
            blke_ref[nb_done + j] = e
            blkn_ref[nb_done + j] = jnp.minimum(cnt - j * MOE_BLK, MOE_BLK)
            return 0
        lax.fori_loop(0, nb, per_block, 0)
        return nb_done + nb
    used = lax.fori_loop(0, N_EXPERTS, per_expert, 0)

    last_e = blke_ref[jnp.maximum(used - 1, 0)]

    def tail(b, _):
        blke_ref[b] = last_e
        blkn_ref[b] = 0
        return 0
    lax.fori_loop(used, n_blocks, tail, 0)

    load.wait()

    def place(a, _):
        code = code_ref[a]
        e = lax.shift_right_logical(code, 16)
        slot_ref[start_ref[e] + (code & 0xFFFF)] = a
        return 0
    lax.fori_loop(0, n_assign, place, 0, unroll=PLAN_UNROLL)

    store = pltpu.make_async_copy(slot_ref, slots_hbm, sem)
    store.start()
    store.wait()


def _plan_call(code_flat, counts, n_tab):
    n_assign = code_flat.shape[0]
    n_slots = n_tab * MOE_BLK
    unused = n_assign + (jnp.arange(n_slots, dtype=I32) & (MOE_BLK - 1))
    smem = pl.BlockSpec(memory_space=pltpu.SMEM)
    hbm = pl.BlockSpec(memory_space=pl.ANY)
    return pl.pallas_call(
        _plan_kernel,
        in_specs=[smem, smem, hbm],
        out_specs=[hbm, smem, smem],
        out_shape=[jax.ShapeDtypeStruct((n_slots,), I32),
                   jax.ShapeDtypeStruct((n_tab,), I32),
                   jax.ShapeDtypeStruct((n_tab,), I32)],
        scratch_shapes=[pltpu.SMEM((n_slots,), I32), pltpu.SMEM((N_EXPERTS,), I32),
                        pltpu.SemaphoreType.DMA],
        name="plan",
    )(code_flat, counts, unused)


def _expert_kernel(blke_ref, blkn_ref, slot_g_ref, slot_s_ref, xn_hbm, wg_ref, wu_ref, wd_ref,
                   y_hbm, x0, x1, y0, y1, gsem, ssem):
    xbufs, ybufs = (x0, x1), (y0, y1)
    s = pl.program_id(0)
    n_tok = xn_hbm.shape[0]

    def rows(j):
        n = blkn_ref[jnp.clip(j, 0, blkn_ref.shape[0] - 1)]
        return jnp.where(j >= 0, (n + (ROW_GROUP - 1)) & ~(ROW_GROUP - 1), 0)
    rows0, rows1, rows2, rows3 = rows(s), rows(s - 1), rows(s - 2), rows(s - 3)

    def stage(nxt):
        cur = 1 - nxt
        x_nxt, x_cur, y_nxt, y_cur = xbufs[nxt], xbufs[cur], ybufs[nxt], ybufs[cur]

        def per_group(n_rows, issue_row):
            for g in range(MOE_BLK // ROW_GROUP):
                @pl.when(g * ROW_GROUP < n_rows)
                def _():
                    for r in range(g * ROW_GROUP, (g + 1) * ROW_GROUP):
                        issue_row(r)

        def gather_row(r):
            tok = slot_g_ref[0, 0, r] & (n_tok - 1)
            pltpu.make_async_copy(xn_hbm.at[pl.ds(tok, 1)], x_nxt.at[pl.ds(r, 1)],
                                  gsem.at[nxt]).start()

        def scatter_row(r):
            pltpu.make_async_copy(y_nxt.at[pl.ds(r, 1)], y_hbm.at[pl.ds(slot_s_ref[0, 0, r], 1)],
                                  ssem.at[nxt]).start()

        def wait_rows(src, dst, sem, n_rows):
            n_rows = pl.multiple_of(n_rows, ROW_GROUP)

            @pl.when(n_rows > 0)
            def _():
                pltpu.make_async_copy(src.at[pl.ds(0, n_rows)], dst.at[pl.ds(0, n_rows)],
                                      sem).wait()

        def compute():
            xb = x_cur[...].astype(BF16)
            a = jnp.dot(xb, wg_ref[0].astype(BF16), preferred_element_type=F32)
            u = jnp.dot(xb, wu_ref[0].astype(BF16), preferred_element_type=F32)
            hmid = (a * jax.nn.sigmoid(a) * u).astype(BF16)
            y_cur[...] = jnp.dot(hmid, wd_ref[0].astype(BF16), preferred_element_type=F32)

        wait_rows(xn_hbm, x_cur, gsem.at[cur], rows1)
        wait_rows(y_cur, y_hbm, ssem.at[cur], rows3)

        if nxt == 0:
            @pl.when(s == 0)
            def _():
                x_nxt[...] = jnp.zeros(x_nxt.shape, F32)
                x_cur[...] = jnp.zeros(x_cur.shape, F32)
                y_nxt[...] = jnp.zeros(y_nxt.shape, F32)
                dump = pltpu.make_async_copy(
                    y_nxt, y_hbm.at[pl.ds(TOP_K * n_tok, MOE_BLK)], ssem.at[nxt])
                dump.start()
                dump.wait()

        per_group(rows0, gather_row)
        per_group(rows2, scatter_row)

        @pl.when(rows1 > 0)
        def _():
            compute()

    for parity in range(2):
        pl.when((s & 1) == parity)(functools.partial(stage, parity))


def _expert_call(blk_e, blk_n, slots, xn, wg, wu, wd):
    T, D = xn.shape
    assert T & (T - 1) == 0, "token id is recovered from the assignment id with a mask"
    n_tab = blk_e.shape[0]
    F = wg.shape[-1]
    prev = lambda j: jnp.maximum(j, 0)
    grid_spec = pltpu.PrefetchScalarGridSpec(
        num_scalar_prefetch=2,
        grid=(n_tab,),
        in_specs=[
            pl.BlockSpec((1, 1, MOE_BLK), lambda s, be, bn: (s, 0, 0), memory_space=pltpu.SMEM),
            pl.BlockSpec((1, 1, MOE_BLK), lambda s, be, bn: (prev(s - 2), 0, 0),
                         memory_space=pltpu.SMEM),
            pl.BlockSpec(memory_space=pl.ANY),
            pl.BlockSpec((1, D, F), lambda s, be, bn: (be[prev(s - 1)], 0, 0)),
            pl.BlockSpec((1, D, F), lambda s, be, bn: (be[prev(s - 1)], 0, 0)),
            pl.BlockSpec((1, F, D), lambda s, be, bn: (be[prev(s - 1)], 0, 0)),
        ],
        out_specs=pl.BlockSpec(memory_space=pl.ANY),
        scratch_shapes=[pltpu.VMEM((MOE_BLK, D), F32)] * 4
        + [pltpu.SemaphoreType.DMA((2,)), pltpu.SemaphoreType.DMA((2,))],
    )
    slots3 = slots.reshape(n_tab, 1, MOE_BLK)
    return pl.pallas_call(
        _expert_kernel,
        grid_spec=grid_spec,
        out_shape=jax.ShapeDtypeStruct((T * TOP_K + MOE_BLK, D), F32),
        compiler_params=pltpu.CompilerParams(
            dimension_semantics=("arbitrary",), vmem_limit_bytes=VMEM_LIMIT,
            disable_bounds_checks=True),
        name="experts",
    )(blk_e, blk_n, slots3, slots3, xn, wg, wu, wd)


def _final_kernel(h1_ref, y1_ref, y2_ref, wts_ref, p_ref, wpp_ref, pg_ref, g3_ref, wpg_ref,
                  o_ref):
    w = wts_ref[...]
    h2 = h1_ref[...] + w[:, 0:1] * y1_ref[...] + w[:, 1:2] * y2_ref[...]
    e = _rms(jnp.dot(p_ref[...].astype(BF16), wpp_ref[...], preferred_element_type=F32),
             pg_ref[...])
    gate = jax.nn.sigmoid(jnp.dot(_rms(h2, g3_ref[...]).astype(BF16), wpg_ref[...],
                                  preferred_element_type=F32))
    o_ref[...] = h2 + gate * e


def _final_call(h1, y, wts, p, wpp, pg, g3, wpg):
    T, D = h1.shape
    tm = TM_FINAL
    row = lambda w: pl.BlockSpec((tm, w), lambda i: (i, 0))
    second = pl.BlockSpec((tm, D), lambda i: (i + T // tm, 0))
    return pl.pallas_call(
        _final_kernel,
        grid=(T // tm,),
        in_specs=[row(D), row(D), second, row(LANES), row(PLE_DIM), _const_spec(wpp.shape),
                  _const_spec(pg.shape), _const_spec(g3.shape), _const_spec(wpg.shape)],
        out_specs=row(D),
        out_shape=jax.ShapeDtypeStruct((T, D), F32),
        compiler_params=pltpu.CompilerParams(
            dimension_semantics=("arbitrary",), vmem_limit_bytes=VMEM_LIMIT),
        name="final",
    )(h1, y, y, wts, p, wpp, pg, g3, wpg)


def _rope_lanes(a):
    z = jnp.zeros(a.shape[:-1] + (ROPE_HALF,), a.dtype)
    return jnp.concatenate([a[..., :ROPE_HALF], z, a[..., ROPE_HALF:], z], axis=-1)


def _head_lanes(a):
    return jnp.concatenate([a[..., :QK_NOPE], _rope_lanes(a[..., QK_NOPE:])], axis=-1)


def _score_bound(q_gain, k_gain):
    return Q_PRESCALE * QK_DIM * jnp.max(jnp.abs(q_gain)) * jnp.max(jnp.abs(k_gain))


def kernel(x, p, positions, norm1_gain, w_in, q_a_gain, w_q_b, kv_a_gain, w_kv_b, q_norm_gain, k_norm_gain, w_pool, pool_scale, w_out, norm2_gain, w_router_group, b_router_group, w_router_expert, b_router_expert, w_exp_gate, w_exp_up, w_exp_down, norm3_gain, w_ple_gate, w_ple_proj, ple_norm_gain):
    B, S, D = x.shape
    T = B * S
    assert x.shape[2] == D_MODEL and S % TQ == 0 and S % TM_PRE == 0
    assert T % TM_POST == 0 and T % TM_FINAL == 0 and T % TM_ROUTE == 0
    layer = 0
    row = lambda a: a[layer].reshape(1, -1)

    n_lat = D_POOL + Q_LORA + KV_LORA
    win = jnp.concatenate([w_in[layer][:, :n_lat], _rope_lanes(w_in[layer][:, n_lat:])],
                          axis=1).astype(BF16)
    wq = _head_lanes(w_q_b[layer].reshape(Q_LORA, N_HEADS, QK_DIM)
                     ).reshape(Q_LORA, N_HEADS * HEAD_W).astype(BF16)
    wkv3 = w_kv_b[layer].reshape(KV_LORA, N_HEADS, QK_NOPE + V_DIM)
    wkv = jnp.concatenate([wkv3[..., :QK_NOPE].reshape(KV_LORA, -1),
                           wkv3[..., QK_NOPE:].reshape(KV_LORA, -1)], axis=1).astype(BF16)
    qng = _head_lanes(q_norm_gain[layer]).reshape(1, HEAD_W)
    kng = _head_lanes(k_norm_gain[layer]).reshape(1, HEAD_W)
    inv_freq = ROPE_THETA ** (-jnp.arange(ROPE_HALF, dtype=F32) / ROPE_HALF)
    invf = _rope_lanes(jnp.concatenate([inv_freq, inv_freq])).reshape(1, LANES)
    sgn = _rope_lanes(jnp.concatenate([-jnp.ones((ROPE_HALF,), F32),
                                       jnp.ones((ROPE_HALF,), F32)])).reshape(1, LANES)
    wr = jnp.zeros((D, LANES), F32)
    wr = wr.at[:, :N_GROUPS].set(w_router_group[layer])
    wr = wr.at[:, N_GROUPS:N_GROUPS + N_EXPERTS].set(w_router_expert[layer])
    wr_hi = wr.astype(BF16)
    wr = jnp.concatenate([wr_hi, (wr - wr_hi.astype(F32)).astype(BF16)], axis=1)
    br = jnp.zeros((1, LANES), F32)
    br = br.at[0, :N_GROUPS].set(b_router_group[layer])
    br = br.at[0, N_GROUPS:N_GROUPS + N_EXPERTS].set(b_router_expert[layer])

    bound = _score_bound(q_norm_gain[layer], k_norm_gain[layer])
    bounded = bound <= SCORE_BOUND_LIMIT
    spare = jnp.arange(LANES) == SPARE_ROPE_LANE
    qpad = jnp.where(spare & bounded, -bound, 0.0).astype(F32).reshape(1, LANES)
    kpad = jnp.where(spare, 1.0, 0.0).astype(F32).reshape(1, LANES)
    ypool, q, k, vt = _pre_call(
        x, positions.reshape(B, S, 1), row(norm1_gain), win, row(q_a_gain), wq, row(kv_a_gain),
        wkv, qng, kng, w_pool[layer].astype(BF16), row(pool_scale), invf, sgn, qpad, kpad)
    yattn = _attn_call(bounded.astype(I32).reshape(1), q, k, vt)
    h1, xn2, logits = _post_call(
        x.reshape(T, D), ypool.reshape(T, D_POOL), yattn.reshape(T, D_ATTN),
        w_out[layer].astype(BF16), row(norm2_gain), wr, br)

    n_assign = T * TOP_K
    n_blocks = (n_assign + N_EXPERTS * (MOE_BLK - 1)) // MOE_BLK
    code, wts, counts = _route_call(logits)
    slots, blk_e, blk_n = _plan_call(code[:, :TOP_K].T.reshape(n_assign), counts[0],
                                     n_blocks + MOE_DRAIN_STEPS)
    y = _expert_call(blk_e, blk_n, slots, xn2,
                     w_exp_gate[layer], w_exp_up[layer], w_exp_down[layer])

    out = _final_call(h1, y, wts, p[layer].reshape(T, PLE_DIM),
                      w_ple_proj[layer].astype(BF16), row(ple_norm_gain), row(norm3_gain),
                      w_ple_gate[layer].astype(BF16))
    return out.reshape(B, S, D)
```

```python
import functools
import math

import jax
import jax.numpy as jnp
from jax import lax
from jax.experimental import pallas as pl
from jax.experimental.pallas import tpu as pltpu

F32 = jnp.float32
BF16 = jnp.bfloat16
I32 = jnp.int32

D_MODEL = 2048
PLE_DIM = 256
EPS = 1e-6
POOL_WINDOWS = (2, 4, 8, 16)
POOL_CH = 256
D_POOL = POOL_CH * len(POOL_WINDOWS)
N_HEADS = 8
Q_LORA = 512
KV_LORA = 512
QK_NOPE = 128
QK_ROPE = 64
QK_DIM = QK_NOPE + QK_ROPE
V_DIM = 128
D_ATTN = N_HEADS * V_DIM
ROPE_THETA = 10000.0
ATTN_SCALE = 1.0 / math.sqrt(QK_DIM)
Q_PRESCALE = ATTN_SCALE * math.log2(math.e)
N_GROUPS = 8
EXPERTS_PER_GROUP = 8
N_EXPERTS = N_GROUPS * EXPERTS_PER_GROUP
TOP_K = 2
D_EXPERT = 512

LANES = 128
PACK_SUBLANES = 8
MXU_DIM = 256
VMEM_LIMIT = 56 * 1024 * 1024

HEAD_W = 2 * LANES
ROPE_HALF = QK_ROPE // 2
POOL_HALO = 16
TM_PRE = 512
PRE_SUB = 256
TM_POST = 512
POST_SUB = 256
TM_FINAL = 256
TQ = 512
TK = PRE_SUB
TM_ROUTE = 256
MOE_BLK = 256
MOE_DRAIN_STEPS = 3
ROW_GROUP = 32
PLAN_UNROLL = 16
NEG_BIG = -1e30
V_ROWS = V_DIM + 16
SPARE_ROPE_LANE = ROPE_HALF
SCORE_BOUND_LIMIT = 50.0


def _const_spec(shape):
    nd = len(shape)
    return pl.BlockSpec(shape, lambda *_: (0,) * nd, pipeline_mode=pl.Buffered(1))


def _rms(x, gain):
    return x * lax.rsqrt(jnp.mean(x * x, axis=-1, keepdims=True) + EPS) * gain


def _store_packed_rows(ref, first_row, x):
    rows, d = x.shape
    half = d // 2
    assert half == PACK_SUBLANES * LANES
    for i in range(PACK_SUBLANES):
        lo = x[:, i * LANES:(i + 1) * LANES].astype(BF16).astype(F32)
        hi = x[:, half + i * LANES:half + (i + 1) * LANES].astype(BF16).astype(F32)
        word = (lax.shift_right_logical(pltpu.bitcast(lo, jnp.uint32), jnp.uint32(16))
                | (pltpu.bitcast(hi, jnp.uint32) & jnp.uint32(0xFFFF0000)))
        ref[pl.ds(first_row * PACK_SUBLANES + i, rows, stride=PACK_SUBLANES), :] = word


def _load_packed_rows(ref, first_row, rows):
    lo, hi = [], []
    for i in range(PACK_SUBLANES):
        word = ref[pl.ds(first_row * PACK_SUBLANES + i, rows, stride=PACK_SUBLANES), :]
        lo.append(pltpu.bitcast(lax.shift_left(word, jnp.uint32(16)), F32))
        hi.append(pltpu.bitcast(word & jnp.uint32(0xFFFF0000), F32))
    return jnp.concatenate(lo + hi, axis=1)


def _pre_kernel(x_ref, pos_ref, g1_ref, win_ref, qag_ref, wq_ref, kvag_ref, wkv_ref,
                qng_ref, kng_ref, wpool_ref, pscale_ref, invf_ref, sgn_ref, qpad_ref, kpad_ref,
                ypool_ref, q_ref, k_ref, vt_ref, carry_ref):
    st = pl.program_id(1)

    @pl.when(st == 0)
    def _():
        carry_ref[...] = jnp.zeros_like(carry_ref)

    halo = carry_ref[...]
    for i in range(x_ref.shape[0] // PRE_SUB):
        halo = _pre_rows(i, st * x_ref.shape[0] + i * PRE_SUB, halo,
                         x_ref, pos_ref, g1_ref, win_ref, qag_ref, wq_ref, kvag_ref, wkv_ref,
                         qng_ref, kng_ref, wpool_ref, pscale_ref, invf_ref, sgn_ref, qpad_ref, kpad_ref,
                         ypool_ref, q_ref, k_ref, vt_ref)
    carry_ref[...] = halo


def _pre_rows(i, first_pos, halo, x_ref, pos_ref, g1_ref, win_ref, qag_ref, wq_ref, kvag_ref,
              wkv_ref, qng_ref, kng_ref, wpool_ref, pscale_ref, invf_ref, sgn_ref, qpad_ref, kpad_ref,
              ypool_ref, q_ref, k_ref, vt_ref):
    tm = PRE_SUB
    rs = pl.ds(i * tm, tm)

    hn = _rms(x_ref[rs, :], g1_ref[...])
    z = jnp.dot(hn.astype(BF16), win_ref[...], preferred_element_type=F32)

    u = z[:, :D_POOL]
    ext = jnp.concatenate([halo, u], axis=0)
    row = lax.broadcasted_iota(I32, (tm, 1), 0) + first_pos
    level = ext
    shift = 1
    for g, w in enumerate(POOL_WINDOWS):
        sl = slice(g * POOL_CH, (g + 1) * POOL_CH)
        while shift < w:
            level = level + pltpu.roll(level, shift, 0)
            shift *= 2
        win_sum = level[POOL_HALO:, sl]
        cnt = jnp.minimum(row + 1, w).astype(F32)
        d = win_sum / cnt - u[:, sl]
        y = jnp.dot(d.astype(BF16), wpool_ref[g], preferred_element_type=F32)
        ypool_ref[rs, sl] = (y * pscale_ref[:, sl]).astype(BF16)

    q_lat = z[:, D_POOL:D_POOL + Q_LORA]
    kv_lat = z[:, D_POOL + Q_LORA:D_POOL + Q_LORA + KV_LORA]
    k_rope = z[:, D_POOL + Q_LORA + KV_LORA:]
    qa = jnp.dot(_rms(q_lat, qag_ref[...]).astype(BF16), wq_ref[...],
                 preferred_element_type=F32)
    kv = jnp.dot(_rms(kv_lat, kvag_ref[...]).astype(BF16), wkv_ref[...],
                 preferred_element_type=F32)

    ang = pos_ref[rs, :].astype(F32) * invf_ref[...]
    cos = jnp.cos(ang)
    sin = jnp.sin(ang) * sgn_ref[...]

    def rot(t):
        return t * cos + pltpu.roll(t, LANES // 2, 1) * sin

    qng = qng_ref[...]
    kng = kng_ref[...]
    kr_rot = rot(k_rope * kng[:, LANES:])
    kr_ssq = jnp.sum(k_rope * k_rope, axis=-1, keepdims=True)
    for h in range(N_HEADS):
        qh = qa[:, h * HEAD_W:(h + 1) * HEAD_W]
        rq = lax.rsqrt(jnp.sum(qh * qh, axis=-1, keepdims=True) / QK_DIM + EPS) * Q_PRESCALE
        qn = qh * rq * qng
        q_ref[0, h, rs, :LANES] = qn[:, :LANES].astype(BF16)
        q_ref[0, h, rs, LANES:] = (rot(qn[:, LANES:]) + qpad_ref[...]).astype(BF16)
        kh = kv[:, h * QK_NOPE:(h + 1) * QK_NOPE]
        rk = lax.rsqrt((jnp.sum(kh * kh, axis=-1, keepdims=True) + kr_ssq) / QK_DIM + EPS)
        k_ref[0, h, rs, :LANES] = (kh * rk * kng[:, :LANES]).astype(BF16)
        k_ref[0, h, rs, LANES:] = (kr_rot * rk + kpad_ref[...]).astype(BF16)
        vh = kv[:, N_HEADS * QK_NOPE + h * V_DIM:N_HEADS * QK_NOPE + (h + 1) * V_DIM]
        vt_ref[0, h, i, :V_DIM, :] = vh.T.astype(BF16)
        vt_ref[0, h, i, V_DIM:, :] = jnp.ones((V_ROWS - V_DIM, tm), BF16)
    return u[tm - POOL_HALO:, :]


def _pre_call(x, pos, g1, win, qag, wq, kvag, wkv, qng, kng, wpool, pscale, invf, sgn, qpad,
              kpad):
    B, S, D = x.shape
    tm = TM_PRE
    grid = (B, S // tm)
    row_spec = lambda w: pl.BlockSpec((None, tm, w), lambda b, s: (b, s, 0))
    head_spec = lambda w: pl.BlockSpec((1, N_HEADS, tm, w), lambda b, s: (b, 0, s, 0))
    consts = [g1, win, qag, wq, kvag, wkv, qng, kng, wpool, pscale, invf, sgn, qpad, kpad]
    return pl.pallas_call(
        _pre_kernel,
        grid=grid,
        in_specs=[row_spec(D), row_spec(1)] + [_const_spec(c.shape) for c in consts],
        out_specs=[row_spec(D_POOL), head_spec(HEAD_W), head_spec(HEAD_W),
                   pl.BlockSpec((1, N_HEADS, tm // PRE_SUB, V_ROWS, PRE_SUB),
                                lambda b, s: (b, 0, s, 0, 0))],
        out_shape=[jax.ShapeDtypeStruct((B, S, D_POOL), BF16),
                   jax.ShapeDtypeStruct((B, N_HEADS, S, HEAD_W), BF16),
                   jax.ShapeDtypeStruct((B, N_HEADS, S, HEAD_W), BF16),
                   jax.ShapeDtypeStruct((B, N_HEADS, S // PRE_SUB, V_ROWS, PRE_SUB), BF16)],
        scratch_shapes=[pltpu.VMEM((POOL_HALO, D_POOL), F32)],
        compiler_params=pltpu.CompilerParams(
            dimension_semantics=("arbitrary", "arbitrary"), vmem_limit_bytes=VMEM_LIMIT),
        name="pre",
    )(x, pos, *consts)


def _attn_kernel(bounded_ref, q_ref, k_ref, vt_ref, o_ref, *chain_scratch):
    S = q_ref.shape[2]
    nq = S // TQ
    per_q = TQ // TK
    assert per_q == 2, "two online-softmax chains take the even / odd key tiles"
    chains = (chain_scratch[0:2], chain_scratch[2:4])
    score_bufs = (chain_scratch[4:6], chain_scratch[6:8])
    key_idx = lax.broadcasted_iota(I32, (TK, TQ), 0)
    qry_idx = lax.broadcasted_iota(I32, (TK, TQ), 1)

    def q_tile(bounded, qi, _):
        q = q_ref[0, 0, pl.ds(pl.multiple_of(qi * TQ, TQ), TQ), :]
        for m_ref, acc_ref in chains:
            m_ref[...] = jnp.full(m_ref.shape, NEG_BIG, F32)
            acc_ref[...] = jnp.zeros(acc_ref.shape, F32)

        def scores(kt):
            k = k_ref[0, 0, pl.ds(pl.multiple_of(kt * TK, TK), TK), :]
            return lax.dot_general(k, q, (((1,), (1,)), ((), ())), preferred_element_type=F32)

        def weighted_values(kt, p):
            return jnp.dot(vt_ref[0, 0, kt], p.astype(BF16), preferred_element_type=F32)

        def fold(chain, st, kt, mask):
            m_ref, acc_ref = chain
            if mask is not None:
                st = jnp.where(mask, st, NEG_BIG)
            m = m_ref[...]
            m_new = jnp.maximum(m, jnp.max(st, axis=0, keepdims=True))
            m_ref[...] = m_new
            acc_ref[...] = (jnp.exp2(m - m_new) * acc_ref[...]
                            + weighted_values(kt, jnp.exp2(st - m_new)))

        def pair_scores(j, buf):
            for c in range(per_q):
                buf[c][...] = scores(per_q * j + c)

        def fold_pair(j, buf, masks=(None, None)):
            if bounded:
                total = None
                for c in range(per_q):
                    st = buf[c][...]
                    if masks[c] is not None:
                        st = jnp.where(masks[c], st, NEG_BIG)
                    pv = weighted_values(per_q * j + c, jnp.exp2(st))
                    total = pv if total is None else total + pv
                chains[0][1][...] += total
            else:
                for c in range(per_q):
                    fold(chains[c], buf[c][...], per_q * j + c, masks[c])

        diag_masks = (key_idx <= qry_idx, key_idx + TK <= qry_idx)
        pair_scores(0, score_bufs[0])

        def two_pairs(i, _):
            pair_scores(2 * i + 1, score_bufs[1])
            fold_pair(2 * i, score_bufs[0])
            pair_scores(2 * i + 2, score_bufs[0])
            fold_pair(2 * i + 1, score_bufs[1])
            return 0
        lax.fori_loop(0, qi // 2, two_pairs, 0)

        @pl.when(qi % 2 == 0)
        def _():
            fold_pair(qi, score_bufs[0], diag_masks)

        @pl.when(qi % 2 == 1)
        def _():
            pair_scores(qi, score_bufs[1])
            fold_pair(qi - 1, score_bufs[0])
            fold_pair(qi, score_bufs[1], diag_masks)

        (m0, acc0), (m1, acc1) = chains
        if bounded:
            acc = acc0[...]
        else:
            m = jnp.maximum(m0[...], m1[...])
            acc = jnp.exp2(m0[...] - m) * acc0[...] + jnp.exp2(m1[...] - m) * acc1[...]
        out = acc[:V_DIM, :] / acc[V_DIM:V_DIM + 1, :]
        o_ref[0, pl.ds(pl.multiple_of(qi * TQ, TQ), TQ), :] = out.T.astype(BF16)
        return 0

    for bounded in (True, False):
        @pl.when((bounded_ref[0] != 0) == bounded)
        def _():
            lax.fori_loop(0, nq, functools.partial(q_tile, bounded), 0)


def _attn_call(bounded, q, k, vt):
    B, H, S, _ = q.shape
    head = lambda w: pl.BlockSpec((1, 1, S, w), lambda b, h, flag: (b, h, 0, 0))
    grid_spec = pltpu.PrefetchScalarGridSpec(
        num_scalar_prefetch=1,
        grid=(B, H),
        in_specs=[head(HEAD_W), head(HEAD_W),
                  pl.BlockSpec((1, 1) + vt.shape[2:], lambda b, h, flag: (b, h, 0, 0, 0))],
        out_specs=pl.BlockSpec((1, S, V_DIM), lambda b, h, flag: (b, 0, h)),
        scratch_shapes=[pltpu.VMEM((1, TQ), F32), pltpu.VMEM((V_ROWS, TQ), F32)] * 2
        + [pltpu.VMEM((TK, TQ), F32)] * 4,
    )
    return pl.pallas_call(
        _attn_kernel,
        grid_spec=grid_spec,
        out_shape=jax.ShapeDtypeStruct((B, S, H * V_DIM), BF16),
        compiler_params=pltpu.CompilerParams(
            dimension_semantics=("arbitrary", "arbitrary"), vmem_limit_bytes=VMEM_LIMIT),
        name="attn",
    )(bounded, q, k, vt)


def _post_kernel(x_ref, yp_ref, ya_ref, wo_ref, g2_ref, wr_ref, br_ref,
                 h1_ref, xn_ref, lg_ref):
    for i in range(x_ref.shape[0] // POST_SUB):
        rs = pl.ds(i * POST_SUB, POST_SUB)
        h1 = (x_ref[rs, :]
              + jnp.dot(yp_ref[rs, :], wo_ref[:D_POOL, :], preferred_element_type=F32)
              + jnp.dot(ya_ref[rs, :], wo_ref[D_POOL:, :], preferred_element_type=F32))
        h1_ref[rs, :] = h1
        xn = _rms(h1, g2_ref[...])
        _store_packed_rows(xn_ref, i * POST_SUB, xn)
        xn_hi = xn.astype(BF16)
        xn_lo = (xn - xn_hi.astype(F32)).astype(BF16)
        both = (jnp.dot(xn_hi, wr_ref[...], preferred_element_type=F32)
                + jnp.dot(xn_lo, wr_ref[...], preferred_element_type=F32))
        lg_ref[rs, :] = both[:, :LANES] + both[:, LANES:] + br_ref[...]


def _post_call(x, yp, ya, wo, g2, wr, br):
    T, D = x.shape
    tm = TM_POST
    row = lambda w: pl.BlockSpec((tm, w), lambda i: (i, 0))
    return pl.pallas_call(
        _post_kernel,
        grid=(T // tm,),
        in_specs=[row(D), row(D_POOL), row(D_ATTN), _const_spec(wo.shape),
                  _const_spec(g2.shape), _const_spec(wr.shape), _const_spec(br.shape)],
        out_specs=[row(D), pl.BlockSpec((tm * PACK_SUBLANES, LANES), lambda i: (i, 0)),
                   row(LANES)],
        out_shape=[jax.ShapeDtypeStruct((T, D), F32),
                   jax.ShapeDtypeStruct((T * PACK_SUBLANES, LANES), jnp.uint32),
                   jax.ShapeDtypeStruct((T, LANES), F32)],
        compiler_params=pltpu.CompilerParams(
            dimension_semantics=("arbitrary",), vmem_limit_bytes=VMEM_LIMIT),
        name="post",
    )(x, yp, ya, wo, g2, wr, br)


def _route_kernel(lg_ref, code_ref, wts_ref, cnt_ref, carry_ref):
    i = pl.program_id(0)
    tm = lg_ref.shape[0]

    @pl.when(i == 0)
    def _():
        carry_ref[...] = jnp.zeros_like(carry_ref)

    lg = lg_ref[...]
    lane = lax.broadcasted_iota(I32, (tm, LANES), 1)
    lane_f = lane.astype(F32)
    neg_inf = -jnp.inf

    def first_argmax(vals):
        mx = jnp.max(vals, axis=-1, keepdims=True)
        idx = jnp.min(jnp.where(vals == mx, lane_f, float(LANES)), axis=-1, keepdims=True)
        return mx, idx.astype(I32)

    is_group = lane < N_GROUPS
    g_logits = jnp.where(is_group, lg, neg_inf)
    g_max, g_sel = first_argmax(g_logits)
    g_w = 1.0 / jnp.sum(jnp.where(is_group, jnp.exp(lg - g_max), 0.0), axis=-1, keepdims=True)

    lo = N_GROUPS + g_sel * EXPERTS_PER_GROUP
    in_group = (lane >= lo) & (lane < lo + EXPERTS_PER_GROUP)
    e_logits = jnp.where(in_group, lg, neg_inf)
    v1, i1 = first_argmax(e_logits)
    v2, i2 = first_argmax(jnp.where(lane == i1, neg_inf, e_logits))
    e1 = i1 - N_GROUPS
    e2 = i2 - N_GROUPS
    t = jnp.exp(v2 - v1)
    w1 = g_w / (1.0 + t)
    w2 = g_w * t / (1.0 + t)
    wts_ref[...] = jnp.where(lane == 0, w1, jnp.where(lane == 1, w2, 0.0))

    hit1 = lane == e1
    hit2 = lane == e2
    onehot = jnp.where(hit1 | hit2, 1.0, 0.0).astype(BF16)
    r = lax.broadcasted_iota(I32, (tm, tm), 0)
    c = lax.broadcasted_iota(I32, (tm, tm), 1)
    lower = jnp.where(c < r, 1.0, 0.0).astype(BF16)
    before = jnp.dot(lower, onehot, preferred_element_type=F32) + carry_ref[0:1, :]
    carry_new = carry_ref[0:1, :] + jnp.sum(onehot.astype(F32), axis=0, keepdims=True)
    carry_ref[...] = jnp.broadcast_to(carry_new, carry_ref.shape)
    pos1 = jnp.sum(jnp.where(hit1, before, 0.0), axis=-1, keepdims=True).astype(I32)
    pos2 = jnp.sum(jnp.where(hit2, before, 0.0), axis=-1, keepdims=True).astype(I32)
    code1 = e1 * 65536 + pos1
    code2 = e2 * 65536 + pos2
    code_ref[...] = jnp.where(lane == 0, code1, jnp.where(lane == 1, code2, 0))
    cnt_ref[...] = carry_ref[...].astype(I32)


def _route_call(logits):
    T = logits.shape[0]
    tm = TM_ROUTE
    row = pl.BlockSpec((tm, LANES), lambda i: (i, 0))
    return pl.pallas_call(
        _route_kernel,
        grid=(T // tm,),
        in_specs=[row],
        out_specs=[row, row, pl.BlockSpec((8, LANES), lambda i: (0, 0))],
        out_shape=[jax.ShapeDtypeStruct((T, LANES), I32), jax.ShapeDtypeStruct((T, LANES), F32),
                   jax.ShapeDtypeStruct((8, LANES), I32)],
        scratch_shapes=[pltpu.VMEM((8, LANES), F32)],
        compiler_params=pltpu.CompilerParams(
            dimension_semantics=("arbitrary",), vmem_limit_bytes=VMEM_LIMIT),
        name="route",
    )(logits)


def _plan_kernel(code_ref, cnt_ref, unused_hbm, slots_hbm, blke_ref, blkn_ref,
                 slot_ref, start_ref, sem):
    n_assign = code_ref.shape[0]
    n_blocks = blke_ref.shape[0]
    load = pltpu.make_async_copy(unused_hbm, slot_ref, sem)
    load.start()

    def per_expert(e, nb_done):
        cnt = cnt_ref[e]
        nb = lax.shift_right_logical(cnt + (MOE_BLK - 1), MOE_BLK.bit_length() - 1)
        start_ref[e] = nb_done * MOE_BLK

        def per_block(j, _):
            blke_ref[nb_done + j] = e
            blkn_ref[nb_done + j] = jnp.minimum(cnt - j * MOE_BLK, MOE_BLK)
            return 0
        lax.fori_loop(0, nb, per_block, 0)
        return nb_done + nb
    used = lax.fori_loop(0, N_EXPERTS, per_expert, 0)

    last_e = blke_ref[jnp.maximum(used - 1, 0)]

    def tail(b, _):
        blke_ref[b] = last_e
        blkn_ref[b] = 0
        return 0
    lax.fori_loop(used, n_blocks, tail, 0)

    load.wait()

    def place(a, _):
        code = code_ref[a]
        e = lax.shift_right_logical(code, 16)
        slot_ref[start_ref[e] + (code & 0xFFFF)] = a
        return 0
    lax.fori_loop(0, n_assign, place, 0, unroll=PLAN_UNROLL)

    store = pltpu.make_async_copy(slot_ref, slots_hbm, sem)
    store.start()
    store.wait()


def _plan_call(code_flat, counts, n_tab):
    n_assign = code_flat.shape[0]
    n_slots = n_tab * MOE_BLK
    unused = n_assign + (jnp.arange(n_slots, dtype=I32) & (MOE_BLK - 1))
    smem = pl.BlockSpec(memory_space=pltpu.SMEM)
    hbm = pl.BlockSpec(memory_space=pl.ANY)
    return pl.pallas_call(
        _plan_kernel,
        in_specs=[smem, smem, hbm],
        out_specs=[hbm, smem, smem],
        out_shape=[jax.ShapeDtypeStruct((n_slots,), I32),
                   jax.ShapeDtypeStruct((n_tab,), I32),
                   jax.ShapeDtypeStruct((n_tab,), I32)],
        scratch_shapes=[pltpu.SMEM((n_slots,), I32), pltpu.SMEM((N_EXPERTS,), I32),
                        pltpu.SemaphoreType.DMA],
        name="plan",
    )(code_flat, counts, unused)


def _expert_kernel(blke_ref, blkn_ref, slot_g_ref, slot_s_ref, xn_hbm, wg_ref, wu_ref, wd_ref,
                   y_hbm, x0, x1, y0, y1, gsem, ssem):
    xbufs, ybufs = (x0, x1), (y0, y1)
    s = pl.program_id(0)
    n_tok = xn_hbm.shape[0] // PACK_SUBLANES

    def tile(ref, r):
        start = r * PACK_SUBLANES
        if not isinstance(r, int):
            start = pl.multiple_of(start, PACK_SUBLANES)
        return ref.at[pl.ds(start, PACK_SUBLANES)]

    def rows(j):
        n = blkn_ref[jnp.clip(j, 0, blkn_ref.shape[0] - 1)]
        return jnp.where(j >= 0, (n + (ROW_GROUP - 1)) & ~(ROW_GROUP - 1), 0)
    rows0, rows1, rows2, rows3 = rows(s), rows(s - 1), rows(s - 2), rows(s - 3)

    def stage(nxt):
        cur = 1 - nxt
        x_nxt, x_cur, y_nxt, y_cur = xbufs[nxt], xbufs[cur], ybufs[nxt], ybufs[cur]

        def per_group(n_rows, issue_row):
            for g in range(MOE_BLK // ROW_GROUP):
                @pl.when(g * ROW_GROUP < n_rows)
                def _():
                    for r in range(g * ROW_GROUP, (g + 1) * ROW_GROUP):
                        issue_row(r)

        def gather_row(r):
            tok = slot_g_ref[0, 0, r] & (n_tok - 1)
            pltpu.make_async_copy(tile(xn_hbm, tok), tile(x_nxt, r), gsem.at[nxt]).start()

        def scatter_row(r):
            pltpu.make_async_copy(tile(y_nxt, r), tile(y_hbm, slot_s_ref[0, 0, r]),
                                  ssem.at[nxt]).start()

        def wait_rows(src, dst, sem, n_rows):
            n_words = pl.multiple_of(n_rows * PACK_SUBLANES, ROW_GROUP * PACK_SUBLANES)

            @pl.when(n_rows > 0)
            def _():
                pltpu.make_async_copy(src.at[pl.ds(0, n_words)], dst.at[pl.ds(0, n_words)],
                                      sem).wait()

        def compute():
            xb = _load_packed_rows(x_cur, 0, MOE_BLK).astype(BF16)
            a = jnp.dot(xb, wg_ref[0].astype(BF16), preferred_element_type=F32)
            u = jnp.dot(xb, wu_ref[0].astype(BF16), preferred_element_type=F32)
            hmid = (a * jax.nn.sigmoid(a) * u).astype(BF16)
            _store_packed_rows(y_cur, 0, jnp.dot(hmid, wd_ref[0].astype(BF16),
                                                 preferred_element_type=F32))

        wait_rows(xn_hbm, x_cur, gsem.at[cur], rows1)
        wait_rows(y_cur, y_hbm, ssem.at[cur], rows3)

        if nxt == 0:
            @pl.when(s == 0)
            def _():
                x_nxt[...] = jnp.zeros(x_nxt.shape, x_nxt.dtype)
                x_cur[...] = jnp.zeros(x_cur.shape, x_cur.dtype)
                y_nxt[...] = jnp.zeros(y_nxt.shape, y_nxt.dtype)
                dump = pltpu.make_async_copy(
                    y_nxt, y_hbm.at[pl.ds(TOP_K * n_tok * PACK_SUBLANES,
                                          MOE_BLK * PACK_SUBLANES)], ssem.at[nxt])
                dump.start()
                dump.wait()

        per_group(rows0, gather_row)
        per_group(rows2, scatter_row)

        @pl.when(rows1 > 0)
        def _():
            compute()

    for parity in range(2):
        pl.when((s & 1) == parity)(functools.partial(stage, parity))


def _expert_call(blk_e, blk_n, slots, xn, wg, wu, wd):
    T = xn.shape[0] // PACK_SUBLANES
    assert T & (T - 1) == 0, "token id is recovered from the assignment id with a mask"
    n_tab = blk_e.shape[0]
    _, D, F = wg.shape
    packed_block = pltpu.VMEM((MOE_BLK * PACK_SUBLANES, LANES), jnp.uint32)
    prev = lambda j: jnp.maximum(j, 0)
    grid_spec = pltpu.PrefetchScalarGridSpec(
        num_scalar_prefetch=2,
        grid=(n_tab,),
        in_specs=[
            pl.BlockSpec((1, 1, MOE_BLK), lambda s, be, bn: (s, 0, 0), memory_space=pltpu.SMEM),
            pl.BlockSpec((1, 1, MOE_BLK), lambda s, be, bn: (prev(s - 2), 0, 0),
                         memory_space=pltpu.SMEM),
            pl.BlockSpec(memory_space=pl.ANY),
            pl.BlockSpec((1, D, F), lambda s, be, bn: (be[prev(s - 1)], 0, 0)),
            pl.BlockSpec((1, D, F), lambda s, be, bn: (be[prev(s - 1)], 0, 0)),
            pl.BlockSpec((1, F, D), lambda s, be, bn: (be[prev(s - 1)], 0, 0)),
        ],
        out_specs=pl.BlockSpec(memory_space=pl.ANY),
        scratch_shapes=[packed_block] * 4
        + [pltpu.SemaphoreType.DMA((2,)), pltpu.SemaphoreType.DMA((2,))],
    )
    slots3 = slots.reshape(n_tab, 1, MOE_BLK)
    return pl.pallas_call(
        _expert_kernel,
        grid_spec=grid_spec,
        out_shape=jax.ShapeDtypeStruct(((T * TOP_K + MOE_BLK) * PACK_SUBLANES, LANES),
                                       jnp.uint32),
        compiler_params=pltpu.CompilerParams(
            dimension_semantics=("arbitrary",), vmem_limit_bytes=VMEM_LIMIT,
            disable_bounds_checks=True),
        name="experts",
    )(blk_e, blk_n, slots3, slots3, xn, wg, wu, wd)


def _final_kernel(h1_ref, y1_ref, y2_ref, wts_ref, p_ref, wpp_ref, pg_ref, g3_ref, wpg_ref,
                  o_ref):
    w = wts_ref[...]
    tm = h1_ref.shape[0]
    h2 = (h1_ref[...] + w[:, 0:1] * _load_packed_rows(y1_ref, 0, tm)
          + w[:, 1:2] * _load_packed_rows(y2_ref, 0, tm))
    e = _rms(jnp.dot(p_ref[...].astype(BF16), wpp_ref[...], preferred_element_type=F32),
             pg_ref[...])
    gate = jax.nn.sigmoid(jnp.dot(_rms(h2, g3_ref[...]).astype(BF16), wpg_ref[...],
                                  preferred_element_type=F32))
    o_ref[...] = h2 + gate * e


def _final_call(h1, y, wts, p, wpp, pg, g3, wpg):
    T, D = h1.shape
    tm = TM_FINAL
    row = lambda w: pl.BlockSpec((tm, w), lambda i: (i, 0))
    first = pl.BlockSpec((tm * PACK_SUBLANES, LANES), lambda i: (i, 0))
    second = pl.BlockSpec((tm * PACK_SUBLANES, LANES), lambda i: (i + T // tm, 0))
    return pl.pallas_call(
        _final_kernel,
        grid=(T // tm,),
        in_specs=[row(D), first, second, row(LANES), row(PLE_DIM), _const_spec(wpp.shape),
                  _const_spec(pg.shape), _const_spec(g3.shape), _const_spec(wpg.shape)],
        out_specs=row(D),
        out_shape=jax.ShapeDtypeStruct((T, D), F32),
        compiler_params=pltpu.CompilerParams(
            dimension_semantics=("arbitrary",), vmem_limit_bytes=VMEM_LIMIT),
        name="final",
    )(h1, y, y, wts, p, wpp, pg, g3, wpg)


def _rope_lanes(a):
    z = jnp.zeros(a.shape[:-1] + (ROPE_HALF,), a.dtype)
    return jnp.concatenate([a[..., :ROPE_HALF], z, a[..., ROPE_HALF:], z], axis=-1)


def _head_lanes(a):
    return jnp.concatenate([a[..., :QK_NOPE], _rope_lanes(a[..., QK_NOPE:])], axis=-1)


def _score_bound(q_gain, k_gain):
    return Q_PRESCALE * QK_DIM * jnp.max(jnp.abs(q_gain)) * jnp.max(jnp.abs(k_gain))


def kernel(x, p, positions, norm1_gain, w_in, q_a_gain, w_q_b, kv_a_gain, w_kv_b, q_norm_gain, k_norm_gain, w_pool, pool_scale, w_out, norm2_gain, w_router_group, b_router_group, w_router_expert, b_router_expert, w_exp_gate, w_exp_up, w_exp_down, norm3_gain, w_ple_gate, w_ple_proj, ple_norm_gain):
    B, S, D = x.shape
    T = B * S
    assert x.shape[2] == D_MODEL and S % TQ == 0 and S % TM_PRE == 0
    assert T % TM_POST == 0 and T % TM_FINAL == 0 and T % TM_ROUTE == 0
    layer = 0
    row = lambda a: a[layer].reshape(1, -1)

    n_lat = D_POOL + Q_LORA + KV_LORA
    win = jnp.concatenate([w_in[layer][:, :n_lat], _rope_lanes(w_in[layer][:, n_lat:])],
                          axis=1).astype(BF16)
    wq = _head_lanes(w_q_b[layer].reshape(Q_LORA, N_HEADS, QK_DIM)
                     ).reshape(Q_LORA, N_HEADS * HEAD_W).astype(BF16)
    wkv3 = w_kv_b[layer].reshape(KV_LORA, N_HEADS, QK_NOPE + V_DIM)
    wkv = jnp.concatenate([wkv3[..., :QK_NOPE].reshape(KV_LORA, -1),
                           wkv3[..., QK_NOPE:].reshape(KV_LORA, -1)], axis=1).astype(BF16)
    qng = _head_lanes(q_norm_gain[layer]).reshape(1, HEAD_W)
    kng = _head_lanes(k_norm_gain[layer]).reshape(1, HEAD_W)
    inv_freq = ROPE_THETA ** (-jnp.arange(ROPE_HALF, dtype=F32) / ROPE_HALF)
    invf = _rope_lanes(jnp.concatenate([inv_freq, inv_freq])).reshape(1, LANES)
    sgn = _rope_lanes(jnp.concatenate([-jnp.ones((ROPE_HALF,), F32),
                                       jnp.ones((ROPE_HALF,), F32)])).reshape(1, LANES)
    wr = jnp.zeros((D, LANES), F32)
    wr = wr.at[:, :N_GROUPS].set(w_router_group[layer])
    wr = wr.at[:, N_GROUPS:N_GROUPS + N_EXPERTS].set(w_router_expert[layer])
    wr_hi = wr.astype(BF16)
    wr = jnp.concatenate([wr_hi, (wr - wr_hi.astype(F32)).astype(BF16)], axis=1)
    br = jnp.zeros((1, LANES), F32)
    br = br.at[0, :N_GROUPS].set(b_router_group[layer])
    br = br.at[0, N_GROUPS:N_GROUPS + N_EXPERTS].set(b_router_expert[layer])

    bound = _score_bound(q_norm_gain[layer], k_norm_gain[layer])
    bounded = bound <= SCORE_BOUND_LIMIT
    spare = jnp.arange(LANES) == SPARE_ROPE_LANE
    qpad = jnp.where(spare & bounded, -bound, 0.0).astype(F32).reshape(1, LANES)
    kpad = jnp.where(spare, 1.0, 0.0).astype(F32).reshape(1, LANES)
    ypool, q, k, vt = _pre_call(
        x, positions.reshape(B, S, 1), row(norm1_gain), win, row(q_a_gain), wq, row(kv_a_gain),
        wkv, qng, kng, w_pool[layer].astype(BF16), row(pool_scale), invf, sgn, qpad, kpad)
    yattn = _attn_call(bounded.astype(I32).reshape(1), q, k, vt)
    h1, xn2, logits = _post_call(
        x.reshape(T, D), ypool.reshape(T, D_POOL), yattn.reshape(T, D_ATTN),
        w_out[layer].astype(BF16), row(norm2_gain), wr, br)

    n_assign = T * TOP_K
    n_blocks = (n_assign + N_EXPERTS * (MOE_BLK - 1)) // MOE_BLK
    code, wts, counts = _route_call(logits)
    slots, blk_e, blk_n = _plan_call(code[:, :TOP_K].T.reshape(n_assign), counts[0],
                                     n_blocks + MOE_DRAIN_STEPS)
    y = _expert_call(blk_e, blk_n, slots, xn2,
                     w_exp_gate[layer], w_exp_up[layer], w_exp_down[layer])

    out = _final_call(h1, y, wts, p[layer].reshape(T, PLE_DIM),
                      w_ple_proj[layer].astype(BF16), row(ple_norm_gain), row(norm3_gain),
                      w_ple_gate[layer].astype(BF16))
    return out.reshape(B, S, D)
```

```python
import functools
import math

import jax
import jax.numpy as jnp
from jax import lax
from jax.experimental import pallas as pl
from jax.experimental.pallas import tpu as pltpu

F32 = jnp.float32
BF16 = jnp.bfloat16
I32 = jnp.int32

D_MODEL = 2048
PLE_DIM = 256
EPS = 1e-6
POOL_WINDOWS = (2, 4, 8, 16)
POOL_CH = 256
D_POOL = POOL_CH * len(POOL_WINDOWS)
N_HEADS = 8
Q_LORA = 512
KV_LORA = 512
QK_NOPE = 128
QK_ROPE = 64
QK_DIM = QK_NOPE + QK_ROPE
V_DIM = 128
D_ATTN = N_HEADS * V_DIM
ROPE_THETA = 10000.0
ATTN_SCALE = 1.0 / math.sqrt(QK_DIM)
Q_PRESCALE = ATTN_SCALE * math.log2(math.e)
N_GROUPS = 8
EXPERTS_PER_GROUP = 8
N_EXPERTS = N_GROUPS * EXPERTS_PER_GROUP
TOP_K = 2
D_EXPERT = 512

LANES = 128
PACK_SUBLANES = 8
MXU_DIM = 256
VMEM_LIMIT = 56 * 1024 * 1024

HEAD_W = 2 * LANES
ROPE_HALF = QK_ROPE // 2
POOL_HALO = 16
TM_PRE = 512
PRE_SUB = 256
TM_POST = 512
POST_SUB = 256
TM_FINAL = 256
TQ = 512
TK = PRE_SUB
TM_ROUTE = 256
MOE_BLK = 256
MOE_DRAIN_STEPS = 3
ROW_GROUP = 32
PLAN_UNROLL = 16
NEG_BIG = -1e30
V_ROWS = V_DIM + 16
SPARE_ROPE_LANE = ROPE_HALF
SCORE_BOUND_LIMIT = 50.0


def _const_spec(shape):
    nd = len(shape)
    return pl.BlockSpec(shape, lambda *_: (0,) * nd, pipeline_mode=pl.Buffered(1))


def _rms(x, gain):
    return x * lax.rsqrt(jnp.mean(x * x, axis=-1, keepdims=True) + EPS) * gain


def _store_packed_rows(ref, first_row, x):
    rows, d = x.shape
    half = d // 2
    assert half == PACK_SUBLANES * LANES
    for i in range(PACK_SUBLANES):
        lo = x[:, i * LANES:(i + 1) * LANES].astype(BF16).astype(F32)
        hi = x[:, half + i * LANES:half + (i + 1) * LANES].astype(BF16).astype(F32)
        word = (lax.shift_right_logical(pltpu.bitcast(lo, jnp.uint32), jnp.uint32(16))
                | (pltpu.bitcast(hi, jnp.uint32) & jnp.uint32(0xFFFF0000)))
        ref[pl.ds(first_row * PACK_SUBLANES + i, rows, stride=PACK_SUBLANES), :] = word


def _load_packed_rows(ref, first_row, rows):
    lo, hi = [], []
    for i in range(PACK_SUBLANES):
        word = ref[pl.ds(first_row * PACK_SUBLANES + i, rows, stride=PACK_SUBLANES), :]
        lo.append(pltpu.bitcast(lax.shift_left(word, jnp.uint32(16)), F32))
        hi.append(pltpu.bitcast(word & jnp.uint32(0xFFFF0000), F32))
    return jnp.concatenate(lo + hi, axis=1)


def _pre_kernel(x_ref, pos_ref, g1_ref, win_ref, qag_ref, wq_ref, kvag_ref, wkv_ref,
                qng_ref, kng_ref, wpool_ref, pscale_ref, invf_ref, sgn_ref, qpad_ref, kpad_ref,
                ypool_ref, q_ref, k_ref, vt_ref, carry_ref):
    st = pl.program_id(1)

    @pl.when(st == 0)
    def _():
        carry_ref[...] = jnp.zeros_like(carry_ref)

    halo = carry_ref[...]
    for i in range(x_ref.shape[0] // PRE_SUB):
        halo = _pre_rows(i, st * x_ref.shape[0] + i * PRE_SUB, halo,
                         x_ref, pos_ref, g1_ref, win_ref, qag_ref, wq_ref, kvag_ref, wkv_ref,
                         qng_ref, kng_ref, wpool_ref, pscale_ref, invf_ref, sgn_ref, qpad_ref, kpad_ref,
                         ypool_ref, q_ref, k_ref, vt_ref)
    carry_ref[...] = halo


def _pre_rows(i, first_pos, halo, x_ref, pos_ref, g1_ref, win_ref, qag_ref, wq_ref, kvag_ref,
              wkv_ref, qng_ref, kng_ref, wpool_ref, pscale_ref, invf_ref, sgn_ref, qpad_ref, kpad_ref,
              ypool_ref, q_ref, k_ref, vt_ref):
    tm = PRE_SUB
    rs = pl.ds(i * tm, tm)

    hn = _rms(x_ref[rs, :], g1_ref[...])
    z = jnp.dot(hn.astype(BF16), win_ref[...], preferred_element_type=F32)

    u = z[:, :D_POOL]
    ext = jnp.concatenate([halo, u], axis=0)
    row = lax.broadcasted_iota(I32, (tm, 1), 0) + first_pos
    level = ext
    shift = 1
    for g, w in enumerate(POOL_WINDOWS):
        sl = slice(g * POOL_CH, (g + 1) * POOL_CH)
        while shift < w:
            level = level + pltpu.roll(level, shift, 0)
            shift *= 2
        win_sum = level[POOL_HALO:, sl]
        cnt = jnp.minimum(row + 1, w).astype(F32)
        d = win_sum / cnt - u[:, sl]
        y = jnp.dot(d.astype(BF16), wpool_ref[g], preferred_element_type=F32)
        ypool_ref[rs, sl] = (y * pscale_ref[:, sl]).astype(BF16)

    q_lat = z[:, D_POOL:D_POOL + Q_LORA]
    kv_lat = z[:, D_POOL + Q_LORA:D_POOL + Q_LORA + KV_LORA]
    k_rope = z[:, D_POOL + Q_LORA + KV_LORA:]
    qa = jnp.dot(_rms(q_lat, qag_ref[...]).astype(BF16), wq_ref[...],
                 preferred_element_type=F32)
    kv = jnp.dot(_rms(kv_lat, kvag_ref[...]).astype(BF16), wkv_ref[...],
                 preferred_element_type=F32)

    ang = pos_ref[rs, :].astype(F32) * invf_ref[...]
    cos = jnp.cos(ang)
    sin = jnp.sin(ang) * sgn_ref[...]

    def rot(t):
        return t * cos + pltpu.roll(t, LANES // 2, 1) * sin

    qng = qng_ref[...]
    kng = kng_ref[...]
    kr_rot = rot(k_rope * kng[:, LANES:])
    kr_ssq = jnp.sum(k_rope * k_rope, axis=-1, keepdims=True)
    for h in range(N_HEADS):
        qh = qa[:, h * HEAD_W:(h + 1) * HEAD_W]
        rq = lax.rsqrt(jnp.sum(qh * qh, axis=-1, keepdims=True) / QK_DIM + EPS) * Q_PRESCALE
        qn = qh * rq * qng
        q_ref[0, h, rs, :LANES] = qn[:, :LANES].astype(BF16)
        q_ref[0, h, rs, LANES:] = (rot(qn[:, LANES:]) + qpad_ref[...]).astype(BF16)
        kh = kv[:, h * QK_NOPE:(h + 1) * QK_NOPE]
        rk = lax.rsqrt((jnp.sum(kh * kh, axis=-1, keepdims=True) + kr_ssq) / QK_DIM + EPS)
        k_ref[0, h, rs, :LANES] = (kh * rk * kng[:, :LANES]).astype(BF16)
        k_ref[0, h, rs, LANES:] = (kr_rot * rk + kpad_ref[...]).astype(BF16)
        vh = kv[:, N_HEADS * QK_NOPE + h * V_DIM:N_HEADS * QK_NOPE + (h + 1) * V_DIM]
        vt_ref[0, h, i, :V_DIM, :] = vh.T.astype(BF16)
        vt_ref[0, h, i, V_DIM:, :] = jnp.ones((V_ROWS - V_DIM, tm), BF16)
    return u[tm - POOL_HALO:, :]


def _pre_call(x, pos, g1, win, qag, wq, kvag, wkv, qng, kng, wpool, pscale, invf, sgn, qpad,
              kpad):
    B, S, D = x.shape
    tm = TM_PRE
    grid = (B, S // tm)
    row_spec = lambda w: pl.BlockSpec((None, tm, w), lambda b, s: (b, s, 0))
    head_spec = lambda w: pl.BlockSpec((1, N_HEADS, tm, w), lambda b, s: (b, 0, s, 0))
    consts = [g1, win, qag, wq, kvag, wkv, qng, kng, wpool, pscale, invf, sgn, qpad, kpad]
    return pl.pallas_call(
        _pre_kernel,
        grid=grid,
        in_specs=[row_spec(D), row_spec(1)] + [_const_spec(c.shape) for c in consts],
        out_specs=[row_spec(D_POOL), head_spec(HEAD_W), head_spec(HEAD_W),
                   pl.BlockSpec((1, N_HEADS, tm // PRE_SUB, V_ROWS, PRE_SUB),
                                lambda b, s: (b, 0, s, 0, 0))],
        out_shape=[jax.ShapeDtypeStruct((B, S, D_POOL), BF16),
                   jax.ShapeDtypeStruct((B, N_HEADS, S, HEAD_W), BF16),
                   jax.ShapeDtypeStruct((B, N_HEADS, S, HEAD_W), BF16),
                   jax.ShapeDtypeStruct((B, N_HEADS, S // PRE_SUB, V_ROWS, PRE_SUB), BF16)],
        scratch_shapes=[pltpu.VMEM((POOL_HALO, D_POOL), F32)],
        compiler_params=pltpu.CompilerParams(
            dimension_semantics=("arbitrary", "arbitrary"), vmem_limit_bytes=VMEM_LIMIT),
        name="pre",
    )(x, pos, *consts)


def _attn_kernel(bounded_ref, q_ref, k_ref, vt_ref, o_ref, *chain_scratch):
    S = q_ref.shape[2]
    nq = S // TQ
    per_q = TQ // TK
    assert per_q == 2, "two online-softmax chains take the even / odd key tiles"
    chains = (chain_scratch[0:2], chain_scratch[2:4])
    score_bufs = (chain_scratch[4:6], chain_scratch[6:8])
    key_idx = lax.broadcasted_iota(I32, (TK, TQ), 0)
    qry_idx = lax.broadcasted_iota(I32, (TK, TQ), 1)

    def q_tile(bounded, qi, _):
        q = q_ref[0, 0, pl.ds(pl.multiple_of(qi * TQ, TQ), TQ), :]
        for m_ref, acc_ref in chains:
            m_ref[...] = jnp.full(m_ref.shape, NEG_BIG, F32)
            acc_ref[...] = jnp.zeros(acc_ref.shape, F32)

        def scores(kt):
            k = k_ref[0, 0, pl.ds(pl.multiple_of(kt * TK, TK), TK), :]
            return lax.dot_general(k, q, (((1,), (1,)), ((), ())), preferred_element_type=F32)

        def weighted_values(kt, p):
            return jnp.dot(vt_ref[0, 0, kt], p.astype(BF16), preferred_element_type=F32)

        def fold(chain, st, kt, mask):
            m_ref, acc_ref = chain
            if mask is not None:
                st = jnp.where(mask, st, NEG_BIG)
            m = m_ref[...]
            m_new = jnp.maximum(m, jnp.max(st, axis=0, keepdims=True))
            m_ref[...] = m_new
            acc_ref[...] = (jnp.exp2(m - m_new) * acc_ref[...]
                            + weighted_values(kt, jnp.exp2(st - m_new)))

        def pair_scores(j, buf):
            for c in range(per_q):
                buf[c][...] = scores(per_q * j + c)

        def fold_pair(j, buf, masks=(None, None)):
            if bounded:
                total = None
                for c in range(per_q):
                    st = buf[c][...]
                    if masks[c] is not None:
                        st = jnp.where(masks[c], st, NEG_BIG)
                    pv = weighted_values(per_q * j + c, jnp.exp2(st))
                    total = pv if total is None else total + pv
                chains[0][1][...] += total
            else:
                for c in range(per_q):
                    fold(chains[c], buf[c][...], per_q * j + c, masks[c])

        diag_masks = (key_idx <= qry_idx, key_idx + TK <= qry_idx)
        pair_scores(0, score_bufs[0])

        def two_pairs(i, _):
            pair_scores(2 * i + 1, score_bufs[1])
            fold_pair(2 * i, score_bufs[0])
            pair_scores(2 * i + 2, score_bufs[0])
            fold_pair(2 * i + 1, score_bufs[1])
            return 0
        lax.fori_loop(0, qi // 2, two_pairs, 0)

        @pl.when(qi % 2 == 0)
        def _():
            fold_pair(qi, score_bufs[0], diag_masks)

        @pl.when(qi % 2 == 1)
        def _():
            pair_scores(qi, score_bufs[1])
            fold_pair(qi - 1, score_bufs[0])
            fold_pair(qi, score_bufs[1], diag_masks)

        (m0, acc0), (m1, acc1) = chains
        if bounded:
            acc = acc0[...]
        else:
            m = jnp.maximum(m0[...], m1[...])
            acc = jnp.exp2(m0[...] - m) * acc0[...] + jnp.exp2(m1[...] - m) * acc1[...]
        out = acc[:V_DIM, :] / acc[V_DIM:V_DIM + 1, :]
        o_ref[0, pl.ds(pl.multiple_of(qi * TQ, TQ), TQ), :] = out.T.astype(BF16)
        return 0

    for bounded in (True, False):
        @pl.when((bounded_ref[0] != 0) == bounded)
        def _():
            lax.fori_loop(0, nq, functools.partial(q_tile, bounded), 0)


def _attn_call(bounded, q, k, vt):
    B, H, S, _ = q.shape
    head = lambda w: pl.BlockSpec((1, 1, S, w), lambda b, h, flag: (b, h, 0, 0))
    grid_spec = pltpu.PrefetchScalarGridSpec(
        num_scalar_prefetch=1,
        grid=(B, H),
        in_specs=[head(HEAD_W), head(HEAD_W),
                  pl.BlockSpec((1, 1) + vt.shape[2:], lambda b, h, flag: (b, h, 0, 0, 0))],
        out_specs=pl.BlockSpec((1, S, V_DIM), lambda b, h, flag: (b, 0, h)),
        scratch_shapes=[pltpu.VMEM((1, TQ), F32), pltpu.VMEM((V_ROWS, TQ), F32)] * 2
        + [pltpu.VMEM((TK, TQ), F32)] * 4,
    )
    return pl.pallas_call(
        _attn_kernel,
        grid_spec=grid_spec,
        out_shape=jax.ShapeDtypeStruct((B, S, H * V_DIM), BF16),
        compiler_params=pltpu.CompilerParams(
            dimension_semantics=("arbitrary", "arbitrary"), vmem_limit_bytes=VMEM_LIMIT),
        name="attn",
    )(bounded, q, k, vt)


def _post_kernel(x_ref, yp_ref, ya_ref, wo_ref, g2_ref, wr_ref, br_ref,
                 h1_ref, xn_ref, lg_ref):
    for i in range(x_ref.shape[0] // POST_SUB):
        rs = pl.ds(i * POST_SUB, POST_SUB)
        h1 = (x_ref[rs, :]
              + jnp.dot(yp_ref[rs, :], wo_ref[:D_POOL, :], preferred_element_type=F32)
              + jnp.dot(ya_ref[rs, :], wo_ref[D_POOL:, :], preferred_element_type=F32))
        h1_ref[rs, :] = h1
        xn = _rms(h1, g2_ref[...])
        _store_packed_rows(xn_ref, i * POST_SUB, xn)
        xn_hi = xn.astype(BF16)
        xn_lo = (xn - xn_hi.astype(F32)).astype(BF16)
        both = (jnp.dot(xn_hi, wr_ref[...], preferred_element_type=F32)
                + jnp.dot(xn_lo, wr_ref[...], preferred_element_type=F32))
        lg_ref[rs, :] = both[:, :LANES] + both[:, LANES:] + br_ref[...]


def _post_call(x, yp, ya, wo, g2, wr, br):
    T, D = x.shape
    tm = TM_POST
    row = lambda w: pl.BlockSpec((tm, w), lambda i: (i, 0))
    return pl.pallas_call(
        _post_kernel,
        grid=(T // tm,),
        in_specs=[row(D), row(D_POOL), row(D_ATTN), _const_spec(wo.shape),
                  _const_spec(g2.shape), _const_spec(wr.shape), _const_spec(br.shape)],
        out_specs=[row(D), pl.BlockSpec((tm * PACK_SUBLANES, LANES), lambda i: (i, 0)),
                   row(LANES)],
        out_shape=[jax.ShapeDtypeStruct((T, D), F32),
                   jax.ShapeDtypeStruct((T * PACK_SUBLANES, LANES), jnp.uint32),
                   jax.ShapeDtypeStruct((T, LANES), F32)],
        compiler_params=pltpu.CompilerParams(
            dimension_semantics=("arbitrary",), vmem_limit_bytes=VMEM_LIMIT),
        name="post",
    )(x, yp, ya, wo, g2, wr, br)


def _route_kernel(lg_ref, code_ref, wts_ref, cnt_ref, carry_ref):
    i = pl.program_id(0)
    tm = lg_ref.shape[0]

    @pl.when(i == 0)
    def _():
        carry_ref[...] = jnp.zeros_like(carry_ref)

    lg = lg_ref[...]
    lane = lax.broadcasted_iota(I32, (tm, LANES), 1)
    lane_f = lane.astype(F32)
    neg_inf = -jnp.inf

    def first_argmax(vals):
        mx = jnp.max(vals, axis=-1, keepdims=True)
        idx = jnp.min(jnp.where(vals == mx, lane_f, float(LANES)), axis=-1, keepdims=True)
        return mx, idx.astype(I32)

    is_group = lane < N_GROUPS
    g_logits = jnp.where(is_group, lg, neg_inf)
    g_max, g_sel = first_argmax(g_logits)
    g_w = 1.0 / jnp.sum(jnp.where(is_group, jnp.exp(lg - g_max), 0.0), axis=-1, keepdims=True)

    lo = N_GROUPS + g_sel * EXPERTS_PER_GROUP
    in_group = (lane >= lo) & (lane < lo + EXPERTS_PER_GROUP)
    e_logits = jnp.where(in_group, lg, neg_inf)
    v1, i1 = first_argmax(e_logits)
    v2, i2 = first_argmax(jnp.where(lane == i1, neg_inf, e_logits))
    e1 = i1 - N_GROUPS
    e2 = i2 - N_GROUPS
    t = jnp.exp(v2 - v1)
    w1 = g_w / (1.0 + t)
    w2 = g_w * t / (1.0 + t)
    wts_ref[...] = jnp.where(lane == 0, w1, jnp.where(lane == 1, w2, 0.0))

    hit1 = lane == e1
    hit2 = lane == e2
    onehot = jnp.where(hit1 | hit2, 1.0, 0.0).astype(BF16)
    r = lax.broadcasted_iota(I32, (tm, tm), 0)
    c = lax.broadcasted_iota(I32, (tm, tm), 1)
    lower = jnp.where(c < r, 1.0, 0.0).astype(BF16)
    before = jnp.dot(lower, onehot, preferred_element_type=F32) + carry_ref[0:1, :]
    carry_new = carry_ref[0:1, :] + jnp.sum(onehot.astype(F32), axis=0, keepdims=True)
    carry_ref[...] = jnp.broadcast_to(carry_new, carry_ref.shape)
    pos1 = jnp.sum(jnp.where(hit1, before, 0.0), axis=-1, keepdims=True).astype(I32)
    pos2 = jnp.sum(jnp.where(hit2, before, 0.0), axis=-1, keepdims=True).astype(I32)
    code1 = e1 * 65536 + pos1
    code2 = e2 * 65536 + pos2
    code_ref[...] = jnp.where(lane == 0, code1, jnp.where(lane == 1, code2, 0))
    cnt_ref[...] = carry_ref[...].astype(I32)


def _route_call(logits):
    T = logits.shape[0]
    tm = TM_ROUTE
    row = pl.BlockSpec((tm, LANES), lambda i: (i, 0))
    return pl.pallas_call(
        _route_kernel,
        grid=(T // tm,),
        in_specs=[row],
        out_specs=[row, row, pl.BlockSpec((8, LANES), lambda i: (0, 0))],
        out_shape=[jax.ShapeDtypeStruct((T, LANES), I32), jax.ShapeDtypeStruct((T, LANES), F32),
                   jax.ShapeDtypeStruct((8, LANES), I32)],
        scratch_shapes=[pltpu.VMEM((8, LANES), F32)],
        compiler_params=pltpu.CompilerParams(
            dimension_semantics=("arbitrary",), vmem_limit_bytes=VMEM_LIMIT),
        name="route",
    )(logits)


def _plan_kernel(code_ref, cnt_ref, unused_hbm, slots_hbm, blke_ref, blkn_ref, blkw_ref,
                 slot_ref, start_ref, sem):
    n_assign = code_ref.shape[0]
    n_blocks = blke_ref.shape[0]
    load = pltpu.make_async_copy(unused_hbm, slot_ref, sem)
    load.start()

    def per_expert(e, nb_done):
        cnt = cnt_ref[e]
        nb = lax.shift_right_logical(cnt + (MOE_BLK - 1), MOE_BLK.bit_length() - 1)
        start_ref[e] = nb_done * MOE_BLK

        def per_block(j, _):
            blke_ref[nb_done + j] = e
            blkn_ref[nb_done + j] = jnp.minimum(cnt - j * MOE_BLK, MOE_BLK)
            return 0
        lax.fori_loop(0, nb, per_block, 0)
        return nb_done + nb
    used = lax.fori_loop(0, N_EXPERTS, per_expert, 0)

    last_e = blke_ref[jnp.maximum(used - 1, 0)]

    def tail(b, _):
        blke_ref[b] = last_e
        blkn_ref[b] = 0
        blkw_ref[b] = 0
        return 0
    lax.fori_loop(used, n_blocks, tail, 0)

    def runs_backward(i, carry):
        later_e, next_e = carry
        b = used - 1 - i
        e = blke_ref[b]
        next_e = jnp.where(e != later_e, later_e, next_e)
        blkw_ref[b] = (next_e + 1) * 4
        return e, next_e
    lax.fori_loop(0, used, runs_backward, (last_e, -1))

    def runs_forward(b, carry):
        earlier_e, run = carry
        e = blke_ref[b]
        first = (e != earlier_e).astype(I32)
        run = run + first
        blkw_ref[b] = blkw_ref[b] + first + (run & 1) * 2
        return e, run
    lax.fori_loop(0, used, runs_forward, (-1, -1))

    load.wait()

    def place(a, _):
        code = code_ref[a]
        e = lax.shift_right_logical(code, 16)
        slot_ref[start_ref[e] + (code & 0xFFFF)] = a
        return 0
    lax.fori_loop(0, n_assign, place, 0, unroll=PLAN_UNROLL)

    store = pltpu.make_async_copy(slot_ref, slots_hbm, sem)
    store.start()
    store.wait()


def _plan_call(code_flat, counts, n_tab):
    n_assign = code_flat.shape[0]
    n_slots = n_tab * MOE_BLK
    unused = n_assign + (jnp.arange(n_slots, dtype=I32) & (MOE_BLK - 1))
    smem = pl.BlockSpec(memory_space=pltpu.SMEM)
    hbm = pl.BlockSpec(memory_space=pl.ANY)
    return pl.pallas_call(
        _plan_kernel,
        in_specs=[smem, smem, hbm],
        out_specs=[hbm, smem, smem, smem],
        out_shape=[jax.ShapeDtypeStruct((n_slots,), I32),
                   jax.ShapeDtypeStruct((n_tab,), I32),
                   jax.ShapeDtypeStruct((n_tab,), I32),
                   jax.ShapeDtypeStruct((n_tab,), I32)],
        scratch_shapes=[pltpu.SMEM((n_slots,), I32), pltpu.SMEM((N_EXPERTS,), I32),
                        pltpu.SemaphoreType.DMA],
        name="plan",
    )(code_flat, counts, unused)


def _expert_kernel(blke_ref, blkn_ref, blkw_ref, slot_g_ref, slot_s_ref, xn_hbm,
                   wg_hbm, wu_hbm, wd_hbm, y_hbm, x0, x1, y0, y1, wg_buf, wu_buf, wd_buf,
                   gsem, ssem, wsem):
    xbufs, ybufs = (x0, x1), (y0, y1)
    s = pl.program_id(0)
    n_tok = xn_hbm.shape[0] // PACK_SUBLANES

    def weight_copies(e, slot):
        return [pltpu.make_async_copy(hbm.at[e], buf.at[slot], wsem.at[slot])
                for hbm, buf in ((wg_hbm, wg_buf), (wu_hbm, wu_buf), (wd_hbm, wd_buf))]

    blk = jnp.maximum(s - 1, 0)
    run_info = blkw_ref[blk]
    w_slot = (run_info >> 1) & 1
    next_expert = (run_info >> 2) - 1

    def tile(ref, r):
        start = r * PACK_SUBLANES
        if not isinstance(r, int):
            start = pl.multiple_of(start, PACK_SUBLANES)
        return ref.at[pl.ds(start, PACK_SUBLANES)]

    def rows(j):
        n = blkn_ref[jnp.clip(j, 0, blkn_ref.shape[0] - 1)]
        return jnp.where(j >= 0, (n + (ROW_GROUP - 1)) & ~(ROW_GROUP - 1), 0)
    rows0, rows1, rows2, rows3 = rows(s), rows(s - 1), rows(s - 2), rows(s - 3)

    def stage(nxt):
        cur = 1 - nxt
        x_nxt, x_cur, y_nxt, y_cur = xbufs[nxt], xbufs[cur], ybufs[nxt], ybufs[cur]

        def per_group(n_rows, issue_row):
            for g in range(MOE_BLK // ROW_GROUP):
                @pl.when(g * ROW_GROUP < n_rows)
                def _():
                    for r in range(g * ROW_GROUP, (g + 1) * ROW_GROUP):
                        issue_row(r)

        def gather_row(r):
            tok = slot_g_ref[0, 0, r] & (n_tok - 1)
            pltpu.make_async_copy(tile(xn_hbm, tok), tile(x_nxt, r), gsem.at[nxt]).start()

        def scatter_row(r):
            pltpu.make_async_copy(tile(y_nxt, r), tile(y_hbm, slot_s_ref[0, 0, r]),
                                  ssem.at[nxt]).start()

        def wait_rows(src, dst, sem, n_rows):
            n_words = pl.multiple_of(n_rows * PACK_SUBLANES, ROW_GROUP * PACK_SUBLANES)

            @pl.when(n_rows > 0)
            def _():
                pltpu.make_async_copy(src.at[pl.ds(0, n_words)], dst.at[pl.ds(0, n_words)],
                                      sem).wait()

        def compute():
            xb = _load_packed_rows(x_cur, 0, MOE_BLK).astype(BF16)
            a = jnp.dot(xb, wg_buf[w_slot].astype(BF16), preferred_element_type=F32)
            u = jnp.dot(xb, wu_buf[w_slot].astype(BF16), preferred_element_type=F32)
            hmid = (a * jax.nn.sigmoid(a) * u).astype(BF16)
            _store_packed_rows(y_cur, 0, jnp.dot(hmid, wd_buf[w_slot].astype(BF16),
                                                 preferred_element_type=F32))

        wait_rows(xn_hbm, x_cur, gsem.at[cur], rows1)
        wait_rows(y_cur, y_hbm, ssem.at[cur], rows3)

        @pl.when((rows1 > 0) & ((run_info & 1) == 1))
        def _():
            for copy in weight_copies(blke_ref[blk], w_slot):
                copy.wait()

            @pl.when(next_expert >= 0)
            def _():
                for copy in weight_copies(next_expert, 1 - w_slot):
                    copy.start()

        if nxt == 0:
            @pl.when((s == 0) & (blkn_ref[0] > 0))
            def _():
                for copy in weight_copies(blke_ref[0], 0):
                    copy.start()

            @pl.when(s == 0)
            def _():
                x_nxt[...] = jnp.zeros(x_nxt.shape, x_nxt.dtype)
                x_cur[...] = jnp.zeros(x_cur.shape, x_cur.dtype)
                y_nxt[...] = jnp.zeros(y_nxt.shape, y_nxt.dtype)
                dump = pltpu.make_async_copy(
                    y_nxt, y_hbm.at[pl.ds(TOP_K * n_tok * PACK_SUBLANES,
                                          MOE_BLK * PACK_SUBLANES)], ssem.at[nxt])
                dump.start()
                dump.wait()

        per_group(rows0, gather_row)
        per_group(rows2, scatter_row)

        @pl.when(rows1 > 0)
        def _():
            compute()

    for parity in range(2):
        pl.when((s & 1) == parity)(functools.partial(stage, parity))


def _expert_call(blk_e, blk_n, blk_w, slots, xn, wg, wu, wd):
    T = xn.shape[0] // PACK_SUBLANES
    assert T & (T - 1) == 0, "token id is recovered from the assignment id with a mask"
    n_tab = blk_e.shape[0]
    _, D, F = wg.shape
    packed_block = pltpu.VMEM((MOE_BLK * PACK_SUBLANES, LANES), jnp.uint32)
    hbm = pl.BlockSpec(memory_space=pl.ANY)
    grid_spec = pltpu.PrefetchScalarGridSpec(
        num_scalar_prefetch=3,
        grid=(n_tab,),
        in_specs=[
            pl.BlockSpec((1, 1, MOE_BLK), lambda s, *_: (s, 0, 0), memory_space=pltpu.SMEM),
            pl.BlockSpec((1, 1, MOE_BLK), lambda s, *_: (jnp.maximum(s - 2, 0), 0, 0),
                         memory_space=pltpu.SMEM),
            hbm, hbm, hbm, hbm,
        ],
        out_specs=hbm,
        scratch_shapes=[packed_block] * 4
        + [pltpu.VMEM((2, D, F), F32), pltpu.VMEM((2, D, F), F32), pltpu.VMEM((2, F, D), F32)]
        + [pltpu.SemaphoreType.DMA((2,))] * 3,
    )
    slots3 = slots.reshape(n_tab, 1, MOE_BLK)
    return pl.pallas_call(
        _expert_kernel,
        grid_spec=grid_spec,
        out_shape=jax.ShapeDtypeStruct(((T * TOP_K + MOE_BLK) * PACK_SUBLANES, LANES),
                                       jnp.uint32),
        compiler_params=pltpu.CompilerParams(
            dimension_semantics=("arbitrary",), vmem_limit_bytes=VMEM_LIMIT,
            disable_bounds_checks=True),
        name="experts",
    )(blk_e, blk_n, blk_w, slots3, slots3, xn, wg, wu, wd)


def _final_kernel(h1_ref, y1_ref, y2_ref, wts_ref, p_ref, wpp_ref, pg_ref, g3_ref, wpg_ref,
                  o_ref):
    w = wts_ref[...]
    tm = h1_ref.shape[0]
    h2 = (h1_ref[...] + w[:, 0:1] * _load_packed_rows(y1_ref, 0, tm)
          + w[:, 1:2] * _load_packed_rows(y2_ref, 0, tm))
    e = _rms(jnp.dot(p_ref[...].astype(BF16), wpp_ref[...], preferred_element_type=F32),
             pg_ref[...])
    gate = jax.nn.sigmoid(jnp.dot(_rms(h2, g3_ref[...]).astype(BF16), wpg_ref[...],
                                  preferred_element_type=F32))
    o_ref[...] = h2 + gate * e


def _final_call(h1, y, wts, p, wpp, pg, g3, wpg):
    T, D = h1.shape
    tm = TM_FINAL
    row = lambda w: pl.BlockSpec((tm, w), lambda i: (i, 0))
    first = pl.BlockSpec((tm * PACK_SUBLANES, LANES), lambda i: (i, 0))
    second = pl.BlockSpec((tm * PACK_SUBLANES, LANES), lambda i: (i + T // tm, 0))
    return pl.pallas_call(
        _final_kernel,
        grid=(T // tm,),
        in_specs=[row(D), first, second, row(LANES), row(PLE_DIM), _const_spec(wpp.shape),
                  _const_spec(pg.shape), _const_spec(g3.shape), _const_spec(wpg.shape)],
        out_specs=row(D),
        out_shape=jax.ShapeDtypeStruct((T, D), F32),
        compiler_params=pltpu.CompilerParams(
            dimension_semantics=("arbitrary",), vmem_limit_bytes=VMEM_LIMIT),
        name="final",
    )(h1, y, y, wts, p, wpp, pg, g3, wpg)


def _rope_lanes(a):
    z = jnp.zeros(a.shape[:-1] + (ROPE_HALF,), a.dtype)
    return jnp.concatenate([a[..., :ROPE_HALF], z, a[..., ROPE_HALF:], z], axis=-1)


def _head_lanes(a):
    return jnp.concatenate([a[..., :QK_NOPE], _rope_lanes(a[..., QK_NOPE:])], axis=-1)


def _score_bound(q_gain, k_gain):
    return Q_PRESCALE * QK_DIM * jnp.max(jnp.abs(q_gain)) * jnp.max(jnp.abs(k_gain))


def kernel(x, p, positions, norm1_gain, w_in, q_a_gain, w_q_b, kv_a_gain, w_kv_b, q_norm_gain, k_norm_gain, w_pool, pool_scale, w_out, norm2_gain, w_router_group, b_router_group, w_router_expert, b_router_expert, w_exp_gate, w_exp_up, w_exp_down, norm3_gain, w_ple_gate, w_ple_proj, ple_norm_gain):
    B, S, D = x.shape
    T = B * S
    assert x.shape[2] == D_MODEL and S % TQ == 0 and S % TM_PRE == 0
    assert T % TM_POST == 0 and T % TM_FINAL == 0 and T % TM_ROUTE == 0
    layer = 0
    row = lambda a: a[layer].reshape(1, -1)

    n_lat = D_POOL + Q_LORA + KV_LORA
    win = jnp.concatenate([w_in[layer][:, :n_lat], _rope_lanes(w_in[layer][:, n_lat:])],
                          axis=1).astype(BF16)
    wq = _head_lanes(w_q_b[layer].reshape(Q_LORA, N_HEADS, QK_DIM)
                     ).reshape(Q_LORA, N_HEADS * HEAD_W).astype(BF16)
    wkv3 = w_kv_b[layer].reshape(KV_LORA, N_HEADS, QK_NOPE + V_DIM)
    wkv = jnp.concatenate([wkv3[..., :QK_NOPE].reshape(KV_LORA, -1),
                           wkv3[..., QK_NOPE:].reshape(KV_LORA, -1)], axis=1).astype(BF16)
    qng = _head_lanes(q_norm_gain[layer]).reshape(1, HEAD_W)
    kng = _head_lanes(k_norm_gain[layer]).reshape(1, HEAD_W)
    inv_freq = ROPE_THETA ** (-jnp.arange(ROPE_HALF, dtype=F32) / ROPE_HALF)
    invf = _rope_lanes(jnp.concatenate([inv_freq, inv_freq])).reshape(1, LANES)
    sgn = _rope_lanes(jnp.concatenate([-jnp.ones((ROPE_HALF,), F32),
                                       jnp.ones((ROPE_HALF,), F32)])).reshape(1, LANES)
    wr = jnp.zeros((D, LANES), F32)
    wr = wr.at[:, :N_GROUPS].set(w_router_group[layer])
    wr = wr.at[:, N_GROUPS:N_GROUPS + N_EXPERTS].set(w_router_expert[layer])
    wr_hi = wr.astype(BF16)
    wr = jnp.concatenate([wr_hi, (wr - wr_hi.astype(F32)).astype(BF16)], axis=1)
    br = jnp.zeros((1, LANES), F32)
    br = br.at[0, :N_GROUPS].set(b_router_group[layer])
    br = br.at[0, N_GROUPS:N_GROUPS + N_EXPERTS].set(b_router_expert[layer])

    bound = _score_bound(q_norm_gain[layer], k_norm_gain[layer])
    bounded = bound <= SCORE_BOUND_LIMIT
    spare = jnp.arange(LANES) == SPARE_ROPE_LANE
    qpad = jnp.where(spare & bounded, -bound, 0.0).astype(F32).reshape(1, LANES)
    kpad = jnp.where(spare, 1.0, 0.0).astype(F32).reshape(1, LANES)
    ypool, q, k, vt = _pre_call(
        x, positions.reshape(B, S, 1), row(norm1_gain), win, row(q_a_gain), wq, row(kv_a_gain),
        wkv, qng, kng, w_pool[layer].astype(BF16), row(pool_scale), invf, sgn, qpad, kpad)
    yattn = _attn_call(bounded.astype(I32).reshape(1), q, k, vt)
    h1, xn2, logits = _post_call(
        x.reshape(T, D), ypool.reshape(T, D_POOL), yattn.reshape(T, D_ATTN),
        w_out[layer].astype(BF16), row(norm2_gain), wr, br)

    n_assign = T * TOP_K
    n_blocks = (n_assign + N_EXPERTS * (MOE_BLK - 1)) // MOE_BLK
    code, wts, counts = _route_call(logits)
    slots, blk_e, blk_n, blk_w = _plan_call(code[:, :TOP_K].T.reshape(n_assign), counts[0],
                                            n_blocks + MOE_DRAIN_STEPS)
    y = _expert_call(blk_e, blk_n, blk_w, slots, xn2,
                     w_exp_gate[layer], w_exp_up[layer], w_exp_down[layer])

    out = _final_call(h1, y, wts, p[layer].reshape(T, PLE_DIM),
                      w_ple_proj[layer].astype(BF16), row(ple_norm_gain), row(norm3_gain),
                      w_ple_gate[layer].astype(BF16))
    return out.reshape(B, S, D)
```

```python
import functools
import math

import jax
import jax.numpy as jnp
from jax import lax
from jax.experimental import pallas as pl
from jax.experimental.pallas import tpu as pltpu

F32 = jnp.float32
BF16 = jnp.bfloat16
I32 = jnp.int32

D_MODEL = 2048
PLE_DIM = 256
EPS = 1e-6
POOL_WINDOWS = (2, 4, 8, 16)
POOL_CH = 256
D_POOL = POOL_CH * len(POOL_WINDOWS)
N_HEADS = 8
Q_LORA = 512
KV_LORA = 512
QK_NOPE = 128
QK_ROPE = 64
QK_DIM = QK_NOPE + QK_ROPE
V_DIM = 128
D_ATTN = N_HEADS * V_DIM
ROPE_THETA = 10000.0
ATTN_SCALE = 1.0 / math.sqrt(QK_DIM)
Q_PRESCALE = ATTN_SCALE * math.log2(math.e)
N_GROUPS = 8
EXPERTS_PER_GROUP = 8
N_EXPERTS = N_GROUPS * EXPERTS_PER_GROUP
TOP_K = 2
D_EXPERT = 512

LANES = 128
PACK_SUBLANES = 8
MXU_DIM = 256
VMEM_LIMIT = 56 * 1024 * 1024

HEAD_W = 2 * LANES
ROPE_HALF = QK_ROPE // 2
POOL_HALO = 16
TM_PRE = 512
PRE_SUB = 256
TM_POST = 512
POST_SUB = 256
TM_FINAL = 256
TQ = 512
TK = PRE_SUB
TM_ROUTE = 256
MOE_BLK = 256
MOE_DRAIN_STEPS = 3
ROW_GROUP = 32
WEIGHT_DMA_PRIORITY = 1
PLAN_UNROLL = 16
NEG_BIG = -1e30
V_ROWS = V_DIM + 16
SPARE_ROPE_LANE = ROPE_HALF
SCORE_BOUND_LIMIT = 50.0


def _const_spec(shape):
    nd = len(shape)
    return pl.BlockSpec(shape, lambda *_: (0,) * nd, pipeline_mode=pl.Buffered(1))


def _rms(x, gain):
    return x * lax.rsqrt(jnp.mean(x * x, axis=-1, keepdims=True) + EPS) * gain


def _store_packed_rows(ref, first_row, x):
    rows, d = x.shape
    half = d // 2
    assert half == PACK_SUBLANES * LANES
    for i in range(PACK_SUBLANES):
        lo = x[:, i * LANES:(i + 1) * LANES].astype(BF16).astype(F32)
        hi = x[:, half + i * LANES:half + (i + 1) * LANES].astype(BF16).astype(F32)
        word = (lax.shift_right_logical(pltpu.bitcast(lo, jnp.uint32), jnp.uint32(16))
                | (pltpu.bitcast(hi, jnp.uint32) & jnp.uint32(0xFFFF0000)))
        ref[pl.ds(first_row * PACK_SUBLANES + i, rows, stride=PACK_SUBLANES), :] = word


def _load_packed_rows(ref, first_row, rows):
    lo, hi = [], []
    for i in range(PACK_SUBLANES):
        word = ref[pl.ds(first_row * PACK_SUBLANES + i, rows, stride=PACK_SUBLANES), :]
        lo.append(pltpu.bitcast(lax.shift_left(word, jnp.uint32(16)), F32))
        hi.append(pltpu.bitcast(word & jnp.uint32(0xFFFF0000), F32))
    return jnp.concatenate(lo + hi, axis=1)


def _pre_kernel(x_ref, pos_ref, g1_ref, win_ref, qag_ref, wq_ref, kvag_ref, wkv_ref,
                qng_ref, kng_ref, wpool_ref, pscale_ref, invf_ref, sgn_ref, qpad_ref, kpad_ref,
                ypool_ref, q_ref, k_ref, vt_ref, carry_ref):
    st = pl.program_id(1)

    @pl.when(st == 0)
    def _():
        carry_ref[...] = jnp.zeros_like(carry_ref)

    halo = carry_ref[...]
    for i in range(x_ref.shape[0] // PRE_SUB):
        halo = _pre_rows(i, st * x_ref.shape[0] + i * PRE_SUB, halo,
                         x_ref, pos_ref, g1_ref, win_ref, qag_ref, wq_ref, kvag_ref, wkv_ref,
                         qng_ref, kng_ref, wpool_ref, pscale_ref, invf_ref, sgn_ref, qpad_ref, kpad_ref,
                         ypool_ref, q_ref, k_ref, vt_ref)
    carry_ref[...] = halo


def _pre_rows(i, first_pos, halo, x_ref, pos_ref, g1_ref, win_ref, qag_ref, wq_ref, kvag_ref,
              wkv_ref, qng_ref, kng_ref, wpool_ref, pscale_ref, invf_ref, sgn_ref, qpad_ref, kpad_ref,
              ypool_ref, q_ref, k_ref, vt_ref):
    tm = PRE_SUB
    rs = pl.ds(i * tm, tm)

    hn = _rms(x_ref[rs, :], g1_ref[...])
    z = jnp.dot(hn.astype(BF16), win_ref[...], preferred_element_type=F32)

    u = z[:, :D_POOL]
    ext = jnp.concatenate([halo, u], axis=0)
    row = lax.broadcasted_iota(I32, (tm, 1), 0) + first_pos
    level = ext
    shift = 1
    for g, w in enumerate(POOL_WINDOWS):
        sl = slice(g * POOL_CH, (g + 1) * POOL_CH)
        while shift < w:
            level = level + pltpu.roll(level, shift, 0)
            shift *= 2
        win_sum = level[POOL_HALO:, sl]
        cnt = jnp.minimum(row + 1, w).astype(F32)
        d = win_sum / cnt - u[:, sl]
        y = jnp.dot(d.astype(BF16), wpool_ref[g], preferred_element_type=F32)
        ypool_ref[rs, sl] = (y * pscale_ref[:, sl]).astype(BF16)

    q_lat = z[:, D_POOL:D_POOL + Q_LORA]
    kv_lat = z[:, D_POOL + Q_LORA:D_POOL + Q_LORA + KV_LORA]
    k_rope = z[:, D_POOL + Q_LORA + KV_LORA:]
    qa = jnp.dot(_rms(q_lat, qag_ref[...]).astype(BF16), wq_ref[...],
                 preferred_element_type=F32)
    kv = jnp.dot(_rms(kv_lat, kvag_ref[...]).astype(BF16), wkv_ref[...],
                 preferred_element_type=F32)

    ang = pos_ref[rs, :].astype(F32) * invf_ref[...]
    cos = jnp.cos(ang)
    sin = jnp.sin(ang) * sgn_ref[...]

    def rot(t):
        return t * cos + pltpu.roll(t, LANES // 2, 1) * sin

    qng = qng_ref[...]
    kng = kng_ref[...]
    kr_rot = rot(k_rope * kng[:, LANES:])
    kr_ssq = jnp.sum(k_rope * k_rope, axis=-1, keepdims=True)
    for h in range(N_HEADS):
        qh = qa[:, h * HEAD_W:(h + 1) * HEAD_W]
        rq = lax.rsqrt(jnp.sum(qh * qh, axis=-1, keepdims=True) / QK_DIM + EPS) * Q_PRESCALE
        qn = qh * rq * qng
        q_ref[0, h, rs, :LANES] = qn[:, :LANES].astype(BF16)
        q_ref[0, h, rs, LANES:] = (rot(qn[:, LANES:]) + qpad_ref[...]).astype(BF16)
        kh = kv[:, h * QK_NOPE:(h + 1) * QK_NOPE]
        rk = lax.rsqrt((jnp.sum(kh * kh, axis=-1, keepdims=True) + kr_ssq) / QK_DIM + EPS)
        k_ref[0, h, rs, :LANES] = (kh * rk * kng[:, :LANES]).astype(BF16)
        k_ref[0, h, rs, LANES:] = (kr_rot * rk + kpad_ref[...]).astype(BF16)
        vh = kv[:, N_HEADS * QK_NOPE + h * V_DIM:N_HEADS * QK_NOPE + (h + 1) * V_DIM]
        vt_ref[0, h, i, :V_DIM, :] = vh.T.astype(BF16)
        vt_ref[0, h, i, V_DIM:, :] = jnp.ones((V_ROWS - V_DIM, tm), BF16)
    return u[tm - POOL_HALO:, :]


def _pre_call(x, pos, g1, win, qag, wq, kvag, wkv, qng, kng, wpool, pscale, invf, sgn, qpad,
              kpad):
    B, S, D = x.shape
    tm = TM_PRE
    grid = (B, S // tm)
    row_spec = lambda w: pl.BlockSpec((None, tm, w), lambda b, s: (b, s, 0))
    head_spec = lambda w: pl.BlockSpec((1, N_HEADS, tm, w), lambda b, s: (b, 0, s, 0))
    consts = [g1, win, qag, wq, kvag, wkv, qng, kng, wpool, pscale, invf, sgn, qpad, kpad]
    return pl.pallas_call(
        _pre_kernel,
        grid=grid,
        in_specs=[row_spec(D), row_spec(1)] + [_const_spec(c.shape) for c in consts],
        out_specs=[row_spec(D_POOL), head_spec(HEAD_W), head_spec(HEAD_W),
                   pl.BlockSpec((1, N_HEADS, tm // PRE_SUB, V_ROWS, PRE_SUB),
                                lambda b, s: (b, 0, s, 0, 0))],
        out_shape=[jax.ShapeDtypeStruct((B, S, D_POOL), BF16),
                   jax.ShapeDtypeStruct((B, N_HEADS, S, HEAD_W), BF16),
                   jax.ShapeDtypeStruct((B, N_HEADS, S, HEAD_W), BF16),
                   jax.ShapeDtypeStruct((B, N_HEADS, S // PRE_SUB, V_ROWS, PRE_SUB), BF16)],
        scratch_shapes=[pltpu.VMEM((POOL_HALO, D_POOL), F32)],
        compiler_params=pltpu.CompilerParams(
            dimension_semantics=("arbitrary", "arbitrary"), vmem_limit_bytes=VMEM_LIMIT),
        name="pre",
    )(x, pos, *consts)


def _attn_kernel(bounded_ref, q_ref, k_ref, vt_ref, o_ref, *chain_scratch):
    S = q_ref.shape[2]
    nq = S // TQ
    per_q = TQ // TK
    assert per_q == 2, "two online-softmax chains take the even / odd key tiles"
    chains = (chain_scratch[0:2], chain_scratch[2:4])
    score_bufs = (chain_scratch[4:6], chain_scratch[6:8])
    key_idx = lax.broadcasted_iota(I32, (TK, TQ), 0)
    qry_idx = lax.broadcasted_iota(I32, (TK, TQ), 1)

    def q_tile(bounded, qi, _):
        q = q_ref[0, 0, pl.ds(pl.multiple_of(qi * TQ, TQ), TQ), :]
        for m_ref, acc_ref in chains:
            m_ref[...] = jnp.full(m_ref.shape, NEG_BIG, F32)
            acc_ref[...] = jnp.zeros(acc_ref.shape, F32)

        def scores(kt):
            k = k_ref[0, 0, pl.ds(pl.multiple_of(kt * TK, TK), TK), :]
            return lax.dot_general(k, q, (((1,), (1,)), ((), ())), preferred_element_type=F32)

        def weighted_values(kt, p):
            return jnp.dot(vt_ref[0, 0, kt], p.astype(BF16), preferred_element_type=F32)

        def fold(chain, st, kt, mask):
            m_ref, acc_ref = chain
            if mask is not None:
                st = jnp.where(mask, st, NEG_BIG)
            m = m_ref[...]
            m_new = jnp.maximum(m, jnp.max(st, axis=0, keepdims=True))
            m_ref[...] = m_new
            acc_ref[...] = (jnp.exp2(m - m_new) * acc_ref[...]
                            + weighted_values(kt, jnp.exp2(st - m_new)))

        def pair_scores(j, buf):
            for c in range(per_q):
                buf[c][...] = scores(per_q * j + c)

        def fold_pair(j, buf, masks=(None, None)):
            if bounded:
                total = None
                for c in range(per_q):
                    st = buf[c][...]
                    if masks[c] is not None:
                        st = jnp.where(masks[c], st, NEG_BIG)
                    pv = weighted_values(per_q * j + c, jnp.exp2(st))
                    total = pv if total is None else total + pv
                chains[0][1][...] += total
            else:
                for c in range(per_q):
                    fold(chains[c], buf[c][...], per_q * j + c, masks[c])

        diag_masks = (key_idx <= qry_idx, key_idx + TK <= qry_idx)
        pair_scores(0, score_bufs[0])

        def two_pairs(i, _):
            pair_scores(2 * i + 1, score_bufs[1])
            fold_pair(2 * i, score_bufs[0])
            pair_scores(2 * i + 2, score_bufs[0])
            fold_pair(2 * i + 1, score_bufs[1])
            return 0
        lax.fori_loop(0, qi // 2, two_pairs, 0)

        @pl.when(qi % 2 == 0)
        def _():
            fold_pair(qi, score_bufs[0], diag_masks)

        @pl.when(qi % 2 == 1)
        def _():
            pair_scores(qi, score_bufs[1])
            fold_pair(qi - 1, score_bufs[0])
            fold_pair(qi, score_bufs[1], diag_masks)

        (m0, acc0), (m1, acc1) = chains
        if bounded:
            acc = acc0[...]
        else:
            m = jnp.maximum(m0[...], m1[...])
            acc = jnp.exp2(m0[...] - m) * acc0[...] + jnp.exp2(m1[...] - m) * acc1[...]
        out = acc[:V_DIM, :] / acc[V_DIM:V_DIM + 1, :]
        o_ref[0, pl.ds(pl.multiple_of(qi * TQ, TQ), TQ), :] = out.T.astype(BF16)
        return 0

    for bounded in (True, False):
        @pl.when((bounded_ref[0] != 0) == bounded)
        def _():
            lax.fori_loop(0, nq, functools.partial(q_tile, bounded), 0)


def _attn_call(bounded, q, k, vt):
    B, H, S, _ = q.shape
    head = lambda w: pl.BlockSpec((1, 1, S, w), lambda b, h, flag: (b, h, 0, 0))
    grid_spec = pltpu.PrefetchScalarGridSpec(
        num_scalar_prefetch=1,
        grid=(B, H),
        in_specs=[head(HEAD_W), head(HEAD_W),
                  pl.BlockSpec((1, 1) + vt.shape[2:], lambda b, h, flag: (b, h, 0, 0, 0))],
        out_specs=pl.BlockSpec((1, S, V_DIM), lambda b, h, flag: (b, 0, h)),
        scratch_shapes=[pltpu.VMEM((1, TQ), F32), pltpu.VMEM((V_ROWS, TQ), F32)] * 2
        + [pltpu.VMEM((TK, TQ), F32)] * 4,
    )
    return pl.pallas_call(
        _attn_kernel,
        grid_spec=grid_spec,
        out_shape=jax.ShapeDtypeStruct((B, S, H * V_DIM), BF16),
        compiler_params=pltpu.CompilerParams(
            dimension_semantics=("arbitrary", "arbitrary"), vmem_limit_bytes=VMEM_LIMIT),
        name="attn",
    )(bounded, q, k, vt)


def _post_kernel(x_ref, yp_ref, ya_ref, wo_ref, g2_ref, wr_ref, br_ref,
                 h1_ref, xn_ref, lg_ref):
    for i in range(x_ref.shape[0] // POST_SUB):
        rs = pl.ds(i * POST_SUB, POST_SUB)
        h1 = (x_ref[rs, :]
              + jnp.dot(yp_ref[rs, :], wo_ref[:D_POOL, :], preferred_element_type=F32)
              + jnp.dot(ya_ref[rs, :], wo_ref[D_POOL:, :], preferred_element_type=F32))
        h1_ref[rs, :] = h1
        xn = _rms(h1, g2_ref[...])
        _store_packed_rows(xn_ref, i * POST_SUB, xn)
        xn_hi = xn.astype(BF16)
        xn_lo = (xn - xn_hi.astype(F32)).astype(BF16)
        both = (jnp.dot(xn_hi, wr_ref[...], preferred_element_type=F32)
                + jnp.dot(xn_lo, wr_ref[...], preferred_element_type=F32))
        lg_ref[rs, :] = both[:, :LANES] + both[:, LANES:] + br_ref[...]


def _post_call(x, yp, ya, wo, g2, wr, br):
    T, D = x.shape
    tm = TM_POST
    row = lambda w: pl.BlockSpec((tm, w), lambda i: (i, 0))
    return pl.pallas_call(
        _post_kernel,
        grid=(T // tm,),
        in_specs=[row(D), row(D_POOL), row(D_ATTN), _const_spec(wo.shape),
                  _const_spec(g2.shape), _const_spec(wr.shape), _const_spec(br.shape)],
        out_specs=[row(D), pl.BlockSpec((tm * PACK_SUBLANES, LANES), lambda i: (i, 0)),
                   row(LANES)],
        out_shape=[jax.ShapeDtypeStruct((T, D), F32),
                   jax.ShapeDtypeStruct((T * PACK_SUBLANES, LANES), jnp.uint32),
                   jax.ShapeDtypeStruct((T, LANES), F32)],
        compiler_params=pltpu.CompilerParams(
            dimension_semantics=("arbitrary",), vmem_limit_bytes=VMEM_LIMIT),
        name="post",
    )(x, yp, ya, wo, g2, wr, br)


def _route_kernel(lg_ref, code_ref, wts_ref, cnt_ref, carry_ref):
    i = pl.program_id(0)
    tm = lg_ref.shape[0]

    @pl.when(i == 0)
    def _():
        carry_ref[...] = jnp.zeros_like(carry_ref)

    lg = lg_ref[...]
    lane = lax.broadcasted_iota(I32, (tm, LANES), 1)
    lane_f = lane.astype(F32)
    neg_inf = -jnp.inf

    def first_argmax(vals):
        mx = jnp.max(vals, axis=-1, keepdims=True)
        idx = jnp.min(jnp.where(vals == mx, lane_f, float(LANES)), axis=-1, keepdims=True)
        return mx, idx.astype(I32)

    is_group = lane < N_GROUPS
    g_logits = jnp.where(is_group, lg, neg_inf)
    g_max, g_sel = first_argmax(g_logits)
    g_w = 1.0 / jnp.sum(jnp.where(is_group, jnp.exp(lg - g_max), 0.0), axis=-1, keepdims=True)

    lo = N_GROUPS + g_sel * EXPERTS_PER_GROUP
    in_group = (lane >= lo) & (lane < lo + EXPERTS_PER_GROUP)
    e_logits = jnp.where(in_group, lg, neg_inf)
    v1, i1 = first_argmax(e_logits)
    v2, i2 = first_argmax(jnp.where(lane == i1, neg_inf, e_logits))
    e1 = i1 - N_GROUPS
    e2 = i2 - N_GROUPS
    t = jnp.exp(v2 - v1)
    w1 = g_w / (1.0 + t)
    w2 = g_w * t / (1.0 + t)
    wts_ref[...] = jnp.where(lane == 0, w1, jnp.where(lane == 1, w2, 0.0))

    hit1 = lane == e1
    hit2 = lane == e2
    onehot = jnp.where(hit1 | hit2, 1.0, 0.0).astype(BF16)
    r = lax.broadcasted_iota(I32, (tm, tm), 0)
    c = lax.broadcasted_iota(I32, (tm, tm), 1)
    lower = jnp.where(c < r, 1.0, 0.0).astype(BF16)
    before = jnp.dot(lower, onehot, preferred_element_type=F32) + carry_ref[0:1, :]
    carry_new = carry_ref[0:1, :] + jnp.sum(onehot.astype(F32), axis=0, keepdims=True)
    carry_ref[...] = jnp.broadcast_to(carry_new, carry_ref.shape)
    pos1 = jnp.sum(jnp.where(hit1, before, 0.0), axis=-1, keepdims=True).astype(I32)
    pos2 = jnp.sum(jnp.where(hit2, before, 0.0), axis=-1, keepdims=True).astype(I32)
    code1 = e1 * 65536 + pos1
    code2 = e2 * 65536 + pos2
    code_ref[...] = jnp.where(lane == 0, code1, jnp.where(lane == 1, code2, 0))
    cnt_ref[...] = carry_ref[...].astype(I32)


def _route_call(logits):
    T = logits.shape[0]
    tm = TM_ROUTE
    row = pl.BlockSpec((tm, LANES), lambda i: (i, 0))
    return pl.pallas_call(
        _route_kernel,
        grid=(T // tm,),
        in_specs=[row],
        out_specs=[row, row, pl.BlockSpec((8, LANES), lambda i: (0, 0))],
        out_shape=[jax.ShapeDtypeStruct((T, LANES), I32), jax.ShapeDtypeStruct((T, LANES), F32),
                   jax.ShapeDtypeStruct((8, LANES), I32)],
        scratch_shapes=[pltpu.VMEM((8, LANES), F32)],
        compiler_params=pltpu.CompilerParams(
            dimension_semantics=("arbitrary",), vmem_limit_bytes=VMEM_LIMIT),
        name="route",
    )(logits)


def _plan_kernel(code_ref, cnt_ref, unused_hbm, slots_hbm, blke_ref, blkn_ref, blkw_ref,
                 slot_ref, start_ref, sem):
    n_assign = code_ref.shape[0]
    n_blocks = blke_ref.shape[0]
    load = pltpu.make_async_copy(unused_hbm, slot_ref, sem)
    load.start()

    def per_expert(e, nb_done):
        cnt = cnt_ref[e]
        nb = lax.shift_right_logical(cnt + (MOE_BLK - 1), MOE_BLK.bit_length() - 1)
        start_ref[e] = nb_done * MOE_BLK

        def per_block(j, _):
            blke_ref[nb_done + j] = e
            blkn_ref[nb_done + j] = jnp.minimum(cnt - j * MOE_BLK, MOE_BLK)
            return 0
        lax.fori_loop(0, nb, per_block, 0)
        return nb_done + nb
    used = lax.fori_loop(0, N_EXPERTS, per_expert, 0)

    last_e = blke_ref[jnp.maximum(used - 1, 0)]

    def tail(b, _):
        blke_ref[b] = last_e
        blkn_ref[b] = 0
        blkw_ref[b] = 0
        return 0
    lax.fori_loop(used, n_blocks, tail, 0)

    def runs_backward(i, carry):
        later_e, next_e = carry
        b = used - 1 - i
        e = blke_ref[b]
        next_e = jnp.where(e != later_e, later_e, next_e)
        blkw_ref[b] = (next_e + 1) * 4
        return e, next_e
    lax.fori_loop(0, used, runs_backward, (last_e, -1))

    def runs_forward(b, carry):
        earlier_e, run = carry
        e = blke_ref[b]
        first = (e != earlier_e).astype(I32)
        run = run + first
        blkw_ref[b] = blkw_ref[b] + first + (run & 1) * 2
        return e, run
    lax.fori_loop(0, used, runs_forward, (-1, -1))

    load.wait()

    def place(a, _):
        code = code_ref[a]
        e = lax.shift_right_logical(code, 16)
        slot_ref[start_ref[e] + (code & 0xFFFF)] = a
        return 0
    lax.fori_loop(0, n_assign, place, 0, unroll=PLAN_UNROLL)

    store = pltpu.make_async_copy(slot_ref, slots_hbm, sem)
    store.start()
    store.wait()


def _plan_call(code_flat, counts, n_tab):
    n_assign = code_flat.shape[0]
    n_slots = n_tab * MOE_BLK
    unused = n_assign + (jnp.arange(n_slots, dtype=I32) & (MOE_BLK - 1))
    smem = pl.BlockSpec(memory_space=pltpu.SMEM)
    hbm = pl.BlockSpec(memory_space=pl.ANY)
    return pl.pallas_call(
        _plan_kernel,
        in_specs=[smem, smem, hbm],
        out_specs=[hbm, smem, smem, smem],
        out_shape=[jax.ShapeDtypeStruct((n_slots,), I32),
                   jax.ShapeDtypeStruct((n_tab,), I32),
                   jax.ShapeDtypeStruct((n_tab,), I32),
                   jax.ShapeDtypeStruct((n_tab,), I32)],
        scratch_shapes=[pltpu.SMEM((n_slots,), I32), pltpu.SMEM((N_EXPERTS,), I32),
                        pltpu.SemaphoreType.DMA],
        name="plan",
    )(code_flat, counts, unused)


def _expert_kernel(blke_ref, blkn_ref, blkw_ref, slot_g_ref, slot_s_ref, xn_hbm,
                   wg_hbm, wu_hbm, wd_hbm, y_hbm, x0, x1, y0, y1, wg_buf, wu_buf, wd_buf,
                   gsem, ssem, wsem):
    xbufs, ybufs = (x0, x1), (y0, y1)
    s = pl.program_id(0)
    n_tok = xn_hbm.shape[0] // PACK_SUBLANES

    def weight_copies(e, slot):
        return [pltpu.make_async_copy(hbm.at[e], buf.at[slot], wsem.at[slot])
                for hbm, buf in ((wg_hbm, wg_buf), (wu_hbm, wu_buf), (wd_hbm, wd_buf))]

    blk = jnp.maximum(s - 1, 0)
    run_info = blkw_ref[blk]
    w_slot = (run_info >> 1) & 1
    next_expert = (run_info >> 2) - 1

    def tile(ref, r):
        start = r * PACK_SUBLANES
        if not isinstance(r, int):
            start = pl.multiple_of(start, PACK_SUBLANES)
        return ref.at[pl.ds(start, PACK_SUBLANES)]

    def rows(j):
        n = blkn_ref[jnp.clip(j, 0, blkn_ref.shape[0] - 1)]
        return jnp.where(j >= 0, (n + (ROW_GROUP - 1)) & ~(ROW_GROUP - 1), 0)
    rows0, rows1, rows2, rows3 = rows(s), rows(s - 1), rows(s - 2), rows(s - 3)

    def stage(nxt):
        cur = 1 - nxt
        x_nxt, x_cur, y_nxt, y_cur = xbufs[nxt], xbufs[cur], ybufs[nxt], ybufs[cur]

        def per_group(n_rows, issue_row):
            for g in range(MOE_BLK // ROW_GROUP):
                @pl.when(g * ROW_GROUP < n_rows)
                def _():
                    for r in range(g * ROW_GROUP, (g + 1) * ROW_GROUP):
                        issue_row(r)

        def gather_row(r):
            tok = slot_g_ref[0, 0, r] & (n_tok - 1)
            pltpu.make_async_copy(tile(xn_hbm, tok), tile(x_nxt, r), gsem.at[nxt]).start()

        def scatter_row(r):
            pltpu.make_async_copy(tile(y_nxt, r), tile(y_hbm, slot_s_ref[0, 0, r]),
                                  ssem.at[nxt]).start()

        def wait_rows(src, dst, sem, n_rows):
            n_words = pl.multiple_of(n_rows * PACK_SUBLANES, ROW_GROUP * PACK_SUBLANES)

            @pl.when(n_rows > 0)
            def _():
                pltpu.make_async_copy(src.at[pl.ds(0, n_words)], dst.at[pl.ds(0, n_words)],
                                      sem).wait()

        def compute():
            xb = _load_packed_rows(x_cur, 0, MOE_BLK).astype(BF16)
            a = jnp.dot(xb, wg_buf[w_slot].astype(BF16), preferred_element_type=F32)
            u = jnp.dot(xb, wu_buf[w_slot].astype(BF16), preferred_element_type=F32)
            hmid = (a * jax.nn.sigmoid(a) * u).astype(BF16)
            _store_packed_rows(y_cur, 0, jnp.dot(hmid, wd_buf[w_slot].astype(BF16),
                                                 preferred_element_type=F32))

        wait_rows(xn_hbm, x_cur, gsem.at[cur], rows1)
        wait_rows(y_cur, y_hbm, ssem.at[cur], rows3)

        @pl.when((rows1 > 0) & ((run_info & 1) == 1))
        def _():
            for copy in weight_copies(blke_ref[blk], w_slot):
                copy.wait()

            @pl.when(next_expert >= 0)
            def _():
                for copy in weight_copies(next_expert, 1 - w_slot):
                    copy.start(priority=WEIGHT_DMA_PRIORITY)

        if nxt == 0:
            @pl.when((s == 0) & (blkn_ref[0] > 0))
            def _():
                for copy in weight_copies(blke_ref[0], 0):
                    copy.start(priority=WEIGHT_DMA_PRIORITY)

            @pl.when(s == 0)
            def _():
                x_nxt[...] = jnp.zeros(x_nxt.shape, x_nxt.dtype)
                x_cur[...] = jnp.zeros(x_cur.shape, x_cur.dtype)
                y_nxt[...] = jnp.zeros(y_nxt.shape, y_nxt.dtype)
                dump = pltpu.make_async_copy(
                    y_nxt, y_hbm.at[pl.ds(TOP_K * n_tok * PACK_SUBLANES,
                                          MOE_BLK * PACK_SUBLANES)], ssem.at[nxt])
                dump.start()
                dump.wait()

        per_group(rows0, gather_row)
        per_group(rows2, scatter_row)

        @pl.when(rows1 > 0)
        def _():
            compute()

    for parity in range(2):
        pl.when((s & 1) == parity)(functools.partial(stage, parity))


def _expert_call(blk_e, blk_n, blk_w, slots, xn, wg, wu, wd):
    T = xn.shape[0] // PACK_SUBLANES
    assert T & (T - 1) == 0, "token id is recovered from the assignment id with a mask"
    n_tab = blk_e.shape[0]
    _, D, F = wg.shape
    packed_block = pltpu.VMEM((MOE_BLK * PACK_SUBLANES, LANES), jnp.uint32)
    hbm = pl.BlockSpec(memory_space=pl.ANY)
    grid_spec = pltpu.PrefetchScalarGridSpec(
        num_scalar_prefetch=3,
        grid=(n_tab,),
        in_specs=[
            pl.BlockSpec((1, 1, MOE_BLK), lambda s, *_: (s, 0, 0), memory_space=pltpu.SMEM),
            pl.BlockSpec((1, 1, MOE_BLK), lambda s, *_: (jnp.maximum(s - 2, 0), 0, 0),
                         memory_space=pltpu.SMEM),
            hbm, hbm, hbm, hbm,
        ],
        out_specs=hbm,
        scratch_shapes=[packed_block] * 4
        + [pltpu.VMEM((2, D, F), F32), pltpu.VMEM((2, D, F), F32), pltpu.VMEM((2, F, D), F32)]
        + [pltpu.SemaphoreType.DMA((2,))] * 3,
    )
    slots3 = slots.reshape(n_tab, 1, MOE_BLK)
    return pl.pallas_call(
        _expert_kernel,
        grid_spec=grid_spec,
        out_shape=jax.ShapeDtypeStruct(((T * TOP_K + MOE_BLK) * PACK_SUBLANES, LANES),
                                       jnp.uint32),
        compiler_params=pltpu.CompilerParams(
            dimension_semantics=("arbitrary",), vmem_limit_bytes=VMEM_LIMIT,
            disable_bounds_checks=True),
        name="experts",
    )(blk_e, blk_n, blk_w, slots3, slots3, xn, wg, wu, wd)


def _final_kernel(h1_ref, y1_ref, y2_ref, wts_ref, p_ref, wpp_ref, pg_ref, g3_ref, wpg_ref,
                  o_ref):
    w = wts_ref[...]
    tm = h1_ref.shape[0]
    h2 = (h1_ref[...] + w[:, 0:1] * _load_packed_rows(y1_ref, 0, tm)
          + w[:, 1:2] * _load_packed_rows(y2_ref, 0, tm))
    e = _rms(jnp.dot(p_ref[...].astype(BF16), wpp_ref[...], preferred_element_type=F32),
             pg_ref[...])
    gate = jax.nn.sigmoid(jnp.dot(_rms(h2, g3_ref[...]).astype(BF16), wpg_ref[...],
                                  preferred_element_type=F32))
    o_ref[...] = h2 + gate * e


def _final_call(h1, y, wts, p, wpp, pg, g3, wpg):
    T, D = h1.shape
    tm = TM_FINAL
    row = lambda w: pl.BlockSpec((tm, w), lambda i: (i, 0))
    first = pl.BlockSpec((tm * PACK_SUBLANES, LANES), lambda i: (i, 0))
    second = pl.BlockSpec((tm * PACK_SUBLANES, LANES), lambda i: (i + T // tm, 0))
    return pl.pallas_call(
        _final_kernel,
        grid=(T // tm,),
        in_specs=[row(D), first, second, row(LANES), row(PLE_DIM), _const_spec(wpp.shape),
                  _const_spec(pg.shape), _const_spec(g3.shape), _const_spec(wpg.shape)],
        out_specs=row(D),
        out_shape=jax.ShapeDtypeStruct((T, D), F32),
        compiler_params=pltpu.CompilerParams(
            dimension_semantics=("arbitrary",), vmem_limit_bytes=VMEM_LIMIT),
        name="final",
    )(h1, y, y, wts, p, wpp, pg, g3, wpg)


def _rope_lanes(a):
    z = jnp.zeros(a.shape[:-1] + (ROPE_HALF,), a.dtype)
    return jnp.concatenate([a[..., :ROPE_HALF], z, a[..., ROPE_HALF:], z], axis=-1)


def _head_lanes(a):
    return jnp.concatenate([a[..., :QK_NOPE], _rope_lanes(a[..., QK_NOPE:])], axis=-1)


def _score_bound(q_gain, k_gain):
    return Q_PRESCALE * QK_DIM * jnp.max(jnp.abs(q_gain)) * jnp.max(jnp.abs(k_gain))


def kernel(x, p, positions, norm1_gain, w_in, q_a_gain, w_q_b, kv_a_gain, w_kv_b, q_norm_gain, k_norm_gain, w_pool, pool_scale, w_out, norm2_gain, w_router_group, b_router_group, w_router_expert, b_router_expert, w_exp_gate, w_exp_up, w_exp_down, norm3_gain, w_ple_gate, w_ple_proj, ple_norm_gain):
    B, S, D = x.shape
    T = B * S
    assert x.shape[2] == D_MODEL and S % TQ == 0 and S % TM_PRE == 0
    assert T % TM_POST == 0 and T % TM_FINAL == 0 and T % TM_ROUTE == 0
    layer = 0
    row = lambda a: a[layer].reshape(1, -1)

    n_lat = D_POOL + Q_LORA + KV_LORA
    win = jnp.concatenate([w_in[layer][:, :n_lat], _rope_lanes(w_in[layer][:, n_lat:])],
                          axis=1).astype(BF16)
    wq = _head_lanes(w_q_b[layer].reshape(Q_LORA, N_HEADS, QK_DIM)
                     ).reshape(Q_LORA, N_HEADS * HEAD_W).astype(BF16)
    wkv3 = w_kv_b[layer].reshape(KV_LORA, N_HEADS, QK_NOPE + V_DIM)
    wkv = jnp.concatenate([wkv3[..., :QK_NOPE].reshape(KV_LORA, -1),
                           wkv3[..., QK_NOPE:].reshape(KV_LORA, -1)], axis=1).astype(BF16)
    qng = _head_lanes(q_norm_gain[layer]).reshape(1, HEAD_W)
    kng = _head_lanes(k_norm_gain[layer]).reshape(1, HEAD_W)
    inv_freq = ROPE_THETA ** (-jnp.arange(ROPE_HALF, dtype=F32) / ROPE_HALF)
    invf = _rope_lanes(jnp.concatenate([inv_freq, inv_freq])).reshape(1, LANES)
    sgn = _rope_lanes(jnp.concatenate([-jnp.ones((ROPE_HALF,), F32),
                                       jnp.ones((ROPE_HALF,), F32)])).reshape(1, LANES)
    wr = jnp.zeros((D, LANES), F32)
    wr = wr.at[:, :N_GROUPS].set(w_router_group[layer])
    wr = wr.at[:, N_GROUPS:N_GROUPS + N_EXPERTS].set(w_router_expert[layer])
    wr_hi = wr.astype(BF16)
    wr = jnp.concatenate([wr_hi, (wr - wr_hi.astype(F32)).astype(BF16)], axis=1)
    br = jnp.zeros((1, LANES), F32)
    br = br.at[0, :N_GROUPS].set(b_router_group[layer])
    br = br.at[0, N_GROUPS:N_GROUPS + N_EXPERTS].set(b_router_expert[layer])

    bound = _score_bound(q_norm_gain[layer], k_norm_gain[layer])
    bounded = bound <= SCORE_BOUND_LIMIT
    spare = jnp.arange(LANES) == SPARE_ROPE_LANE
    qpad = jnp.where(spare & bounded, -bound, 0.0).astype(F32).reshape(1, LANES)
    kpad = jnp.where(spare, 1.0, 0.0).astype(F32).reshape(1, LANES)
    ypool, q, k, vt = _pre_call(
        x, positions.reshape(B, S, 1), row(norm1_gain), win, row(q_a_gain), wq, row(kv_a_gain),
        wkv, qng, kng, w_pool[layer].astype(BF16), row(pool_scale), invf, sgn, qpad, kpad)
    yattn = _attn_call(bounded.astype(I32).reshape(1), q, k, vt)
    h1, xn2, logits = _post_call(
        x.reshape(T, D), ypool.reshape(T, D_POOL), yattn.reshape(T, D_ATTN),
        w_out[layer].astype(BF16), row(norm2_gain), wr, br)

    n_assign = T * TOP_K
    n_blocks = (n_assign + N_EXPERTS * (MOE_BLK - 1)) // MOE_BLK
    code, wts, counts = _route_call(logits)
    slots, blk_e, blk_n, blk_w = _plan_call(code[:, :TOP_K].T.reshape(n_assign), counts[0],
                                            n_blocks + MOE_DRAIN_STEPS)
    y = _expert_call(blk_e, blk_n, blk_w, slots, xn2,
                     w_exp_gate[layer], w_exp_up[layer], w_exp_down[layer])

    out = _final_call(h1, y, wts, p[layer].reshape(T, PLE_DIM),
                      w_ple_proj[layer].astype(BF16), row(ple_norm_gain), row(norm3_gain),
                      w_ple_gate[layer].astype(BF16))
    return out.reshape(B, S, D)
```

```python
import functools
import math

import jax
import jax.numpy as jnp
from jax import lax
from jax.experimental import pallas as pl
from jax.experimental.pallas import tpu as pltpu

F32 = jnp.float32
BF16 = jnp.bfloat16
I32 = jnp.int32

D_MODEL = 2048
PLE_DIM = 256
EPS = 1e-6
POOL_WINDOWS = (2, 4, 8, 16)
POOL_CH = 256
D_POOL = POOL_CH * len(POOL_WINDOWS)
N_HEADS = 8
Q_LORA = 512
KV_LORA = 512
QK_NOPE = 128
QK_ROPE = 64
QK_DIM = QK_NOPE + QK_ROPE
V_DIM = 128
D_ATTN = N_HEADS * V_DIM
ROPE_THETA = 10000.0
ATTN_SCALE = 1.0 / math.sqrt(QK_DIM)
Q_PRESCALE = ATTN_SCALE * math.log2(math.e)
N_GROUPS = 8
EXPERTS_PER_GROUP = 8
N_EXPERTS = N_GROUPS * EXPERTS_PER_GROUP
TOP_K = 2
D_EXPERT = 512

LANES = 128
PACK_SUBLANES = 8
MXU_DIM = 256
VMEM_LIMIT = 56 * 1024 * 1024

HEAD_W = 2 * LANES
ROPE_HALF = QK_ROPE // 2
POOL_HALO = 16
TM_PRE = 512
PRE_SUB = 256
TM_POST = 512
POST_SUB = 256
TM_FINAL = 256
TQ = 512
TK = PRE_SUB
TM_ROUTE = 256
MOE_BLK = 256
MOE_DRAIN_STEPS = 3
ROW_GROUP = 32
WEIGHT_DMA_PRIORITY = 1
PLAN_UNROLL = 16
NEG_BIG = -1e30
V_ROWS = V_DIM + 16
SPARE_ROPE_LANE = ROPE_HALF
SCORE_BOUND_LIMIT = 50.0


def _const_spec(shape):
    nd = len(shape)
    return pl.BlockSpec(shape, lambda *_: (0,) * nd, pipeline_mode=pl.Buffered(1))


def _rms(x, gain):
    return x * lax.rsqrt(jnp.mean(x * x, axis=-1, keepdims=True) + EPS) * gain


def _store_packed_rows(ref, first_row, x):
    rows, d = x.shape
    half = d // 2
    assert half == PACK_SUBLANES * LANES
    for i in range(PACK_SUBLANES):
        lo = x[:, i * LANES:(i + 1) * LANES].astype(BF16).astype(F32)
        hi = x[:, half + i * LANES:half + (i + 1) * LANES].astype(BF16).astype(F32)
        word = (lax.shift_right_logical(pltpu.bitcast(lo, jnp.uint32), jnp.uint32(16))
                | (pltpu.bitcast(hi, jnp.uint32) & jnp.uint32(0xFFFF0000)))
        ref[pl.ds(first_row * PACK_SUBLANES + i, rows, stride=PACK_SUBLANES), :] = word


def _load_packed_rows(ref, first_row, rows):
    lo, hi = [], []
    for i in range(PACK_SUBLANES):
        word = ref[pl.ds(first_row * PACK_SUBLANES + i, rows, stride=PACK_SUBLANES), :]
        lo.append(pltpu.bitcast(lax.shift_left(word, jnp.uint32(16)), F32))
        hi.append(pltpu.bitcast(word & jnp.uint32(0xFFFF0000), F32))
    return jnp.concatenate(lo + hi, axis=1)


def _pre_kernel(x_ref, pos_ref, g1_ref, win_ref, qag_ref, wq_ref, kvag_ref, wkv_ref,
                qng_ref, kng_ref, wpool_ref, pscale_ref, invf_ref, sgn_ref, qpad_ref, kpad_ref,
                ypool_ref, q_ref, k_ref, vt_ref, carry_ref):
    st = pl.program_id(1)

    @pl.when(st == 0)
    def _():
        carry_ref[...] = jnp.zeros_like(carry_ref)

    halo = carry_ref[...]
    for i in range(x_ref.shape[0] // PRE_SUB):
        halo = _pre_rows(i, st * x_ref.shape[0] + i * PRE_SUB, halo,
                         x_ref, pos_ref, g1_ref, win_ref, qag_ref, wq_ref, kvag_ref, wkv_ref,
                         qng_ref, kng_ref, wpool_ref, pscale_ref, invf_ref, sgn_ref, qpad_ref, kpad_ref,
                         ypool_ref, q_ref, k_ref, vt_ref)
    carry_ref[...] = halo


def _pre_rows(i, first_pos, halo, x_ref, pos_ref, g1_ref, win_ref, qag_ref, wq_ref, kvag_ref,
              wkv_ref, qng_ref, kng_ref, wpool_ref, pscale_ref, invf_ref, sgn_ref, qpad_ref, kpad_ref,
              ypool_ref, q_ref, k_ref, vt_ref):
    tm = PRE_SUB
    rs = pl.ds(i * tm, tm)

    hn = _rms(x_ref[rs, :], g1_ref[...])
    z = jnp.dot(hn.astype(BF16), win_ref[...], preferred_element_type=F32)

    u = z[:, :D_POOL]
    ext = jnp.concatenate([halo, u], axis=0)
    row = lax.broadcasted_iota(I32, (tm, 1), 0) + first_pos
    level = ext
    shift = 1
    for g, w in enumerate(POOL_WINDOWS):
        sl = slice(g * POOL_CH, (g + 1) * POOL_CH)
        while shift < w:
            level = level + pltpu.roll(level, shift, 0)
            shift *= 2
        win_sum = level[POOL_HALO:, sl]
        cnt = jnp.minimum(row + 1, w).astype(F32)
        d = win_sum / cnt - u[:, sl]
        y = jnp.dot(d.astype(BF16), wpool_ref[g], preferred_element_type=F32)
        ypool_ref[rs, sl] = (y * pscale_ref[:, sl]).astype(BF16)

    q_lat = z[:, D_POOL:D_POOL + Q_LORA]
    kv_lat = z[:, D_POOL + Q_LORA:D_POOL + Q_LORA + KV_LORA]
    k_rope = z[:, D_POOL + Q_LORA + KV_LORA:]
    qa = jnp.dot(_rms(q_lat, qag_ref[...]).astype(BF16), wq_ref[...],
                 preferred_element_type=F32)
    kv = jnp.dot(_rms(kv_lat, kvag_ref[...]).astype(BF16), wkv_ref[...],
                 preferred_element_type=F32)

    ang = pos_ref[rs, :].astype(F32) * invf_ref[...]
    cos = jnp.cos(ang)
    sin = jnp.sin(ang) * sgn_ref[...]

    def rot(t):
        return t * cos + pltpu.roll(t, LANES // 2, 1) * sin

    qng = qng_ref[...]
    kng = kng_ref[...]
    kr_rot = rot(k_rope * kng[:, LANES:])
    kr_ssq = jnp.sum(k_rope * k_rope, axis=-1, keepdims=True)
    for h in range(N_HEADS):
        qh = qa[:, h * HEAD_W:(h + 1) * HEAD_W]
        rq = lax.rsqrt(jnp.sum(qh * qh, axis=-1, keepdims=True) / QK_DIM + EPS) * Q_PRESCALE
        qn = qh * rq * qng
        q_ref[0, h, rs, :LANES] = qn[:, :LANES].astype(BF16)
        q_ref[0, h, rs, LANES:] = (rot(qn[:, LANES:]) + qpad_ref[...]).astype(BF16)
        kh = kv[:, h * QK_NOPE:(h + 1) * QK_NOPE]
        rk = lax.rsqrt((jnp.sum(kh * kh, axis=-1, keepdims=True) + kr_ssq) / QK_DIM + EPS)
        k_ref[0, h, rs, :LANES] = (kh * rk * kng[:, :LANES]).astype(BF16)
        k_ref[0, h, rs, LANES:] = (kr_rot * rk + kpad_ref[...]).astype(BF16)
        vh = kv[:, N_HEADS * QK_NOPE + h * V_DIM:N_HEADS * QK_NOPE + (h + 1) * V_DIM]
        vt_ref[0, h, i, :V_DIM, :] = vh.T.astype(BF16)
        vt_ref[0, h, i, V_DIM:, :] = jnp.ones((V_ROWS - V_DIM, tm), BF16)
    return u[tm - POOL_HALO:, :]


def _pre_call(x, pos, g1, win, qag, wq, kvag, wkv, qng, kng, wpool, pscale, invf, sgn, qpad,
              kpad):
    B, S, D = x.shape
    tm = TM_PRE
    grid = (B, S // tm)
    row_spec = lambda w: pl.BlockSpec((None, tm, w), lambda b, s: (b, s, 0))
    head_spec = lambda w: pl.BlockSpec((1, N_HEADS, tm, w), lambda b, s: (b, 0, s, 0))
    consts = [g1, win, qag, wq, kvag, wkv, qng, kng, wpool, pscale, invf, sgn, qpad, kpad]
    return pl.pallas_call(
        _pre_kernel,
        grid=grid,
        in_specs=[row_spec(D), row_spec(1)] + [_const_spec(c.shape) for c in consts],
        out_specs=[row_spec(D_POOL), head_spec(HEAD_W), head_spec(HEAD_W),
                   pl.BlockSpec((1, N_HEADS, tm // PRE_SUB, V_ROWS, PRE_SUB),
                                lambda b, s: (b, 0, s, 0, 0))],
        out_shape=[jax.ShapeDtypeStruct((B, S, D_POOL), BF16),
                   jax.ShapeDtypeStruct((B, N_HEADS, S, HEAD_W), BF16),
                   jax.ShapeDtypeStruct((B, N_HEADS, S, HEAD_W), BF16),
                   jax.ShapeDtypeStruct((B, N_HEADS, S // PRE_SUB, V_ROWS, PRE_SUB), BF16)],
        scratch_shapes=[pltpu.VMEM((POOL_HALO, D_POOL), F32)],
        compiler_params=pltpu.CompilerParams(
            dimension_semantics=("arbitrary", "arbitrary"), vmem_limit_bytes=VMEM_LIMIT),
        name="pre",
    )(x, pos, *consts)


def _attn_kernel(bounded_ref, q_ref, k_ref, vt_ref, o_ref, *chain_scratch):
    S = q_ref.shape[2]
    nq = S // TQ
    per_q = TQ // TK
    assert per_q == 2, "two online-softmax chains take the even / odd key tiles"
    chains = (chain_scratch[0:2], chain_scratch[2:4])
    score_bufs = (chain_scratch[4:6], chain_scratch[6:8])
    key_idx = lax.broadcasted_iota(I32, (TK, TQ), 0)
    qry_idx = lax.broadcasted_iota(I32, (TK, TQ), 1)

    diag_masks = (key_idx <= qry_idx, key_idx + TK <= qry_idx)

    def weighted_values(kt, p):
        return jnp.dot(vt_ref[0, 0, kt], p.astype(BF16), preferred_element_type=F32)

    def write_out(qi, acc):
        out = acc[:V_DIM, :] / acc[V_DIM:V_DIM + 1, :]
        start = qi * TQ if isinstance(qi, int) else pl.multiple_of(qi * TQ, TQ)
        o_ref[0, pl.ds(start, TQ), :] = out.T.astype(BF16)

    def bounded_head():
        steps = [(qi, j) for qi in range(nq) for j in range(qi + 1)]

        def pair_scores(step, buf):
            qi, j = step
            q = q_ref[0, 0, qi * TQ:(qi + 1) * TQ, :]
            for c in range(per_q):
                kt = per_q * j + c
                buf[c][...] = lax.dot_general(k_ref[0, 0, kt * TK:(kt + 1) * TK, :], q,
                                              (((1,), (1,)), ((), ())),
                                              preferred_element_type=F32)

        pair_scores(steps[0], score_bufs[0])
        acc = None
        for t, (qi, j) in enumerate(steps):
            if t + 1 < len(steps):
                pair_scores(steps[t + 1], score_bufs[(t + 1) % 2])
            for c in range(per_q):
                st = score_bufs[t % 2][c][...]
                if j == qi:
                    st = jnp.where(diag_masks[c], st, NEG_BIG)
                pv = weighted_values(per_q * j + c, jnp.exp2(st))
                acc = pv if acc is None else acc + pv
            if j == qi:
                write_out(qi, acc)
                acc = None

    def online_q_tile(qi, _):
        q = q_ref[0, 0, pl.ds(pl.multiple_of(qi * TQ, TQ), TQ), :]
        for m_ref, acc_ref in chains:
            m_ref[...] = jnp.full(m_ref.shape, NEG_BIG, F32)
            acc_ref[...] = jnp.zeros(acc_ref.shape, F32)

        def scores(kt):
            k = k_ref[0, 0, pl.ds(pl.multiple_of(kt * TK, TK), TK), :]
            return lax.dot_general(k, q, (((1,), (1,)), ((), ())), preferred_element_type=F32)

        def fold(chain, st, kt, mask):
            m_ref, acc_ref = chain
            if mask is not None:
                st = jnp.where(mask, st, NEG_BIG)
            m = m_ref[...]
            m_new = jnp.maximum(m, jnp.max(st, axis=0, keepdims=True))
            m_ref[...] = m_new
            acc_ref[...] = (jnp.exp2(m - m_new) * acc_ref[...]
                            + weighted_values(kt, jnp.exp2(st - m_new)))

        def pair_scores(j, buf):
            for c in range(per_q):
                buf[c][...] = scores(per_q * j + c)

        def fold_pair(j, buf, masks=(None, None)):
            for c in range(per_q):
                fold(chains[c], buf[c][...], per_q * j + c, masks[c])

        pair_scores(0, score_bufs[0])

        def two_pairs(i, _):
            pair_scores(2 * i + 1, score_bufs[1])
            fold_pair(2 * i, score_bufs[0])
            pair_scores(2 * i + 2, score_bufs[0])
            fold_pair(2 * i + 1, score_bufs[1])
            return 0
        lax.fori_loop(0, qi // 2, two_pairs, 0)

        @pl.when(qi % 2 == 0)
        def _():
            fold_pair(qi, score_bufs[0], diag_masks)

        @pl.when(qi % 2 == 1)
        def _():
            pair_scores(qi, score_bufs[1])
            fold_pair(qi - 1, score_bufs[0])
            fold_pair(qi, score_bufs[1], diag_masks)

        (m0, acc0), (m1, acc1) = chains
        m = jnp.maximum(m0[...], m1[...])
        write_out(qi, jnp.exp2(m0[...] - m) * acc0[...] + jnp.exp2(m1[...] - m) * acc1[...])
        return 0

    pl.when(bounded_ref[0] != 0)(bounded_head)

    @pl.when(bounded_ref[0] == 0)
    def _():
        lax.fori_loop(0, nq, online_q_tile, 0)


def _attn_call(bounded, q, k, vt):
    B, H, S, _ = q.shape
    head = lambda w: pl.BlockSpec((1, 1, S, w), lambda b, h, flag: (b, h, 0, 0))
    grid_spec = pltpu.PrefetchScalarGridSpec(
        num_scalar_prefetch=1,
        grid=(B, H),
        in_specs=[head(HEAD_W), head(HEAD_W),
                  pl.BlockSpec((1, 1) + vt.shape[2:], lambda b, h, flag: (b, h, 0, 0, 0))],
        out_specs=pl.BlockSpec((1, S, V_DIM), lambda b, h, flag: (b, 0, h)),
        scratch_shapes=[pltpu.VMEM((1, TQ), F32), pltpu.VMEM((V_ROWS, TQ), F32)] * 2
        + [pltpu.VMEM((TK, TQ), F32)] * 4,
    )
    return pl.pallas_call(
        _attn_kernel,
        grid_spec=grid_spec,
        out_shape=jax.ShapeDtypeStruct((B, S, H * V_DIM), BF16),
        compiler_params=pltpu.CompilerParams(
            dimension_semantics=("arbitrary", "arbitrary"), vmem_limit_bytes=VMEM_LIMIT),
        name="attn",
    )(bounded, q, k, vt)


def _post_kernel(x_ref, yp_ref, ya_ref, wo_ref, g2_ref, wr_ref, br_ref,
                 h1_ref, xn_ref, lg_ref):
    for i in range(x_ref.shape[0] // POST_SUB):
        rs = pl.ds(i * POST_SUB, POST_SUB)
        h1 = (x_ref[rs, :]
              + jnp.dot(yp_ref[rs, :], wo_ref[:D_POOL, :], preferred_element_type=F32)
              + jnp.dot(ya_ref[rs, :], wo_ref[D_POOL:, :], preferred_element_type=F32))
        h1_ref[rs, :] = h1
        xn = _rms(h1, g2_ref[...])
        _store_packed_rows(xn_ref, i * POST_SUB, xn)
        xn_hi = xn.astype(BF16)
        xn_lo = (xn - xn_hi.astype(F32)).astype(BF16)
        both = (jnp.dot(xn_hi, wr_ref[...], preferred_element_type=F32)
                + jnp.dot(xn_lo, wr_ref[...], preferred_element_type=F32))
        lg_ref[rs, :] = both[:, :LANES] + both[:, LANES:] + br_ref[...]


def _post_call(x, yp, ya, wo, g2, wr, br):
    T, D = x.shape
    tm = TM_POST
    row = lambda w: pl.BlockSpec((tm, w), lambda i: (i, 0))
    return pl.pallas_call(
        _post_kernel,
        grid=(T // tm,),
        in_specs=[row(D), row(D_POOL), row(D_ATTN), _const_spec(wo.shape),
                  _const_spec(g2.shape), _const_spec(wr.shape), _const_spec(br.shape)],
        out_specs=[row(D), pl.BlockSpec((tm * PACK_SUBLANES, LANES), lambda i: (i, 0)),
                   row(LANES)],
        out_shape=[jax.ShapeDtypeStruct((T, D), F32),
                   jax.ShapeDtypeStruct((T * PACK_SUBLANES, LANES), jnp.uint32),
                   jax.ShapeDtypeStruct((T, LANES), F32)],
        compiler_params=pltpu.CompilerParams(
            dimension_semantics=("arbitrary",), vmem_limit_bytes=VMEM_LIMIT),
        name="post",
    )(x, yp, ya, wo, g2, wr, br)


def _route_kernel(lg_ref, code_ref, wts_ref, cnt_ref, carry_ref):
    i = pl.program_id(0)
    tm = lg_ref.shape[0]

    @pl.when(i == 0)
    def _():
        carry_ref[...] = jnp.zeros_like(carry_ref)

    lg = lg_ref[...]
    lane = lax.broadcasted_iota(I32, (tm, LANES), 1)
    lane_f = lane.astype(F32)
    neg_inf = -jnp.inf

    def first_argmax(vals):
        mx = jnp.max(vals, axis=-1, keepdims=True)
        idx = jnp.min(jnp.where(vals == mx, lane_f, float(LANES)), axis=-1, keepdims=True)
        return mx, idx.astype(I32)

    is_group = lane < N_GROUPS
    g_logits = jnp.where(is_group, lg, neg_inf)
    g_max, g_sel = first_argmax(g_logits)
    g_w = 1.0 / jnp.sum(jnp.where(is_group, jnp.exp(lg - g_max), 0.0), axis=-1, keepdims=True)

    lo = N_GROUPS + g_sel * EXPERTS_PER_GROUP
    in_group = (lane >= lo) & (lane < lo + EXPERTS_PER_GROUP)
    e_logits = jnp.where(in_group, lg, neg_inf)
    v1, i1 = first_argmax(e_logits)
    v2, i2 = first_argmax(jnp.where(lane == i1, neg_inf, e_logits))
    e1 = i1 - N_GROUPS
    e2 = i2 - N_GROUPS
    t = jnp.exp(v2 - v1)
    w1 = g_w / (1.0 + t)
    w2 = g_w * t / (1.0 + t)
    wts_ref[...] = jnp.where(lane == 0, w1, jnp.where(lane == 1, w2, 0.0))

    hit1 = lane == e1
    hit2 = lane == e2
    onehot = jnp.where(hit1 | hit2, 1.0, 0.0).astype(BF16)
    r = lax.broadcasted_iota(I32, (tm, tm), 0)
    c = lax.broadcasted_iota(I32, (tm, tm), 1)
    lower = jnp.where(c < r, 1.0, 0.0).astype(BF16)
    before = jnp.dot(lower, onehot, preferred_element_type=F32) + carry_ref[0:1, :]
    carry_new = carry_ref[0:1, :] + jnp.sum(onehot.astype(F32), axis=0, keepdims=True)
    carry_ref[...] = jnp.broadcast_to(carry_new, carry_ref.shape)
    pos1 = jnp.sum(jnp.where(hit1, before, 0.0), axis=-1, keepdims=True).astype(I32)
    pos2 = jnp.sum(jnp.where(hit2, before, 0.0), axis=-1, keepdims=True).astype(I32)
    code1 = e1 * 65536 + pos1
    code2 = e2 * 65536 + pos2
    code_ref[...] = jnp.where(lane == 0, code1, jnp.where(lane == 1, code2, 0))
    cnt_ref[...] = carry_ref[...].astype(I32)


def _route_call(logits):
    T = logits.shape[0]
    tm = TM_ROUTE
    row = pl.BlockSpec((tm, LANES), lambda i: (i, 0))
    return pl.pallas_call(
        _route_kernel,
        grid=(T // tm,),
        in_specs=[row],
        out_specs=[row, row, pl.BlockSpec((8, LANES), lambda i: (0, 0))],
        out_shape=[jax.ShapeDtypeStruct((T, LANES), I32), jax.ShapeDtypeStruct((T, LANES), F32),
                   jax.ShapeDtypeStruct((8, LANES), I32)],
        scratch_shapes=[pltpu.VMEM((8, LANES), F32)],
        compiler_params=pltpu.CompilerParams(
            dimension_semantics=("arbitrary",), vmem_limit_bytes=VMEM_LIMIT),
        name="route",
    )(logits)


def _plan_kernel(code_ref, cnt_ref, unused_hbm, slots_hbm, blke_ref, blkn_ref, blkw_ref,
                 slot_ref, start_ref, sem):
    n_assign = code_ref.shape[0]
    n_blocks = blke_ref.shape[0]
    load = pltpu.make_async_copy(unused_hbm, slot_ref, sem)
    load.start()

    def per_expert(e, nb_done):
        cnt = cnt_ref[e]
        nb = lax.shift_right_logical(cnt + (MOE_BLK - 1), MOE_BLK.bit_length() - 1)
        start_ref[e] = nb_done * MOE_BLK

        def per_block(j, _):
            blke_ref[nb_done + j] = e
            blkn_ref[nb_done + j] = jnp.minimum(cnt - j * MOE_BLK, MOE_BLK)
            return 0
        lax.fori_loop(0, nb, per_block, 0)
        return nb_done + nb
    used = lax.fori_loop(0, N_EXPERTS, per_expert, 0)

    last_e = blke_ref[jnp.maximum(used - 1, 0)]

    def tail(b, _):
        blke_ref[b] = last_e
        blkn_ref[b] = 0
        blkw_ref[b] = 0
        return 0
    lax.fori_loop(used, n_blocks, tail, 0)

    def runs_backward(i, carry):
        later_e, next_e = carry
        b = used - 1 - i
        e = blke_ref[b]
        next_e = jnp.where(e != later_e, later_e, next_e)
        blkw_ref[b] = (next_e + 1) * 4
        return e, next_e
    lax.fori_loop(0, used, runs_backward, (last_e, -1))

    def runs_forward(b, carry):
        earlier_e, run = carry
        e = blke_ref[b]
        first = (e != earlier_e).astype(I32)
        run = run + first
        blkw_ref[b] = blkw_ref[b] + first + (run & 1) * 2
        return e, run
    lax.fori_loop(0, used, runs_forward, (-1, -1))

    load.wait()

    def place(a, _):
        code = code_ref[a]
        e = lax.shift_right_logical(code, 16)
        slot_ref[start_ref[e] + (code & 0xFFFF)] = a
        return 0
    lax.fori_loop(0, n_assign, place, 0, unroll=PLAN_UNROLL)

    store = pltpu.make_async_copy(slot_ref, slots_hbm, sem)
    store.start()
    store.wait()


def _plan_call(code_flat, counts, n_tab):
    n_assign = code_flat.shape[0]
    n_slots = n_tab * MOE_BLK
    unused = n_assign + (jnp.arange(n_slots, dtype=I32) & (MOE_BLK - 1))
    smem = pl.BlockSpec(memory_space=pltpu.SMEM)
    hbm = pl.BlockSpec(memory_space=pl.ANY)
    return pl.pallas_call(
        _plan_kernel,
        in_specs=[smem, smem, hbm],
        out_specs=[hbm, smem, smem, smem],
        out_shape=[jax.ShapeDtypeStruct((n_slots,), I32),
                   jax.ShapeDtypeStruct((n_tab,), I32),
                   jax.ShapeDtypeStruct((n_tab,), I32),
                   jax.ShapeDtypeStruct((n_tab,), I32)],
        scratch_shapes=[pltpu.SMEM((n_slots,), I32), pltpu.SMEM((N_EXPERTS,), I32),
                        pltpu.SemaphoreType.DMA],
        name="plan",
    )(code_flat, counts, unused)


def _expert_kernel(blke_ref, blkn_ref, blkw_ref, slot_g_ref, slot_s_ref, xn_hbm,
                   wg_hbm, wu_hbm, wd_hbm, y_hbm, x0, x1, y0, y1, wg_buf, wu_buf, wd_buf,
                   gsem, ssem, wsem):
    xbufs, ybufs = (x0, x1), (y0, y1)
    s = pl.program_id(0)
    n_tok = xn_hbm.shape[0] // PACK_SUBLANES

    def weight_copies(e, slot):
        return [pltpu.make_async_copy(hbm.at[e], buf.at[slot], wsem.at[slot])
                for hbm, buf in ((wg_hbm, wg_buf), (wu_hbm, wu_buf), (wd_hbm, wd_buf))]

    blk = jnp.maximum(s - 1, 0)
    run_info = blkw_ref[blk]
    w_slot = (run_info >> 1) & 1
    next_expert = (run_info >> 2) - 1

    def tile(ref, r):
        start = r * PACK_SUBLANES
        if not isinstance(r, int):
            start = pl.multiple_of(start, PACK_SUBLANES)
        return ref.at[pl.ds(start, PACK_SUBLANES)]

    def rows(j):
        n = blkn_ref[jnp.clip(j, 0, blkn_ref.shape[0] - 1)]
        return jnp.where(j >= 0, (n + (ROW_GROUP - 1)) & ~(ROW_GROUP - 1), 0)
    rows0, rows1, rows2, rows3 = rows(s), rows(s - 1), rows(s - 2), rows(s - 3)

    def stage(nxt):
        cur = 1 - nxt
        x_nxt, x_cur, y_nxt, y_cur = xbufs[nxt], xbufs[cur], ybufs[nxt], ybufs[cur]

        def per_group(n_rows, issue_row):
            for g in range(MOE_BLK // ROW_GROUP):
                @pl.when(g * ROW_GROUP < n_rows)
                def _():
                    for r in range(g * ROW_GROUP, (g + 1) * ROW_GROUP):
                        issue_row(r)

        def gather_row(r):
            tok = slot_g_ref[0, 0, r] & (n_tok - 1)
            pltpu.make_async_copy(tile(xn_hbm, tok), tile(x_nxt, r), gsem.at[nxt]).start()

        def scatter_row(r):
            pltpu.make_async_copy(tile(y_nxt, r), tile(y_hbm, slot_s_ref[0, 0, r]),
                                  ssem.at[nxt]).start()

        def wait_rows(src, dst, sem, n_rows):
            n_words = pl.multiple_of(n_rows * PACK_SUBLANES, ROW_GROUP * PACK_SUBLANES)

            @pl.when(n_rows > 0)
            def _():
                pltpu.make_async_copy(src.at[pl.ds(0, n_words)], dst.at[pl.ds(0, n_words)],
                                      sem).wait()

        def compute():
            xb = _load_packed_rows(x_cur, 0, MOE_BLK).astype(BF16)
            a = jnp.dot(xb, wg_buf[w_slot].astype(BF16), preferred_element_type=F32)
            u = jnp.dot(xb, wu_buf[w_slot].astype(BF16), preferred_element_type=F32)
            hmid = (a * jax.nn.sigmoid(a) * u).astype(BF16)
            _store_packed_rows(y_cur, 0, jnp.dot(hmid, wd_buf[w_slot].astype(BF16),
                                                 preferred_element_type=F32))

        wait_rows(xn_hbm, x_cur, gsem.at[cur], rows1)
        wait_rows(y_cur, y_hbm, ssem.at[cur], rows3)

        @pl.when((rows1 > 0) & ((run_info & 1) == 1))
        def _():
            for copy in weight_copies(blke_ref[blk], w_slot):
                copy.wait()

            @pl.when(next_expert >= 0)
            def _():
                for copy in weight_copies(next_expert, 1 - w_slot):
                    copy.start(priority=WEIGHT_DMA_PRIORITY)

        if nxt == 0:
            @pl.when((s == 0) & (blkn_ref[0] > 0))
            def _():
                for copy in weight_copies(blke_ref[0], 0):
                    copy.start(priority=WEIGHT_DMA_PRIORITY)

            @pl.when(s == 0)
            def _():
                x_nxt[...] = jnp.zeros(x_nxt.shape, x_nxt.dtype)
                x_cur[...] = jnp.zeros(x_cur.shape, x_cur.dtype)
                y_nxt[...] = jnp.zeros(y_nxt.shape, y_nxt.dtype)
                dump = pltpu.make_async_copy(
                    y_nxt, y_hbm.at[pl.ds(TOP_K * n_tok * PACK_SUBLANES,
                                          MOE_BLK * PACK_SUBLANES)], ssem.at[nxt])
                dump.start()
                dump.wait()

        per_group(rows0, gather_row)
        per_group(rows2, scatter_row)

        @pl.when(rows1 > 0)
        def _():
            compute()

    for parity in range(2):
        pl.when((s & 1) == parity)(functools.partial(stage, parity))


def _expert_call(blk_e, blk_n, blk_w, slots, xn, wg, wu, wd):
    T = xn.shape[0] // PACK_SUBLANES
    assert T & (T - 1) == 0, "token id is recovered from the assignment id with a mask"
    n_tab = blk_e.shape[0]
    _, D, F = wg.shape
    packed_block = pltpu.VMEM((MOE_BLK * PACK_SUBLANES, LANES), jnp.uint32)
    hbm = pl.BlockSpec(memory_space=pl.ANY)
    grid_spec = pltpu.PrefetchScalarGridSpec(
        num_scalar_prefetch=3,
        grid=(n_tab,),
        in_specs=[
            pl.BlockSpec((1, 1, MOE_BLK), lambda s, *_: (s, 0, 0), memory_space=pltpu.SMEM),
            pl.BlockSpec((1, 1, MOE_BLK), lambda s, *_: (jnp.maximum(s - 2, 0), 0, 0),
                         memory_space=pltpu.SMEM),
            hbm, hbm, hbm, hbm,
        ],
        out_specs=hbm,
        scratch_shapes=[packed_block] * 4
        + [pltpu.VMEM((2, D, F), F32), pltpu.VMEM((2, D, F), F32), pltpu.VMEM((2, F, D), F32)]
        + [pltpu.SemaphoreType.DMA((2,))] * 3,
    )
    slots3 = slots.reshape(n_tab, 1, MOE_BLK)
    return pl.pallas_call(
        _expert_kernel,
        grid_spec=grid_spec,
        out_shape=jax.ShapeDtypeStruct(((T * TOP_K + MOE_BLK) * PACK_SUBLANES, LANES),
                                       jnp.uint32),
        compiler_params=pltpu.CompilerParams(
            dimension_semantics=("arbitrary",), vmem_limit_bytes=VMEM_LIMIT,
            disable_bounds_checks=True),
        name="experts",
    )(blk_e, blk_n, blk_w, slots3, slots3, xn, wg, wu, wd)


def _final_kernel(h1_ref, y1_ref, y2_ref, wts_ref, p_ref, wpp_ref, pg_ref, g3_ref, wpg_ref,
                  o_ref):
    w = wts_ref[...]
    tm = h1_ref.shape[0]
    h2 = (h1_ref[...] + w[:, 0:1] * _load_packed_rows(y1_ref, 0, tm)
          + w[:, 1:2] * _load_packed_rows(y2_ref, 0, tm))
    e = _rms(jnp.dot(p_ref[...].astype(BF16), wpp_ref[...], preferred_element_type=F32),
             pg_ref[...])
    gate = jax.nn.sigmoid(jnp.dot(_rms(h2, g3_ref[...]).astype(BF16), wpg_ref[...],
                                  preferred_element_type=F32))
    o_ref[...] = h2 + gate * e


def _final_call(h1, y, wts, p, wpp, pg, g3, wpg):
    T, D = h1.shape
    tm = TM_FINAL
    row = lambda w: pl.BlockSpec((tm, w), lambda i: (i, 0))
    first = pl.BlockSpec((tm * PACK_SUBLANES, LANES), lambda i: (i, 0))
    second = pl.BlockSpec((tm * PACK_SUBLANES, LANES), lambda i: (i + T // tm, 0))
    return pl.pallas_call(
        _final_kernel,
        grid=(T // tm,),
        in_specs=[row(D), first, second, row(LANES), row(PLE_DIM), _const_spec(wpp.shape),
                  _const_spec(pg.shape), _const_spec(g3.shape), _const_spec(wpg.shape)],
        out_specs=row(D),
        out_shape=jax.ShapeDtypeStruct((T, D), F32),
        compiler_params=pltpu.CompilerParams(
            dimension_semantics=("arbitrary",), vmem_limit_bytes=VMEM_LIMIT),
        name="final",
    )(h1, y, y, wts, p, wpp, pg, g3, wpg)


def _rope_lanes(a):
    z = jnp.zeros(a.shape[:-1] + (ROPE_HALF,), a.dtype)
    return jnp.concatenate([a[..., :ROPE_HALF], z, a[..., ROPE_HALF:], z], axis=-1)


def _head_lanes(a):
    return jnp.concatenate([a[..., :QK_NOPE], _rope_lanes(a[..., QK_NOPE:])], axis=-1)


def _score_bound(q_gain, k_gain):
    return Q_PRESCALE * QK_DIM * jnp.max(jnp.abs(q_gain)) * jnp.max(jnp.abs(k_gain))


def kernel(x, p, positions, norm1_gain, w_in, q_a_gain, w_q_b, kv_a_gain, w_kv_b, q_norm_gain, k_norm_gain, w_pool, pool_scale, w_out, norm2_gain, w_router_group, b_router_group, w_router_expert, b_router_expert, w_exp_gate, w_exp_up, w_exp_down, norm3_gain, w_ple_gate, w_ple_proj, ple_norm_gain):
    B, S, D = x.shape
    T = B * S
    assert x.shape[2] == D_MODEL and S % TQ == 0 and S % TM_PRE == 0
    assert T % TM_POST == 0 and T % TM_FINAL == 0 and T % TM_ROUTE == 0
    layer = 0
    row = lambda a: a[layer].reshape(1, -1)

    n_lat = D_POOL + Q_LORA + KV_LORA
    win = jnp.concatenate([w_in[layer][:, :n_lat], _rope_lanes(w_in[layer][:, n_lat:])],
                          axis=1).astype(BF16)
    wq = _head_lanes(w_q_b[layer].reshape(Q_LORA, N_HEADS, QK_DIM)
                     ).reshape(Q_LORA, N_HEADS * HEAD_W).astype(BF16)
    wkv3 = w_kv_b[layer].reshape(KV_LORA, N_HEADS, QK_NOPE + V_DIM)
    wkv = jnp.concatenate([wkv3[..., :QK_NOPE].reshape(KV_LORA, -1),
                           wkv3[..., QK_NOPE:].reshape(KV_LORA, -1)], axis=1).astype(BF16)
    qng = _head_lanes(q_norm_gain[layer]).reshape(1, HEAD_W)
    kng = _head_lanes(k_norm_gain[layer]).reshape(1, HEAD_W)
    inv_freq = ROPE_THETA ** (-jnp.arange(ROPE_HALF, dtype=F32) / ROPE_HALF)
    invf = _rope_lanes(jnp.concatenate([inv_freq, inv_freq])).reshape(1, LANES)
    sgn = _rope_lanes(jnp.concatenate([-jnp.ones((ROPE_HALF,), F32),
                                       jnp.ones((ROPE_HALF,), F32)])).reshape(1, LANES)
    wr = jnp.zeros((D, LANES), F32)
    wr = wr.at[:, :N_GROUPS].set(w_router_group[layer])
    wr = wr.at[:, N_GROUPS:N_GROUPS + N_EXPERTS].set(w_router_expert[layer])
    wr_hi = wr.astype(BF16)
    wr = jnp.concatenate([wr_hi, (wr - wr_hi.astype(F32)).astype(BF16)], axis=1)
    br = jnp.zeros((1, LANES), F32)
    br = br.at[0, :N_GROUPS].set(b_router_group[layer])
    br = br.at[0, N_GROUPS:N_GROUPS + N_EXPERTS].set(b_router_expert[layer])

    bound = _score_bound(q_norm_gain[layer], k_norm_gain[layer])
    bounded = bound <= SCORE_BOUND_LIMIT
    spare = jnp.arange(LANES) == SPARE_ROPE_LANE
    qpad = jnp.where(spare & bounded, -bound, 0.0).astype(F32).reshape(1, LANES)
    kpad = jnp.where(spare, 1.0, 0.0).astype(F32).reshape(1, LANES)
    ypool, q, k, vt = _pre_call(
        x, positions.reshape(B, S, 1), row(norm1_gain), win, row(q_a_gain), wq, row(kv_a_gain),
        wkv, qng, kng, w_pool[layer].astype(BF16), row(pool_scale), invf, sgn, qpad, kpad)
    yattn = _attn_call(bounded.astype(I32).reshape(1), q, k, vt)
    h1, xn2, logits = _post_call(
        x.reshape(T, D), ypool.reshape(T, D_POOL), yattn.reshape(T, D_ATTN),
        w_out[layer].astype(BF16), row(norm2_gain), wr, br)

    n_assign = T * TOP_K
    n_blocks = (n_assign + N_EXPERTS * (MOE_BLK - 1)) // MOE_BLK
    code, wts, counts = _route_call(logits)
    slots, blk_e, blk_n, blk_w = _plan_call(code[:, :TOP_K].T.reshape(n_assign), counts[0],
                                            n_blocks + MOE_DRAIN_STEPS)
    y = _expert_call(blk_e, blk_n, blk_w, slots, xn2,
                     w_exp_gate[layer], w_exp_up[layer], w_exp_down[layer])

    out = _final_call(h1, y, wts, p[layer].reshape(T, PLE_DIM),
                      w_ple_proj[layer].astype(BF16), row(ple_norm_gain), row(norm3_gain),
                      w_ple_gate[layer].astype(BF16))
    return out.reshape(B, S, D)
```

```python
import functools
import math

import jax
import jax.numpy as jnp
import numpy as np
from jax import lax
from jax.experimental import pallas as pl
from jax.experimental.pallas import tpu as pltpu

F32 = jnp.float32
BF16 = jnp.bfloat16
I32 = jnp.int32

D_MODEL = 2048
PLE_DIM = 256
EPS = 1e-6
POOL_WINDOWS = (2, 4, 8, 16)
POOL_CH = 256
D_POOL = POOL_CH * len(POOL_WINDOWS)
N_HEADS = 8
Q_LORA = 512
KV_LORA = 512
QK_NOPE = 128
QK_ROPE = 64
QK_DIM = QK_NOPE + QK_ROPE
V_DIM = 128
D_ATTN = N_HEADS * V_DIM
ROPE_THETA = 10000.0
ATTN_SCALE = 1.0 / math.sqrt(QK_DIM)
Q_PRESCALE = ATTN_SCALE * math.log2(math.e)
N_GROUPS = 8
EXPERTS_PER_GROUP = 8
N_EXPERTS = N_GROUPS * EXPERTS_PER_GROUP
TOP_K = 2
D_EXPERT = 512

LANES = 128
PACK_SUBLANES = 8
MXU_DIM = 256
VMEM_LIMIT = 56 * 1024 * 1024

HEAD_W = 2 * LANES
ROPE_HALF = QK_ROPE // 2
POOL_HALO = 16
TM_PRE = 512
PRE_SUB = 256
TM_POST = 512
POST_SUB = 256
TM_FINAL = 256
TQ = 512
TK = PRE_SUB
MOE_BLK = 256
MOE_DRAIN_STEPS = 3
ROW_GROUP = 32
WEIGHT_DMA_PRIORITY = 1
PLAN_UNROLL = 16
NEG_BIG = -1e30
V_ROWS = V_DIM + 16
SPARE_ROPE_LANE = ROPE_HALF
SCORE_BOUND_LIMIT = 50.0


def _const_spec(shape):
    nd = len(shape)
    return pl.BlockSpec(shape, lambda *_: (0,) * nd, pipeline_mode=pl.Buffered(1))


def _rms(x, gain):
    return x * lax.rsqrt(jnp.mean(x * x, axis=-1, keepdims=True) + EPS) * gain


def _store_packed_rows(ref, first_row, x):
    rows, d = x.shape
    half = d // 2
    assert half == PACK_SUBLANES * LANES
    for i in range(PACK_SUBLANES):
        lo = x[:, i * LANES:(i + 1) * LANES].astype(BF16).astype(F32)
        hi = x[:, half + i * LANES:half + (i + 1) * LANES].astype(BF16).astype(F32)
        word = (lax.shift_right_logical(pltpu.bitcast(lo, jnp.uint32), jnp.uint32(16))
                | (pltpu.bitcast(hi, jnp.uint32) & jnp.uint32(0xFFFF0000)))
        ref[pl.ds(first_row * PACK_SUBLANES + i, rows, stride=PACK_SUBLANES), :] = word


def _load_packed_rows(ref, first_row, rows):
    lo, hi = [], []
    for i in range(PACK_SUBLANES):
        word = ref[pl.ds(first_row * PACK_SUBLANES + i, rows, stride=PACK_SUBLANES), :]
        lo.append(pltpu.bitcast(lax.shift_left(word, jnp.uint32(16)), F32))
        hi.append(pltpu.bitcast(word & jnp.uint32(0xFFFF0000), F32))
    return jnp.concatenate(lo + hi, axis=1)


def _pre_kernel(x_ref, pos_ref, g1_ref, win_ref, wkr_ref, qag_ref, wq_ref, kvag_ref, wkv_ref,
                qng_ref, kng_ref, wpool_ref, pscale_ref, invf_ref, sgn_ref, qpad_ref, kpad_ref,
                ypool_ref, q_ref, k_ref, vt_ref, carry_ref):
    st = pl.program_id(1)

    @pl.when(st == 0)
    def _():
        carry_ref[...] = jnp.zeros_like(carry_ref)

    halo = carry_ref[...]
    for i in range(x_ref.shape[0] // PRE_SUB):
        halo = _pre_rows(i, st * x_ref.shape[0] + i * PRE_SUB, halo,
                         x_ref, pos_ref, g1_ref, win_ref, wkr_ref, qag_ref, wq_ref, kvag_ref, wkv_ref,
                         qng_ref, kng_ref, wpool_ref, pscale_ref, invf_ref, sgn_ref, qpad_ref, kpad_ref,
                         ypool_ref, q_ref, k_ref, vt_ref)
    carry_ref[...] = halo


def _pre_rows(i, first_pos, halo, x_ref, pos_ref, g1_ref, win_ref, wkr_ref, qag_ref, wq_ref, kvag_ref,
              wkv_ref, qng_ref, kng_ref, wpool_ref, pscale_ref, invf_ref, sgn_ref, qpad_ref, kpad_ref,
              ypool_ref, q_ref, k_ref, vt_ref):
    tm = PRE_SUB
    rs = pl.ds(i * tm, tm)

    hn = _rms(x_ref[rs, :], g1_ref[...])
    hn = hn.astype(BF16)
    z = jnp.dot(hn, win_ref[...], preferred_element_type=F32)
    k_rope = jnp.dot(hn, wkr_ref[...], preferred_element_type=F32)

    u = z[:, :D_POOL]
    ext = jnp.concatenate([halo, u], axis=0)
    row = lax.broadcasted_iota(I32, (tm, 1), 0) + first_pos
    level = ext
    shift = 1
    for g, w in enumerate(POOL_WINDOWS):
        sl = slice(g * POOL_CH, (g + 1) * POOL_CH)
        while shift < w:
            level = level + pltpu.roll(level, shift, 0)
            shift *= 2
        win_sum = level[POOL_HALO:, sl]
        cnt = jnp.minimum(row + 1, w).astype(F32)
        d = win_sum / cnt - u[:, sl]
        y = jnp.dot(d.astype(BF16), wpool_ref[g], preferred_element_type=F32)
        ypool_ref[rs, sl] = (y * pscale_ref[:, sl]).astype(BF16)

    q_lat = z[:, D_POOL:D_POOL + Q_LORA]
    kv_lat = z[:, D_POOL + Q_LORA:D_POOL + Q_LORA + KV_LORA]
    qa = jnp.dot(_rms(q_lat, qag_ref[...]).astype(BF16), wq_ref[...],
                 preferred_element_type=F32)
    kv = jnp.dot(_rms(kv_lat, kvag_ref[...]).astype(BF16), wkv_ref[...],
                 preferred_element_type=F32)

    ang = pos_ref[rs, :].astype(F32) * invf_ref[...]
    cos = jnp.cos(ang)
    sin = jnp.sin(ang) * sgn_ref[...]

    def rot(t):
        return t * cos + pltpu.roll(t, LANES // 2, 1) * sin

    qng = qng_ref[...]
    kng = kng_ref[...]
    kr_rot = rot(k_rope * kng[:, LANES:])
    kr_ssq = jnp.sum(k_rope * k_rope, axis=-1, keepdims=True)
    for h in range(N_HEADS):
        qh = qa[:, h * HEAD_W:(h + 1) * HEAD_W]
        rq = lax.rsqrt(jnp.sum(qh * qh, axis=-1, keepdims=True) / QK_DIM + EPS) * Q_PRESCALE
        qn = qh * rq * qng
        q_ref[0, h, rs, :LANES] = qn[:, :LANES].astype(BF16)
        q_ref[0, h, rs, LANES:] = (rot(qn[:, LANES:]) + qpad_ref[...]).astype(BF16)
        kh = kv[:, h * QK_NOPE:(h + 1) * QK_NOPE]
        rk = lax.rsqrt((jnp.sum(kh * kh, axis=-1, keepdims=True) + kr_ssq) / QK_DIM + EPS)
        k_ref[0, h, rs, :LANES] = (kh * rk * kng[:, :LANES]).astype(BF16)
        k_ref[0, h, rs, LANES:] = (kr_rot * rk + kpad_ref[...]).astype(BF16)
        vh = kv[:, N_HEADS * QK_NOPE + h * V_DIM:N_HEADS * QK_NOPE + (h + 1) * V_DIM]
        vt_ref[0, h, i, :V_DIM, :] = vh.T.astype(BF16)
        vt_ref[0, h, i, V_DIM:, :] = jnp.ones((V_ROWS - V_DIM, tm), BF16)
    return u[tm - POOL_HALO:, :]


def _pre_call(x, pos, g1, win, wkr, qag, wq, kvag, wkv, qng, kng, wpool, pscale, invf, sgn,
              qpad, kpad):
    B, S, D = x.shape
    tm = TM_PRE
    grid = (B, S // tm)
    row_spec = lambda w: pl.BlockSpec((None, tm, w), lambda b, s: (b, s, 0))
    head_spec = lambda w: pl.BlockSpec((1, N_HEADS, tm, w), lambda b, s: (b, 0, s, 0))
    consts = [g1, win, wkr, qag, wq, kvag, wkv, qng, kng, wpool, pscale, invf, sgn, qpad, kpad]
    return pl.pallas_call(
        _pre_kernel,
        grid=grid,
        in_specs=[row_spec(D), row_spec(1)] + [_const_spec(c.shape) for c in consts],
        out_specs=[row_spec(D_POOL), head_spec(HEAD_W), head_spec(HEAD_W),
                   pl.BlockSpec((1, N_HEADS, tm // PRE_SUB, V_ROWS, PRE_SUB),
                                lambda b, s: (b, 0, s, 0, 0))],
        out_shape=[jax.ShapeDtypeStruct((B, S, D_POOL), BF16),
                   jax.ShapeDtypeStruct((B, N_HEADS, S, HEAD_W), BF16),
                   jax.ShapeDtypeStruct((B, N_HEADS, S, HEAD_W), BF16),
                   jax.ShapeDtypeStruct((B, N_HEADS, S // PRE_SUB, V_ROWS, PRE_SUB), BF16)],
        scratch_shapes=[pltpu.VMEM((POOL_HALO, D_POOL), F32)],
        compiler_params=pltpu.CompilerParams(
            dimension_semantics=("arbitrary", "arbitrary"), vmem_limit_bytes=VMEM_LIMIT),
        name="pre",
    )(x, pos, *consts)


def _attn_kernel(bounded_ref, q_ref, k_ref, vt_ref, o_ref, *chain_scratch):
    S = q_ref.shape[2]
    nq = S // TQ
    per_q = TQ // TK
    assert per_q == 2, "two online-softmax chains take the even / odd key tiles"
    chains = (chain_scratch[0:2], chain_scratch[2:4])
    score_bufs = (chain_scratch[4:6], chain_scratch[6:8])
    key_idx = lax.broadcasted_iota(I32, (TK, TQ), 0)
    qry_idx = lax.broadcasted_iota(I32, (TK, TQ), 1)

    diag_masks = (key_idx <= qry_idx, key_idx + TK <= qry_idx)

    def weighted_values(kt, p):
        return jnp.dot(vt_ref[0, 0, kt], p.astype(BF16), preferred_element_type=F32)

    def write_out(qi, acc):
        out = acc[:V_DIM, :] / acc[V_DIM:V_DIM + 1, :]
        start = qi * TQ if isinstance(qi, int) else pl.multiple_of(qi * TQ, TQ)
        o_ref[0, pl.ds(start, TQ), :] = out.T.astype(BF16)

    def bounded_head():
        steps = [(qi, j) for qi in range(nq) for j in range(qi + 1)]

        def pair_scores(step, buf):
            qi, j = step
            q = q_ref[0, 0, qi * TQ:(qi + 1) * TQ, :]
            for c in range(per_q):
                kt = per_q * j + c
                buf[c][...] = lax.dot_general(k_ref[0, 0, kt * TK:(kt + 1) * TK, :], q,
                                              (((1,), (1,)), ((), ())),
                                              preferred_element_type=F32)

        pair_scores(steps[0], score_bufs[0])
        acc = None
        for t, (qi, j) in enumerate(steps):
            if t + 1 < len(steps):
                pair_scores(steps[t + 1], score_bufs[(t + 1) % 2])
            for c in range(per_q):
                st = score_bufs[t % 2][c][...]
                if j == qi:
                    st = jnp.where(diag_masks[c], st, NEG_BIG)
                pv = weighted_values(per_q * j + c, jnp.exp2(st))
                acc = pv if acc is None else acc + pv
            if j == qi:
                write_out(qi, acc)
                acc = None

    def online_q_tile(qi, _):
        q = q_ref[0, 0, pl.ds(pl.multiple_of(qi * TQ, TQ), TQ), :]
        for m_ref, acc_ref in chains:
            m_ref[...] = jnp.full(m_ref.shape, NEG_BIG, F32)
            acc_ref[...] = jnp.zeros(acc_ref.shape, F32)

        def scores(kt):
            k = k_ref[0, 0, pl.ds(pl.multiple_of(kt * TK, TK), TK), :]
            return lax.dot_general(k, q, (((1,), (1,)), ((), ())), preferred_element_type=F32)

        def fold(chain, st, kt, mask):
            m_ref, acc_ref = chain
            if mask is not None:
                st = jnp.where(mask, st, NEG_BIG)
            m = m_ref[...]
            m_new = jnp.maximum(m, jnp.max(st, axis=0, keepdims=True))
            m_ref[...] = m_new
            acc_ref[...] = (jnp.exp2(m - m_new) * acc_ref[...]
                            + weighted_values(kt, jnp.exp2(st - m_new)))

        def pair_scores(j, buf):
            for c in range(per_q):
                buf[c][...] = scores(per_q * j + c)

        def fold_pair(j, buf, masks=(None, None)):
            for c in range(per_q):
                fold(chains[c], buf[c][...], per_q * j + c, masks[c])

        pair_scores(0, score_bufs[0])

        def two_pairs(i, _):
            pair_scores(2 * i + 1, score_bufs[1])
            fold_pair(2 * i, score_bufs[0])
            pair_scores(2 * i + 2, score_bufs[0])
            fold_pair(2 * i + 1, score_bufs[1])
            return 0
        lax.fori_loop(0, qi // 2, two_pairs, 0)

        @pl.when(qi % 2 == 0)
        def _():
            fold_pair(qi, score_bufs[0], diag_masks)

        @pl.when(qi % 2 == 1)
        def _():
            pair_scores(qi, score_bufs[1])
            fold_pair(qi - 1, score_bufs[0])
            fold_pair(qi, score_bufs[1], diag_masks)

        (m0, acc0), (m1, acc1) = chains
        m = jnp.maximum(m0[...], m1[...])
        write_out(qi, jnp.exp2(m0[...] - m) * acc0[...] + jnp.exp2(m1[...] - m) * acc1[...])
        return 0

    pl.when(bounded_ref[0] != 0)(bounded_head)

    @pl.when(bounded_ref[0] == 0)
    def _():
        lax.fori_loop(0, nq, online_q_tile, 0)


def _attn_call(bounded, q, k, vt):
    B, H, S, _ = q.shape
    head = lambda w: pl.BlockSpec((1, 1, S, w), lambda b, h, flag: (b, h, 0, 0))
    grid_spec = pltpu.PrefetchScalarGridSpec(
        num_scalar_prefetch=1,
        grid=(B, H),
        in_specs=[head(HEAD_W), head(HEAD_W),
                  pl.BlockSpec((1, 1) + vt.shape[2:], lambda b, h, flag: (b, h, 0, 0, 0))],
        out_specs=pl.BlockSpec((1, S, V_DIM), lambda b, h, flag: (b, 0, h)),
        scratch_shapes=[pltpu.VMEM((1, TQ), F32), pltpu.VMEM((V_ROWS, TQ), F32)] * 2
        + [pltpu.VMEM((TK, TQ), F32)] * 4,
    )
    return pl.pallas_call(
        _attn_kernel,
        grid_spec=grid_spec,
        out_shape=jax.ShapeDtypeStruct((B, S, H * V_DIM), BF16),
        compiler_params=pltpu.CompilerParams(
            dimension_semantics=("arbitrary", "arbitrary"), vmem_limit_bytes=VMEM_LIMIT),
        name="attn",
    )(bounded, q, k, vt)


def _post_kernel(x_ref, yp_ref, ya_ref, wo_ref, g2_ref, wr_ref, br_ref,
                 h1_ref, xn_ref, code_ref, wts_ref, cnt_ref, carry_ref):
    @pl.when(pl.program_id(0) == 0)
    def _():
        carry_ref[...] = jnp.zeros_like(carry_ref)

    for i in range(x_ref.shape[0] // POST_SUB):
        rs = pl.ds(i * POST_SUB, POST_SUB)
        h1 = (x_ref[rs, :]
              + jnp.dot(yp_ref[rs, :], wo_ref[:D_POOL, :], preferred_element_type=F32)
              + jnp.dot(ya_ref[rs, :], wo_ref[D_POOL:, :], preferred_element_type=F32))
        h1_ref[rs, :] = h1
        xn = _rms(h1, g2_ref[...])
        _store_packed_rows(xn_ref, i * POST_SUB, xn)
        xn_hi = xn.astype(BF16)
        xn_lo = (xn - xn_hi.astype(F32)).astype(BF16)
        both = (jnp.dot(xn_hi, wr_ref[...], preferred_element_type=F32)
                + jnp.dot(xn_lo, wr_ref[...], preferred_element_type=F32))
        logits = both[:, :LANES] + both[:, LANES:] + br_ref[...]
        code_t, wts_t = _route_cols(logits.T, carry_ref)
        code_ref[:, rs] = code_t
        wts_cols = jnp.concatenate(
            [wts_t, jnp.zeros((LANES - PACK_SUBLANES, POST_SUB), F32)], axis=0)
        wts_ref[rs, :] = wts_cols.T
    cnt_ref[...] = carry_ref[...].astype(I32)


def _post_call(x, yp, ya, wo, g2, wr, br):
    T, D = x.shape
    tm = TM_POST
    row = lambda w: pl.BlockSpec((tm, w), lambda i: (i, 0))
    return pl.pallas_call(
        _post_kernel,
        grid=(T // tm,),
        in_specs=[row(D), row(D_POOL), row(D_ATTN), _const_spec(wo.shape),
                  _const_spec(g2.shape), _const_spec(wr.shape), _const_spec(br.shape)],
        out_specs=[row(D), pl.BlockSpec((tm * PACK_SUBLANES, LANES), lambda i: (i, 0)),
                   pl.BlockSpec((PACK_SUBLANES, tm), lambda i: (0, i)), row(LANES),
                   pl.BlockSpec((LANES, 1), lambda i: (0, 0))],
        out_shape=[jax.ShapeDtypeStruct((T, D), F32),
                   jax.ShapeDtypeStruct((T * PACK_SUBLANES, LANES), jnp.uint32),
                   jax.ShapeDtypeStruct((PACK_SUBLANES, T), I32),
                   jax.ShapeDtypeStruct((T, LANES), F32),
                   jax.ShapeDtypeStruct((LANES, 1), I32)],
        scratch_shapes=[pltpu.VMEM((LANES, 1), F32)],
        compiler_params=pltpu.CompilerParams(
            dimension_semantics=("arbitrary",), vmem_limit_bytes=VMEM_LIMIT),
        name="post",
    )(x, yp, ya, wo, g2, wr, br)


def _route_cols(lt, carry_ref):
    tm = lt.shape[1]
    sub = lax.broadcasted_iota(I32, (PACK_SUBLANES, tm), 0).astype(F32)

    def first_argmax(vals):
        mx = jnp.max(vals, axis=0, keepdims=True)
        idx = jnp.min(jnp.where(vals == mx, sub, float(PACK_SUBLANES)), axis=0, keepdims=True)
        return mx, idx

    g_logits = lt[0:N_GROUPS, :]
    g_max, g_sel = first_argmax(g_logits)
    g_w = 1.0 / jnp.sum(jnp.exp(g_logits - g_max), axis=0, keepdims=True)

    e_logits = lt[N_GROUPS:N_GROUPS + EXPERTS_PER_GROUP, :]
    for g in range(1, N_GROUPS):
        lo = N_GROUPS + g * EXPERTS_PER_GROUP
        e_logits = jnp.where(g_sel == float(g), lt[lo:lo + EXPERTS_PER_GROUP, :], e_logits)
    v1, i1 = first_argmax(e_logits)
    v2, i2 = first_argmax(jnp.where(sub == i1, -jnp.inf, e_logits))
    e1 = (g_sel * EXPERTS_PER_GROUP + i1).astype(I32)
    e2 = (g_sel * EXPERTS_PER_GROUP + i2).astype(I32)
    t = jnp.exp(v2 - v1)
    w1 = g_w / (1.0 + t)
    w2 = g_w * t / (1.0 + t)

    expert_row = lax.broadcasted_iota(I32, (LANES, tm), 0)
    hit1 = expert_row == e1
    hit2 = expert_row == e2
    onehot = jnp.where(hit1 | hit2, 1.0, 0.0)
    r = lax.broadcasted_iota(I32, (tm, tm), 0)
    c = lax.broadcasted_iota(I32, (tm, tm), 1)
    earlier = jnp.where(r < c, 1.0, 0.0).astype(BF16)
    before = (jnp.dot(onehot.astype(BF16), earlier, preferred_element_type=F32)
              + carry_ref[...])
    carry_ref[...] += jnp.sum(onehot, axis=1, keepdims=True)
    pos1 = jnp.sum(jnp.where(hit1, before, 0.0), axis=0, keepdims=True).astype(I32)
    pos2 = jnp.sum(jnp.where(hit2, before, 0.0), axis=0, keepdims=True).astype(I32)
    code1 = e1 * 65536 + pos1
    code2 = e2 * 65536 + pos2
    first, second = sub == 0.0, sub == 1.0
    return (jnp.where(first, code1, jnp.where(second, code2, 0)),
            jnp.where(first, w1, jnp.where(second, w2, 0.0)))


def _plan_kernel(code_ref, cnt_ref, unused_hbm, slots_hbm, blke_ref, blkn_ref, blkw_ref,
                 slot_ref, start_ref, sem):
    n_assign = code_ref.shape[0]
    n_blocks = blke_ref.shape[0]
    load = pltpu.make_async_copy(unused_hbm, slot_ref, sem)
    load.start()

    def per_expert(e, nb_done):
        cnt = cnt_ref[e]
        nb = lax.shift_right_logical(cnt + (MOE_BLK - 1), MOE_BLK.bit_length() - 1)
        start_ref[e] = nb_done * MOE_BLK

        def per_block(j, _):
            blke_ref[nb_done + j] = e
            blkn_ref[nb_done + j] = jnp.minimum(cnt - j * MOE_BLK, MOE_BLK)
            return 0
        lax.fori_loop(0, nb, per_block, 0)
        return nb_done + nb
    used = lax.fori_loop(0, N_EXPERTS, per_expert, 0)

    last_e = blke_ref[jnp.maximum(used - 1, 0)]

    def tail(b, _):
        blke_ref[b] = last_e
        blkn_ref[b] = 0
        blkw_ref[b] = 0
        return 0
    lax.fori_loop(used, n_blocks, tail, 0)

    def runs_backward(i, carry):
        later_e, next_e = carry
        b = used - 1 - i
        e = blke_ref[b]
        next_e = jnp.where(e != later_e, later_e, next_e)
        blkw_ref[b] = (next_e + 1) * 4
        return e, next_e
    lax.fori_loop(0, used, runs_backward, (last_e, -1))

    def runs_forward(b, carry):
        earlier_e, run = carry
        e = blke_ref[b]
        first = (e != earlier_e).astype(I32)
        run = run + first
        blkw_ref[b] = blkw_ref[b] + first + (run & 1) * 2
        return e, run
    lax.fori_loop(0, used, runs_forward, (-1, -1))

    load.wait()

    def place(a, _):
        code = code_ref[a]
        e = lax.shift_right_logical(code, 16)
        slot_ref[start_ref[e] + (code & 0xFFFF)] = a
        return 0
    lax.fori_loop(0, n_assign, place, 0, unroll=PLAN_UNROLL)

    store = pltpu.make_async_copy(slot_ref, slots_hbm, sem)
    store.start()
    store.wait()


def _plan_call(code_flat, counts, n_tab):
    n_assign = code_flat.shape[0]
    n_slots = n_tab * MOE_BLK
    unused = (n_assign + (np.arange(n_slots) & (MOE_BLK - 1))).astype(np.int32)
    smem = pl.BlockSpec(memory_space=pltpu.SMEM)
    hbm = pl.BlockSpec(memory_space=pl.ANY)
    return pl.pallas_call(
        _plan_kernel,
        in_specs=[smem, smem, hbm],
        out_specs=[hbm, smem, smem, smem],
        out_shape=[jax.ShapeDtypeStruct((n_slots,), I32),
                   jax.ShapeDtypeStruct((n_tab,), I32),
                   jax.ShapeDtypeStruct((n_tab,), I32),
                   jax.ShapeDtypeStruct((n_tab,), I32)],
        scratch_shapes=[pltpu.SMEM((n_slots,), I32), pltpu.SMEM((N_EXPERTS,), I32),
                        pltpu.SemaphoreType.DMA],
        name="plan",
    )(code_flat, counts, unused)


def _expert_kernel(blke_ref, blkn_ref, blkw_ref, slot_g_ref, slot_s_ref, xn_hbm,
                   wg_hbm, wu_hbm, wd_hbm, y_hbm, x0, x1, y0, y1, wg_buf, wu_buf, wd_buf,
                   gsem, ssem, wsem):
    xbufs, ybufs = (x0, x1), (y0, y1)
    s = pl.program_id(0)
    n_tok = xn_hbm.shape[0] // PACK_SUBLANES

    def weight_copies(e, slot):
        return [pltpu.make_async_copy(hbm.at[e], buf.at[slot], wsem.at[slot])
                for hbm, buf in ((wg_hbm, wg_buf), (wu_hbm, wu_buf), (wd_hbm, wd_buf))]

    blk = jnp.maximum(s - 1, 0)
    run_info = blkw_ref[blk]
    w_slot = (run_info >> 1) & 1
    next_expert = (run_info >> 2) - 1

    def tile(ref, r):
        start = r * PACK_SUBLANES
        if not isinstance(r, int):
            start = pl.multiple_of(start, PACK_SUBLANES)
        return ref.at[pl.ds(start, PACK_SUBLANES)]

    def rows(j):
        n = blkn_ref[jnp.clip(j, 0, blkn_ref.shape[0] - 1)]
        return jnp.where(j >= 0, (n + (ROW_GROUP - 1)) & ~(ROW_GROUP - 1), 0)
    rows0, rows1, rows2, rows3 = rows(s), rows(s - 1), rows(s - 2), rows(s - 3)

    def stage(nxt):
        cur = 1 - nxt
        x_nxt, x_cur, y_nxt, y_cur = xbufs[nxt], xbufs[cur], ybufs[nxt], ybufs[cur]

        def per_group(n_rows, issue_row):
            for g in range(MOE_BLK // ROW_GROUP):
                @pl.when(g * ROW_GROUP < n_rows)
                def _():
                    for r in range(g * ROW_GROUP, (g + 1) * ROW_GROUP):
                        issue_row(r)

        def gather_row(r):
            tok = slot_g_ref[0, 0, r] & (n_tok - 1)
            pltpu.make_async_copy(tile(xn_hbm, tok), tile(x_nxt, r), gsem.at[nxt]).start()

        def scatter_row(r):
            pltpu.make_async_copy(tile(y_nxt, r), tile(y_hbm, slot_s_ref[0, 0, r]),
                                  ssem.at[nxt]).start()

        def wait_rows(src, dst, sem, n_rows):
            n_words = pl.multiple_of(n_rows * PACK_SUBLANES, ROW_GROUP * PACK_SUBLANES)

            @pl.when(n_rows > 0)
            def _():
                pltpu.make_async_copy(src.at[pl.ds(0, n_words)], dst.at[pl.ds(0, n_words)],
                                      sem).wait()

        def compute():
            xb = _load_packed_rows(x_cur, 0, MOE_BLK).astype(BF16)
            a = jnp.dot(xb, wg_buf[w_slot].astype(BF16), preferred_element_type=F32)
            u = jnp.dot(xb, wu_buf[w_slot].astype(BF16), preferred_element_type=F32)
            hmid = (a * jax.nn.sigmoid(a) * u).astype(BF16)
            _store_packed_rows(y_cur, 0, jnp.dot(hmid, wd_buf[w_slot].astype(BF16),
                                                 preferred_element_type=F32))

        wait_rows(xn_hbm, x_cur, gsem.at[cur], rows1)
        wait_rows(y_cur, y_hbm, ssem.at[cur], rows3)

        @pl.when((rows1 > 0) & ((run_info & 1) == 1))
        def _():
            for copy in weight_copies(blke_ref[blk], w_slot):
                copy.wait()

            @pl.when(next_expert >= 0)
            def _():
                for copy in weight_copies(next_expert, 1 - w_slot):
                    copy.start(priority=WEIGHT_DMA_PRIORITY)

        if nxt == 0:
            @pl.when((s == 0) & (blkn_ref[0] > 0))
            def _():
                for copy in weight_copies(blke_ref[0], 0):
                    copy.start(priority=WEIGHT_DMA_PRIORITY)

            @pl.when(s == 0)
            def _():
                x_nxt[...] = jnp.zeros(x_nxt.shape, x_nxt.dtype)
                x_cur[...] = jnp.zeros(x_cur.shape, x_cur.dtype)
                y_nxt[...] = jnp.zeros(y_nxt.shape, y_nxt.dtype)
                dump = pltpu.make_async_copy(
                    y_nxt, y_hbm.at[pl.ds(TOP_K * n_tok * PACK_SUBLANES,
                                          MOE_BLK * PACK_SUBLANES)], ssem.at[nxt])
                dump.start()
                dump.wait()

        per_group(rows0, gather_row)
        per_group(rows2, scatter_row)

        @pl.when(rows1 > 0)
        def _():
            compute()

    for parity in range(2):
        pl.when((s & 1) == parity)(functools.partial(stage, parity))


def _expert_call(blk_e, blk_n, blk_w, slots, xn, wg, wu, wd):
    T = xn.shape[0] // PACK_SUBLANES
    assert T & (T - 1) == 0, "token id is recovered from the assignment id with a mask"
    n_tab = blk_e.shape[0]
    _, D, F = wg.shape
    packed_block = pltpu.VMEM((MOE_BLK * PACK_SUBLANES, LANES), jnp.uint32)
    hbm = pl.BlockSpec(memory_space=pl.ANY)
    grid_spec = pltpu.PrefetchScalarGridSpec(
        num_scalar_prefetch=3,
        grid=(n_tab,),
        in_specs=[
            pl.BlockSpec((1, 1, MOE_BLK), lambda s, *_: (s, 0, 0), memory_space=pltpu.SMEM),
            pl.BlockSpec((1, 1, MOE_BLK), lambda s, *_: (jnp.maximum(s - 2, 0), 0, 0),
                         memory_space=pltpu.SMEM),
            hbm, hbm, hbm, hbm,
        ],
        out_specs=hbm,
        scratch_shapes=[packed_block] * 4
        + [pltpu.VMEM((2, D, F), F32), pltpu.VMEM((2, D, F), F32), pltpu.VMEM((2, F, D), F32)]
        + [pltpu.SemaphoreType.DMA((2,))] * 3,
    )
    slots3 = slots.reshape(n_tab, 1, MOE_BLK)
    return pl.pallas_call(
        _expert_kernel,
        grid_spec=grid_spec,
        out_shape=jax.ShapeDtypeStruct(((T * TOP_K + MOE_BLK) * PACK_SUBLANES, LANES),
                                       jnp.uint32),
        compiler_params=pltpu.CompilerParams(
            dimension_semantics=("arbitrary",), vmem_limit_bytes=VMEM_LIMIT,
            disable_bounds_checks=True),
        name="experts",
    )(blk_e, blk_n, blk_w, slots3, slots3, xn, wg, wu, wd)


def _final_kernel(h1_ref, y1_ref, y2_ref, wts_ref, p_ref, wpp_ref, pg_ref, g3_ref, wpg_ref,
                  o_ref):
    w = wts_ref[...]
    tm = h1_ref.shape[0]
    h2 = (h1_ref[...] + w[:, 0:1] * _load_packed_rows(y1_ref, 0, tm)
          + w[:, 1:2] * _load_packed_rows(y2_ref, 0, tm))
    e = _rms(jnp.dot(p_ref[...].astype(BF16), wpp_ref[...], preferred_element_type=F32),
             pg_ref[...])
    gate = jax.nn.sigmoid(jnp.dot(_rms(h2, g3_ref[...]).astype(BF16), wpg_ref[...],
                                  preferred_element_type=F32))
    o_ref[...] = h2 + gate * e


def _final_call(h1, y, wts, p, wpp, pg, g3, wpg):
    T, D = h1.shape
    tm = TM_FINAL
    row = lambda w: pl.BlockSpec((tm, w), lambda i: (i, 0))
    first = pl.BlockSpec((tm * PACK_SUBLANES, LANES), lambda i: (i, 0))
    second = pl.BlockSpec((tm * PACK_SUBLANES, LANES), lambda i: (i + T // tm, 0))
    return pl.pallas_call(
        _final_kernel,
        grid=(T // tm,),
        in_specs=[row(D), first, second, row(LANES), row(PLE_DIM), _const_spec(wpp.shape),
                  _const_spec(pg.shape), _const_spec(g3.shape), _const_spec(wpg.shape)],
        out_specs=row(D),
        out_shape=jax.ShapeDtypeStruct((T, D), F32),
        compiler_params=pltpu.CompilerParams(
            dimension_semantics=("arbitrary",), vmem_limit_bytes=VMEM_LIMIT),
        name="final",
    )(h1, y, y, wts, p, wpp, pg, g3, wpg)


def _rope_lanes(a):
    z = jnp.zeros(a.shape[:-1] + (ROPE_HALF,), a.dtype)
    return jnp.concatenate([a[..., :ROPE_HALF], z, a[..., ROPE_HALF:], z], axis=-1)


def _head_lanes(a):
    return jnp.concatenate([a[..., :QK_NOPE], _rope_lanes(a[..., QK_NOPE:])], axis=-1)


def _score_bound(q_gain, k_gain):
    return Q_PRESCALE * QK_DIM * jnp.max(jnp.abs(q_gain)) * jnp.max(jnp.abs(k_gain))


def kernel(x, p, positions, norm1_gain, w_in, q_a_gain, w_q_b, kv_a_gain, w_kv_b, q_norm_gain, k_norm_gain, w_pool, pool_scale, w_out, norm2_gain, w_router_group, b_router_group, w_router_expert, b_router_expert, w_exp_gate, w_exp_up, w_exp_down, norm3_gain, w_ple_gate, w_ple_proj, ple_norm_gain):
    B, S, D = x.shape
    T = B * S
    assert x.shape[2] == D_MODEL and S % TQ == 0 and S % TM_PRE == 0
    assert T % TM_POST == 0 and T % TM_FINAL == 0
    layer = 0
    row = lambda a: a[layer].reshape(1, -1)

    n_lat = D_POOL + Q_LORA + KV_LORA
    win = w_in[layer][:, :n_lat].astype(BF16)
    wkr = _rope_lanes(w_in[layer][:, n_lat:]).astype(BF16)
    wq = _head_lanes(w_q_b[layer].reshape(Q_LORA, N_HEADS, QK_DIM)
                     ).reshape(Q_LORA, N_HEADS * HEAD_W).astype(BF16)
    wkv3 = w_kv_b[layer].reshape(KV_LORA, N_HEADS, QK_NOPE + V_DIM)
    wkv = jnp.concatenate([wkv3[..., :QK_NOPE].reshape(KV_LORA, -1),
                           wkv3[..., QK_NOPE:].reshape(KV_LORA, -1)], axis=1).astype(BF16)
    qng = _head_lanes(q_norm_gain[layer]).reshape(1, HEAD_W)
    kng = _head_lanes(k_norm_gain[layer]).reshape(1, HEAD_W)
    inv_freq = np.float32(ROPE_THETA) ** (-np.arange(ROPE_HALF, dtype=np.float32) / ROPE_HALF)
    zeros_half = np.zeros((ROPE_HALF,), np.float32)
    ones_half = np.ones((ROPE_HALF,), np.float32)
    invf = np.concatenate([inv_freq, zeros_half, inv_freq, zeros_half]).reshape(1, LANES)
    sgn = np.concatenate([-ones_half, zeros_half, ones_half, zeros_half]).reshape(1, LANES)
    spare = np.arange(LANES) == SPARE_ROPE_LANE
    kpad = spare.astype(np.float32).reshape(1, LANES)
    pad_lanes = LANES - N_GROUPS - N_EXPERTS
    wr = jnp.concatenate([w_router_group[layer], w_router_expert[layer],
                          jnp.zeros((D, pad_lanes), F32)], axis=1)
    wr_hi = wr.astype(BF16)
    wr = jnp.concatenate([wr_hi, (wr - wr_hi.astype(F32)).astype(BF16)], axis=1)
    br = jnp.concatenate([b_router_group[layer], b_router_expert[layer],
                          jnp.zeros((pad_lanes,), F32)]).reshape(1, LANES)

    bound = _score_bound(q_norm_gain[layer], k_norm_gain[layer])
    bounded = bound <= SCORE_BOUND_LIMIT
    qpad = jnp.where(bounded, -bound, 0.0) * kpad
    ypool, q, k, vt = _pre_call(
        x, positions.reshape(B, S, 1), row(norm1_gain), win, wkr, row(q_a_gain), wq,
        row(kv_a_gain), wkv, qng, kng, w_pool[layer].astype(BF16), row(pool_scale), invf, sgn,
        qpad, kpad)
    yattn = _attn_call(bounded.astype(I32).reshape(1), q, k, vt)
    h1, xn2, code, wts, counts = _post_call(
        x.reshape(T, D), ypool.reshape(T, D_POOL), yattn.reshape(T, D_ATTN),
        w_out[layer].astype(BF16), row(norm2_gain), wr, br)

    n_assign = T * TOP_K
    n_blocks = (n_assign + N_EXPERTS * (MOE_BLK - 1)) // MOE_BLK
    slots, blk_e, blk_n, blk_w = _plan_call(code[:TOP_K].reshape(n_assign),
                                            counts.reshape(LANES), n_blocks + MOE_DRAIN_STEPS)
    y = _expert_call(blk_e, blk_n, blk_w, slots, xn2,
                     w_exp_gate[layer], w_exp_up[layer], w_exp_down[layer])

    out = _final_call(h1, y, wts, p[layer].reshape(T, PLE_DIM),
                      w_ple_proj[layer].astype(BF16), row(ple_norm_gain), row(norm3_gain),
                      w_ple_gate[layer].astype(BF16))
    return out.reshape(B, S, D)
```

```python
import functools
import math

import jax
import jax.numpy as jnp
import numpy as np
from jax import lax
from jax.experimental import pallas as pl
from jax.experimental.pallas import tpu as pltpu

F32 = jnp.float32
BF16 = jnp.bfloat16
I32 = jnp.int32

D_MODEL = 2048
PLE_DIM = 256
EPS = 1e-6
POOL_WINDOWS = (2, 4, 8, 16)
POOL_CH = 256
D_POOL = POOL_CH * len(POOL_WINDOWS)
N_HEADS = 8
Q_LORA = 512
KV_LORA = 512
QK_NOPE = 128
QK_ROPE = 64
QK_DIM = QK_NOPE + QK_ROPE
V_DIM = 128
D_ATTN = N_HEADS * V_DIM
ROPE_THETA = 10000.0
ATTN_SCALE = 1.0 / math.sqrt(QK_DIM)
Q_PRESCALE = ATTN_SCALE * math.log2(math.e)
N_GROUPS = 8
EXPERTS_PER_GROUP = 8
N_EXPERTS = N_GROUPS * EXPERTS_PER_GROUP
TOP_K = 2
D_EXPERT = 512

LANES = 128
PACK_SUBLANES = 8
MXU_DIM = 256
VMEM_LIMIT = 56 * 1024 * 1024

HEAD_W = 2 * LANES
ROPE_HALF = QK_ROPE // 2
POOL_HALO = 16
TM_PRE = 512
PRE_SUB = 256
TM_POST = 512
POST_SUB = 256
TM_FINAL = 512
FINAL_SUB = 256
WIN_ROWS = 256
TQ = 512
TK = PRE_SUB
TM_ROUTE = 1024
ROUTE_SUB = 256
MOE_BLK = 256
MOE_DRAIN_STEPS = 3
ROW_GROUP = 32
WEIGHT_DMA_PRIORITY = 1
PLAN_UNROLL = 16
NEG_BIG = -1e30
V_ROWS = V_DIM + 16
SPARE_ROPE_LANE = ROPE_HALF
SCORE_BOUND_LIMIT = 50.0


def _const_spec(shape):
    nd = len(shape)
    return pl.BlockSpec(shape, lambda *_: (0,) * nd, pipeline_mode=pl.Buffered(1))


def _rms(x, gain):
    return x * lax.rsqrt(jnp.mean(x * x, axis=-1, keepdims=True) + EPS) * gain


def _store_packed_rows(ref, first_row, x):
    rows, d = x.shape
    half = d // 2
    assert half == PACK_SUBLANES * LANES
    for i in range(PACK_SUBLANES):
        lo = x[:, i * LANES:(i + 1) * LANES].astype(BF16).astype(F32)
        hi = x[:, half + i * LANES:half + (i + 1) * LANES].astype(BF16).astype(F32)
        word = (lax.shift_right_logical(pltpu.bitcast(lo, jnp.uint32), jnp.uint32(16))
                | (pltpu.bitcast(hi, jnp.uint32) & jnp.uint32(0xFFFF0000)))
        ref[pl.ds(first_row * PACK_SUBLANES + i, rows, stride=PACK_SUBLANES), :] = word


def _load_packed_rows(ref, first_row, rows):
    lo, hi = [], []
    for i in range(PACK_SUBLANES):
        word = ref[pl.ds(first_row * PACK_SUBLANES + i, rows, stride=PACK_SUBLANES), :]
        lo.append(pltpu.bitcast(lax.shift_left(word, jnp.uint32(16)), F32))
        hi.append(pltpu.bitcast(word & jnp.uint32(0xFFFF0000), F32))
    return jnp.concatenate(lo + hi, axis=1)


def _pre_kernel(x_ref, pos_ref, g1_ref, win_ref, qag_ref, wq_ref, kvag_ref, wkv_ref,
                qng_ref, kng_ref, wpool_ref, pscale_ref, invf_ref, sgn_ref, qpad_ref, kpad_ref,
                ypool_ref, q_ref, k_ref, vt_ref, carry_ref):
    st = pl.program_id(1)

    @pl.when(st == 0)
    def _():
        carry_ref[...] = jnp.zeros_like(carry_ref)

    halo = carry_ref[...]
    for i in range(x_ref.shape[0] // PRE_SUB):
        halo = _pre_rows(i, st * x_ref.shape[0] + i * PRE_SUB, halo,
                         x_ref, pos_ref, g1_ref, win_ref, qag_ref, wq_ref, kvag_ref, wkv_ref,
                         qng_ref, kng_ref, wpool_ref, pscale_ref, invf_ref, sgn_ref, qpad_ref, kpad_ref,
                         ypool_ref, q_ref, k_ref, vt_ref)
    carry_ref[...] = halo


def _pre_rows(i, first_pos, halo, x_ref, pos_ref, g1_ref, win_ref, qag_ref, wq_ref, kvag_ref,
              wkv_ref, qng_ref, kng_ref, wpool_ref, pscale_ref, invf_ref, sgn_ref, qpad_ref, kpad_ref,
              ypool_ref, q_ref, k_ref, vt_ref):
    tm = PRE_SUB
    rs = pl.ds(i * tm, tm)

    hn = _rms(x_ref[rs, :], g1_ref[...])
    z = jnp.dot(hn.astype(BF16), win_ref[...], preferred_element_type=F32)
    k_rope = z[:, D_POOL + Q_LORA + KV_LORA:]

    u = z[:, :D_POOL]
    ext = jnp.concatenate([halo, u], axis=0)
    row = lax.broadcasted_iota(I32, (tm, 1), 0) + first_pos
    level = ext
    shift = 1
    for g, w in enumerate(POOL_WINDOWS):
        sl = slice(g * POOL_CH, (g + 1) * POOL_CH)
        while shift < w:
            level = level + pltpu.roll(level, shift, 0)
            shift *= 2
        win_sum = level[POOL_HALO:, sl]
        cnt = jnp.minimum(row + 1, w).astype(F32)
        d = win_sum / cnt - u[:, sl]
        y = jnp.dot(d.astype(BF16), wpool_ref[g], preferred_element_type=F32)
        ypool_ref[rs, sl] = (y * pscale_ref[:, sl]).astype(BF16)

    q_lat = z[:, D_POOL:D_POOL + Q_LORA]
    kv_lat = z[:, D_POOL + Q_LORA:D_POOL + Q_LORA + KV_LORA]
    qa = jnp.dot(_rms(q_lat, qag_ref[...]).astype(BF16), wq_ref[...],
                 preferred_element_type=F32)
    kv = jnp.dot(_rms(kv_lat, kvag_ref[...]).astype(BF16), wkv_ref[...],
                 preferred_element_type=F32)

    ang = pos_ref[rs, :].astype(F32) * invf_ref[...]
    cos = jnp.cos(ang)
    sin = jnp.sin(ang) * sgn_ref[...]

    def rot(t):
        return t * cos + pltpu.roll(t, LANES // 2, 1) * sin

    qng = qng_ref[...]
    kng = kng_ref[...]
    kr_rot = rot(k_rope * kng[:, LANES:])
    kr_ssq = jnp.sum(k_rope * k_rope, axis=-1, keepdims=True)
    for h in range(N_HEADS):
        qh = qa[:, h * HEAD_W:(h + 1) * HEAD_W]
        rq = lax.rsqrt(jnp.sum(qh * qh, axis=-1, keepdims=True) / QK_DIM + EPS) * Q_PRESCALE
        qn = qh * rq * qng
        q_ref[0, h, rs, :LANES] = qn[:, :LANES].astype(BF16)
        q_ref[0, h, rs, LANES:] = (rot(qn[:, LANES:]) + qpad_ref[...]).astype(BF16)
        kh = kv[:, h * QK_NOPE:(h + 1) * QK_NOPE]
        rk = lax.rsqrt((jnp.sum(kh * kh, axis=-1, keepdims=True) + kr_ssq) / QK_DIM + EPS)
        k_ref[0, h, rs, :LANES] = (kh * rk * kng[:, :LANES]).astype(BF16)
        k_ref[0, h, rs, LANES:] = (kr_rot * rk + kpad_ref[...]).astype(BF16)
        vh = kv[:, N_HEADS * QK_NOPE + h * V_DIM:N_HEADS * QK_NOPE + (h + 1) * V_DIM]
        vt_ref[0, h, i, :V_DIM, :] = vh.T.astype(BF16)
        vt_ref[0, h, i, V_DIM:, :] = jnp.ones((V_ROWS - V_DIM, tm), BF16)
    return u[tm - POOL_HALO:, :]


def _win_kernel(w_ref, o_ref):
    w = w_ref[...]
    n_lat = o_ref.shape[1] - LANES
    zeros = jnp.zeros((w.shape[0], ROPE_HALF), F32)
    rope = jnp.concatenate([w[:, n_lat:n_lat + ROPE_HALF], zeros,
                            w[:, n_lat + ROPE_HALF:], zeros], axis=1)
    o_ref[:, :n_lat] = w[:, :n_lat].astype(BF16)
    o_ref[:, n_lat:] = rope.astype(BF16)


def _win_call(w_in):
    D, n_in = w_in.shape
    n_lat = n_in - QK_ROPE
    tm = WIN_ROWS
    return pl.pallas_call(
        _win_kernel,
        grid=(D // tm,),
        in_specs=[pl.BlockSpec((tm, n_in), lambda i: (i, 0))],
        out_specs=pl.BlockSpec((tm, n_lat + LANES), lambda i: (i, 0)),
        out_shape=jax.ShapeDtypeStruct((D, n_lat + LANES), BF16),
        compiler_params=pltpu.CompilerParams(
            dimension_semantics=("arbitrary",), vmem_limit_bytes=VMEM_LIMIT),
        name="win",
    )(w_in)


def _pre_call(x, pos, g1, win, qag, wq, kvag, wkv, qng, kng, wpool, pscale, invf, sgn,
              qpad, kpad):
    B, S, D = x.shape
    tm = TM_PRE
    grid = (B, S // tm)
    row_spec = lambda w: pl.BlockSpec((None, tm, w), lambda b, s: (b, s, 0))
    head_spec = lambda w: pl.BlockSpec((1, N_HEADS, tm, w), lambda b, s: (b, 0, s, 0))
    consts = [g1, win, qag, wq, kvag, wkv, qng, kng, wpool, pscale, invf, sgn, qpad, kpad]
    return pl.pallas_call(
        _pre_kernel,
        grid=grid,
        in_specs=[row_spec(D), row_spec(1)] + [_const_spec(c.shape) for c in consts],
        out_specs=[row_spec(D_POOL), head_spec(HEAD_W), head_spec(HEAD_W),
                   pl.BlockSpec((1, N_HEADS, tm // PRE_SUB, V_ROWS, PRE_SUB),
                                lambda b, s: (b, 0, s, 0, 0))],
        out_shape=[jax.ShapeDtypeStruct((B, S, D_POOL), BF16),
                   jax.ShapeDtypeStruct((B, N_HEADS, S, HEAD_W), BF16),
                   jax.ShapeDtypeStruct((B, N_HEADS, S, HEAD_W), BF16),
                   jax.ShapeDtypeStruct((B, N_HEADS, S // PRE_SUB, V_ROWS, PRE_SUB), BF16)],
        scratch_shapes=[pltpu.VMEM((POOL_HALO, D_POOL), F32)],
        compiler_params=pltpu.CompilerParams(
            dimension_semantics=("arbitrary", "arbitrary"), vmem_limit_bytes=VMEM_LIMIT),
        name="pre",
    )(x, pos, *consts)


def _attn_kernel(bounded_ref, q_ref, k_ref, vt_ref, o_ref, *chain_scratch):
    S = q_ref.shape[2]
    nq = S // TQ
    per_q = TQ // TK
    assert per_q == 2, "two online-softmax chains take the even / odd key tiles"
    chains = (chain_scratch[0:2], chain_scratch[2:4])
    score_bufs = (chain_scratch[4:6], chain_scratch[6:8])
    key_idx = lax.broadcasted_iota(I32, (TK, TQ), 0)
    qry_idx = lax.broadcasted_iota(I32, (TK, TQ), 1)

    diag_masks = (key_idx <= qry_idx, key_idx + TK <= qry_idx)

    def weighted_values(kt, p):
        return jnp.dot(vt_ref[0, 0, kt], p.astype(BF16), preferred_element_type=F32)

    def write_out(qi, acc):
        out = acc[:V_DIM, :] / acc[V_DIM:V_DIM + 1, :]
        start = qi * TQ if isinstance(qi, int) else pl.multiple_of(qi * TQ, TQ)
        o_ref[0, pl.ds(start, TQ), :] = out.T.astype(BF16)

    def bounded_head():
        steps = [(qi, j) for qi in range(nq) for j in range(qi + 1)]

        def pair_scores(step, buf):
            qi, j = step
            q = q_ref[0, 0, qi * TQ:(qi + 1) * TQ, :]
            for c in range(per_q):
                kt = per_q * j + c
                buf[c][...] = lax.dot_general(k_ref[0, 0, kt * TK:(kt + 1) * TK, :], q,
                                              (((1,), (1,)), ((), ())),
                                              preferred_element_type=F32)

        pair_scores(steps[0], score_bufs[0])
        acc = None
        for t, (qi, j) in enumerate(steps):
            if t + 1 < len(steps):
                pair_scores(steps[t + 1], score_bufs[(t + 1) % 2])
            for c in range(per_q):
                st = score_bufs[t % 2][c][...]
                if j == qi:
                    st = jnp.where(diag_masks[c], st, NEG_BIG)
                pv = weighted_values(per_q * j + c, jnp.exp2(st))
                acc = pv if acc is None else acc + pv
            if j == qi:
                write_out(qi, acc)
                acc = None

    def online_q_tile(qi, _):
        q = q_ref[0, 0, pl.ds(pl.multiple_of(qi * TQ, TQ), TQ), :]
        for m_ref, acc_ref in chains:
            m_ref[...] = jnp.full(m_ref.shape, NEG_BIG, F32)
            acc_ref[...] = jnp.zeros(acc_ref.shape, F32)

        def scores(kt):
            k = k_ref[0, 0, pl.ds(pl.multiple_of(kt * TK, TK), TK), :]
            return lax.dot_general(k, q, (((1,), (1,)), ((), ())), preferred_element_type=F32)

        def fold(chain, st, kt, mask):
            m_ref, acc_ref = chain
            if mask is not None:
                st = jnp.where(mask, st, NEG_BIG)
            m = m_ref[...]
            m_new = jnp.maximum(m, jnp.max(st, axis=0, keepdims=True))
            m_ref[...] = m_new
            acc_ref[...] = (jnp.exp2(m - m_new) * acc_ref[...]
                            + weighted_values(kt, jnp.exp2(st - m_new)))

        def pair_scores(j, buf):
            for c in range(per_q):
                buf[c][...] = scores(per_q * j + c)

        def fold_pair(j, buf, masks=(None, None)):
            for c in range(per_q):
                fold(chains[c], buf[c][...], per_q * j + c, masks[c])

        pair_scores(0, score_bufs[0])

        def two_pairs(i, _):
            pair_scores(2 * i + 1, score_bufs[1])
            fold_pair(2 * i, score_bufs[0])
            pair_scores(2 * i + 2, score_bufs[0])
            fold_pair(2 * i + 1, score_bufs[1])
            return 0
        lax.fori_loop(0, qi // 2, two_pairs, 0)

        @pl.when(qi % 2 == 0)
        def _():
            fold_pair(qi, score_bufs[0], diag_masks)

        @pl.when(qi % 2 == 1)
        def _():
            pair_scores(qi, score_bufs[1])
            fold_pair(qi - 1, score_bufs[0])
            fold_pair(qi, score_bufs[1], diag_masks)

        (m0, acc0), (m1, acc1) = chains
        m = jnp.maximum(m0[...], m1[...])
        write_out(qi, jnp.exp2(m0[...] - m) * acc0[...] + jnp.exp2(m1[...] - m) * acc1[...])
        return 0

    pl.when(bounded_ref[0] != 0)(bounded_head)

    @pl.when(bounded_ref[0] == 0)
    def _():
        lax.fori_loop(0, nq, online_q_tile, 0)


def _attn_call(bounded, q, k, vt):
    B, H, S, _ = q.shape
    head = lambda w: pl.BlockSpec((1, 1, S, w), lambda b, h, flag: (b, h, 0, 0))
    grid_spec = pltpu.PrefetchScalarGridSpec(
        num_scalar_prefetch=1,
        grid=(B, H),
        in_specs=[head(HEAD_W), head(HEAD_W),
                  pl.BlockSpec((1, 1) + vt.shape[2:], lambda b, h, flag: (b, h, 0, 0, 0))],
        out_specs=pl.BlockSpec((1, S, V_DIM), lambda b, h, flag: (b, 0, h)),
        scratch_shapes=[pltpu.VMEM((1, TQ), F32), pltpu.VMEM((V_ROWS, TQ), F32)] * 2
        + [pltpu.VMEM((TK, TQ), F32)] * 4,
    )
    return pl.pallas_call(
        _attn_kernel,
        grid_spec=grid_spec,
        out_shape=jax.ShapeDtypeStruct((B, S, H * V_DIM), BF16),
        compiler_params=pltpu.CompilerParams(
            dimension_semantics=("arbitrary", "arbitrary"), vmem_limit_bytes=VMEM_LIMIT),
        name="attn",
    )(bounded, q, k, vt)


def _post_kernel(x_ref, yp_ref, ya_ref, wo_ref, g2_ref, wr_ref, br_ref,
                 h1_ref, xn_ref, lg_ref):
    for i in range(x_ref.shape[0] // POST_SUB):
        rs = pl.ds(i * POST_SUB, POST_SUB)
        h1 = (x_ref[rs, :]
              + jnp.dot(yp_ref[rs, :], wo_ref[:D_POOL, :], preferred_element_type=F32)
              + jnp.dot(ya_ref[rs, :], wo_ref[D_POOL:, :], preferred_element_type=F32))
        h1_ref[rs, :] = h1
        xn = _rms(h1, g2_ref[...])
        _store_packed_rows(xn_ref, i * POST_SUB, xn)
        xn_hi = xn.astype(BF16)
        xn_lo = (xn - xn_hi.astype(F32)).astype(BF16)
        both = (jnp.dot(xn_hi, wr_ref[...], preferred_element_type=F32)
                + jnp.dot(xn_lo, wr_ref[...], preferred_element_type=F32))
        lg_ref[rs, :] = both[:, :LANES] + both[:, LANES:] + br_ref[...]


def _post_call(x, yp, ya, wo, g2, wr, br):
    T, D = x.shape
    tm = TM_POST
    row = lambda w: pl.BlockSpec((tm, w), lambda i: (i, 0))
    return pl.pallas_call(
        _post_kernel,
        grid=(T // tm,),
        in_specs=[row(D), row(D_POOL), row(D_ATTN), _const_spec(wo.shape),
                  _const_spec(g2.shape), _const_spec(wr.shape), _const_spec(br.shape)],
        out_specs=[row(D), pl.BlockSpec((tm * PACK_SUBLANES, LANES), lambda i: (i, 0)),
                   row(LANES)],
        out_shape=[jax.ShapeDtypeStruct((T, D), F32),
                   jax.ShapeDtypeStruct((T * PACK_SUBLANES, LANES), jnp.uint32),
                   jax.ShapeDtypeStruct((T, LANES), F32)],
        compiler_params=pltpu.CompilerParams(
            dimension_semantics=("arbitrary",), vmem_limit_bytes=VMEM_LIMIT),
        name="post",
    )(x, yp, ya, wo, g2, wr, br)


def _route_kernel(lg_ref, code_ref, wts_ref, cnt_ref, carry_ref):
    @pl.when(pl.program_id(0) == 0)
    def _():
        carry_ref[...] = jnp.zeros_like(carry_ref)

    for i in range(lg_ref.shape[0] // ROUTE_SUB):
        rs = pl.ds(i * ROUTE_SUB, ROUTE_SUB)
        code_t, wts_t = _route_cols(lg_ref[rs, :].T, carry_ref)
        code_ref[:, rs] = code_t
        wts_cols = jnp.concatenate(
            [wts_t, jnp.zeros((LANES - PACK_SUBLANES, ROUTE_SUB), F32)], axis=0)
        wts_ref[rs, :] = wts_cols.T
    cnt_ref[...] = carry_ref[...].astype(I32)


def _route_call(logits):
    T = logits.shape[0]
    tm = TM_ROUTE
    row = pl.BlockSpec((tm, LANES), lambda i: (i, 0))
    return pl.pallas_call(
        _route_kernel,
        grid=(T // tm,),
        in_specs=[row],
        out_specs=[pl.BlockSpec((PACK_SUBLANES, tm), lambda i: (0, i)), row,
                   pl.BlockSpec((LANES, 1), lambda i: (0, 0))],
        out_shape=[jax.ShapeDtypeStruct((PACK_SUBLANES, T), I32),
                   jax.ShapeDtypeStruct((T, LANES), F32),
                   jax.ShapeDtypeStruct((LANES, 1), I32)],
        scratch_shapes=[pltpu.VMEM((LANES, 1), F32)],
        compiler_params=pltpu.CompilerParams(
            dimension_semantics=("arbitrary",), vmem_limit_bytes=VMEM_LIMIT),
        name="route",
    )(logits)


def _route_cols(lt, carry_ref):
    tm = lt.shape[1]
    sub = lax.broadcasted_iota(I32, (PACK_SUBLANES, tm), 0).astype(F32)

    def first_argmax(vals):
        mx = jnp.max(vals, axis=0, keepdims=True)
        idx = jnp.min(jnp.where(vals == mx, sub, float(PACK_SUBLANES)), axis=0, keepdims=True)
        return mx, idx

    g_logits = lt[0:N_GROUPS, :]
    g_max, g_sel = first_argmax(g_logits)
    g_w = 1.0 / jnp.sum(jnp.exp(g_logits - g_max), axis=0, keepdims=True)

    e_logits = lt[N_GROUPS:N_GROUPS + EXPERTS_PER_GROUP, :]
    for g in range(1, N_GROUPS):
        lo = N_GROUPS + g * EXPERTS_PER_GROUP
        e_logits = jnp.where(g_sel == float(g), lt[lo:lo + EXPERTS_PER_GROUP, :], e_logits)
    v1, i1 = first_argmax(e_logits)
    v2, i2 = first_argmax(jnp.where(sub == i1, -jnp.inf, e_logits))
    e1 = (g_sel * EXPERTS_PER_GROUP + i1).astype(I32)
    e2 = (g_sel * EXPERTS_PER_GROUP + i2).astype(I32)
    t = jnp.exp(v2 - v1)
    w1 = g_w / (1.0 + t)
    w2 = g_w * t / (1.0 + t)

    expert_row = lax.broadcasted_iota(I32, (LANES, tm), 0)
    hit1 = expert_row == e1
    hit2 = expert_row == e2
    onehot = jnp.where(hit1 | hit2, 1.0, 0.0)
    r = lax.broadcasted_iota(I32, (tm, tm), 0)
    c = lax.broadcasted_iota(I32, (tm, tm), 1)
    earlier = jnp.where(r < c, 1.0, 0.0).astype(BF16)
    before = (jnp.dot(onehot.astype(BF16), earlier, preferred_element_type=F32)
              + carry_ref[...])
    carry_ref[...] += jnp.sum(onehot, axis=1, keepdims=True)
    pos1 = jnp.sum(jnp.where(hit1, before, 0.0), axis=0, keepdims=True).astype(I32)
    pos2 = jnp.sum(jnp.where(hit2, before, 0.0), axis=0, keepdims=True).astype(I32)
    code1 = e1 * 65536 + pos1
    code2 = e2 * 65536 + pos2
    first, second = sub == 0.0, sub == 1.0
    return (jnp.where(first, code1, jnp.where(second, code2, 0)),
            jnp.where(first, w1, jnp.where(second, w2, 0.0)))


def _plan_kernel(code_ref, cnt_ref, unused_hbm, slots_hbm, blke_ref, blkn_ref, blkw_ref,
                 slot_ref, start_ref, sem):
    n_assign = code_ref.shape[0]
    n_blocks = blke_ref.shape[0]
    load = pltpu.make_async_copy(unused_hbm, slot_ref, sem)
    load.start()

    def per_expert(e, nb_done):
        cnt = cnt_ref[e]
        nb = lax.shift_right_logical(cnt + (MOE_BLK - 1), MOE_BLK.bit_length() - 1)
        start_ref[e] = nb_done * MOE_BLK

        def per_block(j, _):
            blke_ref[nb_done + j] = e
            blkn_ref[nb_done + j] = jnp.minimum(cnt - j * MOE_BLK, MOE_BLK)
            return 0
        lax.fori_loop(0, nb, per_block, 0)
        return nb_done + nb
    used = lax.fori_loop(0, N_EXPERTS, per_expert, 0)

    last_e = blke_ref[jnp.maximum(used - 1, 0)]

    def tail(b, _):
        blke_ref[b] = last_e
        blkn_ref[b] = 0
        blkw_ref[b] = 0
        return 0
    lax.fori_loop(used, n_blocks, tail, 0)

    def runs_backward(i, carry):
        later_e, next_e = carry
        b = used - 1 - i
        e = blke_ref[b]
        next_e = jnp.where(e != later_e, later_e, next_e)
        blkw_ref[b] = (next_e + 1) * 4
        return e, next_e
    lax.fori_loop(0, used, runs_backward, (last_e, -1))

    def runs_forward(b, carry):
        earlier_e, run = carry
        e = blke_ref[b]
        first = (e != earlier_e).astype(I32)
        run = run + first
        blkw_ref[b] = blkw_ref[b] + first + (run & 1) * 2
        return e, run
    lax.fori_loop(0, used, runs_forward, (-1, -1))

    load.wait()

    def place(a, _):
        code = code_ref[a]
        e = lax.shift_right_logical(code, 16)
        slot_ref[start_ref[e] + (code & 0xFFFF)] = a
        return 0
    lax.fori_loop(0, n_assign, place, 0, unroll=PLAN_UNROLL)

    store = pltpu.make_async_copy(slot_ref, slots_hbm, sem)
    store.start()
    store.wait()


def _plan_call(code_flat, counts, n_tab):
    n_assign = code_flat.shape[0]
    n_slots = n_tab * MOE_BLK
    unused = (n_assign + (np.arange(n_slots) & (MOE_BLK - 1))).astype(np.int32)
    smem = pl.BlockSpec(memory_space=pltpu.SMEM)
    hbm = pl.BlockSpec(memory_space=pl.ANY)
    return pl.pallas_call(
        _plan_kernel,
        in_specs=[smem, smem, hbm],
        out_specs=[hbm, smem, smem, smem],
        out_shape=[jax.ShapeDtypeStruct((n_slots,), I32),
                   jax.ShapeDtypeStruct((n_tab,), I32),
                   jax.ShapeDtypeStruct((n_tab,), I32),
                   jax.ShapeDtypeStruct((n_tab,), I32)],
        scratch_shapes=[pltpu.SMEM((n_slots,), I32), pltpu.SMEM((N_EXPERTS,), I32),
                        pltpu.SemaphoreType.DMA],
        name="plan",
    )(code_flat, counts, unused)


def _expert_kernel(blke_ref, blkn_ref, blkw_ref, slot_g_ref, slot_s_ref, xn_hbm,
                   wg_hbm, wu_hbm, wd_hbm, y_hbm, x0, x1, y0, y1, wg_buf, wu_buf, wd_buf,
                   gsem, ssem, wsem):
    xbufs, ybufs = (x0, x1), (y0, y1)
    s = pl.program_id(0)
    n_tok = xn_hbm.shape[0] // PACK_SUBLANES

    def weight_copies(e, slot):
        return [pltpu.make_async_copy(hbm.at[e], buf.at[slot], wsem.at[slot])
                for hbm, buf in ((wg_hbm, wg_buf), (wu_hbm, wu_buf), (wd_hbm, wd_buf))]

    blk = jnp.maximum(s - 1, 0)
    run_info = blkw_ref[blk]
    w_slot = (run_info >> 1) & 1
    next_expert = (run_info >> 2) - 1

    def tile(ref, r):
        start = r * PACK_SUBLANES
        if not isinstance(r, int):
            start = pl.multiple_of(start, PACK_SUBLANES)
        return ref.at[pl.ds(start, PACK_SUBLANES)]

    def rows(j):
        n = blkn_ref[jnp.clip(j, 0, blkn_ref.shape[0] - 1)]
        return jnp.where(j >= 0, (n + (ROW_GROUP - 1)) & ~(ROW_GROUP - 1), 0)
    rows0, rows1, rows2, rows3 = rows(s), rows(s - 1), rows(s - 2), rows(s - 3)

    def stage(nxt):
        cur = 1 - nxt
        x_nxt, x_cur, y_nxt, y_cur = xbufs[nxt], xbufs[cur], ybufs[nxt], ybufs[cur]

        def per_group(n_rows, issue_row):
            for g in range(MOE_BLK // ROW_GROUP):
                @pl.when(g * ROW_GROUP < n_rows)
                def _():
                    for r in range(g * ROW_GROUP, (g + 1) * ROW_GROUP):
                        issue_row(r)

        def gather_row(r):
            tok = slot_g_ref[0, 0, r] & (n_tok - 1)
            pltpu.make_async_copy(tile(xn_hbm, tok), tile(x_nxt, r), gsem.at[nxt]).start()

        def scatter_row(r):
            pltpu.make_async_copy(tile(y_nxt, r), tile(y_hbm, slot_s_ref[0, 0, r]),
                                  ssem.at[nxt]).start()

        def wait_rows(src, dst, sem, n_rows):
            n_words = pl.multiple_of(n_rows * PACK_SUBLANES, ROW_GROUP * PACK_SUBLANES)

            @pl.when(n_rows > 0)
            def _():
                pltpu.make_async_copy(src.at[pl.ds(0, n_words)], dst.at[pl.ds(0, n_words)],
                                      sem).wait()

        def compute():
            xb = _load_packed_rows(x_cur, 0, MOE_BLK).astype(BF16)
            a = jnp.dot(xb, wg_buf[w_slot].astype(BF16), preferred_element_type=F32)
            u = jnp.dot(xb, wu_buf[w_slot].astype(BF16), preferred_element_type=F32)
            hmid = (a * jax.nn.sigmoid(a) * u).astype(BF16)
            _store_packed_rows(y_cur, 0, jnp.dot(hmid, wd_buf[w_slot].astype(BF16),
                                                 preferred_element_type=F32))

        wait_rows(xn_hbm, x_cur, gsem.at[cur], rows1)
        wait_rows(y_cur, y_hbm, ssem.at[cur], rows3)

        @pl.when((rows1 > 0) & ((run_info & 1) == 1))
        def _():
            for copy in weight_copies(blke_ref[blk], w_slot):
                copy.wait()

            @pl.when(next_expert >= 0)
            def _():
                for copy in weight_copies(next_expert, 1 - w_slot):
                    copy.start(priority=WEIGHT_DMA_PRIORITY)

        if nxt == 0:
            @pl.when((s == 0) & (blkn_ref[0] > 0))
            def _():
                for copy in weight_copies(blke_ref[0], 0):
                    copy.start(priority=WEIGHT_DMA_PRIORITY)

            @pl.when(s == 0)
            def _():
                x_nxt[...] = jnp.zeros(x_nxt.shape, x_nxt.dtype)
                x_cur[...] = jnp.zeros(x_cur.shape, x_cur.dtype)
                y_nxt[...] = jnp.zeros(y_nxt.shape, y_nxt.dtype)
                dump = pltpu.make_async_copy(
                    y_nxt, y_hbm.at[pl.ds(TOP_K * n_tok * PACK_SUBLANES,
                                          MOE_BLK * PACK_SUBLANES)], ssem.at[nxt])
                dump.start()
                dump.wait()

        per_group(rows0, gather_row)
        per_group(rows2, scatter_row)

        @pl.when(rows1 > 0)
        def _():
            compute()

    for parity in range(2):
        pl.when((s & 1) == parity)(functools.partial(stage, parity))


def _expert_call(blk_e, blk_n, blk_w, slots, xn, wg, wu, wd):
    T = xn.shape[0] // PACK_SUBLANES
    assert T & (T - 1) == 0, "token id is recovered from the assignment id with a mask"
    n_tab = blk_e.shape[0]
    _, D, F = wg.shape
    packed_block = pltpu.VMEM((MOE_BLK * PACK_SUBLANES, LANES), jnp.uint32)
    hbm = pl.BlockSpec(memory_space=pl.ANY)
    grid_spec = pltpu.PrefetchScalarGridSpec(
        num_scalar_prefetch=3,
        grid=(n_tab,),
        in_specs=[
            pl.BlockSpec((1, 1, MOE_BLK), lambda s, *_: (s, 0, 0), memory_space=pltpu.SMEM),
            pl.BlockSpec((1, 1, MOE_BLK), lambda s, *_: (jnp.maximum(s - 2, 0), 0, 0),
                         memory_space=pltpu.SMEM),
            hbm, hbm, hbm, hbm,
        ],
        out_specs=hbm,
        scratch_shapes=[packed_block] * 4
        + [pltpu.VMEM((2, D, F), F32), pltpu.VMEM((2, D, F), F32), pltpu.VMEM((2, F, D), F32)]
        + [pltpu.SemaphoreType.DMA((2,))] * 3,
    )
    slots3 = slots.reshape(n_tab, 1, MOE_BLK)
    return pl.pallas_call(
        _expert_kernel,
        grid_spec=grid_spec,
        out_shape=jax.ShapeDtypeStruct(((T * TOP_K + MOE_BLK) * PACK_SUBLANES, LANES),
                                       jnp.uint32),
        compiler_params=pltpu.CompilerParams(
            dimension_semantics=("arbitrary",), vmem_limit_bytes=VMEM_LIMIT,
            disable_bounds_checks=True),
        name="experts",
    )(blk_e, blk_n, blk_w, slots3, slots3, xn, wg, wu, wd)


def _final_kernel(h1_ref, y1_ref, y2_ref, wts_ref, p_ref, wpp_ref, pg_ref, g3_ref, wpg_ref,
                  o_ref):
    for i in range(h1_ref.shape[0] // FINAL_SUB):
        rs = pl.ds(i * FINAL_SUB, FINAL_SUB)
        w = wts_ref[rs, :]
        h2 = (h1_ref[rs, :]
              + w[:, 0:1] * _load_packed_rows(y1_ref, i * FINAL_SUB, FINAL_SUB)
              + w[:, 1:2] * _load_packed_rows(y2_ref, i * FINAL_SUB, FINAL_SUB))
        e = _rms(jnp.dot(p_ref[rs, :].astype(BF16), wpp_ref[...], preferred_element_type=F32),
                 pg_ref[...])
        gate = jax.nn.sigmoid(jnp.dot(_rms(h2, g3_ref[...]).astype(BF16), wpg_ref[...],
                                      preferred_element_type=F32))
        o_ref[rs, :] = h2 + gate * e


def _final_call(h1, y, wts, p, wpp, pg, g3, wpg):
    T, D = h1.shape
    tm = TM_FINAL
    row = lambda w: pl.BlockSpec((tm, w), lambda i: (i, 0))
    first = pl.BlockSpec((tm * PACK_SUBLANES, LANES), lambda i: (i, 0))
    second = pl.BlockSpec((tm * PACK_SUBLANES, LANES), lambda i: (i + T // tm, 0))
    return pl.pallas_call(
        _final_kernel,
        grid=(T // tm,),
        in_specs=[row(D), first, second, row(LANES), row(PLE_DIM), _const_spec(wpp.shape),
                  _const_spec(pg.shape), _const_spec(g3.shape), _const_spec(wpg.shape)],
        out_specs=row(D),
        out_shape=jax.ShapeDtypeStruct((T, D), F32),
        compiler_params=pltpu.CompilerParams(
            dimension_semantics=("arbitrary",), vmem_limit_bytes=VMEM_LIMIT),
        name="final",
    )(h1, y, y, wts, p, wpp, pg, g3, wpg)


def _rope_lanes(a):
    z = jnp.zeros(a.shape[:-1] + (ROPE_HALF,), a.dtype)
    return jnp.concatenate([a[..., :ROPE_HALF], z, a[..., ROPE_HALF:], z], axis=-1)


def _head_lanes(a):
    return jnp.concatenate([a[..., :QK_NOPE], _rope_lanes(a[..., QK_NOPE:])], axis=-1)


def _score_bound(q_gain, k_gain):
    return Q_PRESCALE * QK_DIM * jnp.max(jnp.abs(q_gain)) * jnp.max(jnp.abs(k_gain))


def kernel(x, p, positions, norm1_gain, w_in, q_a_gain, w_q_b, kv_a_gain, w_kv_b, q_norm_gain, k_norm_gain, w_pool, pool_scale, w_out, norm2_gain, w_router_group, b_router_group, w_router_expert, b_router_expert, w_exp_gate, w_exp_up, w_exp_down, norm3_gain, w_ple_gate, w_ple_proj, ple_norm_gain):
    B, S, D = x.shape
    T = B * S
    assert x.shape[2] == D_MODEL and S % TQ == 0 and S % TM_PRE == 0
    assert T % TM_POST == 0 and T % TM_FINAL == 0
    layer = 0
    row = lambda a: a[layer].reshape(1, -1)

    win = _win_call(w_in[layer])
    wq = _head_lanes(w_q_b[layer].reshape(Q_LORA, N_HEADS, QK_DIM)
                     ).reshape(Q_LORA, N_HEADS * HEAD_W).astype(BF16)
    wkv3 = w_kv_b[layer].reshape(KV_LORA, N_HEADS, QK_NOPE + V_DIM)
    wkv = jnp.concatenate([wkv3[..., :QK_NOPE].reshape(KV_LORA, -1),
                           wkv3[..., QK_NOPE:].reshape(KV_LORA, -1)], axis=1).astype(BF16)
    qng = _head_lanes(q_norm_gain[layer]).reshape(1, HEAD_W)
    kng = _head_lanes(k_norm_gain[layer]).reshape(1, HEAD_W)
    inv_freq = np.float32(ROPE_THETA) ** (-np.arange(ROPE_HALF, dtype=np.float32) / ROPE_HALF)
    zeros_half = np.zeros((ROPE_HALF,), np.float32)
    ones_half = np.ones((ROPE_HALF,), np.float32)
    invf = np.concatenate([inv_freq, zeros_half, inv_freq, zeros_half]).reshape(1, LANES)
    sgn = np.concatenate([-ones_half, zeros_half, ones_half, zeros_half]).reshape(1, LANES)
    spare = np.arange(LANES) == SPARE_ROPE_LANE
    kpad = spare.astype(np.float32).reshape(1, LANES)
    pad_lanes = LANES - N_GROUPS - N_EXPERTS
    wr = jnp.concatenate([w_router_group[layer], w_router_expert[layer],
                          jnp.zeros((D, pad_lanes), F32)], axis=1)
    wr_hi = wr.astype(BF16)
    wr = jnp.concatenate([wr_hi, (wr - wr_hi.astype(F32)).astype(BF16)], axis=1)
    br = jnp.concatenate([b_router_group[layer], b_router_expert[layer],
                          jnp.zeros((pad_lanes,), F32)]).reshape(1, LANES)

    bound = _score_bound(q_norm_gain[layer], k_norm_gain[layer])
    bounded = bound <= SCORE_BOUND_LIMIT
    qpad = jnp.where(bounded, -bound, 0.0) * kpad
    ypool, q, k, vt = _pre_call(
        x, positions.reshape(B, S, 1), row(norm1_gain), win, row(q_a_gain), wq,
        row(kv_a_gain), wkv, qng, kng, w_pool[layer].astype(BF16), row(pool_scale), invf, sgn,
        qpad, kpad)
    yattn = _attn_call(bounded.astype(I32).reshape(1), q, k, vt)
    h1, xn2, logits = _post_call(
        x.reshape(T, D), ypool.reshape(T, D_POOL), yattn.reshape(T, D_ATTN),
        w_out[layer].astype(BF16), row(norm2_gain), wr, br)
    code, wts, counts = _route_call(logits)

    n_assign = T * TOP_K
    n_blocks = (n_assign + N_EXPERTS * (MOE_BLK - 1)) // MOE_BLK
    slots, blk_e, blk_n, blk_w = _plan_call(code[:TOP_K].reshape(n_assign),
                                            counts.reshape(LANES), n_blocks + MOE_DRAIN_STEPS)
    y = _expert_call(blk_e, blk_n, blk_w, slots, xn2,
                     w_exp_gate[layer], w_exp_up[layer], w_exp_down[layer])

    out = _final_call(h1, y, wts, p[layer].reshape(T, PLE_DIM),
                      w_ple_proj[layer].astype(BF16), row(ple_norm_gain), row(norm3_gain),
                      w_ple_gate[layer].astype(BF16))
    return out.reshape(B, S, D)
```

```python
import functools
import math

import jax
import jax.numpy as jnp
import numpy as np
from jax import lax
from jax.experimental import pallas as pl
from jax.experimental.pallas import tpu as pltpu

F32 = jnp.float32
BF16 = jnp.bfloat16
I32 = jnp.int32

D_MODEL = 2048
PLE_DIM = 256
EPS = 1e-6
POOL_WINDOWS = (2, 4, 8, 16)
POOL_CH = 256
D_POOL = POOL_CH * len(POOL_WINDOWS)
N_HEADS = 8
Q_LORA = 512
KV_LORA = 512
QK_NOPE = 128
QK_ROPE = 64
QK_DIM = QK_NOPE + QK_ROPE
V_DIM = 128
D_ATTN = N_HEADS * V_DIM
ROPE_THETA = 10000.0
ATTN_SCALE = 1.0 / math.sqrt(QK_DIM)
Q_PRESCALE = ATTN_SCALE * math.log2(math.e)
N_GROUPS = 8
EXPERTS_PER_GROUP = 8
N_EXPERTS = N_GROUPS * EXPERTS_PER_GROUP
TOP_K = 2
D_EXPERT = 512

LANES = 128
PACK_SUBLANES = 8
MXU_DIM = 256
VMEM_LIMIT = 56 * 1024 * 1024

HEAD_W = 2 * LANES
ROPE_HALF = QK_ROPE // 2
POOL_HALO = 16
TM_PRE = 512
PRE_SUB = 256
TM_POST = 512
POST_SUB = 256
TM_FINAL = 512
FINAL_SUB = 256
WIN_ROWS = 256
TQ = 512
TK = PRE_SUB
TM_ROUTE = 1024
ROUTE_SUB = 256
MOE_BLK = 256
MOE_DRAIN_STEPS = 3
ROW_GROUP = 32
WEIGHT_DMA_PRIORITY = 1
PLAN_UNROLL = 16
NEG_BIG = -1e30
V_ROWS = V_DIM + 16
SPARE_ROPE_LANE = ROPE_HALF
SCORE_BOUND_LIMIT = 50.0


def _const_spec(shape):
    nd = len(shape)
    return pl.BlockSpec(shape, lambda *_: (0,) * nd, pipeline_mode=pl.Buffered(1))


def _rms(x, gain):
    return x * lax.rsqrt(jnp.mean(x * x, axis=-1, keepdims=True) + EPS) * gain


def _store_packed_rows(ref, first_row, x):
    rows, d = x.shape
    half = d // 2
    assert half == PACK_SUBLANES * LANES
    for i in range(PACK_SUBLANES):
        lo = x[:, i * LANES:(i + 1) * LANES].astype(BF16).astype(F32)
        hi = x[:, half + i * LANES:half + (i + 1) * LANES].astype(BF16).astype(F32)
        word = (lax.shift_right_logical(pltpu.bitcast(lo, jnp.uint32), jnp.uint32(16))
                | (pltpu.bitcast(hi, jnp.uint32) & jnp.uint32(0xFFFF0000)))
        ref[pl.ds(first_row * PACK_SUBLANES + i, rows, stride=PACK_SUBLANES), :] = word


def _load_packed_rows(ref, first_row, rows):
    lo, hi = [], []
    for i in range(PACK_SUBLANES):
        word = ref[pl.ds(first_row * PACK_SUBLANES + i, rows, stride=PACK_SUBLANES), :]
        lo.append(pltpu.bitcast(lax.shift_left(word, jnp.uint32(16)), F32))
        hi.append(pltpu.bitcast(word & jnp.uint32(0xFFFF0000), F32))
    return jnp.concatenate(lo + hi, axis=1)


def _pre_kernel(x_ref, pos_ref, g1_ref, win_ref, qag_ref, wq_ref, kvag_ref, wkv_ref,
                qng_ref, kng_ref, wpool_ref, pscale_ref, invf_ref, sgn_ref, qpad_ref, kpad_ref,
                ypool_ref, q_ref, k_ref, vt_ref, carry_ref):
    st = pl.program_id(1)

    @pl.when(st == 0)
    def _():
        carry_ref[...] = jnp.zeros_like(carry_ref)

    halo = carry_ref[...]
    for i in range(x_ref.shape[0] // PRE_SUB):
        halo = _pre_rows(i, st * x_ref.shape[0] + i * PRE_SUB, halo,
                         x_ref, pos_ref, g1_ref, win_ref, qag_ref, wq_ref, kvag_ref, wkv_ref,
                         qng_ref, kng_ref, wpool_ref, pscale_ref, invf_ref, sgn_ref, qpad_ref, kpad_ref,
                         ypool_ref, q_ref, k_ref, vt_ref)
    carry_ref[...] = halo


def _pre_rows(i, first_pos, halo, x_ref, pos_ref, g1_ref, win_ref, qag_ref, wq_ref, kvag_ref,
              wkv_ref, qng_ref, kng_ref, wpool_ref, pscale_ref, invf_ref, sgn_ref, qpad_ref, kpad_ref,
              ypool_ref, q_ref, k_ref, vt_ref):
    tm = PRE_SUB
    rs = pl.ds(i * tm, tm)

    hn = _rms(x_ref[rs, :], g1_ref[...])
    z = jnp.dot(hn.astype(BF16), win_ref[...], preferred_element_type=F32)
    k_rope = z[:, D_POOL + Q_LORA + KV_LORA:]

    u = z[:, :D_POOL]
    ext = jnp.concatenate([halo, u], axis=0)
    row = lax.broadcasted_iota(I32, (tm, 1), 0) + first_pos
    level = ext
    shift = 1
    for g, w in enumerate(POOL_WINDOWS):
        sl = slice(g * POOL_CH, (g + 1) * POOL_CH)
        while shift < w:
            level = level + pltpu.roll(level, shift, 0)
            shift *= 2
        win_sum = level[POOL_HALO:, sl]
        cnt = jnp.minimum(row + 1, w).astype(F32)
        d = win_sum / cnt - u[:, sl]
        y = jnp.dot(d.astype(BF16), wpool_ref[g], preferred_element_type=F32)
        ypool_ref[rs, sl] = (y * pscale_ref[:, sl]).astype(BF16)

    q_lat = z[:, D_POOL:D_POOL + Q_LORA]
    kv_lat = z[:, D_POOL + Q_LORA:D_POOL + Q_LORA + KV_LORA]
    qa = jnp.dot(_rms(q_lat, qag_ref[...]).astype(BF16), wq_ref[...],
                 preferred_element_type=F32)
    kv = jnp.dot(_rms(kv_lat, kvag_ref[...]).astype(BF16), wkv_ref[...],
                 preferred_element_type=F32)

    ang = pos_ref[rs, :].astype(F32) * invf_ref[...]
    cos = jnp.cos(ang)
    sin = jnp.sin(ang) * sgn_ref[...]

    def rot(t):
        return t * cos + pltpu.roll(t, LANES // 2, 1) * sin

    qng = qng_ref[...]
    kng = kng_ref[...]
    kr_rot = rot(k_rope * kng[:, LANES:])
    kr_ssq = jnp.sum(k_rope * k_rope, axis=-1, keepdims=True)
    for h in range(N_HEADS):
        qh = qa[:, h * HEAD_W:(h + 1) * HEAD_W]
        rq = lax.rsqrt(jnp.sum(qh * qh, axis=-1, keepdims=True) / QK_DIM + EPS) * Q_PRESCALE
        qn = qh * rq * qng
        q_ref[0, h, rs, :LANES] = qn[:, :LANES].astype(BF16)
        q_ref[0, h, rs, LANES:] = (rot(qn[:, LANES:]) + qpad_ref[...]).astype(BF16)
        kh = kv[:, h * QK_NOPE:(h + 1) * QK_NOPE]
        rk = lax.rsqrt((jnp.sum(kh * kh, axis=-1, keepdims=True) + kr_ssq) / QK_DIM + EPS)
        k_ref[0, h, rs, :LANES] = (kh * rk * kng[:, :LANES]).astype(BF16)
        k_ref[0, h, rs, LANES:] = (kr_rot * rk + kpad_ref[...]).astype(BF16)
        vh = kv[:, N_HEADS * QK_NOPE + h * V_DIM:N_HEADS * QK_NOPE + (h + 1) * V_DIM]
        vt_ref[0, h, i, :V_DIM, :] = vh.T.astype(BF16)
        vt_ref[0, h, i, V_DIM:, :] = jnp.ones((V_ROWS - V_DIM, tm), BF16)
    return u[tm - POOL_HALO:, :]


def _win_kernel(wt_ref, o_ref):
    w = wt_ref[...].T
    n_lat = o_ref.shape[1] - LANES
    zeros = jnp.zeros((w.shape[0], ROPE_HALF), F32)
    rope = jnp.concatenate([w[:, n_lat:n_lat + ROPE_HALF], zeros,
                            w[:, n_lat + ROPE_HALF:], zeros], axis=1)
    o_ref[:, :n_lat] = w[:, :n_lat].astype(BF16)
    o_ref[:, n_lat:] = rope.astype(BF16)


def _win_call(w_in_t):
    n_in, D = w_in_t.shape
    n_lat = n_in - QK_ROPE
    tm = WIN_ROWS
    return pl.pallas_call(
        _win_kernel,
        grid=(D // tm,),
        in_specs=[pl.BlockSpec((n_in, tm), lambda i: (0, i))],
        out_specs=pl.BlockSpec((tm, n_lat + LANES), lambda i: (i, 0)),
        out_shape=jax.ShapeDtypeStruct((D, n_lat + LANES), BF16),
        compiler_params=pltpu.CompilerParams(
            dimension_semantics=("arbitrary",), vmem_limit_bytes=VMEM_LIMIT),
        name="win",
    )(w_in_t)


def _pre_call(x, pos, g1, win, qag, wq, kvag, wkv, qng, kng, wpool, pscale, invf, sgn,
              qpad, kpad):
    B, S, D = x.shape
    tm = TM_PRE
    grid = (B, S // tm)
    row_spec = lambda w: pl.BlockSpec((None, tm, w), lambda b, s: (b, s, 0))
    head_spec = lambda w: pl.BlockSpec((1, N_HEADS, tm, w), lambda b, s: (b, 0, s, 0))
    consts = [g1, win, qag, wq, kvag, wkv, qng, kng, wpool, pscale, invf, sgn, qpad, kpad]
    return pl.pallas_call(
        _pre_kernel,
        grid=grid,
        in_specs=[row_spec(D), row_spec(1)] + [_const_spec(c.shape) for c in consts],
        out_specs=[row_spec(D_POOL), head_spec(HEAD_W), head_spec(HEAD_W),
                   pl.BlockSpec((1, N_HEADS, tm // PRE_SUB, V_ROWS, PRE_SUB),
                                lambda b, s: (b, 0, s, 0, 0))],
        out_shape=[jax.ShapeDtypeStruct((B, S, D_POOL), BF16),
                   jax.ShapeDtypeStruct((B, N_HEADS, S, HEAD_W), BF16),
                   jax.ShapeDtypeStruct((B, N_HEADS, S, HEAD_W), BF16),
                   jax.ShapeDtypeStruct((B, N_HEADS, S // PRE_SUB, V_ROWS, PRE_SUB), BF16)],
        scratch_shapes=[pltpu.VMEM((POOL_HALO, D_POOL), F32)],
        compiler_params=pltpu.CompilerParams(
            dimension_semantics=("arbitrary", "arbitrary"), vmem_limit_bytes=VMEM_LIMIT),
        name="pre",
    )(x, pos, *consts)


def _attn_kernel(bounded_ref, q_ref, k_ref, vt_ref, o_ref, *chain_scratch):
    S = q_ref.shape[2]
    nq = S // TQ
    per_q = TQ // TK
    assert per_q == 2, "two online-softmax chains take the even / odd key tiles"
    chains = (chain_scratch[0:2], chain_scratch[2:4])
    score_bufs = (chain_scratch[4:6], chain_scratch[6:8])
    key_idx = lax.broadcasted_iota(I32, (TK, TQ), 0)
    qry_idx = lax.broadcasted_iota(I32, (TK, TQ), 1)

    diag_masks = (key_idx <= qry_idx, key_idx + TK <= qry_idx)

    def weighted_values(kt, p):
        return jnp.dot(vt_ref[0, 0, kt], p.astype(BF16), preferred_element_type=F32)

    def write_out(qi, acc):
        out = acc[:V_DIM, :] / acc[V_DIM:V_DIM + 1, :]
        start = qi * TQ if isinstance(qi, int) else pl.multiple_of(qi * TQ, TQ)
        o_ref[0, pl.ds(start, TQ), :] = out.T.astype(BF16)

    def bounded_head():
        steps = [(qi, j) for qi in range(nq) for j in range(qi + 1)]

        def pair_scores(step, buf):
            qi, j = step
            q = q_ref[0, 0, qi * TQ:(qi + 1) * TQ, :]
            for c in range(per_q):
                kt = per_q * j + c
                buf[c][...] = lax.dot_general(k_ref[0, 0, kt * TK:(kt + 1) * TK, :], q,
                                              (((1,), (1,)), ((), ())),
                                              preferred_element_type=F32)

        pair_scores(steps[0], score_bufs[0])
        acc = None
        for t, (qi, j) in enumerate(steps):
            if t + 1 < len(steps):
                pair_scores(steps[t + 1], score_bufs[(t + 1) % 2])
            for c in range(per_q):
                st = score_bufs[t % 2][c][...]
                if j == qi:
                    st = jnp.where(diag_masks[c], st, NEG_BIG)
                pv = weighted_values(per_q * j + c, jnp.exp2(st))
                acc = pv if acc is None else acc + pv
            if j == qi:
                write_out(qi, acc)
                acc = None

    def online_q_tile(qi, _):
        q = q_ref[0, 0, pl.ds(pl.multiple_of(qi * TQ, TQ), TQ), :]
        for m_ref, acc_ref in chains:
            m_ref[...] = jnp.full(m_ref.shape, NEG_BIG, F32)
            acc_ref[...] = jnp.zeros(acc_ref.shape, F32)

        def scores(kt):
            k = k_ref[0, 0, pl.ds(pl.multiple_of(kt * TK, TK), TK), :]
            return lax.dot_general(k, q, (((1,), (1,)), ((), ())), preferred_element_type=F32)

        def fold(chain, st, kt, mask):
            m_ref, acc_ref = chain
            if mask is not None:
                st = jnp.where(mask, st, NEG_BIG)
            m = m_ref[...]
            m_new = jnp.maximum(m, jnp.max(st, axis=0, keepdims=True))
            m_ref[...] = m_new
            acc_ref[...] = (jnp.exp2(m - m_new) * acc_ref[...]
                            + weighted_values(kt, jnp.exp2(st - m_new)))

        def pair_scores(j, buf):
            for c in range(per_q):
                buf[c][...] = scores(per_q * j + c)

        def fold_pair(j, buf, masks=(None, None)):
            for c in range(per_q):
                fold(chains[c], buf[c][...], per_q * j + c, masks[c])

        pair_scores(0, score_bufs[0])

        def two_pairs(i, _):
            pair_scores(2 * i + 1, score_bufs[1])
            fold_pair(2 * i, score_bufs[0])
            pair_scores(2 * i + 2, score_bufs[0])
            fold_pair(2 * i + 1, score_bufs[1])
            return 0
        lax.fori_loop(0, qi // 2, two_pairs, 0)

        @pl.when(qi % 2 == 0)
        def _():
            fold_pair(qi, score_bufs[0], diag_masks)

        @pl.when(qi % 2 == 1)
        def _():
            pair_scores(qi, score_bufs[1])
            fold_pair(qi - 1, score_bufs[0])
            fold_pair(qi, score_bufs[1], diag_masks)

        (m0, acc0), (m1, acc1) = chains
        m = jnp.maximum(m0[...], m1[...])
        write_out(qi, jnp.exp2(m0[...] - m) * acc0[...] + jnp.exp2(m1[...] - m) * acc1[...])
        return 0

    pl.when(bounded_ref[0] != 0)(bounded_head)

    @pl.when(bounded_ref[0] == 0)
    def _():
        lax.fori_loop(0, nq, online_q_tile, 0)


def _attn_call(bounded, q, k, vt):
    B, H, S, _ = q.shape
    head = lambda w: pl.BlockSpec((1, 1, S, w), lambda b, h, flag: (b, h, 0, 0))
    grid_spec = pltpu.PrefetchScalarGridSpec(
        num_scalar_prefetch=1,
        grid=(B, H),
        in_specs=[head(HEAD_W), head(HEAD_W),
                  pl.BlockSpec((1, 1) + vt.shape[2:], lambda b, h, flag: (b, h, 0, 0, 0))],
        out_specs=pl.BlockSpec((1, S, V_DIM), lambda b, h, flag: (b, 0, h)),
        scratch_shapes=[pltpu.VMEM((1, TQ), F32), pltpu.VMEM((V_ROWS, TQ), F32)] * 2
        + [pltpu.VMEM((TK, TQ), F32)] * 4,
    )
    return pl.pallas_call(
        _attn_kernel,
        grid_spec=grid_spec,
        out_shape=jax.ShapeDtypeStruct((B, S, H * V_DIM), BF16),
        compiler_params=pltpu.CompilerParams(
            dimension_semantics=("arbitrary", "arbitrary"), vmem_limit_bytes=VMEM_LIMIT),
        name="attn",
    )(bounded, q, k, vt)


def _post_kernel(x_ref, yp_ref, ya_ref, wo_ref, g2_ref, wr_ref, br_ref,
                 h1_ref, xn_ref, lg_ref):
    for i in range(x_ref.shape[0] // POST_SUB):
        rs = pl.ds(i * POST_SUB, POST_SUB)
        h1 = (x_ref[rs, :]
              + jnp.dot(yp_ref[rs, :], wo_ref[:D_POOL, :], preferred_element_type=F32)
              + jnp.dot(ya_ref[rs, :], wo_ref[D_POOL:, :], preferred_element_type=F32))
        h1_ref[rs, :] = h1
        xn = _rms(h1, g2_ref[...])
        _store_packed_rows(xn_ref, i * POST_SUB, xn)
        xn_hi = xn.astype(BF16)
        xn_lo = (xn - xn_hi.astype(F32)).astype(BF16)
        both = (jnp.dot(xn_hi, wr_ref[...], preferred_element_type=F32)
                + jnp.dot(xn_lo, wr_ref[...], preferred_element_type=F32))
        lg_ref[rs, :] = both[:, :LANES] + both[:, LANES:] + br_ref[...]


def _post_call(x, yp, ya, wo, g2, wr, br):
    T, D = x.shape
    tm = TM_POST
    row = lambda w: pl.BlockSpec((tm, w), lambda i: (i, 0))
    return pl.pallas_call(
        _post_kernel,
        grid=(T // tm,),
        in_specs=[row(D), row(D_POOL), row(D_ATTN), _const_spec(wo.shape),
                  _const_spec(g2.shape), _const_spec(wr.shape), _const_spec(br.shape)],
        out_specs=[row(D), pl.BlockSpec((tm * PACK_SUBLANES, LANES), lambda i: (i, 0)),
                   row(LANES)],
        out_shape=[jax.ShapeDtypeStruct((T, D), F32),
                   jax.ShapeDtypeStruct((T * PACK_SUBLANES, LANES), jnp.uint32),
                   jax.ShapeDtypeStruct((T, LANES), F32)],
        compiler_params=pltpu.CompilerParams(
            dimension_semantics=("arbitrary",), vmem_limit_bytes=VMEM_LIMIT),
        name="post",
    )(x, yp, ya, wo, g2, wr, br)


def _route_kernel(lg_ref, code_ref, wts_ref, cnt_ref, carry_ref):
    @pl.when(pl.program_id(0) == 0)
    def _():
        carry_ref[...] = jnp.zeros_like(carry_ref)

    for i in range(lg_ref.shape[0] // ROUTE_SUB):
        rs = pl.ds(i * ROUTE_SUB, ROUTE_SUB)
        code_t, wts_t = _route_cols(lg_ref[rs, :].T, carry_ref)
        code_ref[:, rs] = code_t
        wts_cols = jnp.concatenate(
            [wts_t, jnp.zeros((LANES - PACK_SUBLANES, ROUTE_SUB), F32)], axis=0)
        wts_ref[rs, :] = wts_cols.T
    cnt_ref[...] = carry_ref[...].astype(I32)


def _route_call(logits):
    T = logits.shape[0]
    tm = TM_ROUTE
    row = pl.BlockSpec((tm, LANES), lambda i: (i, 0))
    return pl.pallas_call(
        _route_kernel,
        grid=(T // tm,),
        in_specs=[row],
        out_specs=[pl.BlockSpec((PACK_SUBLANES, tm), lambda i: (0, i)), row,
                   pl.BlockSpec((LANES, 1), lambda i: (0, 0))],
        out_shape=[jax.ShapeDtypeStruct((PACK_SUBLANES, T), I32),
                   jax.ShapeDtypeStruct((T, LANES), F32),
                   jax.ShapeDtypeStruct((LANES, 1), I32)],
        scratch_shapes=[pltpu.VMEM((LANES, 1), F32)],
        compiler_params=pltpu.CompilerParams(
            dimension_semantics=("arbitrary",), vmem_limit_bytes=VMEM_LIMIT),
        name="route",
    )(logits)


def _route_cols(lt, carry_ref):
    tm = lt.shape[1]
    sub = lax.broadcasted_iota(I32, (PACK_SUBLANES, tm), 0).astype(F32)

    def first_argmax(vals):
        mx = jnp.max(vals, axis=0, keepdims=True)
        idx = jnp.min(jnp.where(vals == mx, sub, float(PACK_SUBLANES)), axis=0, keepdims=True)
        return mx, idx

    g_logits = lt[0:N_GROUPS, :]
    g_max, g_sel = first_argmax(g_logits)
    g_w = 1.0 / jnp.sum(jnp.exp(g_logits - g_max), axis=0, keepdims=True)

    e_logits = lt[N_GROUPS:N_GROUPS + EXPERTS_PER_GROUP, :]
    for g in range(1, N_GROUPS):
        lo = N_GROUPS + g * EXPERTS_PER_GROUP
        e_logits = jnp.where(g_sel == float(g), lt[lo:lo + EXPERTS_PER_GROUP, :], e_logits)
    v1, i1 = first_argmax(e_logits)
    v2, i2 = first_argmax(jnp.where(sub == i1, -jnp.inf, e_logits))
    e1 = (g_sel * EXPERTS_PER_GROUP + i1).astype(I32)
    e2 = (g_sel * EXPERTS_PER_GROUP + i2).astype(I32)
    t = jnp.exp(v2 - v1)
    w1 = g_w / (1.0 + t)
    w2 = g_w * t / (1.0 + t)

    expert_row = lax.broadcasted_iota(I32, (LANES, tm), 0)
    hit1 = expert_row == e1
    hit2 = expert_row == e2
    onehot = jnp.where(hit1 | hit2, 1.0, 0.0)
    r = lax.broadcasted_iota(I32, (tm, tm), 0)
    c = lax.broadcasted_iota(I32, (tm, tm), 1)
    earlier = jnp.where(r < c, 1.0, 0.0).astype(BF16)
    before = (jnp.dot(onehot.astype(BF16), earlier, preferred_element_type=F32)
              + carry_ref[...])
    carry_ref[...] += jnp.sum(onehot, axis=1, keepdims=True)
    pos1 = jnp.sum(jnp.where(hit1, before, 0.0), axis=0, keepdims=True).astype(I32)
    pos2 = jnp.sum(jnp.where(hit2, before, 0.0), axis=0, keepdims=True).astype(I32)
    code1 = e1 * 65536 + pos1
    code2 = e2 * 65536 + pos2
    first, second = sub == 0.0, sub == 1.0
    return (jnp.where(first, code1, jnp.where(second, code2, 0)),
            jnp.where(first, w1, jnp.where(second, w2, 0.0)))


def _plan_kernel(code_ref, cnt_ref, unused_hbm, slots_hbm, blke_ref, blkn_ref, blkw_ref,
                 slot_ref, start_ref, sem):
    n_assign = code_ref.shape[0]
    n_blocks = blke_ref.shape[0]
    load = pltpu.make_async_copy(unused_hbm, slot_ref, sem)
    load.start()

    def per_expert(e, nb_done):
        cnt = cnt_ref[e]
        nb = lax.shift_right_logical(cnt + (MOE_BLK - 1), MOE_BLK.bit_length() - 1)
        start_ref[e] = nb_done * MOE_BLK

        def per_block(j, _):
            blke_ref[nb_done + j] = e
            blkn_ref[nb_done + j] = jnp.minimum(cnt - j * MOE_BLK, MOE_BLK)
            return 0
        lax.fori_loop(0, nb, per_block, 0)
        return nb_done + nb
    used = lax.fori_loop(0, N_EXPERTS, per_expert, 0)

    last_e = blke_ref[jnp.maximum(used - 1, 0)]

    def tail(b, _):
        blke_ref[b] = last_e
        blkn_ref[b] = 0
        blkw_ref[b] = 0
        return 0
    lax.fori_loop(used, n_blocks, tail, 0)

    def runs_backward(i, carry):
        later_e, next_e = carry
        b = used - 1 - i
        e = blke_ref[b]
        next_e = jnp.where(e != later_e, later_e, next_e)
        blkw_ref[b] = (next_e + 1) * 4
        return e, next_e
    lax.fori_loop(0, used, runs_backward, (last_e, -1))

    def runs_forward(b, carry):
        earlier_e, run = carry
        e = blke_ref[b]
        first = (e != earlier_e).astype(I32)
        run = run + first
        blkw_ref[b] = blkw_ref[b] + first + (run & 1) * 2
        return e, run
    lax.fori_loop(0, used, runs_forward, (-1, -1))

    load.wait()

    def place(a, _):
        code = code_ref[a]
        e = lax.shift_right_logical(code, 16)
        slot_ref[start_ref[e] + (code & 0xFFFF)] = a
        return 0
    lax.fori_loop(0, n_assign, place, 0, unroll=PLAN_UNROLL)

    store = pltpu.make_async_copy(slot_ref, slots_hbm, sem)
    store.start()
    store.wait()


def _plan_call(code_flat, counts, n_tab):
    n_assign = code_flat.shape[0]
    n_slots = n_tab * MOE_BLK
    unused = (n_assign + (np.arange(n_slots) & (MOE_BLK - 1))).astype(np.int32)
    smem = pl.BlockSpec(memory_space=pltpu.SMEM)
    hbm = pl.BlockSpec(memory_space=pl.ANY)
    return pl.pallas_call(
        _plan_kernel,
        in_specs=[smem, smem, hbm],
        out_specs=[hbm, smem, smem, smem],
        out_shape=[jax.ShapeDtypeStruct((n_slots,), I32),
                   jax.ShapeDtypeStruct((n_tab,), I32),
                   jax.ShapeDtypeStruct((n_tab,), I32),
                   jax.ShapeDtypeStruct((n_tab,), I32)],
        scratch_shapes=[pltpu.SMEM((n_slots,), I32), pltpu.SMEM((N_EXPERTS,), I32),
                        pltpu.SemaphoreType.DMA],
        name="plan",
    )(code_flat, counts, unused)


def _expert_kernel(blke_ref, blkn_ref, blkw_ref, slot_g_ref, slot_s_ref, xn_hbm,
                   wg_hbm, wu_hbm, wd_hbm, y_hbm, x0, x1, y0, y1, wg_buf, wu_buf, wd_buf,
                   gsem, ssem, wsem):
    xbufs, ybufs = (x0, x1), (y0, y1)
    s = pl.program_id(0)
    n_tok = xn_hbm.shape[0] // PACK_SUBLANES

    def weight_copies(e, slot):
        return [pltpu.make_async_copy(hbm.at[e], buf.at[slot], wsem.at[slot])
                for hbm, buf in ((wg_hbm, wg_buf), (wu_hbm, wu_buf), (wd_hbm, wd_buf))]

    blk = jnp.maximum(s - 1, 0)
    run_info = blkw_ref[blk]
    w_slot = (run_info >> 1) & 1
    next_expert = (run_info >> 2) - 1

    def tile(ref, r):
        start = r * PACK_SUBLANES
        if not isinstance(r, int):
            start = pl.multiple_of(start, PACK_SUBLANES)
        return ref.at[pl.ds(start, PACK_SUBLANES)]

    def rows(j):
        n = blkn_ref[jnp.clip(j, 0, blkn_ref.shape[0] - 1)]
        return jnp.where(j >= 0, (n + (ROW_GROUP - 1)) & ~(ROW_GROUP - 1), 0)
    rows0, rows1, rows2, rows3 = rows(s), rows(s - 1), rows(s - 2), rows(s - 3)

    def stage(nxt):
        cur = 1 - nxt
        x_nxt, x_cur, y_nxt, y_cur = xbufs[nxt], xbufs[cur], ybufs[nxt], ybufs[cur]

        def per_group(n_rows, issue_row):
            for g in range(MOE_BLK // ROW_GROUP):
                @pl.when(g * ROW_GROUP < n_rows)
                def _():
                    for r in range(g * ROW_GROUP, (g + 1) * ROW_GROUP):
                        issue_row(r)

        def gather_row(r):
            tok = slot_g_ref[0, 0, r] & (n_tok - 1)
            pltpu.make_async_copy(tile(xn_hbm, tok), tile(x_nxt, r), gsem.at[nxt]).start()

        def scatter_row(r):
            pltpu.make_async_copy(tile(y_nxt, r), tile(y_hbm, slot_s_ref[0, 0, r]),
                                  ssem.at[nxt]).start()

        def wait_rows(src, dst, sem, n_rows):
            n_words = pl.multiple_of(n_rows * PACK_SUBLANES, ROW_GROUP * PACK_SUBLANES)

            @pl.when(n_rows > 0)
            def _():
                pltpu.make_async_copy(src.at[pl.ds(0, n_words)], dst.at[pl.ds(0, n_words)],
                                      sem).wait()

        def compute():
            xb = _load_packed_rows(x_cur, 0, MOE_BLK).astype(BF16)
            a = jnp.dot(xb, wg_buf[w_slot].astype(BF16), preferred_element_type=F32)
            u = jnp.dot(xb, wu_buf[w_slot].astype(BF16), preferred_element_type=F32)
            hmid = (a * jax.nn.sigmoid(a) * u).astype(BF16)
            _store_packed_rows(y_cur, 0, jnp.dot(hmid, wd_buf[w_slot].astype(BF16),
                                                 preferred_element_type=F32))

        wait_rows(xn_hbm, x_cur, gsem.at[cur], rows1)
        wait_rows(y_cur, y_hbm, ssem.at[cur], rows3)

        @pl.when((rows1 > 0) & ((run_info & 1) == 1))
        def _():
            for copy in weight_copies(blke_ref[blk], w_slot):
                copy.wait()

            @pl.when(next_expert >= 0)
            def _():
                for copy in weight_copies(next_expert, 1 - w_slot):
                    copy.start(priority=WEIGHT_DMA_PRIORITY)

        if nxt == 0:
            @pl.when((s == 0) & (blkn_ref[0] > 0))
            def _():
                for copy in weight_copies(blke_ref[0], 0):
                    copy.start(priority=WEIGHT_DMA_PRIORITY)

            @pl.when(s == 0)
            def _():
                x_nxt[...] = jnp.zeros(x_nxt.shape, x_nxt.dtype)
                x_cur[...] = jnp.zeros(x_cur.shape, x_cur.dtype)
                y_nxt[...] = jnp.zeros(y_nxt.shape, y_nxt.dtype)
                dump = pltpu.make_async_copy(
                    y_nxt, y_hbm.at[pl.ds(TOP_K * n_tok * PACK_SUBLANES,
                                          MOE_BLK * PACK_SUBLANES)], ssem.at[nxt])
                dump.start()
                dump.wait()

        per_group(rows0, gather_row)
        per_group(rows2, scatter_row)

        @pl.when(rows1 > 0)
        def _():
            compute()

    for parity in range(2):
        pl.when((s & 1) == parity)(functools.partial(stage, parity))


def _expert_call(blk_e, blk_n, blk_w, slots, xn, wg, wu, wd):
    T = xn.shape[0] // PACK_SUBLANES
    assert T & (T - 1) == 0, "token id is recovered from the assignment id with a mask"
    n_tab = blk_e.shape[0]
    _, D, F = wg.shape
    packed_block = pltpu.VMEM((MOE_BLK * PACK_SUBLANES, LANES), jnp.uint32)
    hbm = pl.BlockSpec(memory_space=pl.ANY)
    grid_spec = pltpu.PrefetchScalarGridSpec(
        num_scalar_prefetch=3,
        grid=(n_tab,),
        in_specs=[
            pl.BlockSpec((1, 1, MOE_BLK), lambda s, *_: (s, 0, 0), memory_space=pltpu.SMEM),
            pl.BlockSpec((1, 1, MOE_BLK), lambda s, *_: (jnp.maximum(s - 2, 0), 0, 0),
                         memory_space=pltpu.SMEM),
            hbm, hbm, hbm, hbm,
        ],
        out_specs=hbm,
        scratch_shapes=[packed_block] * 4
        + [pltpu.VMEM((2, D, F), F32), pltpu.VMEM((2, D, F), F32), pltpu.VMEM((2, F, D), F32)]
        + [pltpu.SemaphoreType.DMA((2,))] * 3,
    )
    slots3 = slots.reshape(n_tab, 1, MOE_BLK)
    return pl.pallas_call(
        _expert_kernel,
        grid_spec=grid_spec,
        out_shape=jax.ShapeDtypeStruct(((T * TOP_K + MOE_BLK) * PACK_SUBLANES, LANES),
                                       jnp.uint32),
        compiler_params=pltpu.CompilerParams(
            dimension_semantics=("arbitrary",), vmem_limit_bytes=VMEM_LIMIT,
            disable_bounds_checks=True),
        name="experts",
    )(blk_e, blk_n, blk_w, slots3, slots3, xn, wg, wu, wd)


def _final_kernel(h1_ref, y1_ref, y2_ref, wts_ref, p_ref, wpp_ref, pg_ref, g3_ref, wpg_ref,
                  o_ref):
    for i in range(h1_ref.shape[0] // FINAL_SUB):
        rs = pl.ds(i * FINAL_SUB, FINAL_SUB)
        w = wts_ref[rs, :]
        h2 = (h1_ref[rs, :]
              + w[:, 0:1] * _load_packed_rows(y1_ref, i * FINAL_SUB, FINAL_SUB)
              + w[:, 1:2] * _load_packed_rows(y2_ref, i * FINAL_SUB, FINAL_SUB))
        e = _rms(jnp.dot(p_ref[rs, :].astype(BF16), wpp_ref[...], preferred_element_type=F32),
                 pg_ref[...])
        gate = jax.nn.sigmoid(jnp.dot(_rms(h2, g3_ref[...]).astype(BF16), wpg_ref[...],
                                      preferred_element_type=F32))
        o_ref[rs, :] = h2 + gate * e


def _final_call(h1, y, wts, p, wpp, pg, g3, wpg):
    T, D = h1.shape
    tm = TM_FINAL
    row = lambda w: pl.BlockSpec((tm, w), lambda i: (i, 0))
    first = pl.BlockSpec((tm * PACK_SUBLANES, LANES), lambda i: (i, 0))
    second = pl.BlockSpec((tm * PACK_SUBLANES, LANES), lambda i: (i + T // tm, 0))
    return pl.pallas_call(
        _final_kernel,
        grid=(T // tm,),
        in_specs=[row(D), first, second, row(LANES), row(PLE_DIM), _const_spec(wpp.shape),
                  _const_spec(pg.shape), _const_spec(g3.shape), _const_spec(wpg.shape)],
        out_specs=row(D),
        out_shape=jax.ShapeDtypeStruct((T, D), F32),
        compiler_params=pltpu.CompilerParams(
            dimension_semantics=("arbitrary",), vmem_limit_bytes=VMEM_LIMIT),
        name="final",
    )(h1, y, y, wts, p, wpp, pg, g3, wpg)


def _rope_lanes(a):
    z = jnp.zeros(a.shape[:-1] + (ROPE_HALF,), a.dtype)
    return jnp.concatenate([a[..., :ROPE_HALF], z, a[..., ROPE_HALF:], z], axis=-1)


def _head_lanes(a):
    return jnp.concatenate([a[..., :QK_NOPE], _rope_lanes(a[..., QK_NOPE:])], axis=-1)


def _score_bound(q_gain, k_gain):
    return Q_PRESCALE * QK_DIM * jnp.max(jnp.abs(q_gain)) * jnp.max(jnp.abs(k_gain))


def kernel(x, p, positions, norm1_gain, w_in, q_a_gain, w_q_b, kv_a_gain, w_kv_b, q_norm_gain, k_norm_gain, w_pool, pool_scale, w_out, norm2_gain, w_router_group, b_router_group, w_router_expert, b_router_expert, w_exp_gate, w_exp_up, w_exp_down, norm3_gain, w_ple_gate, w_ple_proj, ple_norm_gain):
    B, S, D = x.shape
    T = B * S
    assert x.shape[2] == D_MODEL and S % TQ == 0 and S % TM_PRE == 0
    assert T % TM_POST == 0 and T % TM_FINAL == 0
    layer = 0
    row = lambda a: a[layer].reshape(1, -1)

    win = _win_call(w_in[layer].T)
    wq = _head_lanes(w_q_b[layer].reshape(Q_LORA, N_HEADS, QK_DIM)
                     ).reshape(Q_LORA, N_HEADS * HEAD_W).astype(BF16)
    wkv3 = w_kv_b[layer].reshape(KV_LORA, N_HEADS, QK_NOPE + V_DIM)
    wkv = jnp.concatenate([wkv3[..., :QK_NOPE].reshape(KV_LORA, -1),
                           wkv3[..., QK_NOPE:].reshape(KV_LORA, -1)], axis=1).astype(BF16)
    qng = _head_lanes(q_norm_gain[layer]).reshape(1, HEAD_W)
    kng = _head_lanes(k_norm_gain[layer]).reshape(1, HEAD_W)
    inv_freq = np.float32(ROPE_THETA) ** (-np.arange(ROPE_HALF, dtype=np.float32) / ROPE_HALF)
    zeros_half = np.zeros((ROPE_HALF,), np.float32)
    ones_half = np.ones((ROPE_HALF,), np.float32)
    invf = np.concatenate([inv_freq, zeros_half, inv_freq, zeros_half]).reshape(1, LANES)
    sgn = np.concatenate([-ones_half, zeros_half, ones_half, zeros_half]).reshape(1, LANES)
    spare = np.arange(LANES) == SPARE_ROPE_LANE
    kpad = spare.astype(np.float32).reshape(1, LANES)
    pad_lanes = LANES - N_GROUPS - N_EXPERTS
    wr = jnp.concatenate([w_router_group[layer], w_router_expert[layer],
                          jnp.zeros((D, pad_lanes), F32)], axis=1)
    wr_hi = wr.astype(BF16)
    wr = jnp.concatenate([wr_hi, (wr - wr_hi.astype(F32)).astype(BF16)], axis=1)
    br = jnp.concatenate([b_router_group[layer], b_router_expert[layer],
                          jnp.zeros((pad_lanes,), F32)]).reshape(1, LANES)

    bound = _score_bound(q_norm_gain[layer], k_norm_gain[layer])
    bounded = bound <= SCORE_BOUND_LIMIT
    qpad = jnp.where(bounded, -bound, 0.0) * kpad
    ypool, q, k, vt = _pre_call(
        x, positions.reshape(B, S, 1), row(norm1_gain), win, row(q_a_gain), wq,
        row(kv_a_gain), wkv, qng, kng, w_pool[layer].astype(BF16), row(pool_scale), invf, sgn,
        qpad, kpad)
    yattn = _attn_call(bounded.astype(I32).reshape(1), q, k, vt)
    h1, xn2, logits = _post_call(
        x.reshape(T, D), ypool.reshape(T, D_POOL), yattn.reshape(T, D_ATTN),
        w_out[layer].astype(BF16), row(norm2_gain), wr, br)
    code, wts, counts = _route_call(logits)

    n_assign = T * TOP_K
    n_blocks = (n_assign + N_EXPERTS * (MOE_BLK - 1)) // MOE_BLK
    slots, blk_e, blk_n, blk_w = _plan_call(code[:TOP_K].reshape(n_assign),
                                            counts.reshape(LANES), n_blocks + MOE_DRAIN_STEPS)
    y = _expert_call(blk_e, blk_n, blk_w, slots, xn2,
                     w_exp_gate[layer], w_exp_up[layer], w_exp_down[layer])

    out = _final_call(h1, y, wts, p[layer].reshape(T, PLE_DIM),
                      w_ple_proj[layer].astype(BF16), row(ple_norm_gain), row(norm3_gain),
                      w_ple_gate[layer].astype(BF16))
    return out.reshape(B, S, D)
```

```python
import functools
import math

import jax
import jax.numpy as jnp
import numpy as np
from jax import lax
from jax.experimental import pallas as pl
from jax.experimental.pallas import tpu as pltpu

F32 = jnp.float32
BF16 = jnp.bfloat16
I32 = jnp.int32

D_MODEL = 2048
PLE_DIM = 256
EPS = 1e-6
POOL_WINDOWS = (2, 4, 8, 16)
POOL_CH = 256
D_POOL = POOL_CH * len(POOL_WINDOWS)
N_HEADS = 8
Q_LORA = 512
KV_LORA = 512
QK_NOPE = 128
QK_ROPE = 64
QK_DIM = QK_NOPE + QK_ROPE
V_DIM = 128
D_ATTN = N_HEADS * V_DIM
ROPE_THETA = 10000.0
ATTN_SCALE = 1.0 / math.sqrt(QK_DIM)
Q_PRESCALE = ATTN_SCALE * math.log2(math.e)
N_GROUPS = 8
EXPERTS_PER_GROUP = 8
N_EXPERTS = N_GROUPS * EXPERTS_PER_GROUP
TOP_K = 2
D_EXPERT = 512

LANES = 128
PACK_SUBLANES = 8
MXU_DIM = 256
VMEM_LIMIT = 56 * 1024 * 1024

HEAD_W = 2 * LANES
ROPE_HALF = QK_ROPE // 2
POOL_HALO = 16
TM_PRE = 512
PRE_SUB = 256
TM_POST = 512
POST_SUB = 256
TM_FINAL = 512
FINAL_SUB = 256
WIN_ROWS = 256
TQ = 512
TK = PRE_SUB
TM_ROUTE = 1024
ROUTE_SUB = 256
MOE_BLK = 256
MOE_DRAIN_STEPS = 3
ROW_GROUP = 32
WEIGHT_DMA_PRIORITY = 1
WEIGHT_SLOTS = 3
PLAN_UNROLL = 16
NEG_BIG = -1e30
V_ROWS = V_DIM + 16
SPARE_ROPE_LANE = ROPE_HALF
SCORE_BOUND_LIMIT = 50.0


def _const_spec(shape):
    nd = len(shape)
    return pl.BlockSpec(shape, lambda *_: (0,) * nd, pipeline_mode=pl.Buffered(1))


def _rms(x, gain):
    return x * lax.rsqrt(jnp.mean(x * x, axis=-1, keepdims=True) + EPS) * gain


def _store_packed_rows(ref, first_row, x):
    rows, d = x.shape
    half = d // 2
    assert half == PACK_SUBLANES * LANES
    for i in range(PACK_SUBLANES):
        lo = x[:, i * LANES:(i + 1) * LANES].astype(BF16).astype(F32)
        hi = x[:, half + i * LANES:half + (i + 1) * LANES].astype(BF16).astype(F32)
        word = (lax.shift_right_logical(pltpu.bitcast(lo, jnp.uint32), jnp.uint32(16))
                | (pltpu.bitcast(hi, jnp.uint32) & jnp.uint32(0xFFFF0000)))
        ref[pl.ds(first_row * PACK_SUBLANES + i, rows, stride=PACK_SUBLANES), :] = word


def _load_packed_rows(ref, first_row, rows):
    lo, hi = [], []
    for i in range(PACK_SUBLANES):
        word = ref[pl.ds(first_row * PACK_SUBLANES + i, rows, stride=PACK_SUBLANES), :]
        lo.append(pltpu.bitcast(lax.shift_left(word, jnp.uint32(16)), F32))
        hi.append(pltpu.bitcast(word & jnp.uint32(0xFFFF0000), F32))
    return jnp.concatenate(lo + hi, axis=1)


def _pre_kernel(x_ref, pos_ref, g1_ref, win_ref, qag_ref, wq_ref, kvag_ref, wkv_ref,
                qng_ref, kng_ref, wpool_ref, pscale_ref, invf_ref, sgn_ref, qpad_ref, kpad_ref,
                ypool_ref, q_ref, k_ref, vt_ref, carry_ref):
    st = pl.program_id(1)

    @pl.when(st == 0)
    def _():
        carry_ref[...] = jnp.zeros_like(carry_ref)

    halo = carry_ref[...]
    for i in range(x_ref.shape[0] // PRE_SUB):
        halo = _pre_rows(i, st * x_ref.shape[0] + i * PRE_SUB, halo,
                         x_ref, pos_ref, g1_ref, win_ref, qag_ref, wq_ref, kvag_ref, wkv_ref,
                         qng_ref, kng_ref, wpool_ref, pscale_ref, invf_ref, sgn_ref, qpad_ref, kpad_ref,
                         ypool_ref, q_ref, k_ref, vt_ref)
    carry_ref[...] = halo


def _pre_rows(i, first_pos, halo, x_ref, pos_ref, g1_ref, win_ref, qag_ref, wq_ref, kvag_ref,
              wkv_ref, qng_ref, kng_ref, wpool_ref, pscale_ref, invf_ref, sgn_ref, qpad_ref, kpad_ref,
              ypool_ref, q_ref, k_ref, vt_ref):
    tm = PRE_SUB
    rs = pl.ds(i * tm, tm)

    hn = _rms(x_ref[rs, :], g1_ref[...])
    z = jnp.dot(hn.astype(BF16), win_ref[...], preferred_element_type=F32)
    k_rope = z[:, D_POOL + Q_LORA + KV_LORA:]

    u = z[:, :D_POOL]
    ext = jnp.concatenate([halo, u], axis=0)
    row = lax.broadcasted_iota(I32, (tm, 1), 0) + first_pos
    level = ext
    shift = 1
    for g, w in enumerate(POOL_WINDOWS):
        sl = slice(g * POOL_CH, (g + 1) * POOL_CH)
        while shift < w:
            level = level + pltpu.roll(level, shift, 0)
            shift *= 2
        win_sum = level[POOL_HALO:, sl]
        cnt = jnp.minimum(row + 1, w).astype(F32)
        d = win_sum / cnt - u[:, sl]
        y = jnp.dot(d.astype(BF16), wpool_ref[g], preferred_element_type=F32)
        ypool_ref[rs, sl] = (y * pscale_ref[:, sl]).astype(BF16)

    q_lat = z[:, D_POOL:D_POOL + Q_LORA]
    kv_lat = z[:, D_POOL + Q_LORA:D_POOL + Q_LORA + KV_LORA]
    qa = jnp.dot(_rms(q_lat, qag_ref[...]).astype(BF16), wq_ref[...],
                 preferred_element_type=F32)
    kv = jnp.dot(_rms(kv_lat, kvag_ref[...]).astype(BF16), wkv_ref[...],
                 preferred_element_type=F32)

    ang = pos_ref[rs, :].astype(F32) * invf_ref[...]
    cos = jnp.cos(ang)
    sin = jnp.sin(ang) * sgn_ref[...]

    def rot(t):
        return t * cos + pltpu.roll(t, LANES // 2, 1) * sin

    qng = qng_ref[...]
    kng = kng_ref[...]
    kr_rot = rot(k_rope * kng[:, LANES:])
    kr_ssq = jnp.sum(k_rope * k_rope, axis=-1, keepdims=True)
    for h in range(N_HEADS):
        qh = qa[:, h * HEAD_W:(h + 1) * HEAD_W]
        rq = lax.rsqrt(jnp.sum(qh * qh, axis=-1, keepdims=True) / QK_DIM + EPS) * Q_PRESCALE
        qn = qh * rq * qng
        q_ref[0, h, rs, :LANES] = qn[:, :LANES].astype(BF16)
        q_ref[0, h, rs, LANES:] = (rot(qn[:, LANES:]) + qpad_ref[...]).astype(BF16)
        kh = kv[:, h * QK_NOPE:(h + 1) * QK_NOPE]
        rk = lax.rsqrt((jnp.sum(kh * kh, axis=-1, keepdims=True) + kr_ssq) / QK_DIM + EPS)
        k_ref[0, h, rs, :LANES] = (kh * rk * kng[:, :LANES]).astype(BF16)
        k_ref[0, h, rs, LANES:] = (kr_rot * rk + kpad_ref[...]).astype(BF16)
        vh = kv[:, N_HEADS * QK_NOPE + h * V_DIM:N_HEADS * QK_NOPE + (h + 1) * V_DIM]
        vt_ref[0, h, i, :V_DIM, :] = vh.T.astype(BF16)
        vt_ref[0, h, i, V_DIM:, :] = jnp.ones((V_ROWS - V_DIM, tm), BF16)
    return u[tm - POOL_HALO:, :]


def _win_kernel(wt_ref, o_ref):
    w = wt_ref[...].T
    n_lat = o_ref.shape[1] - LANES
    zeros = jnp.zeros((w.shape[0], ROPE_HALF), F32)
    rope = jnp.concatenate([w[:, n_lat:n_lat + ROPE_HALF], zeros,
                            w[:, n_lat + ROPE_HALF:], zeros], axis=1)
    o_ref[:, :n_lat] = w[:, :n_lat].astype(BF16)
    o_ref[:, n_lat:] = rope.astype(BF16)


def _win_call(w_in_t):
    n_in, D = w_in_t.shape
    n_lat = n_in - QK_ROPE
    tm = WIN_ROWS
    return pl.pallas_call(
        _win_kernel,
        grid=(D // tm,),
        in_specs=[pl.BlockSpec((n_in, tm), lambda i: (0, i))],
        out_specs=pl.BlockSpec((tm, n_lat + LANES), lambda i: (i, 0)),
        out_shape=jax.ShapeDtypeStruct((D, n_lat + LANES), BF16),
        compiler_params=pltpu.CompilerParams(
            dimension_semantics=("arbitrary",), vmem_limit_bytes=VMEM_LIMIT),
        name="win",
    )(w_in_t)


def _pre_call(x, pos, g1, win, qag, wq, kvag, wkv, qng, kng, wpool, pscale, invf, sgn,
              qpad, kpad):
    B, S, D = x.shape
    tm = TM_PRE
    grid = (B, S // tm)
    row_spec = lambda w: pl.BlockSpec((None, tm, w), lambda b, s: (b, s, 0))
    head_spec = lambda w: pl.BlockSpec((1, N_HEADS, tm, w), lambda b, s: (b, 0, s, 0))
    consts = [g1, win, qag, wq, kvag, wkv, qng, kng, wpool, pscale, invf, sgn, qpad, kpad]
    return pl.pallas_call(
        _pre_kernel,
        grid=grid,
        in_specs=[row_spec(D), row_spec(1)] + [_const_spec(c.shape) for c in consts],
        out_specs=[row_spec(D_POOL), head_spec(HEAD_W), head_spec(HEAD_W),
                   pl.BlockSpec((1, N_HEADS, tm // PRE_SUB, V_ROWS, PRE_SUB),
                                lambda b, s: (b, 0, s, 0, 0))],
        out_shape=[jax.ShapeDtypeStruct((B, S, D_POOL), BF16),
                   jax.ShapeDtypeStruct((B, N_HEADS, S, HEAD_W), BF16),
                   jax.ShapeDtypeStruct((B, N_HEADS, S, HEAD_W), BF16),
                   jax.ShapeDtypeStruct((B, N_HEADS, S // PRE_SUB, V_ROWS, PRE_SUB), BF16)],
        scratch_shapes=[pltpu.VMEM((POOL_HALO, D_POOL), F32)],
        compiler_params=pltpu.CompilerParams(
            dimension_semantics=("arbitrary", "arbitrary"), vmem_limit_bytes=VMEM_LIMIT),
        name="pre",
    )(x, pos, *consts)


def _attn_kernel(bounded_ref, q_ref, k_ref, vt_ref, o_ref, *chain_scratch):
    S = q_ref.shape[2]
    nq = S // TQ
    per_q = TQ // TK
    assert per_q == 2, "two online-softmax chains take the even / odd key tiles"
    chains = (chain_scratch[0:2], chain_scratch[2:4])
    score_bufs = (chain_scratch[4:6], chain_scratch[6:8])
    key_idx = lax.broadcasted_iota(I32, (TK, TQ), 0)
    qry_idx = lax.broadcasted_iota(I32, (TK, TQ), 1)

    diag_masks = (key_idx <= qry_idx, key_idx + TK <= qry_idx)

    def weighted_values(kt, p):
        return jnp.dot(vt_ref[0, 0, kt], p.astype(BF16), preferred_element_type=F32)

    def write_out(qi, acc):
        out = acc[:V_DIM, :] / acc[V_DIM:V_DIM + 1, :]
        start = qi * TQ if isinstance(qi, int) else pl.multiple_of(qi * TQ, TQ)
        o_ref[0, pl.ds(start, TQ), :] = out.T.astype(BF16)

    def bounded_head():
        steps = [(qi, j) for qi in range(nq) for j in range(qi + 1)]

        def pair_scores(step, buf):
            qi, j = step
            q = q_ref[0, 0, qi * TQ:(qi + 1) * TQ, :]
            for c in range(per_q):
                kt = per_q * j + c
                buf[c][...] = lax.dot_general(k_ref[0, 0, kt * TK:(kt + 1) * TK, :], q,
                                              (((1,), (1,)), ((), ())),
                                              preferred_element_type=F32)

        pair_scores(steps[0], score_bufs[0])
        acc = None
        for t, (qi, j) in enumerate(steps):
            if t + 1 < len(steps):
                pair_scores(steps[t + 1], score_bufs[(t + 1) % 2])
            for c in range(per_q):
                st = score_bufs[t % 2][c][...]
                if j == qi:
                    st = jnp.where(diag_masks[c], st, NEG_BIG)
                pv = weighted_values(per_q * j + c, jnp.exp2(st))
                acc = pv if acc is None else acc + pv
            if j == qi:
                write_out(qi, acc)
                acc = None

    def online_q_tile(qi, _):
        q = q_ref[0, 0, pl.ds(pl.multiple_of(qi * TQ, TQ), TQ), :]
        for m_ref, acc_ref in chains:
            m_ref[...] = jnp.full(m_ref.shape, NEG_BIG, F32)
            acc_ref[...] = jnp.zeros(acc_ref.shape, F32)

        def scores(kt):
            k = k_ref[0, 0, pl.ds(pl.multiple_of(kt * TK, TK), TK), :]
            return lax.dot_general(k, q, (((1,), (1,)), ((), ())), preferred_element_type=F32)

        def fold(chain, st, kt, mask):
            m_ref, acc_ref = chain
            if mask is not None:
                st = jnp.where(mask, st, NEG_BIG)
            m = m_ref[...]
            m_new = jnp.maximum(m, jnp.max(st, axis=0, keepdims=True))
            m_ref[...] = m_new
            acc_ref[...] = (jnp.exp2(m - m_new) * acc_ref[...]
                            + weighted_values(kt, jnp.exp2(st - m_new)))

        def pair_scores(j, buf):
            for c in range(per_q):
                buf[c][...] = scores(per_q * j + c)

        def fold_pair(j, buf, masks=(None, None)):
            for c in range(per_q):
                fold(chains[c], buf[c][...], per_q * j + c, masks[c])

        pair_scores(0, score_bufs[0])

        def two_pairs(i, _):
            pair_scores(2 * i + 1, score_bufs[1])
            fold_pair(2 * i, score_bufs[0])
            pair_scores(2 * i + 2, score_bufs[0])
            fold_pair(2 * i + 1, score_bufs[1])
            return 0
        lax.fori_loop(0, qi // 2, two_pairs, 0)

        @pl.when(qi % 2 == 0)
        def _():
            fold_pair(qi, score_bufs[0], diag_masks)

        @pl.when(qi % 2 == 1)
        def _():
            pair_scores(qi, score_bufs[1])
            fold_pair(qi - 1, score_bufs[0])
            fold_pair(qi, score_bufs[1], diag_masks)

        (m0, acc0), (m1, acc1) = chains
        m = jnp.maximum(m0[...], m1[...])
        write_out(qi, jnp.exp2(m0[...] - m) * acc0[...] + jnp.exp2(m1[...] - m) * acc1[...])
        return 0

    pl.when(bounded_ref[0] != 0)(bounded_head)

    @pl.when(bounded_ref[0] == 0)
    def _():
        lax.fori_loop(0, nq, online_q_tile, 0)


def _attn_call(bounded, q, k, vt):
    B, H, S, _ = q.shape
    head = lambda w: pl.BlockSpec((1, 1, S, w), lambda b, h, flag: (b, h, 0, 0))
    grid_spec = pltpu.PrefetchScalarGridSpec(
        num_scalar_prefetch=1,
        grid=(B, H),
        in_specs=[head(HEAD_W), head(HEAD_W),
                  pl.BlockSpec((1, 1) + vt.shape[2:], lambda b, h, flag: (b, h, 0, 0, 0))],
        out_specs=pl.BlockSpec((1, S, V_DIM), lambda b, h, flag: (b, 0, h)),
        scratch_shapes=[pltpu.VMEM((1, TQ), F32), pltpu.VMEM((V_ROWS, TQ), F32)] * 2
        + [pltpu.VMEM((TK, TQ), F32)] * 4,
    )
    return pl.pallas_call(
        _attn_kernel,
        grid_spec=grid_spec,
        out_shape=jax.ShapeDtypeStruct((B, S, H * V_DIM), BF16),
        compiler_params=pltpu.CompilerParams(
            dimension_semantics=("arbitrary", "arbitrary"), vmem_limit_bytes=VMEM_LIMIT),
        name="attn",
    )(bounded, q, k, vt)


def _post_kernel(x_ref, yp_ref, ya_ref, wo_ref, g2_ref, wr_ref, br_ref,
                 h1_ref, xn_ref, lg_ref):
    for i in range(x_ref.shape[0] // POST_SUB):
        rs = pl.ds(i * POST_SUB, POST_SUB)
        h1 = (x_ref[rs, :]
              + jnp.dot(yp_ref[rs, :], wo_ref[:D_POOL, :], preferred_element_type=F32)
              + jnp.dot(ya_ref[rs, :], wo_ref[D_POOL:, :], preferred_element_type=F32))
        h1_ref[rs, :] = h1
        xn = _rms(h1, g2_ref[...])
        _store_packed_rows(xn_ref, i * POST_SUB, xn)
        xn_hi = xn.astype(BF16)
        xn_lo = (xn - xn_hi.astype(F32)).astype(BF16)
        both = (jnp.dot(xn_hi, wr_ref[...], preferred_element_type=F32)
                + jnp.dot(xn_lo, wr_ref[...], preferred_element_type=F32))
        lg_ref[rs, :] = both[:, :LANES] + both[:, LANES:] + br_ref[...]


def _post_call(x, yp, ya, wo, g2, wr, br):
    T, D = x.shape
    tm = TM_POST
    row = lambda w: pl.BlockSpec((tm, w), lambda i: (i, 0))
    return pl.pallas_call(
        _post_kernel,
        grid=(T // tm,),
        in_specs=[row(D), row(D_POOL), row(D_ATTN), _const_spec(wo.shape),
                  _const_spec(g2.shape), _const_spec(wr.shape), _const_spec(br.shape)],
        out_specs=[row(D), pl.BlockSpec((tm * PACK_SUBLANES, LANES), lambda i: (i, 0)),
                   row(LANES)],
        out_shape=[jax.ShapeDtypeStruct((T, D), F32),
                   jax.ShapeDtypeStruct((T * PACK_SUBLANES, LANES), jnp.uint32),
                   jax.ShapeDtypeStruct((T, LANES), F32)],
        compiler_params=pltpu.CompilerParams(
            dimension_semantics=("arbitrary",), vmem_limit_bytes=VMEM_LIMIT),
        name="post",
    )(x, yp, ya, wo, g2, wr, br)


def _route_kernel(lg_ref, code_ref, wts_ref, cnt_ref, carry_ref):
    @pl.when(pl.program_id(0) == 0)
    def _():
        carry_ref[...] = jnp.zeros_like(carry_ref)

    for i in range(lg_ref.shape[0] // ROUTE_SUB):
        rs = pl.ds(i * ROUTE_SUB, ROUTE_SUB)
        code_t, wts_t = _route_cols(lg_ref[rs, :].T, carry_ref)
        code_ref[:, rs] = code_t
        wts_cols = jnp.concatenate(
            [wts_t, jnp.zeros((LANES - PACK_SUBLANES, ROUTE_SUB), F32)], axis=0)
        wts_ref[rs, :] = wts_cols.T
    cnt_ref[...] = carry_ref[...].astype(I32)


def _route_call(logits):
    T = logits.shape[0]
    tm = TM_ROUTE
    row = pl.BlockSpec((tm, LANES), lambda i: (i, 0))
    return pl.pallas_call(
        _route_kernel,
        grid=(T // tm,),
        in_specs=[row],
        out_specs=[pl.BlockSpec((PACK_SUBLANES, tm), lambda i: (0, i)), row,
                   pl.BlockSpec((LANES, 1), lambda i: (0, 0))],
        out_shape=[jax.ShapeDtypeStruct((PACK_SUBLANES, T), I32),
                   jax.ShapeDtypeStruct((T, LANES), F32),
                   jax.ShapeDtypeStruct((LANES, 1), I32)],
        scratch_shapes=[pltpu.VMEM((LANES, 1), F32)],
        compiler_params=pltpu.CompilerParams(
            dimension_semantics=("arbitrary",), vmem_limit_bytes=VMEM_LIMIT),
        name="route",
    )(logits)


def _route_cols(lt, carry_ref):
    tm = lt.shape[1]
    sub = lax.broadcasted_iota(I32, (PACK_SUBLANES, tm), 0).astype(F32)

    def first_argmax(vals):
        mx = jnp.max(vals, axis=0, keepdims=True)
        idx = jnp.min(jnp.where(vals == mx, sub, float(PACK_SUBLANES)), axis=0, keepdims=True)
        return mx, idx

    g_logits = lt[0:N_GROUPS, :]
    g_max, g_sel = first_argmax(g_logits)
    g_w = 1.0 / jnp.sum(jnp.exp(g_logits - g_max), axis=0, keepdims=True)

    e_logits = lt[N_GROUPS:N_GROUPS + EXPERTS_PER_GROUP, :]
    for g in range(1, N_GROUPS):
        lo = N_GROUPS + g * EXPERTS_PER_GROUP
        e_logits = jnp.where(g_sel == float(g), lt[lo:lo + EXPERTS_PER_GROUP, :], e_logits)
    v1, i1 = first_argmax(e_logits)
    v2, i2 = first_argmax(jnp.where(sub == i1, -jnp.inf, e_logits))
    e1 = (g_sel * EXPERTS_PER_GROUP + i1).astype(I32)
    e2 = (g_sel * EXPERTS_PER_GROUP + i2).astype(I32)
    t = jnp.exp(v2 - v1)
    w1 = g_w / (1.0 + t)
    w2 = g_w * t / (1.0 + t)

    expert_row = lax.broadcasted_iota(I32, (LANES, tm), 0)
    hit1 = expert_row == e1
    hit2 = expert_row == e2
    onehot = jnp.where(hit1 | hit2, 1.0, 0.0)
    r = lax.broadcasted_iota(I32, (tm, tm), 0)
    c = lax.broadcasted_iota(I32, (tm, tm), 1)
    earlier = jnp.where(r < c, 1.0, 0.0).astype(BF16)
    before = (jnp.dot(onehot.astype(BF16), earlier, preferred_element_type=F32)
              + carry_ref[...])
    carry_ref[...] += jnp.sum(onehot, axis=1, keepdims=True)
    pos1 = jnp.sum(jnp.where(hit1, before, 0.0), axis=0, keepdims=True).astype(I32)
    pos2 = jnp.sum(jnp.where(hit2, before, 0.0), axis=0, keepdims=True).astype(I32)
    code1 = e1 * 65536 + pos1
    code2 = e2 * 65536 + pos2
    first, second = sub == 0.0, sub == 1.0
    return (jnp.where(first, code1, jnp.where(second, code2, 0)),
            jnp.where(first, w1, jnp.where(second, w2, 0.0)))


def _plan_kernel(code_ref, cnt_ref, unused_hbm, slots_hbm, blke_ref, blkn_ref, blkw_ref,
                 rune_ref, slot_ref, start_ref, sem):
    n_assign = code_ref.shape[0]
    n_blocks = blke_ref.shape[0]
    load = pltpu.make_async_copy(unused_hbm, slot_ref, sem)
    load.start()

    def per_expert(e, nb_done):
        cnt = cnt_ref[e]
        nb = lax.shift_right_logical(cnt + (MOE_BLK - 1), MOE_BLK.bit_length() - 1)
        start_ref[e] = nb_done * MOE_BLK

        def per_block(j, _):
            blke_ref[nb_done + j] = e
            blkn_ref[nb_done + j] = jnp.minimum(cnt - j * MOE_BLK, MOE_BLK)
            return 0
        lax.fori_loop(0, nb, per_block, 0)
        return nb_done + nb
    used = lax.fori_loop(0, N_EXPERTS, per_expert, 0)

    last_e = blke_ref[jnp.maximum(used - 1, 0)]

    def tail(b, _):
        blke_ref[b] = last_e
        blkn_ref[b] = 0
        blkw_ref[b] = 0
        return 0
    lax.fori_loop(used, n_blocks, tail, 0)

    def no_run(r, _):
        rune_ref[r] = -1
        return 0
    lax.fori_loop(0, rune_ref.shape[0], no_run, 0)

    def runs_forward(b, carry):
        earlier_e, run = carry
        e = blke_ref[b]
        first = (e != earlier_e).astype(I32)
        run = run + first
        blkw_ref[b] = first + run * 2
        rune_ref[run] = e
        return e, run
    lax.fori_loop(0, used, runs_forward, (-1, -1))

    load.wait()

    def place(a, _):
        code = code_ref[a]
        e = lax.shift_right_logical(code, 16)
        slot_ref[start_ref[e] + (code & 0xFFFF)] = a
        return 0
    lax.fori_loop(0, n_assign, place, 0, unroll=PLAN_UNROLL)

    store = pltpu.make_async_copy(slot_ref, slots_hbm, sem)
    store.start()
    store.wait()


def _plan_call(code_flat, counts, n_tab):
    n_assign = code_flat.shape[0]
    n_slots = n_tab * MOE_BLK
    unused = (n_assign + (np.arange(n_slots) & (MOE_BLK - 1))).astype(np.int32)
    smem = pl.BlockSpec(memory_space=pltpu.SMEM)
    hbm = pl.BlockSpec(memory_space=pl.ANY)
    return pl.pallas_call(
        _plan_kernel,
        in_specs=[smem, smem, hbm],
        out_specs=[hbm, smem, smem, smem, smem],
        out_shape=[jax.ShapeDtypeStruct((n_slots,), I32),
                   jax.ShapeDtypeStruct((n_tab,), I32),
                   jax.ShapeDtypeStruct((n_tab,), I32),
                   jax.ShapeDtypeStruct((n_tab,), I32),
                   jax.ShapeDtypeStruct((N_EXPERTS + WEIGHT_SLOTS,), I32)],
        scratch_shapes=[pltpu.SMEM((n_slots,), I32), pltpu.SMEM((N_EXPERTS,), I32),
                        pltpu.SemaphoreType.DMA],
        name="plan",
    )(code_flat, counts, unused)


def _expert_kernel(blke_ref, blkn_ref, blkw_ref, rune_ref, slot_g_ref, slot_s_ref, xn_hbm,
                   wg_hbm, wu_hbm, wd_hbm, y_hbm, x0, x1, y0, y1, wg_buf, wu_buf, wd_buf,
                   gsem, ssem, wsem):
    xbufs, ybufs = (x0, x1), (y0, y1)
    s = pl.program_id(0)
    n_tok = xn_hbm.shape[0] // PACK_SUBLANES

    def weight_copies(e, slot):
        return [pltpu.make_async_copy(hbm.at[e], buf.at[slot], wsem.at[slot])
                for hbm, buf in ((wg_hbm, wg_buf), (wu_hbm, wu_buf), (wd_hbm, wd_buf))]

    blk = jnp.maximum(s - 1, 0)
    run_info = blkw_ref[blk]
    run = run_info >> 1
    w_slot = lax.rem(run, WEIGHT_SLOTS)
    ahead_run = run + (WEIGHT_SLOTS - 1)
    ahead_expert = rune_ref[ahead_run]
    ahead_slot = lax.rem(ahead_run, WEIGHT_SLOTS)

    def tile(ref, r):
        start = r * PACK_SUBLANES
        if not isinstance(r, int):
            start = pl.multiple_of(start, PACK_SUBLANES)
        return ref.at[pl.ds(start, PACK_SUBLANES)]

    def rows(j):
        n = blkn_ref[jnp.clip(j, 0, blkn_ref.shape[0] - 1)]
        return jnp.where(j >= 0, (n + (ROW_GROUP - 1)) & ~(ROW_GROUP - 1), 0)
    rows0, rows1, rows2, rows3 = rows(s), rows(s - 1), rows(s - 2), rows(s - 3)

    def stage(nxt):
        cur = 1 - nxt
        x_nxt, x_cur, y_nxt, y_cur = xbufs[nxt], xbufs[cur], ybufs[nxt], ybufs[cur]

        def per_group(n_rows, issue_row):
            for g in range(MOE_BLK // ROW_GROUP):
                @pl.when(g * ROW_GROUP < n_rows)
                def _():
                    for r in range(g * ROW_GROUP, (g + 1) * ROW_GROUP):
                        issue_row(r)

        def gather_row(r):
            tok = slot_g_ref[0, 0, r] & (n_tok - 1)
            pltpu.make_async_copy(tile(xn_hbm, tok), tile(x_nxt, r), gsem.at[nxt]).start()

        def scatter_row(r):
            pltpu.make_async_copy(tile(y_nxt, r), tile(y_hbm, slot_s_ref[0, 0, r]),
                                  ssem.at[nxt]).start()

        def wait_rows(src, dst, sem, n_rows):
            n_words = pl.multiple_of(n_rows * PACK_SUBLANES, ROW_GROUP * PACK_SUBLANES)

            @pl.when(n_rows > 0)
            def _():
                pltpu.make_async_copy(src.at[pl.ds(0, n_words)], dst.at[pl.ds(0, n_words)],
                                      sem).wait()

        def compute():
            xb = _load_packed_rows(x_cur, 0, MOE_BLK).astype(BF16)
            a = jnp.dot(xb, wg_buf[w_slot].astype(BF16), preferred_element_type=F32)
            u = jnp.dot(xb, wu_buf[w_slot].astype(BF16), preferred_element_type=F32)
            hmid = (a * jax.nn.sigmoid(a) * u).astype(BF16)
            _store_packed_rows(y_cur, 0, jnp.dot(hmid, wd_buf[w_slot].astype(BF16),
                                                 preferred_element_type=F32))

        wait_rows(xn_hbm, x_cur, gsem.at[cur], rows1)
        wait_rows(y_cur, y_hbm, ssem.at[cur], rows3)

        @pl.when((rows1 > 0) & ((run_info & 1) == 1))
        def _():
            for copy in weight_copies(blke_ref[blk], w_slot):
                copy.wait()

            @pl.when(ahead_expert >= 0)
            def _():
                for copy in weight_copies(ahead_expert, ahead_slot):
                    copy.start(priority=WEIGHT_DMA_PRIORITY)

        if nxt == 0:
            for first_run in range(WEIGHT_SLOTS - 1):
                @pl.when((s == 0) & (rune_ref[first_run] >= 0))
                def _():
                    for copy in weight_copies(rune_ref[first_run], first_run):
                        copy.start(priority=WEIGHT_DMA_PRIORITY)

            @pl.when(s == 0)
            def _():
                x_nxt[...] = jnp.zeros(x_nxt.shape, x_nxt.dtype)
                x_cur[...] = jnp.zeros(x_cur.shape, x_cur.dtype)
                y_nxt[...] = jnp.zeros(y_nxt.shape, y_nxt.dtype)
                dump = pltpu.make_async_copy(
                    y_nxt, y_hbm.at[pl.ds(TOP_K * n_tok * PACK_SUBLANES,
                                          MOE_BLK * PACK_SUBLANES)], ssem.at[nxt])
                dump.start()
                dump.wait()

        per_group(rows0, gather_row)
        per_group(rows2, scatter_row)

        @pl.when(rows1 > 0)
        def _():
            compute()

    for parity in range(2):
        pl.when((s & 1) == parity)(functools.partial(stage, parity))


def _expert_call(blk_e, blk_n, blk_w, run_e, slots, xn, wg, wu, wd):
    T = xn.shape[0] // PACK_SUBLANES
    assert T & (T - 1) == 0, "token id is recovered from the assignment id with a mask"
    n_tab = blk_e.shape[0]
    _, D, F = wg.shape
    packed_block = pltpu.VMEM((MOE_BLK * PACK_SUBLANES, LANES), jnp.uint32)
    hbm = pl.BlockSpec(memory_space=pl.ANY)
    grid_spec = pltpu.PrefetchScalarGridSpec(
        num_scalar_prefetch=4,
        grid=(n_tab,),
        in_specs=[
            pl.BlockSpec((1, 1, MOE_BLK), lambda s, *_: (s, 0, 0), memory_space=pltpu.SMEM),
            pl.BlockSpec((1, 1, MOE_BLK), lambda s, *_: (jnp.maximum(s - 2, 0), 0, 0),
                         memory_space=pltpu.SMEM),
            hbm, hbm, hbm, hbm,
        ],
        out_specs=hbm,
        scratch_shapes=[packed_block] * 4
        + [pltpu.VMEM((WEIGHT_SLOTS, D, F), F32), pltpu.VMEM((WEIGHT_SLOTS, D, F), F32),
           pltpu.VMEM((WEIGHT_SLOTS, F, D), F32)]
        + [pltpu.SemaphoreType.DMA((2,)), pltpu.SemaphoreType.DMA((2,)),
           pltpu.SemaphoreType.DMA((WEIGHT_SLOTS,))],
    )
    slots3 = slots.reshape(n_tab, 1, MOE_BLK)
    return pl.pallas_call(
        _expert_kernel,
        grid_spec=grid_spec,
        out_shape=jax.ShapeDtypeStruct(((T * TOP_K + MOE_BLK) * PACK_SUBLANES, LANES),
                                       jnp.uint32),
        compiler_params=pltpu.CompilerParams(
            dimension_semantics=("arbitrary",), vmem_limit_bytes=VMEM_LIMIT,
            disable_bounds_checks=True),
        name="experts",
    )(blk_e, blk_n, blk_w, run_e, slots3, slots3, xn, wg, wu, wd)


def _final_kernel(h1_ref, y1_ref, y2_ref, wts_ref, p_ref, wpp_ref, pg_ref, g3_ref, wpg_ref,
                  o_ref):
    for i in range(h1_ref.shape[0] // FINAL_SUB):
        rs = pl.ds(i * FINAL_SUB, FINAL_SUB)
        w = wts_ref[rs, :]
        h2 = (h1_ref[rs, :]
              + w[:, 0:1] * _load_packed_rows(y1_ref, i * FINAL_SUB, FINAL_SUB)
              + w[:, 1:2] * _load_packed_rows(y2_ref, i * FINAL_SUB, FINAL_SUB))
        e = _rms(jnp.dot(p_ref[rs, :].astype(BF16), wpp_ref[...], preferred_element_type=F32),
                 pg_ref[...])
        gate = jax.nn.sigmoid(jnp.dot(_rms(h2, g3_ref[...]).astype(BF16), wpg_ref[...],
                                      preferred_element_type=F32))
        o_ref[rs, :] = h2 + gate * e


def _final_call(h1, y, wts, p, wpp, pg, g3, wpg):
    T, D = h1.shape
    tm = TM_FINAL
    row = lambda w: pl.BlockSpec((tm, w), lambda i: (i, 0))
    first = pl.BlockSpec((tm * PACK_SUBLANES, LANES), lambda i: (i, 0))
    second = pl.BlockSpec((tm * PACK_SUBLANES, LANES), lambda i: (i + T // tm, 0))
    return pl.pallas_call(
        _final_kernel,
        grid=(T // tm,),
        in_specs=[row(D), first, second, row(LANES), row(PLE_DIM), _const_spec(wpp.shape),
                  _const_spec(pg.shape), _const_spec(g3.shape), _const_spec(wpg.shape)],
        out_specs=row(D),
        out_shape=jax.ShapeDtypeStruct((T, D), F32),
        compiler_params=pltpu.CompilerParams(
            dimension_semantics=("arbitrary",), vmem_limit_bytes=VMEM_LIMIT),
        name="final",
    )(h1, y, y, wts, p, wpp, pg, g3, wpg)


def _rope_lanes(a):
    z = jnp.zeros(a.shape[:-1] + (ROPE_HALF,), a.dtype)
    return jnp.concatenate([a[..., :ROPE_HALF], z, a[..., ROPE_HALF:], z], axis=-1)


def _head_lanes(a):
    return jnp.concatenate([a[..., :QK_NOPE], _rope_lanes(a[..., QK_NOPE:])], axis=-1)


def _score_bound(q_gain, k_gain):
    return Q_PRESCALE * QK_DIM * jnp.max(jnp.abs(q_gain)) * jnp.max(jnp.abs(k_gain))


def kernel(x, p, positions, norm1_gain, w_in, q_a_gain, w_q_b, kv_a_gain, w_kv_b, q_norm_gain, k_norm_gain, w_pool, pool_scale, w_out, norm2_gain, w_router_group, b_router_group, w_router_expert, b_router_expert, w_exp_gate, w_exp_up, w_exp_down, norm3_gain, w_ple_gate, w_ple_proj, ple_norm_gain):
    B, S, D = x.shape
    T = B * S
    assert x.shape[2] == D_MODEL and S % TQ == 0 and S % TM_PRE == 0
    assert T % TM_POST == 0 and T % TM_FINAL == 0
    layer = 0
    row = lambda a: a[layer].reshape(1, -1)

    win = _win_call(w_in[layer].T)
    wq = _head_lanes(w_q_b[layer].reshape(Q_LORA, N_HEADS, QK_DIM)
                     ).reshape(Q_LORA, N_HEADS * HEAD_W).astype(BF16)
    wkv3 = w_kv_b[layer].reshape(KV_LORA, N_HEADS, QK_NOPE + V_DIM)
    wkv = jnp.concatenate([wkv3[..., :QK_NOPE].reshape(KV_LORA, -1),
                           wkv3[..., QK_NOPE:].reshape(KV_LORA, -1)], axis=1).astype(BF16)
    qng = _head_lanes(q_norm_gain[layer]).reshape(1, HEAD_W)
    kng = _head_lanes(k_norm_gain[layer]).reshape(1, HEAD_W)
    inv_freq = np.float32(ROPE_THETA) ** (-np.arange(ROPE_HALF, dtype=np.float32) / ROPE_HALF)
    zeros_half = np.zeros((ROPE_HALF,), np.float32)
    ones_half = np.ones((ROPE_HALF,), np.float32)
    invf = np.concatenate([inv_freq, zeros_half, inv_freq, zeros_half]).reshape(1, LANES)
    sgn = np.concatenate([-ones_half, zeros_half, ones_half, zeros_half]).reshape(1, LANES)
    spare = np.arange(LANES) == SPARE_ROPE_LANE
    kpad = spare.astype(np.float32).reshape(1, LANES)
    pad_lanes = LANES - N_GROUPS - N_EXPERTS
    wr = jnp.concatenate([w_router_group[layer], w_router_expert[layer],
                          jnp.zeros((D, pad_lanes), F32)], axis=1)
    wr_hi = wr.astype(BF16)
    wr = jnp.concatenate([wr_hi, (wr - wr_hi.astype(F32)).astype(BF16)], axis=1)
    br = jnp.concatenate([b_router_group[layer], b_router_expert[layer],
                          jnp.zeros((pad_lanes,), F32)]).reshape(1, LANES)

    bound = _score_bound(q_norm_gain[layer], k_norm_gain[layer])
    bounded = bound <= SCORE_BOUND_LIMIT
    qpad = jnp.where(bounded, -bound, 0.0) * kpad
    ypool, q, k, vt = _pre_call(
        x, positions.reshape(B, S, 1), row(norm1_gain), win, row(q_a_gain), wq,
        row(kv_a_gain), wkv, qng, kng, w_pool[layer].astype(BF16), row(pool_scale), invf, sgn,
        qpad, kpad)
    yattn = _attn_call(bounded.astype(I32).reshape(1), q, k, vt)
    h1, xn2, logits = _post_call(
        x.reshape(T, D), ypool.reshape(T, D_POOL), yattn.reshape(T, D_ATTN),
        w_out[layer].astype(BF16), row(norm2_gain), wr, br)
    code, wts, counts = _route_call(logits)

    n_assign = T * TOP_K
    n_blocks = (n_assign + N_EXPERTS * (MOE_BLK - 1)) // MOE_BLK
    slots, blk_e, blk_n, blk_w, run_e = _plan_call(
        code[:TOP_K].reshape(n_assign), counts.reshape(LANES), n_blocks + MOE_DRAIN_STEPS)
    y = _expert_call(blk_e, blk_n, blk_w, run_e, slots, xn2,
                     w_exp_gate[layer], w_exp_up[layer], w_exp_down[layer])

    out = _final_call(h1, y, wts, p[layer].reshape(T, PLE_DIM),
                      w_ple_proj[layer].astype(BF16), row(ple_norm_gain), row(norm3_gain),
                      w_ple_gate[layer].astype(BF16))
    return out.reshape(B, S, D)
```

```python
import functools
import math

import jax
import jax.numpy as jnp
import numpy as np
from jax import lax
from jax.experimental import pallas as pl
from jax.experimental.pallas import tpu as pltpu

F32 = jnp.float32
BF16 = jnp.bfloat16
I32 = jnp.int32

D_MODEL = 2048
PLE_DIM = 256
EPS = 1e-6
POOL_WINDOWS = (2, 4, 8, 16)
POOL_CH = 256
D_POOL = POOL_CH * len(POOL_WINDOWS)
N_HEADS = 8
Q_LORA = 512
KV_LORA = 512
QK_NOPE = 128
QK_ROPE = 64
QK_DIM = QK_NOPE + QK_ROPE
V_DIM = 128
D_ATTN = N_HEADS * V_DIM
ROPE_THETA = 10000.0
ATTN_SCALE = 1.0 / math.sqrt(QK_DIM)
Q_PRESCALE = ATTN_SCALE * math.log2(math.e)
N_GROUPS = 8
EXPERTS_PER_GROUP = 8
N_EXPERTS = N_GROUPS * EXPERTS_PER_GROUP
TOP_K = 2
D_EXPERT = 512

LANES = 128
PACK_SUBLANES = 8
MXU_DIM = 256
VMEM_LIMIT = 56 * 1024 * 1024

HEAD_W = 2 * LANES
ROPE_HALF = QK_ROPE // 2
POOL_HALO = 16
TM_PRE = 512
PRE_SUB = 256
TM_POST = 512
POST_SUB = 256
TM_FINAL = 512
FINAL_SUB = 256
WIN_ROWS = 256
TQ = 512
TK = PRE_SUB
TM_ROUTE = 1024
ROUTE_SUB = 256
MOE_BLK = 256
MOE_DRAIN_STEPS = 3
ROW_GROUP = 32
WEIGHT_DMA_PRIORITY = 1
WEIGHT_SLOTS = 4
PLAN_UNROLL = 16
NEG_BIG = -1e30
V_ROWS = V_DIM + 16
SPARE_ROPE_LANE = ROPE_HALF
SCORE_BOUND_LIMIT = 50.0


def _const_spec(shape):
    nd = len(shape)
    return pl.BlockSpec(shape, lambda *_: (0,) * nd, pipeline_mode=pl.Buffered(1))


def _rms(x, gain):
    return x * lax.rsqrt(jnp.mean(x * x, axis=-1, keepdims=True) + EPS) * gain


def _store_packed_rows(ref, first_row, x):
    rows, d = x.shape
    half = d // 2
    assert half == PACK_SUBLANES * LANES
    for i in range(PACK_SUBLANES):
        lo = x[:, i * LANES:(i + 1) * LANES].astype(BF16).astype(F32)
        hi = x[:, half + i * LANES:half + (i + 1) * LANES].astype(BF16).astype(F32)
        word = (lax.shift_right_logical(pltpu.bitcast(lo, jnp.uint32), jnp.uint32(16))
                | (pltpu.bitcast(hi, jnp.uint32) & jnp.uint32(0xFFFF0000)))
        ref[pl.ds(first_row * PACK_SUBLANES + i, rows, stride=PACK_SUBLANES), :] = word


def _load_packed_rows(ref, first_row, rows):
    lo, hi = [], []
    for i in range(PACK_SUBLANES):
        word = ref[pl.ds(first_row * PACK_SUBLANES + i, rows, stride=PACK_SUBLANES), :]
        lo.append(pltpu.bitcast(lax.shift_left(word, jnp.uint32(16)), F32))
        hi.append(pltpu.bitcast(word & jnp.uint32(0xFFFF0000), F32))
    return jnp.concatenate(lo + hi, axis=1)


def _pre_kernel(x_ref, pos_ref, g1_ref, win_ref, qag_ref, wq_ref, kvag_ref, wkv_ref,
                qng_ref, kng_ref, wpool_ref, pscale_ref, invf_ref, sgn_ref, qpad_ref, kpad_ref,
                ypool_ref, q_ref, k_ref, vt_ref, carry_ref):
    st = pl.program_id(1)

    @pl.when(st == 0)
    def _():
        carry_ref[...] = jnp.zeros_like(carry_ref)

    halo = carry_ref[...]
    for i in range(x_ref.shape[0] // PRE_SUB):
        halo = _pre_rows(i, st * x_ref.shape[0] + i * PRE_SUB, halo,
                         x_ref, pos_ref, g1_ref, win_ref, qag_ref, wq_ref, kvag_ref, wkv_ref,
                         qng_ref, kng_ref, wpool_ref, pscale_ref, invf_ref, sgn_ref, qpad_ref, kpad_ref,
                         ypool_ref, q_ref, k_ref, vt_ref)
    carry_ref[...] = halo


def _pre_rows(i, first_pos, halo, x_ref, pos_ref, g1_ref, win_ref, qag_ref, wq_ref, kvag_ref,
              wkv_ref, qng_ref, kng_ref, wpool_ref, pscale_ref, invf_ref, sgn_ref, qpad_ref, kpad_ref,
              ypool_ref, q_ref, k_ref, vt_ref):
    tm = PRE_SUB
    rs = pl.ds(i * tm, tm)

    hn = _rms(x_ref[rs, :], g1_ref[...])
    z = jnp.dot(hn.astype(BF16), win_ref[...], preferred_element_type=F32)
    k_rope = z[:, D_POOL + Q_LORA + KV_LORA:]

    u = z[:, :D_POOL]
    ext = jnp.concatenate([halo, u], axis=0)
    row = lax.broadcasted_iota(I32, (tm, 1), 0) + first_pos
    level = ext
    shift = 1
    for g, w in enumerate(POOL_WINDOWS):
        sl = slice(g * POOL_CH, (g + 1) * POOL_CH)
        while shift < w:
            level = level + pltpu.roll(level, shift, 0)
            shift *= 2
        win_sum = level[POOL_HALO:, sl]
        cnt = jnp.minimum(row + 1, w).astype(F32)
        d = win_sum / cnt - u[:, sl]
        y = jnp.dot(d.astype(BF16), wpool_ref[g], preferred_element_type=F32)
        ypool_ref[rs, sl] = (y * pscale_ref[:, sl]).astype(BF16)

    q_lat = z[:, D_POOL:D_POOL + Q_LORA]
    kv_lat = z[:, D_POOL + Q_LORA:D_POOL + Q_LORA + KV_LORA]
    qa = jnp.dot(_rms(q_lat, qag_ref[...]).astype(BF16), wq_ref[...],
                 preferred_element_type=F32)
    kv = jnp.dot(_rms(kv_lat, kvag_ref[...]).astype(BF16), wkv_ref[...],
                 preferred_element_type=F32)

    ang = pos_ref[rs, :].astype(F32) * invf_ref[...]
    cos = jnp.cos(ang)
    sin = jnp.sin(ang) * sgn_ref[...]

    def rot(t):
        return t * cos + pltpu.roll(t, LANES // 2, 1) * sin

    qng = qng_ref[...]
    kng = kng_ref[...]
    kr_rot = rot(k_rope * kng[:, LANES:])
    kr_ssq = jnp.sum(k_rope * k_rope, axis=-1, keepdims=True)
    for h in range(N_HEADS):
        qh = qa[:, h * HEAD_W:(h + 1) * HEAD_W]
        rq = lax.rsqrt(jnp.sum(qh * qh, axis=-1, keepdims=True) / QK_DIM + EPS) * Q_PRESCALE
        qn = qh * rq * qng
        q_ref[0, h, rs, :LANES] = qn[:, :LANES].astype(BF16)
        q_ref[0, h, rs, LANES:] = (rot(qn[:, LANES:]) + qpad_ref[...]).astype(BF16)
        kh = kv[:, h * QK_NOPE:(h + 1) * QK_NOPE]
        rk = lax.rsqrt((jnp.sum(kh * kh, axis=-1, keepdims=True) + kr_ssq) / QK_DIM + EPS)
        k_ref[0, h, rs, :LANES] = (kh * rk * kng[:, :LANES]).astype(BF16)
        k_ref[0, h, rs, LANES:] = (kr_rot * rk + kpad_ref[...]).astype(BF16)
        vh = kv[:, N_HEADS * QK_NOPE + h * V_DIM:N_HEADS * QK_NOPE + (h + 1) * V_DIM]
        vt_ref[0, h, i, :V_DIM, :] = vh.T.astype(BF16)
        vt_ref[0, h, i, V_DIM:, :] = jnp.ones((V_ROWS - V_DIM, tm), BF16)
    return u[tm - POOL_HALO:, :]


def _win_kernel(wt_ref, o_ref):
    w = wt_ref[...].T
    n_lat = o_ref.shape[1] - LANES
    zeros = jnp.zeros((w.shape[0], ROPE_HALF), F32)
    rope = jnp.concatenate([w[:, n_lat:n_lat + ROPE_HALF], zeros,
                            w[:, n_lat + ROPE_HALF:], zeros], axis=1)
    o_ref[:, :n_lat] = w[:, :n_lat].astype(BF16)
    o_ref[:, n_lat:] = rope.astype(BF16)


def _win_call(w_in_t):
    n_in, D = w_in_t.shape
    n_lat = n_in - QK_ROPE
    tm = WIN_ROWS
    return pl.pallas_call(
        _win_kernel,
        grid=(D // tm,),
        in_specs=[pl.BlockSpec((n_in, tm), lambda i: (0, i))],
        out_specs=pl.BlockSpec((tm, n_lat + LANES), lambda i: (i, 0)),
        out_shape=jax.ShapeDtypeStruct((D, n_lat + LANES), BF16),
        compiler_params=pltpu.CompilerParams(
            dimension_semantics=("arbitrary",), vmem_limit_bytes=VMEM_LIMIT),
        name="win",
    )(w_in_t)


def _pre_call(x, pos, g1, win, qag, wq, kvag, wkv, qng, kng, wpool, pscale, invf, sgn,
              qpad, kpad):
    B, S, D = x.shape
    tm = TM_PRE
    grid = (B, S // tm)
    row_spec = lambda w: pl.BlockSpec((None, tm, w), lambda b, s: (b, s, 0))
    head_spec = lambda w: pl.BlockSpec((1, N_HEADS, tm, w), lambda b, s: (b, 0, s, 0))
    consts = [g1, win, qag, wq, kvag, wkv, qng, kng, wpool, pscale, invf, sgn, qpad, kpad]
    return pl.pallas_call(
        _pre_kernel,
        grid=grid,
        in_specs=[row_spec(D), row_spec(1)] + [_const_spec(c.shape) for c in consts],
        out_specs=[row_spec(D_POOL), head_spec(HEAD_W), head_spec(HEAD_W),
                   pl.BlockSpec((1, N_HEADS, tm // PRE_SUB, V_ROWS, PRE_SUB),
                                lambda b, s: (b, 0, s, 0, 0))],
        out_shape=[jax.ShapeDtypeStruct((B, S, D_POOL), BF16),
                   jax.ShapeDtypeStruct((B, N_HEADS, S, HEAD_W), BF16),
                   jax.ShapeDtypeStruct((B, N_HEADS, S, HEAD_W), BF16),
                   jax.ShapeDtypeStruct((B, N_HEADS, S // PRE_SUB, V_ROWS, PRE_SUB), BF16)],
        scratch_shapes=[pltpu.VMEM((POOL_HALO, D_POOL), F32)],
        compiler_params=pltpu.CompilerParams(
            dimension_semantics=("arbitrary", "arbitrary"), vmem_limit_bytes=VMEM_LIMIT),
        name="pre",
    )(x, pos, *consts)


def _attn_kernel(bounded_ref, q_ref, k_ref, vt_ref, o_ref, *chain_scratch):
    S = q_ref.shape[2]
    nq = S // TQ
    per_q = TQ // TK
    assert per_q == 2, "two online-softmax chains take the even / odd key tiles"
    chains = (chain_scratch[0:2], chain_scratch[2:4])
    score_bufs = (chain_scratch[4:6], chain_scratch[6:8])
    key_idx = lax.broadcasted_iota(I32, (TK, TQ), 0)
    qry_idx = lax.broadcasted_iota(I32, (TK, TQ), 1)

    diag_masks = (key_idx <= qry_idx, key_idx + TK <= qry_idx)

    def weighted_values(kt, p):
        return jnp.dot(vt_ref[0, 0, kt], p.astype(BF16), preferred_element_type=F32)

    def write_out(qi, acc):
        out = acc[:V_DIM, :] / acc[V_DIM:V_DIM + 1, :]
        start = qi * TQ if isinstance(qi, int) else pl.multiple_of(qi * TQ, TQ)
        o_ref[0, pl.ds(start, TQ), :] = out.T.astype(BF16)

    def bounded_head():
        steps = [(qi, j) for qi in range(nq) for j in range(qi + 1)]

        def pair_scores(step, buf):
            qi, j = step
            q = q_ref[0, 0, qi * TQ:(qi + 1) * TQ, :]
            for c in range(per_q):
                kt = per_q * j + c
                buf[c][...] = lax.dot_general(k_ref[0, 0, kt * TK:(kt + 1) * TK, :], q,
                                              (((1,), (1,)), ((), ())),
                                              preferred_element_type=F32)

        pair_scores(steps[0], score_bufs[0])
        acc = None
        for t, (qi, j) in enumerate(steps):
            if t + 1 < len(steps):
                pair_scores(steps[t + 1], score_bufs[(t + 1) % 2])
            for c in range(per_q):
                st = score_bufs[t % 2][c][...]
                if j == qi:
                    st = jnp.where(diag_masks[c], st, NEG_BIG)
                pv = weighted_values(per_q * j + c, jnp.exp2(st))
                acc = pv if acc is None else acc + pv
            if j == qi:
                write_out(qi, acc)
                acc = None

    def online_q_tile(qi, _):
        q = q_ref[0, 0, pl.ds(pl.multiple_of(qi * TQ, TQ), TQ), :]
        for m_ref, acc_ref in chains:
            m_ref[...] = jnp.full(m_ref.shape, NEG_BIG, F32)
            acc_ref[...] = jnp.zeros(acc_ref.shape, F32)

        def scores(kt):
            k = k_ref[0, 0, pl.ds(pl.multiple_of(kt * TK, TK), TK), :]
            return lax.dot_general(k, q, (((1,), (1,)), ((), ())), preferred_element_type=F32)

        def fold(chain, st, kt, mask):
            m_ref, acc_ref = chain
            if mask is not None:
                st = jnp.where(mask, st, NEG_BIG)
            m = m_ref[...]
            m_new = jnp.maximum(m, jnp.max(st, axis=0, keepdims=True))
            m_ref[...] = m_new
            acc_ref[...] = (jnp.exp2(m - m_new) * acc_ref[...]
                            + weighted_values(kt, jnp.exp2(st - m_new)))

        def pair_scores(j, buf):
            for c in range(per_q):
                buf[c][...] = scores(per_q * j + c)

        def fold_pair(j, buf, masks=(None, None)):
            for c in range(per_q):
                fold(chains[c], buf[c][...], per_q * j + c, masks[c])

        pair_scores(0, score_bufs[0])

        def two_pairs(i, _):
            pair_scores(2 * i + 1, score_bufs[1])
            fold_pair(2 * i, score_bufs[0])
            pair_scores(2 * i + 2, score_bufs[0])
            fold_pair(2 * i + 1, score_bufs[1])
            return 0
        lax.fori_loop(0, qi // 2, two_pairs, 0)

        @pl.when(qi % 2 == 0)
        def _():
            fold_pair(qi, score_bufs[0], diag_masks)

        @pl.when(qi % 2 == 1)
        def _():
            pair_scores(qi, score_bufs[1])
            fold_pair(qi - 1, score_bufs[0])
            fold_pair(qi, score_bufs[1], diag_masks)

        (m0, acc0), (m1, acc1) = chains
        m = jnp.maximum(m0[...], m1[...])
        write_out(qi, jnp.exp2(m0[...] - m) * acc0[...] + jnp.exp2(m1[...] - m) * acc1[...])
        return 0

    pl.when(bounded_ref[0] != 0)(bounded_head)

    @pl.when(bounded_ref[0] == 0)
    def _():
        lax.fori_loop(0, nq, online_q_tile, 0)


def _attn_call(bounded, q, k, vt):
    B, H, S, _ = q.shape
    head = lambda w: pl.BlockSpec((1, 1, S, w), lambda b, h, flag: (b, h, 0, 0))
    grid_spec = pltpu.PrefetchScalarGridSpec(
        num_scalar_prefetch=1,
        grid=(B, H),
        in_specs=[head(HEAD_W), head(HEAD_W),
                  pl.BlockSpec((1, 1) + vt.shape[2:], lambda b, h, flag: (b, h, 0, 0, 0))],
        out_specs=pl.BlockSpec((1, S, V_DIM), lambda b, h, flag: (b, 0, h)),
        scratch_shapes=[pltpu.VMEM((1, TQ), F32), pltpu.VMEM((V_ROWS, TQ), F32)] * 2
        + [pltpu.VMEM((TK, TQ), F32)] * 4,
    )
    return pl.pallas_call(
        _attn_kernel,
        grid_spec=grid_spec,
        out_shape=jax.ShapeDtypeStruct((B, S, H * V_DIM), BF16),
        compiler_params=pltpu.CompilerParams(
            dimension_semantics=("arbitrary", "arbitrary"), vmem_limit_bytes=VMEM_LIMIT),
        name="attn",
    )(bounded, q, k, vt)


def _post_kernel(x_ref, yp_ref, ya_ref, wo_ref, g2_ref, wr_ref, br_ref,
                 h1_ref, xn_ref, lg_ref):
    for i in range(x_ref.shape[0] // POST_SUB):
        rs = pl.ds(i * POST_SUB, POST_SUB)
        h1 = (x_ref[rs, :]
              + jnp.dot(yp_ref[rs, :], wo_ref[:D_POOL, :], preferred_element_type=F32)
              + jnp.dot(ya_ref[rs, :], wo_ref[D_POOL:, :], preferred_element_type=F32))
        h1_ref[rs, :] = h1
        xn = _rms(h1, g2_ref[...])
        _store_packed_rows(xn_ref, i * POST_SUB, xn)
        xn_hi = xn.astype(BF16)
        xn_lo = (xn - xn_hi.astype(F32)).astype(BF16)
        both = (jnp.dot(xn_hi, wr_ref[...], preferred_element_type=F32)
                + jnp.dot(xn_lo, wr_ref[...], preferred_element_type=F32))
        lg_ref[rs, :] = both[:, :LANES] + both[:, LANES:] + br_ref[...]


def _post_call(x, yp, ya, wo, g2, wr, br):
    T, D = x.shape
    tm = TM_POST
    row = lambda w: pl.BlockSpec((tm, w), lambda i: (i, 0))
    return pl.pallas_call(
        _post_kernel,
        grid=(T // tm,),
        in_specs=[row(D), row(D_POOL), row(D_ATTN), _const_spec(wo.shape),
                  _const_spec(g2.shape), _const_spec(wr.shape), _const_spec(br.shape)],
        out_specs=[row(D), pl.BlockSpec((tm * PACK_SUBLANES, LANES), lambda i: (i, 0)),
                   row(LANES)],
        out_shape=[jax.ShapeDtypeStruct((T, D), F32),
                   jax.ShapeDtypeStruct((T * PACK_SUBLANES, LANES), jnp.uint32),
                   jax.ShapeDtypeStruct((T, LANES), F32)],
        compiler_params=pltpu.CompilerParams(
            dimension_semantics=("arbitrary",), vmem_limit_bytes=VMEM_LIMIT),
        name="post",
    )(x, yp, ya, wo, g2, wr, br)


def _route_kernel(lg_ref, code_ref, wts_ref, cnt_ref, carry_ref):
    @pl.when(pl.program_id(0) == 0)
    def _():
        carry_ref[...] = jnp.zeros_like(carry_ref)

    for i in range(lg_ref.shape[0] // ROUTE_SUB):
        rs = pl.ds(i * ROUTE_SUB, ROUTE_SUB)
        code_t, wts_t = _route_cols(lg_ref[rs, :].T, carry_ref)
        code_ref[:, rs] = code_t
        wts_cols = jnp.concatenate(
            [wts_t, jnp.zeros((LANES - PACK_SUBLANES, ROUTE_SUB), F32)], axis=0)
        wts_ref[rs, :] = wts_cols.T
    cnt_ref[...] = carry_ref[...].astype(I32)


def _route_call(logits):
    T = logits.shape[0]
    tm = TM_ROUTE
    row = pl.BlockSpec((tm, LANES), lambda i: (i, 0))
    return pl.pallas_call(
        _route_kernel,
        grid=(T // tm,),
        in_specs=[row],
        out_specs=[pl.BlockSpec((PACK_SUBLANES, tm), lambda i: (0, i)), row,
                   pl.BlockSpec((LANES, 1), lambda i: (0, 0))],
        out_shape=[jax.ShapeDtypeStruct((PACK_SUBLANES, T), I32),
                   jax.ShapeDtypeStruct((T, LANES), F32),
                   jax.ShapeDtypeStruct((LANES, 1), I32)],
        scratch_shapes=[pltpu.VMEM((LANES, 1), F32)],
        compiler_params=pltpu.CompilerParams(
            dimension_semantics=("arbitrary",), vmem_limit_bytes=VMEM_LIMIT),
        name="route",
    )(logits)


def _route_cols(lt, carry_ref):
    tm = lt.shape[1]
    sub = lax.broadcasted_iota(I32, (PACK_SUBLANES, tm), 0).astype(F32)

    def first_argmax(vals):
        mx = jnp.max(vals, axis=0, keepdims=True)
        idx = jnp.min(jnp.where(vals == mx, sub, float(PACK_SUBLANES)), axis=0, keepdims=True)
        return mx, idx

    g_logits = lt[0:N_GROUPS, :]
    g_max, g_sel = first_argmax(g_logits)
    g_w = 1.0 / jnp.sum(jnp.exp(g_logits - g_max), axis=0, keepdims=True)

    e_logits = lt[N_GROUPS:N_GROUPS + EXPERTS_PER_GROUP, :]
    for g in range(1, N_GROUPS):
        lo = N_GROUPS + g * EXPERTS_PER_GROUP
        e_logits = jnp.where(g_sel == float(g), lt[lo:lo + EXPERTS_PER_GROUP, :], e_logits)
    v1, i1 = first_argmax(e_logits)
    v2, i2 = first_argmax(jnp.where(sub == i1, -jnp.inf, e_logits))
    e1 = (g_sel * EXPERTS_PER_GROUP + i1).astype(I32)
    e2 = (g_sel * EXPERTS_PER_GROUP + i2).astype(I32)
    t = jnp.exp(v2 - v1)
    w1 = g_w / (1.0 + t)
    w2 = g_w * t / (1.0 + t)

    expert_row = lax.broadcasted_iota(I32, (LANES, tm), 0)
    hit1 = expert_row == e1
    hit2 = expert_row == e2
    onehot = jnp.where(hit1 | hit2, 1.0, 0.0)
    r = lax.broadcasted_iota(I32, (tm, tm), 0)
    c = lax.broadcasted_iota(I32, (tm, tm), 1)
    earlier = jnp.where(r < c, 1.0, 0.0).astype(BF16)
    before = (jnp.dot(onehot.astype(BF16), earlier, preferred_element_type=F32)
              + carry_ref[...])
    carry_ref[...] += jnp.sum(onehot, axis=1, keepdims=True)
    pos1 = jnp.sum(jnp.where(hit1, before, 0.0), axis=0, keepdims=True).astype(I32)
    pos2 = jnp.sum(jnp.where(hit2, before, 0.0), axis=0, keepdims=True).astype(I32)
    code1 = e1 * 65536 + pos1
    code2 = e2 * 65536 + pos2
    first, second = sub == 0.0, sub == 1.0
    return (jnp.where(first, code1, jnp.where(second, code2, 0)),
            jnp.where(first, w1, jnp.where(second, w2, 0.0)))


def _plan_kernel(code_ref, cnt_ref, unused_hbm, slots_hbm, blke_ref, blkn_ref, blkw_ref,
                 rune_ref, slot_ref, start_ref, sem):
    n_assign = code_ref.shape[0]
    n_blocks = blke_ref.shape[0]
    load = pltpu.make_async_copy(unused_hbm, slot_ref, sem)
    load.start()

    def per_expert(e, nb_done):
        cnt = cnt_ref[e]
        nb = lax.shift_right_logical(cnt + (MOE_BLK - 1), MOE_BLK.bit_length() - 1)
        start_ref[e] = nb_done * MOE_BLK

        def per_block(j, _):
            blke_ref[nb_done + j] = e
            blkn_ref[nb_done + j] = jnp.minimum(cnt - j * MOE_BLK, MOE_BLK)
            return 0
        lax.fori_loop(0, nb, per_block, 0)
        return nb_done + nb
    used = lax.fori_loop(0, N_EXPERTS, per_expert, 0)

    last_e = blke_ref[jnp.maximum(used - 1, 0)]

    def tail(b, _):
        blke_ref[b] = last_e
        blkn_ref[b] = 0
        blkw_ref[b] = 0
        return 0
    lax.fori_loop(used, n_blocks, tail, 0)

    def no_run(r, _):
        rune_ref[r] = -1
        return 0
    lax.fori_loop(0, rune_ref.shape[0], no_run, 0)

    def runs_forward(b, carry):
        earlier_e, run = carry
        e = blke_ref[b]
        first = (e != earlier_e).astype(I32)
        run = run + first
        blkw_ref[b] = first + run * 2
        rune_ref[run] = e
        return e, run
    lax.fori_loop(0, used, runs_forward, (-1, -1))

    load.wait()

    def place(a, _):
        code = code_ref[a]
        e = lax.shift_right_logical(code, 16)
        slot_ref[start_ref[e] + (code & 0xFFFF)] = a
        return 0
    lax.fori_loop(0, n_assign, place, 0, unroll=PLAN_UNROLL)

    store = pltpu.make_async_copy(slot_ref, slots_hbm, sem)
    store.start()
    store.wait()


def _plan_call(code_flat, counts, n_tab):
    n_assign = code_flat.shape[0]
    n_slots = n_tab * MOE_BLK
    unused = (n_assign + (np.arange(n_slots) & (MOE_BLK - 1))).astype(np.int32)
    smem = pl.BlockSpec(memory_space=pltpu.SMEM)
    hbm = pl.BlockSpec(memory_space=pl.ANY)
    return pl.pallas_call(
        _plan_kernel,
        in_specs=[smem, smem, hbm],
        out_specs=[hbm, smem, smem, smem, smem],
        out_shape=[jax.ShapeDtypeStruct((n_slots,), I32),
                   jax.ShapeDtypeStruct((n_tab,), I32),
                   jax.ShapeDtypeStruct((n_tab,), I32),
                   jax.ShapeDtypeStruct((n_tab,), I32),
                   jax.ShapeDtypeStruct((N_EXPERTS + WEIGHT_SLOTS,), I32)],
        scratch_shapes=[pltpu.SMEM((n_slots,), I32), pltpu.SMEM((N_EXPERTS,), I32),
                        pltpu.SemaphoreType.DMA],
        name="plan",
    )(code_flat, counts, unused)


def _expert_kernel(blke_ref, blkn_ref, blkw_ref, rune_ref, slot_g_ref, slot_s_ref, xn_hbm,
                   wg_hbm, wu_hbm, wd_hbm, y_hbm, x0, x1, y0, y1, wg_buf, wu_buf, wd_buf,
                   gsem, ssem, wsem):
    xbufs, ybufs = (x0, x1), (y0, y1)
    s = pl.program_id(0)
    n_tok = xn_hbm.shape[0] // PACK_SUBLANES

    def weight_copies(e, slot):
        return [pltpu.make_async_copy(hbm.at[e], buf.at[slot], wsem.at[slot])
                for hbm, buf in ((wg_hbm, wg_buf), (wu_hbm, wu_buf), (wd_hbm, wd_buf))]

    blk = jnp.maximum(s - 1, 0)
    run_info = blkw_ref[blk]
    run = run_info >> 1
    w_slot = lax.rem(run, WEIGHT_SLOTS)
    ahead_run = run + (WEIGHT_SLOTS - 1)
    ahead_expert = rune_ref[ahead_run]
    ahead_slot = lax.rem(ahead_run, WEIGHT_SLOTS)

    def tile(ref, r):
        start = r * PACK_SUBLANES
        if not isinstance(r, int):
            start = pl.multiple_of(start, PACK_SUBLANES)
        return ref.at[pl.ds(start, PACK_SUBLANES)]

    def rows(j):
        n = blkn_ref[jnp.clip(j, 0, blkn_ref.shape[0] - 1)]
        return jnp.where(j >= 0, (n + (ROW_GROUP - 1)) & ~(ROW_GROUP - 1), 0)
    rows0, rows1, rows2, rows3 = rows(s), rows(s - 1), rows(s - 2), rows(s - 3)

    def stage(nxt):
        cur = 1 - nxt
        x_nxt, x_cur, y_nxt, y_cur = xbufs[nxt], xbufs[cur], ybufs[nxt], ybufs[cur]

        def per_group(n_rows, issue_row):
            for g in range(MOE_BLK // ROW_GROUP):
                @pl.when(g * ROW_GROUP < n_rows)
                def _():
                    for r in range(g * ROW_GROUP, (g + 1) * ROW_GROUP):
                        issue_row(r)

        def gather_row(r):
            tok = slot_g_ref[0, 0, r] & (n_tok - 1)
            pltpu.make_async_copy(tile(xn_hbm, tok), tile(x_nxt, r), gsem.at[nxt]).start()

        def scatter_row(r):
            pltpu.make_async_copy(tile(y_nxt, r), tile(y_hbm, slot_s_ref[0, 0, r]),
                                  ssem.at[nxt]).start()

        def wait_rows(src, dst, sem, n_rows):
            n_words = pl.multiple_of(n_rows * PACK_SUBLANES, ROW_GROUP * PACK_SUBLANES)

            @pl.when(n_rows > 0)
            def _():
                pltpu.make_async_copy(src.at[pl.ds(0, n_words)], dst.at[pl.ds(0, n_words)],
                                      sem).wait()

        def compute():
            xb = _load_packed_rows(x_cur, 0, MOE_BLK).astype(BF16)
            a = jnp.dot(xb, wg_buf[w_slot].astype(BF16), preferred_element_type=F32)
            u = jnp.dot(xb, wu_buf[w_slot].astype(BF16), preferred_element_type=F32)
            hmid = (a * jax.nn.sigmoid(a) * u).astype(BF16)
            _store_packed_rows(y_cur, 0, jnp.dot(hmid, wd_buf[w_slot].astype(BF16),
                                                 preferred_element_type=F32))

        wait_rows(xn_hbm, x_cur, gsem.at[cur], rows1)
        wait_rows(y_cur, y_hbm, ssem.at[cur], rows3)

        @pl.when((rows1 > 0) & ((run_info & 1) == 1))
        def _():
            for copy in weight_copies(blke_ref[blk], w_slot):
                copy.wait()

            @pl.when(ahead_expert >= 0)
            def _():
                for copy in weight_copies(ahead_expert, ahead_slot):
                    copy.start(priority=WEIGHT_DMA_PRIORITY)

        if nxt == 0:
            for first_run in range(WEIGHT_SLOTS - 1):
                @pl.when((s == 0) & (rune_ref[first_run] >= 0))
                def _():
                    for copy in weight_copies(rune_ref[first_run], first_run):
                        copy.start(priority=WEIGHT_DMA_PRIORITY)

            @pl.when(s == 0)
            def _():
                x_nxt[...] = jnp.zeros(x_nxt.shape, x_nxt.dtype)
                x_cur[...] = jnp.zeros(x_cur.shape, x_cur.dtype)
                y_nxt[...] = jnp.zeros(y_nxt.shape, y_nxt.dtype)
                dump = pltpu.make_async_copy(
                    y_nxt, y_hbm.at[pl.ds(TOP_K * n_tok * PACK_SUBLANES,
                                          MOE_BLK * PACK_SUBLANES)], ssem.at[nxt])
                dump.start()
                dump.wait()

        per_group(rows0, gather_row)
        per_group(rows2, scatter_row)

        @pl.when(rows1 > 0)
        def _():
            compute()

    for parity in range(2):
        pl.when((s & 1) == parity)(functools.partial(stage, parity))


def _expert_call(blk_e, blk_n, blk_w, run_e, slots, xn, wg, wu, wd):
    T = xn.shape[0] // PACK_SUBLANES
    assert T & (T - 1) == 0, "token id is recovered from the assignment id with a mask"
    n_tab = blk_e.shape[0]
    _, D, F = wg.shape
    packed_block = pltpu.VMEM((MOE_BLK * PACK_SUBLANES, LANES), jnp.uint32)
    hbm = pl.BlockSpec(memory_space=pl.ANY)
    grid_spec = pltpu.PrefetchScalarGridSpec(
        num_scalar_prefetch=4,
        grid=(n_tab,),
        in_specs=[
            pl.BlockSpec((1, 1, MOE_BLK), lambda s, *_: (s, 0, 0), memory_space=pltpu.SMEM),
            pl.BlockSpec((1, 1, MOE_BLK), lambda s, *_: (jnp.maximum(s - 2, 0), 0, 0),
                         memory_space=pltpu.SMEM),
            hbm, hbm, hbm, hbm,
        ],
        out_specs=hbm,
        scratch_shapes=[packed_block] * 4
        + [pltpu.VMEM((WEIGHT_SLOTS, D, F), F32), pltpu.VMEM((WEIGHT_SLOTS, D, F), F32),
           pltpu.VMEM((WEIGHT_SLOTS, F, D), F32)]
        + [pltpu.SemaphoreType.DMA((2,)), pltpu.SemaphoreType.DMA((2,)),
           pltpu.SemaphoreType.DMA((WEIGHT_SLOTS,))],
    )
    slots3 = slots.reshape(n_tab, 1, MOE_BLK)
    return pl.pallas_call(
        _expert_kernel,
        grid_spec=grid_spec,
        out_shape=jax.ShapeDtypeStruct(((T * TOP_K + MOE_BLK) * PACK_SUBLANES, LANES),
                                       jnp.uint32),
        compiler_params=pltpu.CompilerParams(
            dimension_semantics=("arbitrary",), vmem_limit_bytes=VMEM_LIMIT,
            disable_bounds_checks=True),
        name="experts",
    )(blk_e, blk_n, blk_w, run_e, slots3, slots3, xn, wg, wu, wd)


def _final_kernel(h1_ref, y1_ref, y2_ref, wts_ref, p_ref, wpp_ref, pg_ref, g3_ref, wpg_ref,
                  o_ref):
    for i in range(h1_ref.shape[0] // FINAL_SUB):
        rs = pl.ds(i * FINAL_SUB, FINAL_SUB)
        w = wts_ref[rs, :]
        h2 = (h1_ref[rs, :]
              + w[:, 0:1] * _load_packed_rows(y1_ref, i * FINAL_SUB, FINAL_SUB)
              + w[:, 1:2] * _load_packed_rows(y2_ref, i * FINAL_SUB, FINAL_SUB))
        e = _rms(jnp.dot(p_ref[rs, :].astype(BF16), wpp_ref[...], preferred_element_type=F32),
                 pg_ref[...])
        gate = jax.nn.sigmoid(jnp.dot(_rms(h2, g3_ref[...]).astype(BF16), wpg_ref[...],
                                      preferred_element_type=F32))
        o_ref[rs, :] = h2 + gate * e


def _final_call(h1, y, wts, p, wpp, pg, g3, wpg):
    T, D = h1.shape
    tm = TM_FINAL
    row = lambda w: pl.BlockSpec((tm, w), lambda i: (i, 0))
    first = pl.BlockSpec((tm * PACK_SUBLANES, LANES), lambda i: (i, 0))
    second = pl.BlockSpec((tm * PACK_SUBLANES, LANES), lambda i: (i + T // tm, 0))
    return pl.pallas_call(
        _final_kernel,
        grid=(T // tm,),
        in_specs=[row(D), first, second, row(LANES), row(PLE_DIM), _const_spec(wpp.shape),
                  _const_spec(pg.shape), _const_spec(g3.shape), _const_spec(wpg.shape)],
        out_specs=row(D),
        out_shape=jax.ShapeDtypeStruct((T, D), F32),
        compiler_params=pltpu.CompilerParams(
            dimension_semantics=("arbitrary",), vmem_limit_bytes=VMEM_LIMIT),
        name="final",
    )(h1, y, y, wts, p, wpp, pg, g3, wpg)


def _rope_lanes(a):
    z = jnp.zeros(a.shape[:-1] + (ROPE_HALF,), a.dtype)
    return jnp.concatenate([a[..., :ROPE_HALF], z, a[..., ROPE_HALF:], z], axis=-1)


def _head_lanes(a):
    return jnp.concatenate([a[..., :QK_NOPE], _rope_lanes(a[..., QK_NOPE:])], axis=-1)


def _score_bound(q_gain, k_gain):
    return Q_PRESCALE * QK_DIM * jnp.max(jnp.abs(q_gain)) * jnp.max(jnp.abs(k_gain))


def kernel(x, p, positions, norm1_gain, w_in, q_a_gain, w_q_b, kv_a_gain, w_kv_b, q_norm_gain, k_norm_gain, w_pool, pool_scale, w_out, norm2_gain, w_router_group, b_router_group, w_router_expert, b_router_expert, w_exp_gate, w_exp_up, w_exp_down, norm3_gain, w_ple_gate, w_ple_proj, ple_norm_gain):
    B, S, D = x.shape
    T = B * S
    assert x.shape[2] == D_MODEL and S % TQ == 0 and S % TM_PRE == 0
    assert T % TM_POST == 0 and T % TM_FINAL == 0
    layer = 0
    row = lambda a: a[layer].reshape(1, -1)

    win = _win_call(w_in[layer].T)
    wq = _head_lanes(w_q_b[layer].reshape(Q_LORA, N_HEADS, QK_DIM)
                     ).reshape(Q_LORA, N_HEADS * HEAD_W).astype(BF16)
    wkv3 = w_kv_b[layer].reshape(KV_LORA, N_HEADS, QK_NOPE + V_DIM)
    wkv = jnp.concatenate([wkv3[..., :QK_NOPE].reshape(KV_LORA, -1),
                           wkv3[..., QK_NOPE:].reshape(KV_LORA, -1)], axis=1).astype(BF16)
    qng = _head_lanes(q_norm_gain[layer]).reshape(1, HEAD_W)
    kng = _head_lanes(k_norm_gain[layer]).reshape(1, HEAD_W)
    inv_freq = np.float32(ROPE_THETA) ** (-np.arange(ROPE_HALF, dtype=np.float32) / ROPE_HALF)
    zeros_half = np.zeros((ROPE_HALF,), np.float32)
    ones_half = np.ones((ROPE_HALF,), np.float32)
    invf = np.concatenate([inv_freq, zeros_half, inv_freq, zeros_half]).reshape(1, LANES)
    sgn = np.concatenate([-ones_half, zeros_half, ones_half, zeros_half]).reshape(1, LANES)
    spare = np.arange(LANES) == SPARE_ROPE_LANE
    kpad = spare.astype(np.float32).reshape(1, LANES)
    pad_lanes = LANES - N_GROUPS - N_EXPERTS
    wr = jnp.concatenate([w_router_group[layer], w_router_expert[layer],
                          jnp.zeros((D, pad_lanes), F32)], axis=1)
    wr_hi = wr.astype(BF16)
    wr = jnp.concatenate([wr_hi, (wr - wr_hi.astype(F32)).astype(BF16)], axis=1)
    br = jnp.concatenate([b_router_group[layer], b_router_expert[layer],
                          jnp.zeros((pad_lanes,), F32)]).reshape(1, LANES)

    bound = _score_bound(q_norm_gain[layer], k_norm_gain[layer])
    bounded = bound <= SCORE_BOUND_LIMIT
    qpad = jnp.where(bounded, -bound, 0.0) * kpad
    ypool, q, k, vt = _pre_call(
        x, positions.reshape(B, S, 1), row(norm1_gain), win, row(q_a_gain), wq,
        row(kv_a_gain), wkv, qng, kng, w_pool[layer].astype(BF16), row(pool_scale), invf, sgn,
        qpad, kpad)
    yattn = _attn_call(bounded.astype(I32).reshape(1), q, k, vt)
    h1, xn2, logits = _post_call(
        x.reshape(T, D), ypool.reshape(T, D_POOL), yattn.reshape(T, D_ATTN),
        w_out[layer].astype(BF16), row(norm2_gain), wr, br)
    code, wts, counts = _route_call(logits)

    n_assign = T * TOP_K
    n_blocks = (n_assign + N_EXPERTS * (MOE_BLK - 1)) // MOE_BLK
    slots, blk_e, blk_n, blk_w, run_e = _plan_call(
        code[:TOP_K].reshape(n_assign), counts.reshape(LANES), n_blocks + MOE_DRAIN_STEPS)
    y = _expert_call(blk_e, blk_n, blk_w, run_e, slots, xn2,
                     w_exp_gate[layer], w_exp_up[layer], w_exp_down[layer])

    out = _final_call(h1, y, wts, p[layer].reshape(T, PLE_DIM),
                      w_ple_proj[layer].astype(BF16), row(ple_norm_gain), row(norm3_gain),
                      w_ple_gate[layer].astype(BF16))
    return out.reshape(B, S, D)
```

```python
import functools
import math

import jax
import jax.numpy as jnp
import numpy as np
from jax import lax
from jax.experimental import pallas as pl
from jax.experimental.pallas import tpu as pltpu

F32 = jnp.float32
BF16 = jnp.bfloat16
I32 = jnp.int32

D_MODEL = 2048
PLE_DIM = 256
EPS = 1e-6
POOL_WINDOWS = (2, 4, 8, 16)
POOL_CH = 256
D_POOL = POOL_CH * len(POOL_WINDOWS)
N_HEADS = 8
Q_LORA = 512
KV_LORA = 512
QK_NOPE = 128
QK_ROPE = 64
QK_DIM = QK_NOPE + QK_ROPE
V_DIM = 128
D_ATTN = N_HEADS * V_DIM
ROPE_THETA = 10000.0
ATTN_SCALE = 1.0 / math.sqrt(QK_DIM)
Q_PRESCALE = ATTN_SCALE * math.log2(math.e)
N_GROUPS = 8
EXPERTS_PER_GROUP = 8
N_EXPERTS = N_GROUPS * EXPERTS_PER_GROUP
TOP_K = 2

LANES = 128
PACK_SUBLANES = 8
VMEM_LIMIT = 56 * 1024 * 1024

HEAD_W = 2 * LANES
ROPE_HALF = QK_ROPE // 2
POOL_HALO = 16
TM_PRE = 512
PRE_SUB = 256
TM_POST = 512
POST_SUB = 256
TM_FINAL = 512
FINAL_SUB = 256
WIN_ROWS = 256
CAST_ROWS = 512
TQ = 512
TK = PRE_SUB
TM_ROUTE = 1024
ROUTE_SUB = 256
MOE_BLK = 256
MOE_DRAIN_STEPS = 3
ROW_GROUP = 32
WEIGHT_DMA_PRIORITY = 1
WEIGHT_SLOTS = 3
PLAN_UNROLL = 16
NEG_BIG = -1e30
V_ROWS = V_DIM + 16
SPARE_ROPE_LANE = ROPE_HALF
SCORE_BOUND_LIMIT = 50.0


def _const_spec(shape):
    nd = len(shape)
    return pl.BlockSpec(shape, lambda *_: (0,) * nd, pipeline_mode=pl.Buffered(1))


def _rms(x, gain):
    return x * lax.rsqrt(jnp.mean(x * x, axis=-1, keepdims=True) + EPS) * gain


def _load_weight_as_bf16(w_hbm, w_bf16, landing, sem):
    n_chunks = w_hbm.shape[0] // CAST_ROWS

    def chunk(c):
        return pltpu.make_async_copy(w_hbm.at[pl.ds(c * CAST_ROWS, CAST_ROWS)],
                                     landing.at[c % 2], sem.at[c % 2])

    @pl.when(pl.program_id(0) == 0)
    def _():
        chunk(0).start()
        for c in range(n_chunks):
            if c + 1 < n_chunks:
                chunk(c + 1).start()
            chunk(c).wait()
            w_bf16[pl.ds(c * CAST_ROWS, CAST_ROWS), :] = landing[c % 2].astype(BF16)


def _bf16_weight_scratch(shape):
    rows, cols = shape
    assert rows % CAST_ROWS == 0
    return [pltpu.VMEM((rows, cols), BF16), pltpu.VMEM((2, CAST_ROWS, cols), F32),
            pltpu.SemaphoreType.DMA((2,))]


def _store_packed_rows(ref, first_row, x):
    rows, d = x.shape
    half = d // 2
    assert half == PACK_SUBLANES * LANES
    for i in range(PACK_SUBLANES):
        lo = x[:, i * LANES:(i + 1) * LANES].astype(BF16).astype(F32)
        hi = x[:, half + i * LANES:half + (i + 1) * LANES].astype(BF16).astype(F32)
        word = (lax.shift_right_logical(pltpu.bitcast(lo, jnp.uint32), jnp.uint32(16))
                | (pltpu.bitcast(hi, jnp.uint32) & jnp.uint32(0xFFFF0000)))
        ref[pl.ds(first_row * PACK_SUBLANES + i, rows, stride=PACK_SUBLANES), :] = word


def _load_packed_rows(ref, first_row, rows):
    lo, hi = [], []
    for i in range(PACK_SUBLANES):
        word = ref[pl.ds(first_row * PACK_SUBLANES + i, rows, stride=PACK_SUBLANES), :]
        lo.append(pltpu.bitcast(lax.shift_left(word, jnp.uint32(16)), F32))
        hi.append(pltpu.bitcast(word & jnp.uint32(0xFFFF0000), F32))
    return jnp.concatenate(lo + hi, axis=1)


def _pre_kernel(x_ref, pos_ref, g1_ref, win_ref, qag_ref, wq_ref, kvag_ref, wkv_ref,
                qng_ref, kng_ref, wpool_ref, pscale_ref, invf_ref, sgn_ref, qpad_ref, kpad_ref,
                ypool_ref, q_ref, k_ref, vt_ref, carry_ref):
    st = pl.program_id(1)

    @pl.when(st == 0)
    def _():
        carry_ref[...] = jnp.zeros_like(carry_ref)

    halo = carry_ref[...]
    for i in range(x_ref.shape[0] // PRE_SUB):
        halo = _pre_rows(i, st * x_ref.shape[0] + i * PRE_SUB, halo,
                         x_ref, pos_ref, g1_ref, win_ref, qag_ref, wq_ref, kvag_ref, wkv_ref,
                         qng_ref, kng_ref, wpool_ref, pscale_ref, invf_ref, sgn_ref, qpad_ref, kpad_ref,
                         ypool_ref, q_ref, k_ref, vt_ref)
    carry_ref[...] = halo


def _pre_rows(i, first_pos, halo, x_ref, pos_ref, g1_ref, win_ref, qag_ref, wq_ref, kvag_ref,
              wkv_ref, qng_ref, kng_ref, wpool_ref, pscale_ref, invf_ref, sgn_ref, qpad_ref, kpad_ref,
              ypool_ref, q_ref, k_ref, vt_ref):
    tm = PRE_SUB
    rs = pl.ds(i * tm, tm)

    hn = _rms(x_ref[rs, :], g1_ref[...])
    z = jnp.dot(hn.astype(BF16), win_ref[...], preferred_element_type=F32)
    k_rope = z[:, D_POOL + Q_LORA + KV_LORA:]

    u = z[:, :D_POOL]
    ext = jnp.concatenate([halo, u], axis=0)
    row = lax.broadcasted_iota(I32, (tm, 1), 0) + first_pos
    level = ext
    shift = 1
    for g, w in enumerate(POOL_WINDOWS):
        sl = slice(g * POOL_CH, (g + 1) * POOL_CH)
        while shift < w:
            level = level + pltpu.roll(level, shift, 0)
            shift *= 2
        win_sum = level[POOL_HALO:, sl]
        cnt = jnp.minimum(row + 1, w).astype(F32)
        d = win_sum / cnt - u[:, sl]
        y = jnp.dot(d.astype(BF16), wpool_ref[g], preferred_element_type=F32)
        ypool_ref[rs, sl] = (y * pscale_ref[:, sl]).astype(BF16)

    q_lat = z[:, D_POOL:D_POOL + Q_LORA]
    kv_lat = z[:, D_POOL + Q_LORA:D_POOL + Q_LORA + KV_LORA]
    qa = jnp.dot(_rms(q_lat, qag_ref[...]).astype(BF16), wq_ref[...],
                 preferred_element_type=F32)
    kv = jnp.dot(_rms(kv_lat, kvag_ref[...]).astype(BF16), wkv_ref[...],
                 preferred_element_type=F32)

    ang = pos_ref[rs, :].astype(F32) * invf_ref[...]
    cos = jnp.cos(ang)
    sin = jnp.sin(ang) * sgn_ref[...]

    def rot(t):
        return t * cos + pltpu.roll(t, LANES // 2, 1) * sin

    qng = qng_ref[...]
    kng = kng_ref[...]
    kr_rot = rot(k_rope * kng[:, LANES:])
    kr_ssq = jnp.sum(k_rope * k_rope, axis=-1, keepdims=True)
    for h in range(N_HEADS):
        qh = qa[:, h * HEAD_W:(h + 1) * HEAD_W]
        rq = lax.rsqrt(jnp.sum(qh * qh, axis=-1, keepdims=True) / QK_DIM + EPS) * Q_PRESCALE
        qn = qh * rq * qng
        q_ref[0, h, rs, :LANES] = qn[:, :LANES].astype(BF16)
        q_ref[0, h, rs, LANES:] = (rot(qn[:, LANES:]) + qpad_ref[...]).astype(BF16)
        kh = kv[:, h * QK_NOPE:(h + 1) * QK_NOPE]
        rk = lax.rsqrt((jnp.sum(kh * kh, axis=-1, keepdims=True) + kr_ssq) / QK_DIM + EPS)
        k_ref[0, h, rs, :LANES] = (kh * rk * kng[:, :LANES]).astype(BF16)
        k_ref[0, h, rs, LANES:] = (kr_rot * rk + kpad_ref[...]).astype(BF16)
        vh = kv[:, N_HEADS * QK_NOPE + h * V_DIM:N_HEADS * QK_NOPE + (h + 1) * V_DIM]
        vt_ref[0, h, i, :V_DIM, :] = vh.T.astype(BF16)
        vt_ref[0, h, i, V_DIM:, :] = jnp.ones((V_ROWS - V_DIM, tm), BF16)
    return u[tm - POOL_HALO:, :]


def _win_kernel(wt_ref, o_ref):
    w = wt_ref[...].T
    n_lat = o_ref.shape[1] - LANES
    zeros = jnp.zeros((w.shape[0], ROPE_HALF), F32)
    rope = jnp.concatenate([w[:, n_lat:n_lat + ROPE_HALF], zeros,
                            w[:, n_lat + ROPE_HALF:], zeros], axis=1)
    o_ref[:, :n_lat] = w[:, :n_lat].astype(BF16)
    o_ref[:, n_lat:] = rope.astype(BF16)


def _win_call(w_in_t):
    n_in, D = w_in_t.shape
    n_lat = n_in - QK_ROPE
    tm = WIN_ROWS
    return pl.pallas_call(
        _win_kernel,
        grid=(D // tm,),
        in_specs=[pl.BlockSpec((n_in, tm), lambda i: (0, i))],
        out_specs=pl.BlockSpec((tm, n_lat + LANES), lambda i: (i, 0)),
        out_shape=jax.ShapeDtypeStruct((D, n_lat + LANES), BF16),
        compiler_params=pltpu.CompilerParams(
            dimension_semantics=("arbitrary",), vmem_limit_bytes=VMEM_LIMIT),
        name="win",
    )(w_in_t)


def _pre_call(x, pos, g1, win, qag, wq, kvag, wkv, qng, kng, wpool, pscale, invf, sgn,
              qpad, kpad):
    B, S, D = x.shape
    tm = TM_PRE
    grid = (B, S // tm)
    row_spec = lambda w: pl.BlockSpec((None, tm, w), lambda b, s: (b, s, 0))
    head_spec = lambda w: pl.BlockSpec((1, N_HEADS, tm, w), lambda b, s: (b, 0, s, 0))
    consts = [g1, win, qag, wq, kvag, wkv, qng, kng, wpool, pscale, invf, sgn, qpad, kpad]
    return pl.pallas_call(
        _pre_kernel,
        grid=grid,
        in_specs=[row_spec(D), row_spec(1)] + [_const_spec(c.shape) for c in consts],
        out_specs=[row_spec(D_POOL), head_spec(HEAD_W), head_spec(HEAD_W),
                   pl.BlockSpec((1, N_HEADS, tm // PRE_SUB, V_ROWS, PRE_SUB),
                                lambda b, s: (b, 0, s, 0, 0))],
        out_shape=[jax.ShapeDtypeStruct((B, S, D_POOL), BF16),
                   jax.ShapeDtypeStruct((B, N_HEADS, S, HEAD_W), BF16),
                   jax.ShapeDtypeStruct((B, N_HEADS, S, HEAD_W), BF16),
                   jax.ShapeDtypeStruct((B, N_HEADS, S // PRE_SUB, V_ROWS, PRE_SUB), BF16)],
        scratch_shapes=[pltpu.VMEM((POOL_HALO, D_POOL), F32)],
        compiler_params=pltpu.CompilerParams(
            dimension_semantics=("arbitrary", "arbitrary"), vmem_limit_bytes=VMEM_LIMIT),
        name="pre",
    )(x, pos, *consts)


def _attn_kernel(bounded_ref, q_ref, k_ref, vt_ref, o_ref, *chain_scratch):
    S = q_ref.shape[2]
    nq = S // TQ
    per_q = TQ // TK
    assert per_q == 2, "two online-softmax chains take the even / odd key tiles"
    chains = (chain_scratch[0:2], chain_scratch[2:4])
    score_bufs = (chain_scratch[4:6], chain_scratch[6:8])
    key_idx = lax.broadcasted_iota(I32, (TK, TQ), 0)
    qry_idx = lax.broadcasted_iota(I32, (TK, TQ), 1)

    diag_masks = (key_idx <= qry_idx, key_idx + TK <= qry_idx)

    def weighted_values(kt, p):
        return jnp.dot(vt_ref[0, 0, kt], p.astype(BF16), preferred_element_type=F32)

    def write_out(qi, acc):
        out = acc[:V_DIM, :] / acc[V_DIM:V_DIM + 1, :]
        start = qi * TQ if isinstance(qi, int) else pl.multiple_of(qi * TQ, TQ)
        o_ref[0, pl.ds(start, TQ), :] = out.T.astype(BF16)

    def bounded_head():
        steps = [(qi, j) for qi in range(nq) for j in range(qi + 1)]

        def pair_scores(step, buf):
            qi, j = step
            q = q_ref[0, 0, qi * TQ:(qi + 1) * TQ, :]
            for c in range(per_q):
                kt = per_q * j + c
                buf[c][...] = lax.dot_general(k_ref[0, 0, kt * TK:(kt + 1) * TK, :], q,
                                              (((1,), (1,)), ((), ())),
                                              preferred_element_type=F32)

        pair_scores(steps[0], score_bufs[0])
        acc = None
        for t, (qi, j) in enumerate(steps):
            if t + 1 < len(steps):
                pair_scores(steps[t + 1], score_bufs[(t + 1) % 2])
            for c in range(per_q):
                st = score_bufs[t % 2][c][...]
                if j == qi:
                    st = jnp.where(diag_masks[c], st, NEG_BIG)
                pv = weighted_values(per_q * j + c, jnp.exp2(st))
                acc = pv if acc is None else acc + pv
            if j == qi:
                write_out(qi, acc)
                acc = None

    def online_q_tile(qi, _):
        q = q_ref[0, 0, pl.ds(pl.multiple_of(qi * TQ, TQ), TQ), :]
        for m_ref, acc_ref in chains:
            m_ref[...] = jnp.full(m_ref.shape, NEG_BIG, F32)
            acc_ref[...] = jnp.zeros(acc_ref.shape, F32)

        def scores(kt):
            k = k_ref[0, 0, pl.ds(pl.multiple_of(kt * TK, TK), TK), :]
            return lax.dot_general(k, q, (((1,), (1,)), ((), ())), preferred_element_type=F32)

        def fold(chain, st, kt, mask):
            m_ref, acc_ref = chain
            if mask is not None:
                st = jnp.where(mask, st, NEG_BIG)
            m = m_ref[...]
            m_new = jnp.maximum(m, jnp.max(st, axis=0, keepdims=True))
            m_ref[...] = m_new
            acc_ref[...] = (jnp.exp2(m - m_new) * acc_ref[...]
                            + weighted_values(kt, jnp.exp2(st - m_new)))

        def pair_scores(j, buf):
            for c in range(per_q):
                buf[c][...] = scores(per_q * j + c)

        def fold_pair(j, buf, masks=(None, None)):
            for c in range(per_q):
                fold(chains[c], buf[c][...], per_q * j + c, masks[c])

        pair_scores(0, score_bufs[0])

        def two_pairs(i, _):
            pair_scores(2 * i + 1, score_bufs[1])
            fold_pair(2 * i, score_bufs[0])
            pair_scores(2 * i + 2, score_bufs[0])
            fold_pair(2 * i + 1, score_bufs[1])
            return 0
        lax.fori_loop(0, qi // 2, two_pairs, 0)

        @pl.when(qi % 2 == 0)
        def _():
            fold_pair(qi, score_bufs[0], diag_masks)

        @pl.when(qi % 2 == 1)
        def _():
            pair_scores(qi, score_bufs[1])
            fold_pair(qi - 1, score_bufs[0])
            fold_pair(qi, score_bufs[1], diag_masks)

        (m0, acc0), (m1, acc1) = chains
        m = jnp.maximum(m0[...], m1[...])
        write_out(qi, jnp.exp2(m0[...] - m) * acc0[...] + jnp.exp2(m1[...] - m) * acc1[...])
        return 0

    pl.when(bounded_ref[0] != 0)(bounded_head)

    @pl.when(bounded_ref[0] == 0)
    def _():
        lax.fori_loop(0, nq, online_q_tile, 0)


def _attn_call(bounded, q, k, vt):
    B, H, S, _ = q.shape
    head = lambda w: pl.BlockSpec((1, 1, S, w), lambda b, h, flag: (b, h, 0, 0))
    grid_spec = pltpu.PrefetchScalarGridSpec(
        num_scalar_prefetch=1,
        grid=(B, H),
        in_specs=[head(HEAD_W), head(HEAD_W),
                  pl.BlockSpec((1, 1) + vt.shape[2:], lambda b, h, flag: (b, h, 0, 0, 0))],
        out_specs=pl.BlockSpec((1, S, V_DIM), lambda b, h, flag: (b, 0, h)),
        scratch_shapes=[pltpu.VMEM((1, TQ), F32), pltpu.VMEM((V_ROWS, TQ), F32)] * 2
        + [pltpu.VMEM((TK, TQ), F32)] * 4,
    )
    return pl.pallas_call(
        _attn_kernel,
        grid_spec=grid_spec,
        out_shape=jax.ShapeDtypeStruct((B, S, H * V_DIM), BF16),
        compiler_params=pltpu.CompilerParams(
            dimension_semantics=("arbitrary", "arbitrary"), vmem_limit_bytes=VMEM_LIMIT),
        name="attn",
    )(bounded, q, k, vt)


def _post_kernel(x_ref, yp_ref, ya_ref, wo_hbm, g2_ref, wr_ref, br_ref,
                 h1_ref, xn_ref, lg_ref, wo_ref, landing, sem):
    _load_weight_as_bf16(wo_hbm, wo_ref, landing, sem)
    for i in range(x_ref.shape[0] // POST_SUB):
        rs = pl.ds(i * POST_SUB, POST_SUB)
        h1 = (x_ref[rs, :]
              + jnp.dot(yp_ref[rs, :], wo_ref[:D_POOL, :], preferred_element_type=F32)
              + jnp.dot(ya_ref[rs, :], wo_ref[D_POOL:, :], preferred_element_type=F32))
        h1_ref[rs, :] = h1
        xn = _rms(h1, g2_ref[...])
        _store_packed_rows(xn_ref, i * POST_SUB, xn)
        xn_hi = xn.astype(BF16)
        xn_lo = (xn - xn_hi.astype(F32)).astype(BF16)
        both = (jnp.dot(xn_hi, wr_ref[...], preferred_element_type=F32)
                + jnp.dot(xn_lo, wr_ref[...], preferred_element_type=F32))
        lg_ref[rs, :] = both[:, :LANES] + both[:, LANES:] + br_ref[...]


def _post_call(x, yp, ya, wo, g2, wr, br):
    T, D = x.shape
    tm = TM_POST
    row = lambda w: pl.BlockSpec((tm, w), lambda i: (i, 0))
    return pl.pallas_call(
        _post_kernel,
        grid=(T // tm,),
        in_specs=[row(D), row(D_POOL), row(D_ATTN), pl.BlockSpec(memory_space=pl.ANY),
                  _const_spec(g2.shape), _const_spec(wr.shape), _const_spec(br.shape)],
        out_specs=[row(D), pl.BlockSpec((tm * PACK_SUBLANES, LANES), lambda i: (i, 0)),
                   row(LANES)],
        out_shape=[jax.ShapeDtypeStruct((T, D), F32),
                   jax.ShapeDtypeStruct((T * PACK_SUBLANES, LANES), jnp.uint32),
                   jax.ShapeDtypeStruct((T, LANES), F32)],
        scratch_shapes=_bf16_weight_scratch(wo.shape),
        compiler_params=pltpu.CompilerParams(
            dimension_semantics=("arbitrary",), vmem_limit_bytes=VMEM_LIMIT),
        name="post",
    )(x, yp, ya, wo, g2, wr, br)


def _route_kernel(lg_ref, code_ref, wts_ref, cnt_ref, carry_ref):
    @pl.when(pl.program_id(0) == 0)
    def _():
        carry_ref[...] = jnp.zeros_like(carry_ref)

    for i in range(lg_ref.shape[0] // ROUTE_SUB):
        rs = pl.ds(i * ROUTE_SUB, ROUTE_SUB)
        code_t, wts_t = _route_cols(lg_ref[rs, :].T, carry_ref)
        code_ref[:, rs] = code_t
        wts_cols = jnp.concatenate(
            [wts_t, jnp.zeros((LANES - PACK_SUBLANES, ROUTE_SUB), F32)], axis=0)
        wts_ref[rs, :] = wts_cols.T
    cnt_ref[...] = carry_ref[...].astype(I32)


def _route_call(logits):
    T = logits.shape[0]
    tm = TM_ROUTE
    row = pl.BlockSpec((tm, LANES), lambda i: (i, 0))
    return pl.pallas_call(
        _route_kernel,
        grid=(T // tm,),
        in_specs=[row],
        out_specs=[pl.BlockSpec((PACK_SUBLANES, tm), lambda i: (0, i)), row,
                   pl.BlockSpec((LANES, 1), lambda i: (0, 0))],
        out_shape=[jax.ShapeDtypeStruct((PACK_SUBLANES, T), I32),
                   jax.ShapeDtypeStruct((T, LANES), F32),
                   jax.ShapeDtypeStruct((LANES, 1), I32)],
        scratch_shapes=[pltpu.VMEM((LANES, 1), F32)],
        compiler_params=pltpu.CompilerParams(
            dimension_semantics=("arbitrary",), vmem_limit_bytes=VMEM_LIMIT),
        name="route",
    )(logits)


def _route_cols(lt, carry_ref):
    tm = lt.shape[1]
    sub = lax.broadcasted_iota(I32, (PACK_SUBLANES, tm), 0).astype(F32)

    def first_argmax(vals):
        mx = jnp.max(vals, axis=0, keepdims=True)
        idx = jnp.min(jnp.where(vals == mx, sub, float(PACK_SUBLANES)), axis=0, keepdims=True)
        return mx, idx

    g_logits = lt[0:N_GROUPS, :]
    g_max, g_sel = first_argmax(g_logits)
    g_w = 1.0 / jnp.sum(jnp.exp(g_logits - g_max), axis=0, keepdims=True)

    e_logits = lt[N_GROUPS:N_GROUPS + EXPERTS_PER_GROUP, :]
    for g in range(1, N_GROUPS):
        lo = N_GROUPS + g * EXPERTS_PER_GROUP
        e_logits = jnp.where(g_sel == float(g), lt[lo:lo + EXPERTS_PER_GROUP, :], e_logits)
    v1, i1 = first_argmax(e_logits)
    v2, i2 = first_argmax(jnp.where(sub == i1, -jnp.inf, e_logits))
    e1 = (g_sel * EXPERTS_PER_GROUP + i1).astype(I32)
    e2 = (g_sel * EXPERTS_PER_GROUP + i2).astype(I32)
    t = jnp.exp(v2 - v1)
    w1 = g_w / (1.0 + t)
    w2 = g_w * t / (1.0 + t)

    expert_row = lax.broadcasted_iota(I32, (LANES, tm), 0)
    hit1 = expert_row == e1
    hit2 = expert_row == e2
    onehot = jnp.where(hit1 | hit2, 1.0, 0.0)
    r = lax.broadcasted_iota(I32, (tm, tm), 0)
    c = lax.broadcasted_iota(I32, (tm, tm), 1)
    earlier = jnp.where(r < c, 1.0, 0.0).astype(BF16)
    before = (jnp.dot(onehot.astype(BF16), earlier, preferred_element_type=F32)
              + carry_ref[...])
    carry_ref[...] += jnp.sum(onehot, axis=1, keepdims=True)
    pos1 = jnp.sum(jnp.where(hit1, before, 0.0), axis=0, keepdims=True).astype(I32)
    pos2 = jnp.sum(jnp.where(hit2, before, 0.0), axis=0, keepdims=True).astype(I32)
    code1 = e1 * 65536 + pos1
    code2 = e2 * 65536 + pos2
    first, second = sub == 0.0, sub == 1.0
    return (jnp.where(first, code1, jnp.where(second, code2, 0)),
            jnp.where(first, w1, jnp.where(second, w2, 0.0)))


def _plan_kernel(code_ref, cnt_ref, unused_hbm, slots_hbm, blke_ref, blkn_ref, blkw_ref,
                 rune_ref, slot_ref, start_ref, sem):
    n_assign = code_ref.shape[0]
    n_blocks = blke_ref.shape[0]
    load = pltpu.make_async_copy(unused_hbm, slot_ref, sem)
    load.start()

    def per_expert(e, nb_done):
        cnt = cnt_ref[e]
        nb = lax.shift_right_logical(cnt + (MOE_BLK - 1), MOE_BLK.bit_length() - 1)
        start_ref[e] = nb_done * MOE_BLK

        def per_block(j, _):
            blke_ref[nb_done + j] = e
            blkn_ref[nb_done + j] = jnp.minimum(cnt - j * MOE_BLK, MOE_BLK)
            return 0
        lax.fori_loop(0, nb, per_block, 0)
        return nb_done + nb
    used = lax.fori_loop(0, N_EXPERTS, per_expert, 0)

    last_e = blke_ref[jnp.maximum(used - 1, 0)]

    def tail(b, _):
        blke_ref[b] = last_e
        blkn_ref[b] = 0
        blkw_ref[b] = 0
        return 0
    lax.fori_loop(used, n_blocks, tail, 0)

    def no_run(r, _):
        rune_ref[r] = -1
        return 0
    lax.fori_loop(0, rune_ref.shape[0], no_run, 0)

    def runs_forward(b, carry):
        earlier_e, run = carry
        e = blke_ref[b]
        first = (e != earlier_e).astype(I32)
        run = run + first
        blkw_ref[b] = first + run * 2
        rune_ref[run] = e
        return e, run
    lax.fori_loop(0, used, runs_forward, (-1, -1))

    load.wait()

    def place(a, _):
        code = code_ref[a]
        e = lax.shift_right_logical(code, 16)
        slot_ref[start_ref[e] + (code & 0xFFFF)] = a
        return 0
    lax.fori_loop(0, n_assign, place, 0, unroll=PLAN_UNROLL)

    store = pltpu.make_async_copy(slot_ref, slots_hbm, sem)
    store.start()
    store.wait()


def _plan_call(code_flat, counts, n_tab):
    n_assign = code_flat.shape[0]
    n_slots = n_tab * MOE_BLK
    unused = (n_assign + (np.arange(n_slots) & (MOE_BLK - 1))).astype(np.int32)
    smem = pl.BlockSpec(memory_space=pltpu.SMEM)
    hbm = pl.BlockSpec(memory_space=pl.ANY)
    return pl.pallas_call(
        _plan_kernel,
        in_specs=[smem, smem, hbm],
        out_specs=[hbm, smem, smem, smem, smem],
        out_shape=[jax.ShapeDtypeStruct((n_slots,), I32),
                   jax.ShapeDtypeStruct((n_tab,), I32),
                   jax.ShapeDtypeStruct((n_tab,), I32),
                   jax.ShapeDtypeStruct((n_tab,), I32),
                   jax.ShapeDtypeStruct((N_EXPERTS + WEIGHT_SLOTS,), I32)],
        scratch_shapes=[pltpu.SMEM((n_slots,), I32), pltpu.SMEM((N_EXPERTS,), I32),
                        pltpu.SemaphoreType.DMA],
        name="plan",
    )(code_flat, counts, unused)


def _expert_kernel(blke_ref, blkn_ref, blkw_ref, rune_ref, slot_g_ref, slot_s_ref, xn_hbm,
                   wg_hbm, wu_hbm, wd_hbm, y_hbm, x0, x1, y0, y1, wg_buf, wu_buf, wd_buf,
                   gsem, ssem, wsem):
    xbufs, ybufs = (x0, x1), (y0, y1)
    s = pl.program_id(0)
    n_tok = xn_hbm.shape[0] // PACK_SUBLANES

    def weight_copies(e, slot):
        return [pltpu.make_async_copy(hbm.at[e], buf.at[slot], wsem.at[slot])
                for hbm, buf in ((wg_hbm, wg_buf), (wu_hbm, wu_buf), (wd_hbm, wd_buf))]

    blk = jnp.maximum(s - 1, 0)
    run_info = blkw_ref[blk]
    run = run_info >> 1
    w_slot = lax.rem(run, WEIGHT_SLOTS)
    ahead_run = run + (WEIGHT_SLOTS - 1)
    ahead_expert = rune_ref[ahead_run]
    ahead_slot = lax.rem(ahead_run, WEIGHT_SLOTS)

    def tile(ref, r):
        start = r * PACK_SUBLANES
        if not isinstance(r, int):
            start = pl.multiple_of(start, PACK_SUBLANES)
        return ref.at[pl.ds(start, PACK_SUBLANES)]

    def rows(j):
        n = blkn_ref[jnp.clip(j, 0, blkn_ref.shape[0] - 1)]
        return jnp.where(j >= 0, (n + (ROW_GROUP - 1)) & ~(ROW_GROUP - 1), 0)
    rows0, rows1, rows2, rows3 = rows(s), rows(s - 1), rows(s - 2), rows(s - 3)

    def stage(nxt):
        cur = 1 - nxt
        x_nxt, x_cur, y_nxt, y_cur = xbufs[nxt], xbufs[cur], ybufs[nxt], ybufs[cur]

        def per_group(n_rows, issue_row):
            for g in range(MOE_BLK // ROW_GROUP):
                @pl.when(g * ROW_GROUP < n_rows)
                def _():
                    for r in range(g * ROW_GROUP, (g + 1) * ROW_GROUP):
                        issue_row(r)

        def gather_row(r):
            tok = slot_g_ref[0, 0, r] & (n_tok - 1)
            pltpu.make_async_copy(tile(xn_hbm, tok), tile(x_nxt, r), gsem.at[nxt]).start()

        def scatter_row(r):
            pltpu.make_async_copy(tile(y_nxt, r), tile(y_hbm, slot_s_ref[0, 0, r]),
                                  ssem.at[nxt]).start()

        def wait_rows(src, dst, sem, n_rows):
            n_words = pl.multiple_of(n_rows * PACK_SUBLANES, ROW_GROUP * PACK_SUBLANES)

            @pl.when(n_rows > 0)
            def _():
                pltpu.make_async_copy(src.at[pl.ds(0, n_words)], dst.at[pl.ds(0, n_words)],
                                      sem).wait()

        def compute():
            xb = _load_packed_rows(x_cur, 0, MOE_BLK).astype(BF16)
            a = jnp.dot(xb, wg_buf[w_slot].astype(BF16), preferred_element_type=F32)
            u = jnp.dot(xb, wu_buf[w_slot].astype(BF16), preferred_element_type=F32)
            hmid = (a * jax.nn.sigmoid(a) * u).astype(BF16)
            _store_packed_rows(y_cur, 0, jnp.dot(hmid, wd_buf[w_slot].astype(BF16),
                                                 preferred_element_type=F32))

        wait_rows(xn_hbm, x_cur, gsem.at[cur], rows1)
        wait_rows(y_cur, y_hbm, ssem.at[cur], rows3)

        @pl.when((rows1 > 0) & ((run_info & 1) == 1))
        def _():
            for copy in weight_copies(blke_ref[blk], w_slot):
                copy.wait()

            @pl.when(ahead_expert >= 0)
            def _():
                for copy in weight_copies(ahead_expert, ahead_slot):
                    copy.start(priority=WEIGHT_DMA_PRIORITY)

        if nxt == 0:
            for first_run in range(WEIGHT_SLOTS - 1):
                @pl.when((s == 0) & (rune_ref[first_run] >= 0))
                def _():
                    for copy in weight_copies(rune_ref[first_run], first_run):
                        copy.start(priority=WEIGHT_DMA_PRIORITY)

            @pl.when(s == 0)
            def _():
                x_nxt[...] = jnp.zeros(x_nxt.shape, x_nxt.dtype)
                x_cur[...] = jnp.zeros(x_cur.shape, x_cur.dtype)
                y_nxt[...] = jnp.zeros(y_nxt.shape, y_nxt.dtype)
                dump = pltpu.make_async_copy(
                    y_nxt, y_hbm.at[pl.ds(TOP_K * n_tok * PACK_SUBLANES,
                                          MOE_BLK * PACK_SUBLANES)], ssem.at[nxt])
                dump.start()
                dump.wait()

        per_group(rows0, gather_row)
        per_group(rows2, scatter_row)

        @pl.when(rows1 > 0)
        def _():
            compute()

    for parity in range(2):
        pl.when((s & 1) == parity)(functools.partial(stage, parity))


def _expert_call(blk_e, blk_n, blk_w, run_e, slots, xn, wg, wu, wd):
    T = xn.shape[0] // PACK_SUBLANES
    assert T & (T - 1) == 0, "token id is recovered from the assignment id with a mask"
    n_tab = blk_e.shape[0]
    _, D, F = wg.shape
    packed_block = pltpu.VMEM((MOE_BLK * PACK_SUBLANES, LANES), jnp.uint32)
    hbm = pl.BlockSpec(memory_space=pl.ANY)
    grid_spec = pltpu.PrefetchScalarGridSpec(
        num_scalar_prefetch=4,
        grid=(n_tab,),
        in_specs=[
            pl.BlockSpec((1, 1, MOE_BLK), lambda s, *_: (s, 0, 0), memory_space=pltpu.SMEM),
            pl.BlockSpec((1, 1, MOE_BLK), lambda s, *_: (jnp.maximum(s - 2, 0), 0, 0),
                         memory_space=pltpu.SMEM),
            hbm, hbm, hbm, hbm,
        ],
        out_specs=hbm,
        scratch_shapes=[packed_block] * 4
        + [pltpu.VMEM((WEIGHT_SLOTS, D, F), F32), pltpu.VMEM((WEIGHT_SLOTS, D, F), F32),
           pltpu.VMEM((WEIGHT_SLOTS, F, D), F32)]
        + [pltpu.SemaphoreType.DMA((2,)), pltpu.SemaphoreType.DMA((2,)),
           pltpu.SemaphoreType.DMA((WEIGHT_SLOTS,))],
    )
    slots3 = slots.reshape(n_tab, 1, MOE_BLK)
    return pl.pallas_call(
        _expert_kernel,
        grid_spec=grid_spec,
        out_shape=jax.ShapeDtypeStruct(((T * TOP_K + MOE_BLK) * PACK_SUBLANES, LANES),
                                       jnp.uint32),
        compiler_params=pltpu.CompilerParams(
            dimension_semantics=("arbitrary",), vmem_limit_bytes=VMEM_LIMIT,
            disable_bounds_checks=True),
        name="experts",
    )(blk_e, blk_n, blk_w, run_e, slots3, slots3, xn, wg, wu, wd)


def _final_kernel(h1_ref, y1_ref, y2_ref, wts_ref, p_ref, wpp_ref, pg_ref, g3_ref, wpg_hbm,
                  o_ref, wpg_ref, landing, sem):
    _load_weight_as_bf16(wpg_hbm, wpg_ref, landing, sem)
    for i in range(h1_ref.shape[0] // FINAL_SUB):
        rs = pl.ds(i * FINAL_SUB, FINAL_SUB)
        w = wts_ref[rs, :]
        h2 = (h1_ref[rs, :]
              + w[:, 0:1] * _load_packed_rows(y1_ref, i * FINAL_SUB, FINAL_SUB)
              + w[:, 1:2] * _load_packed_rows(y2_ref, i * FINAL_SUB, FINAL_SUB))
        e = _rms(jnp.dot(p_ref[rs, :].astype(BF16), wpp_ref[...], preferred_element_type=F32),
                 pg_ref[...])
        gate = jax.nn.sigmoid(jnp.dot(_rms(h2, g3_ref[...]).astype(BF16), wpg_ref[...],
                                      preferred_element_type=F32))
        o_ref[rs, :] = h2 + gate * e


def _final_call(h1, y, wts, p, wpp, pg, g3, wpg):
    T, D = h1.shape
    tm = TM_FINAL
    row = lambda w: pl.BlockSpec((tm, w), lambda i: (i, 0))
    first = pl.BlockSpec((tm * PACK_SUBLANES, LANES), lambda i: (i, 0))
    second = pl.BlockSpec((tm * PACK_SUBLANES, LANES), lambda i: (i + T // tm, 0))
    return pl.pallas_call(
        _final_kernel,
        grid=(T // tm,),
        in_specs=[row(D), first, second, row(LANES), row(PLE_DIM), _const_spec(wpp.shape),
                  _const_spec(pg.shape), _const_spec(g3.shape),
                  pl.BlockSpec(memory_space=pl.ANY)],
        out_specs=row(D),
        out_shape=jax.ShapeDtypeStruct((T, D), F32),
        scratch_shapes=_bf16_weight_scratch(wpg.shape),
        compiler_params=pltpu.CompilerParams(
            dimension_semantics=("arbitrary",), vmem_limit_bytes=VMEM_LIMIT),
        name="final",
    )(h1, y, y, wts, p, wpp, pg, g3, wpg)


def _rope_lanes(a):
    z = jnp.zeros(a.shape[:-1] + (ROPE_HALF,), a.dtype)
    return jnp.concatenate([a[..., :ROPE_HALF], z, a[..., ROPE_HALF:], z], axis=-1)


def _head_lanes(a):
    return jnp.concatenate([a[..., :QK_NOPE], _rope_lanes(a[..., QK_NOPE:])], axis=-1)


def _score_bound(q_gain, k_gain):
    return Q_PRESCALE * QK_DIM * jnp.max(jnp.abs(q_gain)) * jnp.max(jnp.abs(k_gain))


def kernel(x, p, positions, norm1_gain, w_in, q_a_gain, w_q_b, kv_a_gain, w_kv_b, q_norm_gain, k_norm_gain, w_pool, pool_scale, w_out, norm2_gain, w_router_group, b_router_group, w_router_expert, b_router_expert, w_exp_gate, w_exp_up, w_exp_down, norm3_gain, w_ple_gate, w_ple_proj, ple_norm_gain):
    B, S, D = x.shape
    T = B * S
    assert x.shape[2] == D_MODEL and S % TQ == 0 and S % TM_PRE == 0
    assert T % TM_POST == 0 and T % TM_FINAL == 0
    layer = 0
    row = lambda a: a[layer].reshape(1, -1)

    win = _win_call(w_in[layer].T)
    wq = _head_lanes(w_q_b[layer].reshape(Q_LORA, N_HEADS, QK_DIM)
                     ).reshape(Q_LORA, N_HEADS * HEAD_W).astype(BF16)
    wkv3 = w_kv_b[layer].reshape(KV_LORA, N_HEADS, QK_NOPE + V_DIM)
    wkv = jnp.concatenate([wkv3[..., :QK_NOPE].reshape(KV_LORA, -1),
                           wkv3[..., QK_NOPE:].reshape(KV_LORA, -1)], axis=1).astype(BF16)
    qng = _head_lanes(q_norm_gain[layer]).reshape(1, HEAD_W)
    kng = _head_lanes(k_norm_gain[layer]).reshape(1, HEAD_W)
    inv_freq = np.float32(ROPE_THETA) ** (-np.arange(ROPE_HALF, dtype=np.float32) / ROPE_HALF)
    zeros_half = np.zeros((ROPE_HALF,), np.float32)
    ones_half = np.ones((ROPE_HALF,), np.float32)
    invf = np.concatenate([inv_freq, zeros_half, inv_freq, zeros_half]).reshape(1, LANES)
    sgn = np.concatenate([-ones_half, zeros_half, ones_half, zeros_half]).reshape(1, LANES)
    spare = np.arange(LANES) == SPARE_ROPE_LANE
    kpad = spare.astype(np.float32).reshape(1, LANES)
    pad_lanes = LANES - N_GROUPS - N_EXPERTS
    wr = jnp.concatenate([w_router_group[layer], w_router_expert[layer],
                          jnp.zeros((D, pad_lanes), F32)], axis=1)
    wr_hi = wr.astype(BF16)
    wr = jnp.concatenate([wr_hi, (wr - wr_hi.astype(F32)).astype(BF16)], axis=1)
    br = jnp.concatenate([b_router_group[layer], b_router_expert[layer],
                          jnp.zeros((pad_lanes,), F32)]).reshape(1, LANES)

    bound = _score_bound(q_norm_gain[layer], k_norm_gain[layer])
    bounded = bound <= SCORE_BOUND_LIMIT
    qpad = jnp.where(bounded, -bound, 0.0) * kpad
    ypool, q, k, vt = _pre_call(
        x, positions.reshape(B, S, 1), row(norm1_gain), win, row(q_a_gain), wq,
        row(kv_a_gain), wkv, qng, kng, w_pool[layer].astype(BF16), row(pool_scale), invf, sgn,
        qpad, kpad)
    yattn = _attn_call(bounded.astype(I32).reshape(1), q, k, vt)
    h1, xn2, logits = _post_call(
        x.reshape(T, D), ypool.reshape(T, D_POOL), yattn.reshape(T, D_ATTN),
        w_out[layer], row(norm2_gain), wr, br)
    code, wts, counts = _route_call(logits)

    n_assign = T * TOP_K
    n_blocks = (n_assign + N_EXPERTS * (MOE_BLK - 1)) // MOE_BLK
    slots, blk_e, blk_n, blk_w, run_e = _plan_call(
        code[:TOP_K].reshape(n_assign), counts.reshape(LANES), n_blocks + MOE_DRAIN_STEPS)
    y = _expert_call(blk_e, blk_n, blk_w, run_e, slots, xn2,
                     w_exp_gate[layer], w_exp_up[layer], w_exp_down[layer])

    out = _final_call(h1, y, wts, p[layer].reshape(T, PLE_DIM),
                      w_ple_proj[layer].astype(BF16), row(ple_norm_gain), row(norm3_gain),
                      w_ple_gate[layer])
    return out.reshape(B, S, D)
```

```python
import functools
import math

import jax
import jax.numpy as jnp
import numpy as np
from jax import lax
from jax.experimental import pallas as pl
from jax.experimental.pallas import tpu as pltpu

F32 = jnp.float32
BF16 = jnp.bfloat16
I32 = jnp.int32

D_MODEL = 2048
PLE_DIM = 256
EPS = 1e-6
POOL_WINDOWS = (2, 4, 8, 16)
POOL_CH = 256
D_POOL = POOL_CH * len(POOL_WINDOWS)
N_HEADS = 8
Q_LORA = 512
KV_LORA = 512
QK_NOPE = 128
QK_ROPE = 64
QK_DIM = QK_NOPE + QK_ROPE
V_DIM = 128
D_ATTN = N_HEADS * V_DIM
ROPE_THETA = 10000.0
ATTN_SCALE = 1.0 / math.sqrt(QK_DIM)
Q_PRESCALE = ATTN_SCALE * math.log2(math.e)
N_GROUPS = 8
EXPERTS_PER_GROUP = 8
N_EXPERTS = N_GROUPS * EXPERTS_PER_GROUP
TOP_K = 2

LANES = 128
PACK_SUBLANES = 8
VMEM_LIMIT = 56 * 1024 * 1024

HEAD_W = 2 * LANES
ROPE_HALF = QK_ROPE // 2
POOL_HALO = 16
TM_PRE = 512
PRE_SUB = 256
TM_POST = 512
POST_SUB = 256
TM_FINAL = 512
FINAL_SUB = 256
WIN_ROWS = 256
CAST_ROWS = 512
TQ = 512
TK = 256
TM_ROUTE = 1024
ROUTE_SUB = 256
MOE_BLK = 256
MOE_DRAIN_STEPS = 3
ROW_GROUP = 32
WEIGHT_DMA_PRIORITY = 1
WEIGHT_SLOTS = 3
PLAN_UNROLL = 16
NEG_BIG = -1e30
V_ROWS = V_DIM + 16
SPARE_ROPE_LANE = ROPE_HALF
SCORE_BOUND_LIMIT = 50.0


def _const_spec(shape):
    nd = len(shape)
    return pl.BlockSpec(shape, lambda *_: (0,) * nd, pipeline_mode=pl.Buffered(1))


def _rms(x, gain):
    return x * lax.rsqrt(jnp.mean(x * x, axis=-1, keepdims=True) + EPS) * gain


def _load_weight_as_bf16(w_hbm, w_bf16, landing, sem):
    n_chunks = w_hbm.shape[0] // CAST_ROWS

    def chunk(c):
        return pltpu.make_async_copy(w_hbm.at[pl.ds(c * CAST_ROWS, CAST_ROWS)],
                                     landing.at[c % 2], sem.at[c % 2])

    @pl.when(pl.program_id(0) == 0)
    def _():
        chunk(0).start()
        for c in range(n_chunks):
            if c + 1 < n_chunks:
                chunk(c + 1).start()
            chunk(c).wait()
            w_bf16[pl.ds(c * CAST_ROWS, CAST_ROWS), :] = landing[c % 2].astype(BF16)


def _bf16_weight_scratch(shape):
    rows, cols = shape
    assert rows % CAST_ROWS == 0
    return [pltpu.VMEM((rows, cols), BF16), pltpu.VMEM((2, CAST_ROWS, cols), F32),
            pltpu.SemaphoreType.DMA((2,))]


def _store_packed_rows(ref, first_row, x):
    rows, d = x.shape
    half = d // 2
    assert half == PACK_SUBLANES * LANES
    for i in range(PACK_SUBLANES):
        lo = x[:, i * LANES:(i + 1) * LANES].astype(BF16).astype(F32)
        hi = x[:, half + i * LANES:half + (i + 1) * LANES].astype(BF16).astype(F32)
        word = (lax.shift_right_logical(pltpu.bitcast(lo, jnp.uint32), jnp.uint32(16))
                | (pltpu.bitcast(hi, jnp.uint32) & jnp.uint32(0xFFFF0000)))
        ref[pl.ds(first_row * PACK_SUBLANES + i, rows, stride=PACK_SUBLANES), :] = word


def _load_packed_rows(ref, first_row, rows):
    lo, hi = [], []
    for i in range(PACK_SUBLANES):
        word = ref[pl.ds(first_row * PACK_SUBLANES + i, rows, stride=PACK_SUBLANES), :]
        lo.append(pltpu.bitcast(lax.shift_left(word, jnp.uint32(16)), F32))
        hi.append(pltpu.bitcast(word & jnp.uint32(0xFFFF0000), F32))
    return jnp.concatenate(lo + hi, axis=1)


def _pre_kernel(x_ref, pos_ref, g1_ref, win_ref, qag_ref, wq_ref, kvag_ref, wkv_ref,
                qng_ref, kng_ref, wpool_ref, pscale_ref, invf_ref, sgn_ref, qpad_ref, kpad_ref,
                ypool_ref, q_ref, k_ref, vt_ref, carry_ref):
    st = pl.program_id(1)
    tm = PRE_SUB
    n_sub = x_ref.shape[0] // tm

    @pl.when(st == 0)
    def _():
        carry_ref[...] = jnp.zeros_like(carry_ref)

    zs = [_pre_project(pl.ds(i * tm, tm), x_ref, g1_ref, win_ref) for i in range(n_sub)]
    halo = carry_ref[...]
    mixed = []
    for i in range(n_sub):
        first_pos = st * x_ref.shape[0] + i * tm
        qa, kv, halo = _pre_mix(pl.ds(i * tm, tm), first_pos, zs[i], halo, qag_ref, wq_ref,
                                kvag_ref, wkv_ref, wpool_ref, pscale_ref, ypool_ref)
        mixed.append((qa, kv))
    carry_ref[...] = halo
    for i in range(n_sub):
        k_rope = zs[i][:, D_POOL + Q_LORA + KV_LORA:]
        _pre_heads(i, *mixed[i], k_rope, pos_ref, qng_ref, kng_ref, invf_ref, sgn_ref,
                   qpad_ref, kpad_ref, q_ref, k_ref, vt_ref)


def _pre_project(rs, x_ref, g1_ref, win_ref):
    hn = _rms(x_ref[rs, :], g1_ref[...])
    return jnp.dot(hn.astype(BF16), win_ref[...], preferred_element_type=F32)


def _pre_mix(rs, first_pos, z, halo, qag_ref, wq_ref, kvag_ref, wkv_ref, wpool_ref, pscale_ref,
             ypool_ref):
    tm = z.shape[0]
    u = z[:, :D_POOL]
    ext = jnp.concatenate([halo, u], axis=0)
    row = lax.broadcasted_iota(I32, (tm, 1), 0) + first_pos
    level = ext
    shift = 1
    for g, w in enumerate(POOL_WINDOWS):
        sl = slice(g * POOL_CH, (g + 1) * POOL_CH)
        while shift < w:
            level = level + pltpu.roll(level, shift, 0)
            shift *= 2
        win_sum = level[POOL_HALO:, sl]
        cnt = jnp.minimum(row + 1, w).astype(F32)
        d = win_sum / cnt - u[:, sl]
        y = jnp.dot(d.astype(BF16), wpool_ref[g], preferred_element_type=F32)
        ypool_ref[rs, sl] = (y * pscale_ref[:, sl]).astype(BF16)

    q_lat = z[:, D_POOL:D_POOL + Q_LORA]
    kv_lat = z[:, D_POOL + Q_LORA:D_POOL + Q_LORA + KV_LORA]
    qa = jnp.dot(_rms(q_lat, qag_ref[...]).astype(BF16), wq_ref[...],
                 preferred_element_type=F32)
    kv = jnp.dot(_rms(kv_lat, kvag_ref[...]).astype(BF16), wkv_ref[...],
                 preferred_element_type=F32)
    return qa, kv, u[tm - POOL_HALO:, :]


def _pre_heads(i, qa, kv, k_rope, pos_ref, qng_ref, kng_ref, invf_ref, sgn_ref, qpad_ref,
               kpad_ref, q_ref, k_ref, vt_ref):
    tm = qa.shape[0]
    rs = pl.ds(i * tm, tm)
    ang = pos_ref[rs, :].astype(F32) * invf_ref[...]
    cos = jnp.cos(ang)
    sin = jnp.sin(ang) * sgn_ref[...]

    def rot(t):
        return t * cos + pltpu.roll(t, LANES // 2, 1) * sin

    qng = qng_ref[...]
    kng = kng_ref[...]
    kr_rot = rot(k_rope * kng[:, LANES:])
    kr_ssq = jnp.sum(k_rope * k_rope, axis=-1, keepdims=True)
    for h in range(N_HEADS):
        qh = qa[:, h * HEAD_W:(h + 1) * HEAD_W]
        rq = lax.rsqrt(jnp.sum(qh * qh, axis=-1, keepdims=True) / QK_DIM + EPS) * Q_PRESCALE
        qn = qh * rq * qng
        q_ref[0, h, rs, :LANES] = qn[:, :LANES].astype(BF16)
        q_ref[0, h, rs, LANES:] = (rot(qn[:, LANES:]) + qpad_ref[...]).astype(BF16)
        kh = kv[:, h * QK_NOPE:(h + 1) * QK_NOPE]
        rk = lax.rsqrt((jnp.sum(kh * kh, axis=-1, keepdims=True) + kr_ssq) / QK_DIM + EPS)
        k_ref[0, h, rs, :LANES] = (kh * rk * kng[:, :LANES]).astype(BF16)
        k_ref[0, h, rs, LANES:] = (kr_rot * rk + kpad_ref[...]).astype(BF16)
        vh = kv[:, N_HEADS * QK_NOPE + h * V_DIM:N_HEADS * QK_NOPE + (h + 1) * V_DIM]
        key_tile, keys = (i * tm) // TK, pl.ds((i * tm) % TK, tm)
        vt_ref[0, h, key_tile, :V_DIM, keys] = vh.T.astype(BF16)
        vt_ref[0, h, key_tile, V_DIM:, keys] = jnp.ones((V_ROWS - V_DIM, tm), BF16)


def _win_kernel(wt_ref, o_ref):
    w = wt_ref[...].T
    n_lat = o_ref.shape[1] - LANES
    zeros = jnp.zeros((w.shape[0], ROPE_HALF), F32)
    rope = jnp.concatenate([w[:, n_lat:n_lat + ROPE_HALF], zeros,
                            w[:, n_lat + ROPE_HALF:], zeros], axis=1)
    o_ref[:, :n_lat] = w[:, :n_lat].astype(BF16)
    o_ref[:, n_lat:] = rope.astype(BF16)


def _win_call(w_in_t):
    n_in, D = w_in_t.shape
    n_lat = n_in - QK_ROPE
    tm = WIN_ROWS
    return pl.pallas_call(
        _win_kernel,
        grid=(D // tm,),
        in_specs=[pl.BlockSpec((n_in, tm), lambda i: (0, i))],
        out_specs=pl.BlockSpec((tm, n_lat + LANES), lambda i: (i, 0)),
        out_shape=jax.ShapeDtypeStruct((D, n_lat + LANES), BF16),
        compiler_params=pltpu.CompilerParams(
            dimension_semantics=("arbitrary",), vmem_limit_bytes=VMEM_LIMIT),
        name="win",
    )(w_in_t)


def _pre_call(x, pos, g1, win, qag, wq, kvag, wkv, qng, kng, wpool, pscale, invf, sgn,
              qpad, kpad):
    B, S, D = x.shape
    tm = TM_PRE
    grid = (B, S // tm)
    row_spec = lambda w: pl.BlockSpec((None, tm, w), lambda b, s: (b, s, 0))
    head_spec = lambda w: pl.BlockSpec((1, N_HEADS, tm, w), lambda b, s: (b, 0, s, 0))
    consts = [g1, win, qag, wq, kvag, wkv, qng, kng, wpool, pscale, invf, sgn, qpad, kpad]
    return pl.pallas_call(
        _pre_kernel,
        grid=grid,
        in_specs=[row_spec(D), row_spec(1)] + [_const_spec(c.shape) for c in consts],
        out_specs=[row_spec(D_POOL), head_spec(HEAD_W), head_spec(HEAD_W),
                   pl.BlockSpec((1, N_HEADS, tm // TK, V_ROWS, TK),
                                lambda b, s: (b, 0, s, 0, 0))],
        out_shape=[jax.ShapeDtypeStruct((B, S, D_POOL), BF16),
                   jax.ShapeDtypeStruct((B, N_HEADS, S, HEAD_W), BF16),
                   jax.ShapeDtypeStruct((B, N_HEADS, S, HEAD_W), BF16),
                   jax.ShapeDtypeStruct((B, N_HEADS, S // TK, V_ROWS, TK), BF16)],
        scratch_shapes=[pltpu.VMEM((POOL_HALO, D_POOL), F32)],
        compiler_params=pltpu.CompilerParams(
            dimension_semantics=("arbitrary", "arbitrary"), vmem_limit_bytes=VMEM_LIMIT),
        name="pre",
    )(x, pos, *consts)


def _attn_kernel(bounded_ref, q_ref, k_ref, vt_ref, o_ref, *chain_scratch):
    S = q_ref.shape[2]
    nq = S // TQ
    per_q = TQ // TK
    assert per_q == 2, "two online-softmax chains take the even / odd key tiles"
    chains = (chain_scratch[0:2], chain_scratch[2:4])
    score_bufs = (chain_scratch[4:6], chain_scratch[6:8])
    key_idx = lax.broadcasted_iota(I32, (TK, TQ), 0)
    qry_idx = lax.broadcasted_iota(I32, (TK, TQ), 1)

    diag_masks = (key_idx <= qry_idx, key_idx + TK <= qry_idx)

    def weighted_values(kt, p):
        return jnp.dot(vt_ref[0, 0, kt], p.astype(BF16), preferred_element_type=F32)

    def write_out(qi, acc):
        out = acc[:V_DIM, :] / acc[V_DIM:V_DIM + 1, :]
        start = qi * TQ if isinstance(qi, int) else pl.multiple_of(qi * TQ, TQ)
        o_ref[0, pl.ds(start, TQ), :] = out.T.astype(BF16)

    def bounded_head():
        steps = [(qi, j) for qi in range(nq) for j in range(qi + 1)]

        def pair_scores(step, buf):
            qi, j = step
            q = q_ref[0, 0, qi * TQ:(qi + 1) * TQ, :]
            for c in range(per_q):
                kt = per_q * j + c
                buf[c][...] = lax.dot_general(k_ref[0, 0, kt * TK:(kt + 1) * TK, :], q,
                                              (((1,), (1,)), ((), ())),
                                              preferred_element_type=F32)

        pair_scores(steps[0], score_bufs[0])
        acc = None
        for t, (qi, j) in enumerate(steps):
            if t + 1 < len(steps):
                pair_scores(steps[t + 1], score_bufs[(t + 1) % 2])
            for c in range(per_q):
                st = score_bufs[t % 2][c][...]
                if j == qi:
                    st = jnp.where(diag_masks[c], st, NEG_BIG)
                pv = weighted_values(per_q * j + c, jnp.exp2(st))
                acc = pv if acc is None else acc + pv
            if j == qi:
                write_out(qi, acc)
                acc = None

    def online_q_tile(qi, _):
        q = q_ref[0, 0, pl.ds(pl.multiple_of(qi * TQ, TQ), TQ), :]
        for m_ref, acc_ref in chains:
            m_ref[...] = jnp.full(m_ref.shape, NEG_BIG, F32)
            acc_ref[...] = jnp.zeros(acc_ref.shape, F32)

        def scores(kt):
            k = k_ref[0, 0, pl.ds(pl.multiple_of(kt * TK, TK), TK), :]
            return lax.dot_general(k, q, (((1,), (1,)), ((), ())), preferred_element_type=F32)

        def fold(chain, st, kt, mask):
            m_ref, acc_ref = chain
            if mask is not None:
                st = jnp.where(mask, st, NEG_BIG)
            m = m_ref[...]
            m_new = jnp.maximum(m, jnp.max(st, axis=0, keepdims=True))
            m_ref[...] = m_new
            acc_ref[...] = (jnp.exp2(m - m_new) * acc_ref[...]
                            + weighted_values(kt, jnp.exp2(st - m_new)))

        def pair_scores(j, buf):
            for c in range(per_q):
                buf[c][...] = scores(per_q * j + c)

        def fold_pair(j, buf, masks=(None, None)):
            for c in range(per_q):
                fold(chains[c], buf[c][...], per_q * j + c, masks[c])

        pair_scores(0, score_bufs[0])

        def two_pairs(i, _):
            pair_scores(2 * i + 1, score_bufs[1])
            fold_pair(2 * i, score_bufs[0])
            pair_scores(2 * i + 2, score_bufs[0])
            fold_pair(2 * i + 1, score_bufs[1])
            return 0
        lax.fori_loop(0, qi // 2, two_pairs, 0)

        @pl.when(qi % 2 == 0)
        def _():
            fold_pair(qi, score_bufs[0], diag_masks)

        @pl.when(qi % 2 == 1)
        def _():
            pair_scores(qi, score_bufs[1])
            fold_pair(qi - 1, score_bufs[0])
            fold_pair(qi, score_bufs[1], diag_masks)

        (m0, acc0), (m1, acc1) = chains
        m = jnp.maximum(m0[...], m1[...])
        write_out(qi, jnp.exp2(m0[...] - m) * acc0[...] + jnp.exp2(m1[...] - m) * acc1[...])
        return 0

    pl.when(bounded_ref[0] != 0)(bounded_head)

    @pl.when(bounded_ref[0] == 0)
    def _():
        lax.fori_loop(0, nq, online_q_tile, 0)


def _attn_call(bounded, q, k, vt):
    B, H, S, _ = q.shape
    head = lambda w: pl.BlockSpec((1, 1, S, w), lambda b, h, flag: (b, h, 0, 0))
    grid_spec = pltpu.PrefetchScalarGridSpec(
        num_scalar_prefetch=1,
        grid=(B, H),
        in_specs=[head(HEAD_W), head(HEAD_W),
                  pl.BlockSpec((1, 1) + vt.shape[2:], lambda b, h, flag: (b, h, 0, 0, 0))],
        out_specs=pl.BlockSpec((1, S, V_DIM), lambda b, h, flag: (b, 0, h)),
        scratch_shapes=[pltpu.VMEM((1, TQ), F32), pltpu.VMEM((V_ROWS, TQ), F32)] * 2
        + [pltpu.VMEM((TK, TQ), F32)] * 4,
    )
    return pl.pallas_call(
        _attn_kernel,
        grid_spec=grid_spec,
        out_shape=jax.ShapeDtypeStruct((B, S, H * V_DIM), BF16),
        compiler_params=pltpu.CompilerParams(
            dimension_semantics=("arbitrary", "arbitrary"), vmem_limit_bytes=VMEM_LIMIT),
        name="attn",
    )(bounded, q, k, vt)


def _post_kernel(x_ref, yp_ref, ya_ref, wo_hbm, g2_ref, wr_ref, br_ref,
                 h1_ref, xn_ref, lg_ref, wo_ref, landing, sem):
    _load_weight_as_bf16(wo_hbm, wo_ref, landing, sem)
    for i in range(x_ref.shape[0] // POST_SUB):
        rs = pl.ds(i * POST_SUB, POST_SUB)
        h1 = (x_ref[rs, :]
              + jnp.dot(yp_ref[rs, :], wo_ref[:D_POOL, :], preferred_element_type=F32)
              + jnp.dot(ya_ref[rs, :], wo_ref[D_POOL:, :], preferred_element_type=F32))
        h1_ref[rs, :] = h1
        xn = _rms(h1, g2_ref[...])
        _store_packed_rows(xn_ref, i * POST_SUB, xn)
        xn_hi = xn.astype(BF16)
        xn_lo = (xn - xn_hi.astype(F32)).astype(BF16)
        both = (jnp.dot(xn_hi, wr_ref[...], preferred_element_type=F32)
                + jnp.dot(xn_lo, wr_ref[...], preferred_element_type=F32))
        lg_ref[rs, :] = both[:, :LANES] + both[:, LANES:] + br_ref[...]


def _post_call(x, yp, ya, wo, g2, wr, br):
    T, D = x.shape
    tm = TM_POST
    row = lambda w: pl.BlockSpec((tm, w), lambda i: (i, 0))
    return pl.pallas_call(
        _post_kernel,
        grid=(T // tm,),
        in_specs=[row(D), row(D_POOL), row(D_ATTN), pl.BlockSpec(memory_space=pl.ANY),
                  _const_spec(g2.shape), _const_spec(wr.shape), _const_spec(br.shape)],
        out_specs=[row(D), pl.BlockSpec((tm * PACK_SUBLANES, LANES), lambda i: (i, 0)),
                   row(LANES)],
        out_shape=[jax.ShapeDtypeStruct((T, D), F32),
                   jax.ShapeDtypeStruct((T * PACK_SUBLANES, LANES), jnp.uint32),
                   jax.ShapeDtypeStruct((T, LANES), F32)],
        scratch_shapes=_bf16_weight_scratch(wo.shape),
        compiler_params=pltpu.CompilerParams(
            dimension_semantics=("arbitrary",), vmem_limit_bytes=VMEM_LIMIT),
        name="post",
    )(x, yp, ya, wo, g2, wr, br)


def _route_kernel(lg_ref, dest_ref, wts_ref, cnt_ref, carry_ref, code_ref, start_ref):
    n_tiles = code_ref.shape[0]
    g = pl.program_id(0)

    @pl.when(g == 0)
    def _():
        carry_ref[...] = jnp.zeros_like(carry_ref)

    @pl.when(g < n_tiles)
    def _():
        for i in range(lg_ref.shape[0] // ROUTE_SUB):
            rs = pl.ds(i * ROUTE_SUB, ROUTE_SUB)
            code_t, wts_t = _route_cols(lg_ref[rs, :].T, carry_ref)
            code_ref[g, :, rs] = code_t
            wts_cols = jnp.concatenate(
                [wts_t, jnp.zeros((LANES - PACK_SUBLANES, ROUTE_SUB), F32)], axis=0)
            wts_ref[rs, :] = wts_cols.T
        cnt_ref[...] = carry_ref[...].astype(I32)

    @pl.when(g == n_tiles)
    def _():
        blocks = jnp.floor((carry_ref[...] + (MOE_BLK - 1)) * (1.0 / MOE_BLK))
        r = lax.broadcasted_iota(I32, (LANES, LANES), 0)
        c = lax.broadcasted_iota(I32, (LANES, LANES), 1)
        earlier = jnp.where(c < r, 1.0, 0.0).astype(BF16)
        before = jnp.dot(earlier, jnp.broadcast_to(blocks, (LANES, LANES)).astype(BF16),
                         preferred_element_type=F32)
        start_ref[...] = before[:, 0:1] * MOE_BLK

    @pl.when(g >= n_tiles)
    def _():
        code = code_ref[g - n_tiles]
        expert = lax.shift_right_logical(code, 16)
        rank = code & 0xFFFF
        expert_row = lax.broadcasted_iota(I32, (LANES, code.shape[1]), 0)
        sub = lax.broadcasted_iota(I32, code.shape, 0)
        dest = jnp.zeros(code.shape, I32)
        for choice in range(TOP_K):
            hit = expert_row == expert[choice:choice + 1, :]
            start = jnp.sum(jnp.where(hit, start_ref[...], 0.0), axis=0, keepdims=True)
            dest = jnp.where(sub == choice, start.astype(I32) + rank, dest)
        dest_ref[...] = dest


def _route_call(logits):
    T = logits.shape[0]
    tm = TM_ROUTE
    n_tiles = T // tm
    pass0 = lambda g: jnp.minimum(g, n_tiles - 1)
    pass1 = lambda g: jnp.maximum(g - n_tiles, 0)
    return pl.pallas_call(
        _route_kernel,
        grid=(2 * n_tiles,),
        in_specs=[pl.BlockSpec((tm, LANES), lambda g: (pass0(g), 0))],
        out_specs=[pl.BlockSpec((PACK_SUBLANES, tm), lambda g: (0, pass1(g))),
                   pl.BlockSpec((tm, LANES), lambda g: (pass0(g), 0)),
                   pl.BlockSpec((LANES, 1), lambda g: (0, 0))],
        out_shape=[jax.ShapeDtypeStruct((PACK_SUBLANES, T), I32),
                   jax.ShapeDtypeStruct((T, LANES), F32),
                   jax.ShapeDtypeStruct((LANES, 1), I32)],
        scratch_shapes=[pltpu.VMEM((LANES, 1), F32),
                        pltpu.VMEM((n_tiles, PACK_SUBLANES, tm), I32),
                        pltpu.VMEM((LANES, 1), F32)],
        compiler_params=pltpu.CompilerParams(
            dimension_semantics=("arbitrary",), vmem_limit_bytes=VMEM_LIMIT),
        name="route",
    )(logits)


def _route_cols(lt, carry_ref):
    tm = lt.shape[1]
    sub = lax.broadcasted_iota(I32, (PACK_SUBLANES, tm), 0).astype(F32)

    def first_argmax(vals):
        mx = jnp.max(vals, axis=0, keepdims=True)
        idx = jnp.min(jnp.where(vals == mx, sub, float(PACK_SUBLANES)), axis=0, keepdims=True)
        return mx, idx

    g_logits = lt[0:N_GROUPS, :]
    g_max, g_sel = first_argmax(g_logits)
    g_w = 1.0 / jnp.sum(jnp.exp(g_logits - g_max), axis=0, keepdims=True)

    e_logits = lt[N_GROUPS:N_GROUPS + EXPERTS_PER_GROUP, :]
    for g in range(1, N_GROUPS):
        lo = N_GROUPS + g * EXPERTS_PER_GROUP
        e_logits = jnp.where(g_sel == float(g), lt[lo:lo + EXPERTS_PER_GROUP, :], e_logits)
    v1, i1 = first_argmax(e_logits)
    v2, i2 = first_argmax(jnp.where(sub == i1, -jnp.inf, e_logits))
    e1 = (g_sel * EXPERTS_PER_GROUP + i1).astype(I32)
    e2 = (g_sel * EXPERTS_PER_GROUP + i2).astype(I32)
    t = jnp.exp(v2 - v1)
    w1 = g_w / (1.0 + t)
    w2 = g_w * t / (1.0 + t)

    expert_row = lax.broadcasted_iota(I32, (LANES, tm), 0)
    hit1 = expert_row == e1
    hit2 = expert_row == e2
    onehot = jnp.where(hit1 | hit2, 1.0, 0.0)
    r = lax.broadcasted_iota(I32, (tm, tm), 0)
    c = lax.broadcasted_iota(I32, (tm, tm), 1)
    earlier = jnp.where(r < c, 1.0, 0.0).astype(BF16)
    before = (jnp.dot(onehot.astype(BF16), earlier, preferred_element_type=F32)
              + carry_ref[...])
    carry_ref[...] += jnp.sum(onehot, axis=1, keepdims=True)
    pos1 = jnp.sum(jnp.where(hit1, before, 0.0), axis=0, keepdims=True).astype(I32)
    pos2 = jnp.sum(jnp.where(hit2, before, 0.0), axis=0, keepdims=True).astype(I32)
    code1 = e1 * 65536 + pos1
    code2 = e2 * 65536 + pos2
    first, second = sub == 0.0, sub == 1.0
    return (jnp.where(first, code1, jnp.where(second, code2, 0)),
            jnp.where(first, w1, jnp.where(second, w2, 0.0)))


def _plan_kernel(dest_ref, cnt_ref, unused_hbm, slots_hbm, blke_ref, blkn_ref, blkw_ref,
                 rune_ref, slot_ref, sem):
    n_assign = dest_ref.shape[0]
    n_blocks = blke_ref.shape[0]
    load = pltpu.make_async_copy(unused_hbm, slot_ref, sem)
    load.start()

    def per_expert(e, nb_done):
        cnt = cnt_ref[e]
        nb = lax.shift_right_logical(cnt + (MOE_BLK - 1), MOE_BLK.bit_length() - 1)

        def per_block(j, _):
            blke_ref[nb_done + j] = e
            blkn_ref[nb_done + j] = jnp.minimum(cnt - j * MOE_BLK, MOE_BLK)
            return 0
        lax.fori_loop(0, nb, per_block, 0)
        return nb_done + nb
    used = lax.fori_loop(0, N_EXPERTS, per_expert, 0)

    last_e = blke_ref[jnp.maximum(used - 1, 0)]

    def tail(b, _):
        blke_ref[b] = last_e
        blkn_ref[b] = 0
        blkw_ref[b] = 0
        return 0
    lax.fori_loop(used, n_blocks, tail, 0)

    def no_run(r, _):
        rune_ref[r] = -1
        return 0
    lax.fori_loop(0, rune_ref.shape[0], no_run, 0)

    def runs_forward(b, carry):
        earlier_e, run = carry
        e = blke_ref[b]
        first = (e != earlier_e).astype(I32)
        run = run + first
        blkw_ref[b] = first + run * 2
        rune_ref[run] = e
        return e, run
    lax.fori_loop(0, used, runs_forward, (-1, -1))

    load.wait()

    def place(a, _):
        slot_ref[dest_ref[a]] = a
        return 0
    lax.fori_loop(0, n_assign, place, 0, unroll=PLAN_UNROLL)

    store = pltpu.make_async_copy(slot_ref, slots_hbm, sem)
    store.start()
    store.wait()


def _plan_call(dest_flat, counts, n_tab):
    n_assign = dest_flat.shape[0]
    n_slots = n_tab * MOE_BLK
    unused = (n_assign + (np.arange(n_slots) & (MOE_BLK - 1))).astype(np.int32)
    smem = pl.BlockSpec(memory_space=pltpu.SMEM)
    hbm = pl.BlockSpec(memory_space=pl.ANY)
    return pl.pallas_call(
        _plan_kernel,
        in_specs=[smem, smem, hbm],
        out_specs=[hbm, smem, smem, smem, smem],
        out_shape=[jax.ShapeDtypeStruct((n_slots,), I32),
                   jax.ShapeDtypeStruct((n_tab,), I32),
                   jax.ShapeDtypeStruct((n_tab,), I32),
                   jax.ShapeDtypeStruct((n_tab,), I32),
                   jax.ShapeDtypeStruct((N_EXPERTS + WEIGHT_SLOTS,), I32)],
        scratch_shapes=[pltpu.SMEM((n_slots,), I32), pltpu.SemaphoreType.DMA],
        name="plan",
    )(dest_flat, counts, unused)


def _expert_kernel(blke_ref, blkn_ref, blkw_ref, rune_ref, slot_g_ref, slot_s_ref, xn_hbm,
                   wg_hbm, wu_hbm, wd_hbm, y_hbm, x0, x1, y0, y1, wg_buf, wu_buf, wd_buf,
                   gsem, ssem, wsem):
    xbufs, ybufs = (x0, x1), (y0, y1)
    s = pl.program_id(0)
    n_tok = xn_hbm.shape[0] // PACK_SUBLANES

    def weight_copies(e, slot):
        return [pltpu.make_async_copy(hbm.at[e], buf.at[slot], wsem.at[slot])
                for hbm, buf in ((wg_hbm, wg_buf), (wu_hbm, wu_buf), (wd_hbm, wd_buf))]

    blk = jnp.maximum(s - 1, 0)
    run_info = blkw_ref[blk]
    run = run_info >> 1
    w_slot = lax.rem(run, WEIGHT_SLOTS)
    ahead_run = run + (WEIGHT_SLOTS - 1)
    ahead_expert = rune_ref[ahead_run]
    ahead_slot = lax.rem(ahead_run, WEIGHT_SLOTS)

    def tile(ref, r):
        start = r * PACK_SUBLANES
        if not isinstance(r, int):
            start = pl.multiple_of(start, PACK_SUBLANES)
        return ref.at[pl.ds(start, PACK_SUBLANES)]

    def rows(j):
        n = blkn_ref[jnp.clip(j, 0, blkn_ref.shape[0] - 1)]
        return jnp.where(j >= 0, (n + (ROW_GROUP - 1)) & ~(ROW_GROUP - 1), 0)
    rows0, rows1, rows2, rows3 = rows(s), rows(s - 1), rows(s - 2), rows(s - 3)

    def stage(nxt):
        cur = 1 - nxt
        x_nxt, x_cur, y_nxt, y_cur = xbufs[nxt], xbufs[cur], ybufs[nxt], ybufs[cur]

        def per_group(n_rows, issue_row):
            for g in range(MOE_BLK // ROW_GROUP):
                @pl.when(g * ROW_GROUP < n_rows)
                def _():
                    for r in range(g * ROW_GROUP, (g + 1) * ROW_GROUP):
                        issue_row(r)

        def gather_row(r):
            tok = slot_g_ref[0, 0, r] & (n_tok - 1)
            pltpu.make_async_copy(tile(xn_hbm, tok), tile(x_nxt, r), gsem.at[nxt]).start()

        def scatter_row(r):
            pltpu.make_async_copy(tile(y_nxt, r), tile(y_hbm, slot_s_ref[0, 0, r]),
                                  ssem.at[nxt]).start()

        def wait_rows(src, dst, sem, n_rows):
            n_words = pl.multiple_of(n_rows * PACK_SUBLANES, ROW_GROUP * PACK_SUBLANES)

            @pl.when(n_rows > 0)
            def _():
                pltpu.make_async_copy(src.at[pl.ds(0, n_words)], dst.at[pl.ds(0, n_words)],
                                      sem).wait()

        def compute():
            xb = _load_packed_rows(x_cur, 0, MOE_BLK).astype(BF16)
            a = jnp.dot(xb, wg_buf[w_slot].astype(BF16), preferred_element_type=F32)
            u = jnp.dot(xb, wu_buf[w_slot].astype(BF16), preferred_element_type=F32)
            hmid = (a * jax.nn.sigmoid(a) * u).astype(BF16)
            _store_packed_rows(y_cur, 0, jnp.dot(hmid, wd_buf[w_slot].astype(BF16),
                                                 preferred_element_type=F32))

        wait_rows(xn_hbm, x_cur, gsem.at[cur], rows1)
        wait_rows(y_cur, y_hbm, ssem.at[cur], rows3)

        @pl.when((rows1 > 0) & ((run_info & 1) == 1))
        def _():
            for copy in weight_copies(blke_ref[blk], w_slot):
                copy.wait()

            @pl.when(ahead_expert >= 0)
            def _():
                for copy in weight_copies(ahead_expert, ahead_slot):
                    copy.start(priority=WEIGHT_DMA_PRIORITY)

        if nxt == 0:
            for first_run in range(WEIGHT_SLOTS - 1):
                @pl.when((s == 0) & (rune_ref[first_run] >= 0))
                def _():
                    for copy in weight_copies(rune_ref[first_run], first_run):
                        copy.start(priority=WEIGHT_DMA_PRIORITY)

            @pl.when(s == 0)
            def _():
                x_nxt[...] = jnp.zeros(x_nxt.shape, x_nxt.dtype)
                x_cur[...] = jnp.zeros(x_cur.shape, x_cur.dtype)
                y_nxt[...] = jnp.zeros(y_nxt.shape, y_nxt.dtype)
                dump = pltpu.make_async_copy(
                    y_nxt, y_hbm.at[pl.ds(TOP_K * n_tok * PACK_SUBLANES,
                                          MOE_BLK * PACK_SUBLANES)], ssem.at[nxt])
                dump.start()
                dump.wait()

        per_group(rows0, gather_row)
        per_group(rows2, scatter_row)

        @pl.when(rows1 > 0)
        def _():
            compute()

    for parity in range(2):
        pl.when((s & 1) == parity)(functools.partial(stage, parity))


def _expert_call(blk_e, blk_n, blk_w, run_e, slots, xn, wg, wu, wd):
    T = xn.shape[0] // PACK_SUBLANES
    assert T & (T - 1) == 0, "token id is recovered from the assignment id with a mask"
    n_tab = blk_e.shape[0]
    _, D, F = wg.shape
    packed_block = pltpu.VMEM((MOE_BLK * PACK_SUBLANES, LANES), jnp.uint32)
    hbm = pl.BlockSpec(memory_space=pl.ANY)
    grid_spec = pltpu.PrefetchScalarGridSpec(
        num_scalar_prefetch=4,
        grid=(n_tab,),
        in_specs=[
            pl.BlockSpec((1, 1, MOE_BLK), lambda s, *_: (s, 0, 0), memory_space=pltpu.SMEM),
            pl.BlockSpec((1, 1, MOE_BLK), lambda s, *_: (jnp.maximum(s - 2, 0), 0, 0),
                         memory_space=pltpu.SMEM),
            hbm, hbm, hbm, hbm,
        ],
        out_specs=hbm,
        scratch_shapes=[packed_block] * 4
        + [pltpu.VMEM((WEIGHT_SLOTS, D, F), F32), pltpu.VMEM((WEIGHT_SLOTS, D, F), F32),
           pltpu.VMEM((WEIGHT_SLOTS, F, D), F32)]
        + [pltpu.SemaphoreType.DMA((2,)), pltpu.SemaphoreType.DMA((2,)),
           pltpu.SemaphoreType.DMA((WEIGHT_SLOTS,))],
    )
    slots3 = slots.reshape(n_tab, 1, MOE_BLK)
    return pl.pallas_call(
        _expert_kernel,
        grid_spec=grid_spec,
        out_shape=jax.ShapeDtypeStruct(((T * TOP_K + MOE_BLK) * PACK_SUBLANES, LANES),
                                       jnp.uint32),
        compiler_params=pltpu.CompilerParams(
            dimension_semantics=("arbitrary",), vmem_limit_bytes=VMEM_LIMIT,
            disable_bounds_checks=True),
        name="experts",
    )(blk_e, blk_n, blk_w, run_e, slots3, slots3, xn, wg, wu, wd)


def _final_kernel(h1_ref, y1_ref, y2_ref, wts_ref, p_ref, wpp_ref, pg_ref, g3_ref, wpg_hbm,
                  o_ref, wpg_ref, landing, sem):
    _load_weight_as_bf16(wpg_hbm, wpg_ref, landing, sem)
    for i in range(h1_ref.shape[0] // FINAL_SUB):
        rs = pl.ds(i * FINAL_SUB, FINAL_SUB)
        w = wts_ref[rs, :]
        h2 = (h1_ref[rs, :]
              + w[:, 0:1] * _load_packed_rows(y1_ref, i * FINAL_SUB, FINAL_SUB)
              + w[:, 1:2] * _load_packed_rows(y2_ref, i * FINAL_SUB, FINAL_SUB))
        e = _rms(jnp.dot(p_ref[rs, :].astype(BF16), wpp_ref[...], preferred_element_type=F32),
                 pg_ref[...])
        gate = jax.nn.sigmoid(jnp.dot(_rms(h2, g3_ref[...]).astype(BF16), wpg_ref[...],
                                      preferred_element_type=F32))
        o_ref[rs, :] = h2 + gate * e


def _final_call(h1, y, wts, p, wpp, pg, g3, wpg):
    T, D = h1.shape
    tm = TM_FINAL
    row = lambda w: pl.BlockSpec((tm, w), lambda i: (i, 0))
    first = pl.BlockSpec((tm * PACK_SUBLANES, LANES), lambda i: (i, 0))
    second = pl.BlockSpec((tm * PACK_SUBLANES, LANES), lambda i: (i + T // tm, 0))
    return pl.pallas_call(
        _final_kernel,
        grid=(T // tm,),
        in_specs=[row(D), first, second, row(LANES), row(PLE_DIM), _const_spec(wpp.shape),
                  _const_spec(pg.shape), _const_spec(g3.shape),
                  pl.BlockSpec(memory_space=pl.ANY)],
        out_specs=row(D),
        out_shape=jax.ShapeDtypeStruct((T, D), F32),
        scratch_shapes=_bf16_weight_scratch(wpg.shape),
        compiler_params=pltpu.CompilerParams(
            dimension_semantics=("arbitrary",), vmem_limit_bytes=VMEM_LIMIT),
        name="final",
    )(h1, y, y, wts, p, wpp, pg, g3, wpg)


def _rope_lanes(a):
    z = jnp.zeros(a.shape[:-1] + (ROPE_HALF,), a.dtype)
    return jnp.concatenate([a[..., :ROPE_HALF], z, a[..., ROPE_HALF:], z], axis=-1)


def _head_lanes(a):
    return jnp.concatenate([a[..., :QK_NOPE], _rope_lanes(a[..., QK_NOPE:])], axis=-1)


def _score_bound(q_gain, k_gain):
    return Q_PRESCALE * QK_DIM * jnp.max(jnp.abs(q_gain)) * jnp.max(jnp.abs(k_gain))


def kernel(x, p, positions, norm1_gain, w_in, q_a_gain, w_q_b, kv_a_gain, w_kv_b, q_norm_gain, k_norm_gain, w_pool, pool_scale, w_out, norm2_gain, w_router_group, b_router_group, w_router_expert, b_router_expert, w_exp_gate, w_exp_up, w_exp_down, norm3_gain, w_ple_gate, w_ple_proj, ple_norm_gain):
    B, S, D = x.shape
    T = B * S
    assert x.shape[2] == D_MODEL and S % TQ == 0 and S % TM_PRE == 0
    assert T % TM_POST == 0 and T % TM_FINAL == 0
    layer = 0
    row = lambda a: a[layer].reshape(1, -1)

    win = _win_call(w_in[layer].T)
    wq = _head_lanes(w_q_b[layer].reshape(Q_LORA, N_HEADS, QK_DIM)
                     ).reshape(Q_LORA, N_HEADS * HEAD_W).astype(BF16)
    wkv3 = w_kv_b[layer].reshape(KV_LORA, N_HEADS, QK_NOPE + V_DIM)
    wkv = jnp.concatenate([wkv3[..., :QK_NOPE].reshape(KV_LORA, -1),
                           wkv3[..., QK_NOPE:].reshape(KV_LORA, -1)], axis=1).astype(BF16)
    qng = _head_lanes(q_norm_gain[layer]).reshape(1, HEAD_W)
    kng = _head_lanes(k_norm_gain[layer]).reshape(1, HEAD_W)
    inv_freq = np.float32(ROPE_THETA) ** (-np.arange(ROPE_HALF, dtype=np.float32) / ROPE_HALF)
    zeros_half = np.zeros((ROPE_HALF,), np.float32)
    ones_half = np.ones((ROPE_HALF,), np.float32)
    invf = np.concatenate([inv_freq, zeros_half, inv_freq, zeros_half]).reshape(1, LANES)
    sgn = np.concatenate([-ones_half, zeros_half, ones_half, zeros_half]).reshape(1, LANES)
    spare = np.arange(LANES) == SPARE_ROPE_LANE
    kpad = spare.astype(np.float32).reshape(1, LANES)
    pad_lanes = LANES - N_GROUPS - N_EXPERTS
    wr = jnp.concatenate([w_router_group[layer], w_router_expert[layer],
                          jnp.zeros((D, pad_lanes), F32)], axis=1)
    wr_hi = wr.astype(BF16)
    wr = jnp.concatenate([wr_hi, (wr - wr_hi.astype(F32)).astype(BF16)], axis=1)
    br = jnp.concatenate([b_router_group[layer], b_router_expert[layer],
                          jnp.zeros((pad_lanes,), F32)]).reshape(1, LANES)

    bound = _score_bound(q_norm_gain[layer], k_norm_gain[layer])
    bounded = bound <= SCORE_BOUND_LIMIT
    qpad = jnp.where(bounded, -bound, 0.0) * kpad
    ypool, q, k, vt = _pre_call(
        x, positions.reshape(B, S, 1), row(norm1_gain), win, row(q_a_gain), wq,
        row(kv_a_gain), wkv, qng, kng, w_pool[layer].astype(BF16), row(pool_scale), invf, sgn,
        qpad, kpad)
    yattn = _attn_call(bounded.astype(I32).reshape(1), q, k, vt)
    h1, xn2, logits = _post_call(
        x.reshape(T, D), ypool.reshape(T, D_POOL), yattn.reshape(T, D_ATTN),
        w_out[layer], row(norm2_gain), wr, br)
    dest, wts, counts = _route_call(logits)

    n_assign = T * TOP_K
    n_blocks = (n_assign + N_EXPERTS * (MOE_BLK - 1)) // MOE_BLK
    slots, blk_e, blk_n, blk_w, run_e = _plan_call(
        dest[:TOP_K].reshape(n_assign), counts.reshape(LANES), n_blocks + MOE_DRAIN_STEPS)
    y = _expert_call(blk_e, blk_n, blk_w, run_e, slots, xn2,
                     w_exp_gate[layer], w_exp_up[layer], w_exp_down[layer])

    out = _final_call(h1, y, wts, p[layer].reshape(T, PLE_DIM),
                      w_ple_proj[layer].astype(BF16), row(ple_norm_gain), row(norm3_gain),
                      w_ple_gate[layer])
    return out.reshape(B, S, D)
```

```python
import functools
import math

import jax
import jax.numpy as jnp
import numpy as np
from jax import lax
from jax.experimental import pallas as pl
from jax.experimental.pallas import tpu as pltpu

F32 = jnp.float32
BF16 = jnp.bfloat16
I32 = jnp.int32

D_MODEL = 2048
PLE_DIM = 256
EPS = 1e-6
POOL_WINDOWS = (2, 4, 8, 16)
POOL_CH = 256
D_POOL = POOL_CH * len(POOL_WINDOWS)
N_HEADS = 8
Q_LORA = 512
KV_LORA = 512
QK_NOPE = 128
QK_ROPE = 64
QK_DIM = QK_NOPE + QK_ROPE
V_DIM = 128
D_ATTN = N_HEADS * V_DIM
ROPE_THETA = 10000.0
ATTN_SCALE = 1.0 / math.sqrt(QK_DIM)
Q_PRESCALE = ATTN_SCALE * math.log2(math.e)
N_GROUPS = 8
EXPERTS_PER_GROUP = 8
N_EXPERTS = N_GROUPS * EXPERTS_PER_GROUP
TOP_K = 2

LANES = 128
PACK_SUBLANES = 8
VMEM_LIMIT = 56 * 1024 * 1024

HEAD_W = 2 * LANES
ROPE_HALF = QK_ROPE // 2
POOL_HALO = 16
TM_PRE = 512
PRE_SUB = 256
TM_POST = 512
POST_SUB = 256
TM_FINAL = 512
FINAL_SUB = 256
WIN_ROWS = 256
CAST_ROWS = 512
TQ = 512
TK = 256
TM_ROUTE = 1024
ROUTE_SUB = 256
MOE_BLK = 256
MOE_DRAIN_STEPS = 3
ROW_GROUP = 32
WEIGHT_DMA_PRIORITY = 1
WEIGHT_SLOTS = 3
PLAN_UNROLL = 32
NEG_BIG = -1e30
V_ROWS = V_DIM + 16
SPARE_ROPE_LANE = ROPE_HALF
SCORE_BOUND_LIMIT = 50.0


def _const_spec(shape):
    nd = len(shape)
    return pl.BlockSpec(shape, lambda *_: (0,) * nd, pipeline_mode=pl.Buffered(1))


def _rms(x, gain):
    return x * lax.rsqrt(jnp.mean(x * x, axis=-1, keepdims=True) + EPS) * gain


def _load_weight_as_bf16(w_hbm, w_bf16, landing, sem):
    n_chunks = w_hbm.shape[0] // CAST_ROWS

    def chunk(c):
        return pltpu.make_async_copy(w_hbm.at[pl.ds(c * CAST_ROWS, CAST_ROWS)],
                                     landing.at[c % 2], sem.at[c % 2])

    @pl.when(pl.program_id(0) == 0)
    def _():
        chunk(0).start()
        for c in range(n_chunks):
            if c + 1 < n_chunks:
                chunk(c + 1).start()
            chunk(c).wait()
            w_bf16[pl.ds(c * CAST_ROWS, CAST_ROWS), :] = landing[c % 2].astype(BF16)


def _bf16_weight_scratch(shape):
    rows, cols = shape
    assert rows % CAST_ROWS == 0
    return [pltpu.VMEM((rows, cols), BF16), pltpu.VMEM((2, CAST_ROWS, cols), F32),
            pltpu.SemaphoreType.DMA((2,))]


def _store_packed_rows(ref, first_row, x):
    rows, d = x.shape
    half = d // 2
    assert half == PACK_SUBLANES * LANES
    for i in range(PACK_SUBLANES):
        lo = x[:, i * LANES:(i + 1) * LANES].astype(BF16).astype(F32)
        hi = x[:, half + i * LANES:half + (i + 1) * LANES].astype(BF16).astype(F32)
        word = (lax.shift_right_logical(pltpu.bitcast(lo, jnp.uint32), jnp.uint32(16))
                | (pltpu.bitcast(hi, jnp.uint32) & jnp.uint32(0xFFFF0000)))
        ref[pl.ds(first_row * PACK_SUBLANES + i, rows, stride=PACK_SUBLANES), :] = word


def _load_packed_rows(ref, first_row, rows):
    lo, hi = [], []
    for i in range(PACK_SUBLANES):
        word = ref[pl.ds(first_row * PACK_SUBLANES + i, rows, stride=PACK_SUBLANES), :]
        lo.append(pltpu.bitcast(lax.shift_left(word, jnp.uint32(16)), F32))
        hi.append(pltpu.bitcast(word & jnp.uint32(0xFFFF0000), F32))
    return jnp.concatenate(lo + hi, axis=1)


def _pre_kernel(x_ref, pos_ref, g1_ref, win_ref, qag_ref, wq_ref, kvag_ref, wkv_ref,
                qng_ref, kng_ref, wpool_ref, pscale_ref, invf_ref, sgn_ref, qpad_ref, kpad_ref,
                ypool_ref, q_ref, k_ref, vt_ref, carry_ref):
    st = pl.program_id(1)
    tm = PRE_SUB
    n_sub = x_ref.shape[0] // tm

    @pl.when(st == 0)
    def _():
        carry_ref[...] = jnp.zeros_like(carry_ref)

    zs = [_pre_project(pl.ds(i * tm, tm), x_ref, g1_ref, win_ref) for i in range(n_sub)]
    halo = carry_ref[...]
    mixed = []
    for i in range(n_sub):
        first_pos = st * x_ref.shape[0] + i * tm
        qa, kv, halo = _pre_mix(pl.ds(i * tm, tm), first_pos, zs[i], halo, qag_ref, wq_ref,
                                kvag_ref, wkv_ref, wpool_ref, pscale_ref, ypool_ref)
        mixed.append((qa, kv))
    carry_ref[...] = halo
    for i in range(n_sub):
        k_rope = zs[i][:, D_POOL + Q_LORA + KV_LORA:]
        _pre_heads(i, *mixed[i], k_rope, pos_ref, qng_ref, kng_ref, invf_ref, sgn_ref,
                   qpad_ref, kpad_ref, q_ref, k_ref, vt_ref)


def _pre_project(rs, x_ref, g1_ref, win_ref):
    hn = _rms(x_ref[rs, :], g1_ref[...])
    return jnp.dot(hn.astype(BF16), win_ref[...], preferred_element_type=F32)


def _pre_mix(rs, first_pos, z, halo, qag_ref, wq_ref, kvag_ref, wkv_ref, wpool_ref, pscale_ref,
             ypool_ref):
    tm = z.shape[0]
    u = z[:, :D_POOL]
    ext = jnp.concatenate([halo, u], axis=0)
    row = lax.broadcasted_iota(I32, (tm, 1), 0) + first_pos
    level = ext
    shift = 1
    for g, w in enumerate(POOL_WINDOWS):
        sl = slice(g * POOL_CH, (g + 1) * POOL_CH)
        while shift < w:
            level = level + pltpu.roll(level, shift, 0)
            shift *= 2
        win_sum = level[POOL_HALO:, sl]
        cnt = jnp.minimum(row + 1, w).astype(F32)
        d = win_sum / cnt - u[:, sl]
        y = jnp.dot(d.astype(BF16), wpool_ref[g], preferred_element_type=F32)
        ypool_ref[rs, sl] = (y * pscale_ref[:, sl]).astype(BF16)

    q_lat = z[:, D_POOL:D_POOL + Q_LORA]
    kv_lat = z[:, D_POOL + Q_LORA:D_POOL + Q_LORA + KV_LORA]
    qa = jnp.dot(_rms(q_lat, qag_ref[...]).astype(BF16), wq_ref[...],
                 preferred_element_type=F32)
    kv = jnp.dot(_rms(kv_lat, kvag_ref[...]).astype(BF16), wkv_ref[...],
                 preferred_element_type=F32)
    return qa, kv, u[tm - POOL_HALO:, :]


def _pre_heads(i, qa, kv, k_rope, pos_ref, qng_ref, kng_ref, invf_ref, sgn_ref, qpad_ref,
               kpad_ref, q_ref, k_ref, vt_ref):
    tm = qa.shape[0]
    rs = pl.ds(i * tm, tm)
    ang = pos_ref[rs, :].astype(F32) * invf_ref[...]
    cos = jnp.cos(ang)
    sin = jnp.sin(ang) * sgn_ref[...]

    def rot(t):
        return t * cos + pltpu.roll(t, LANES // 2, 1) * sin

    qng = qng_ref[...]
    kng = kng_ref[...]
    kr_rot = rot(k_rope * kng[:, LANES:])
    kr_ssq = jnp.sum(k_rope * k_rope, axis=-1, keepdims=True)
    for h in range(N_HEADS):
        qh = qa[:, h * HEAD_W:(h + 1) * HEAD_W]
        rq = lax.rsqrt(jnp.sum(qh * qh, axis=-1, keepdims=True) / QK_DIM + EPS) * Q_PRESCALE
        qn = qh * rq * qng
        q_ref[0, h, rs, :LANES] = qn[:, :LANES].astype(BF16)
        q_ref[0, h, rs, LANES:] = (rot(qn[:, LANES:]) + qpad_ref[...]).astype(BF16)
        kh = kv[:, h * QK_NOPE:(h + 1) * QK_NOPE]
        rk = lax.rsqrt((jnp.sum(kh * kh, axis=-1, keepdims=True) + kr_ssq) / QK_DIM + EPS)
        k_ref[0, h, rs, :LANES] = (kh * rk * kng[:, :LANES]).astype(BF16)
        k_ref[0, h, rs, LANES:] = (kr_rot * rk + kpad_ref[...]).astype(BF16)
        vh = kv[:, N_HEADS * QK_NOPE + h * V_DIM:N_HEADS * QK_NOPE + (h + 1) * V_DIM]
        key_tile, keys = (i * tm) // TK, pl.ds((i * tm) % TK, tm)
        vt_ref[0, h, key_tile, :V_DIM, keys] = vh.T.astype(BF16)
        vt_ref[0, h, key_tile, V_DIM:, keys] = jnp.ones((V_ROWS - V_DIM, tm), BF16)


def _win_kernel(wt_ref, o_ref):
    w = wt_ref[...].T
    n_lat = o_ref.shape[1] - LANES
    zeros = jnp.zeros((w.shape[0], ROPE_HALF), F32)
    rope = jnp.concatenate([w[:, n_lat:n_lat + ROPE_HALF], zeros,
                            w[:, n_lat + ROPE_HALF:], zeros], axis=1)
    o_ref[:, :n_lat] = w[:, :n_lat].astype(BF16)
    o_ref[:, n_lat:] = rope.astype(BF16)


def _win_call(w_in_t):
    n_in, D = w_in_t.shape
    n_lat = n_in - QK_ROPE
    tm = WIN_ROWS
    return pl.pallas_call(
        _win_kernel,
        grid=(D // tm,),
        in_specs=[pl.BlockSpec((n_in, tm), lambda i: (0, i))],
        out_specs=pl.BlockSpec((tm, n_lat + LANES), lambda i: (i, 0)),
        out_shape=jax.ShapeDtypeStruct((D, n_lat + LANES), BF16),
        compiler_params=pltpu.CompilerParams(
            dimension_semantics=("arbitrary",), vmem_limit_bytes=VMEM_LIMIT),
        name="win",
    )(w_in_t)


def _pre_call(x, pos, g1, win, qag, wq, kvag, wkv, qng, kng, wpool, pscale, invf, sgn,
              qpad, kpad):
    B, S, D = x.shape
    tm = TM_PRE
    grid = (B, S // tm)
    row_spec = lambda w: pl.BlockSpec((None, tm, w), lambda b, s: (b, s, 0))
    head_spec = lambda w: pl.BlockSpec((1, N_HEADS, tm, w), lambda b, s: (b, 0, s, 0))
    consts = [g1, win, qag, wq, kvag, wkv, qng, kng, wpool, pscale, invf, sgn, qpad, kpad]
    return pl.pallas_call(
        _pre_kernel,
        grid=grid,
        in_specs=[row_spec(D), row_spec(1)] + [_const_spec(c.shape) for c in consts],
        out_specs=[row_spec(D_POOL), head_spec(HEAD_W), head_spec(HEAD_W),
                   pl.BlockSpec((1, N_HEADS, tm // TK, V_ROWS, TK),
                                lambda b, s: (b, 0, s, 0, 0))],
        out_shape=[jax.ShapeDtypeStruct((B, S, D_POOL), BF16),
                   jax.ShapeDtypeStruct((B, N_HEADS, S, HEAD_W), BF16),
                   jax.ShapeDtypeStruct((B, N_HEADS, S, HEAD_W), BF16),
                   jax.ShapeDtypeStruct((B, N_HEADS, S // TK, V_ROWS, TK), BF16)],
        scratch_shapes=[pltpu.VMEM((POOL_HALO, D_POOL), F32)],
        compiler_params=pltpu.CompilerParams(
            dimension_semantics=("arbitrary", "arbitrary"), vmem_limit_bytes=VMEM_LIMIT),
        name="pre",
    )(x, pos, *consts)


def _attn_kernel(bounded_ref, q_ref, k_ref, vt_ref, o_ref, *chain_scratch):
    S = q_ref.shape[2]
    nq = S // TQ
    per_q = TQ // TK
    assert per_q == 2, "two online-softmax chains take the even / odd key tiles"
    chains = (chain_scratch[0:2], chain_scratch[2:4])
    score_bufs = (chain_scratch[4:6], chain_scratch[6:8])
    key_idx = lax.broadcasted_iota(I32, (TK, TQ), 0)
    qry_idx = lax.broadcasted_iota(I32, (TK, TQ), 1)

    diag_masks = (key_idx <= qry_idx, key_idx + TK <= qry_idx)

    def weighted_values(kt, p):
        return jnp.dot(vt_ref[0, 0, kt], p.astype(BF16), preferred_element_type=F32)

    def write_out(qi, acc):
        out = acc[:V_DIM, :] / acc[V_DIM:V_DIM + 1, :]
        start = qi * TQ if isinstance(qi, int) else pl.multiple_of(qi * TQ, TQ)
        o_ref[0, pl.ds(start, TQ), :] = out.T.astype(BF16)

    def bounded_head():
        steps = [(qi, j) for qi in range(nq) for j in range(qi + 1)]

        def pair_scores(step, buf):
            qi, j = step
            q = q_ref[0, 0, qi * TQ:(qi + 1) * TQ, :]
            for c in range(per_q):
                kt = per_q * j + c
                buf[c][...] = lax.dot_general(k_ref[0, 0, kt * TK:(kt + 1) * TK, :], q,
                                              (((1,), (1,)), ((), ())),
                                              preferred_element_type=F32)

        pair_scores(steps[0], score_bufs[0])
        acc = None
        for t, (qi, j) in enumerate(steps):
            if t + 1 < len(steps):
                pair_scores(steps[t + 1], score_bufs[(t + 1) % 2])
            for c in range(per_q):
                st = score_bufs[t % 2][c][...]
                if j == qi:
                    st = jnp.where(diag_masks[c], st, NEG_BIG)
                pv = weighted_values(per_q * j + c, jnp.exp2(st))
                acc = pv if acc is None else acc + pv
            if j == qi:
                write_out(qi, acc)
                acc = None

    def online_q_tile(qi, _):
        q = q_ref[0, 0, pl.ds(pl.multiple_of(qi * TQ, TQ), TQ), :]
        for m_ref, acc_ref in chains:
            m_ref[...] = jnp.full(m_ref.shape, NEG_BIG, F32)
            acc_ref[...] = jnp.zeros(acc_ref.shape, F32)

        def scores(kt):
            k = k_ref[0, 0, pl.ds(pl.multiple_of(kt * TK, TK), TK), :]
            return lax.dot_general(k, q, (((1,), (1,)), ((), ())), preferred_element_type=F32)

        def fold(chain, st, kt, mask):
            m_ref, acc_ref = chain
            if mask is not None:
                st = jnp.where(mask, st, NEG_BIG)
            m = m_ref[...]
            m_new = jnp.maximum(m, jnp.max(st, axis=0, keepdims=True))
            m_ref[...] = m_new
            acc_ref[...] = (jnp.exp2(m - m_new) * acc_ref[...]
                            + weighted_values(kt, jnp.exp2(st - m_new)))

        def pair_scores(j, buf):
            for c in range(per_q):
                buf[c][...] = scores(per_q * j + c)

        def fold_pair(j, buf, masks=(None, None)):
            for c in range(per_q):
                fold(chains[c], buf[c][...], per_q * j + c, masks[c])

        pair_scores(0, score_bufs[0])

        def two_pairs(i, _):
            pair_scores(2 * i + 1, score_bufs[1])
            fold_pair(2 * i, score_bufs[0])
            pair_scores(2 * i + 2, score_bufs[0])
            fold_pair(2 * i + 1, score_bufs[1])
            return 0
        lax.fori_loop(0, qi // 2, two_pairs, 0)

        @pl.when(qi % 2 == 0)
        def _():
            fold_pair(qi, score_bufs[0], diag_masks)

        @pl.when(qi % 2 == 1)
        def _():
            pair_scores(qi, score_bufs[1])
            fold_pair(qi - 1, score_bufs[0])
            fold_pair(qi, score_bufs[1], diag_masks)

        (m0, acc0), (m1, acc1) = chains
        m = jnp.maximum(m0[...], m1[...])
        write_out(qi, jnp.exp2(m0[...] - m) * acc0[...] + jnp.exp2(m1[...] - m) * acc1[...])
        return 0

    pl.when(bounded_ref[0] != 0)(bounded_head)

    @pl.when(bounded_ref[0] == 0)
    def _():
        lax.fori_loop(0, nq, online_q_tile, 0)


def _attn_call(bounded, q, k, vt):
    B, H, S, _ = q.shape
    head = lambda w: pl.BlockSpec((1, 1, S, w), lambda b, h, flag: (b, h, 0, 0))
    grid_spec = pltpu.PrefetchScalarGridSpec(
        num_scalar_prefetch=1,
        grid=(B, H),
        in_specs=[head(HEAD_W), head(HEAD_W),
                  pl.BlockSpec((1, 1) + vt.shape[2:], lambda b, h, flag: (b, h, 0, 0, 0))],
        out_specs=pl.BlockSpec((1, S, V_DIM), lambda b, h, flag: (b, 0, h)),
        scratch_shapes=[pltpu.VMEM((1, TQ), F32), pltpu.VMEM((V_ROWS, TQ), F32)] * 2
        + [pltpu.VMEM((TK, TQ), F32)] * 4,
    )
    return pl.pallas_call(
        _attn_kernel,
        grid_spec=grid_spec,
        out_shape=jax.ShapeDtypeStruct((B, S, H * V_DIM), BF16),
        compiler_params=pltpu.CompilerParams(
            dimension_semantics=("arbitrary", "arbitrary"), vmem_limit_bytes=VMEM_LIMIT),
        name="attn",
    )(bounded, q, k, vt)


def _post_kernel(x_ref, yp_ref, ya_ref, wo_hbm, g2_ref, wr_ref, br_ref,
                 h1_ref, xn_ref, lg_ref, wo_ref, landing, sem):
    _load_weight_as_bf16(wo_hbm, wo_ref, landing, sem)
    for i in range(x_ref.shape[0] // POST_SUB):
        rs = pl.ds(i * POST_SUB, POST_SUB)
        h1 = (x_ref[rs, :]
              + jnp.dot(yp_ref[rs, :], wo_ref[:D_POOL, :], preferred_element_type=F32)
              + jnp.dot(ya_ref[rs, :], wo_ref[D_POOL:, :], preferred_element_type=F32))
        h1_ref[rs, :] = h1
        xn = _rms(h1, g2_ref[...])
        _store_packed_rows(xn_ref, i * POST_SUB, xn)
        xn_hi = xn.astype(BF16)
        xn_lo = (xn - xn_hi.astype(F32)).astype(BF16)
        both = (jnp.dot(xn_hi, wr_ref[...], preferred_element_type=F32)
                + jnp.dot(xn_lo, wr_ref[...], preferred_element_type=F32))
        lg_ref[rs, :] = both[:, :LANES] + both[:, LANES:] + br_ref[...]


def _post_call(x, yp, ya, wo, g2, wr, br):
    T, D = x.shape
    tm = TM_POST
    row = lambda w: pl.BlockSpec((tm, w), lambda i: (i, 0))
    return pl.pallas_call(
        _post_kernel,
        grid=(T // tm,),
        in_specs=[row(D), row(D_POOL), row(D_ATTN), pl.BlockSpec(memory_space=pl.ANY),
                  _const_spec(g2.shape), _const_spec(wr.shape), _const_spec(br.shape)],
        out_specs=[row(D), pl.BlockSpec((tm * PACK_SUBLANES, LANES), lambda i: (i, 0)),
                   row(LANES)],
        out_shape=[jax.ShapeDtypeStruct((T, D), F32),
                   jax.ShapeDtypeStruct((T * PACK_SUBLANES, LANES), jnp.uint32),
                   jax.ShapeDtypeStruct((T, LANES), F32)],
        scratch_shapes=_bf16_weight_scratch(wo.shape),
        compiler_params=pltpu.CompilerParams(
            dimension_semantics=("arbitrary",), vmem_limit_bytes=VMEM_LIMIT),
        name="post",
    )(x, yp, ya, wo, g2, wr, br)


def _route_kernel(lg_ref, dest_ref, wts_ref, cnt_ref, carry_ref, code_ref, start_ref):
    n_tiles = code_ref.shape[0]
    g = pl.program_id(0)

    @pl.when(g == 0)
    def _():
        carry_ref[...] = jnp.zeros_like(carry_ref)

    @pl.when(g < n_tiles)
    def _():
        for i in range(lg_ref.shape[0] // ROUTE_SUB):
            rs = pl.ds(i * ROUTE_SUB, ROUTE_SUB)
            code_t, wts_t = _route_cols(lg_ref[rs, :].T, carry_ref)
            code_ref[g, :, rs] = code_t
            wts_cols = jnp.concatenate(
                [wts_t, jnp.zeros((LANES - PACK_SUBLANES, ROUTE_SUB), F32)], axis=0)
            wts_ref[rs, :] = wts_cols.T
        cnt_ref[...] = carry_ref[...].astype(I32)

    @pl.when(g == n_tiles)
    def _():
        blocks = jnp.floor((carry_ref[...] + (MOE_BLK - 1)) * (1.0 / MOE_BLK))
        r = lax.broadcasted_iota(I32, (LANES, LANES), 0)
        c = lax.broadcasted_iota(I32, (LANES, LANES), 1)
        earlier = jnp.where(c < r, 1.0, 0.0).astype(BF16)
        before = jnp.dot(earlier, jnp.broadcast_to(blocks, (LANES, LANES)).astype(BF16),
                         preferred_element_type=F32)
        start_ref[...] = before[:, 0:1] * MOE_BLK

    @pl.when(g >= n_tiles)
    def _():
        code = code_ref[g - n_tiles]
        expert = lax.shift_right_logical(code, 16)
        rank = code & 0xFFFF
        expert_row = lax.broadcasted_iota(I32, (LANES, code.shape[1]), 0)
        sub = lax.broadcasted_iota(I32, code.shape, 0)
        dest = jnp.zeros(code.shape, I32)
        for choice in range(TOP_K):
            hit = expert_row == expert[choice:choice + 1, :]
            start = jnp.sum(jnp.where(hit, start_ref[...], 0.0), axis=0, keepdims=True)
            dest = jnp.where(sub == choice, start.astype(I32) + rank, dest)
        dest_ref[...] = dest


def _route_call(logits):
    T = logits.shape[0]
    tm = TM_ROUTE
    n_tiles = T // tm
    pass0 = lambda g: jnp.minimum(g, n_tiles - 1)
    pass1 = lambda g: jnp.maximum(g - n_tiles, 0)
    return pl.pallas_call(
        _route_kernel,
        grid=(2 * n_tiles,),
        in_specs=[pl.BlockSpec((tm, LANES), lambda g: (pass0(g), 0))],
        out_specs=[pl.BlockSpec((PACK_SUBLANES, tm), lambda g: (0, pass1(g))),
                   pl.BlockSpec((tm, LANES), lambda g: (pass0(g), 0)),
                   pl.BlockSpec((LANES, 1), lambda g: (0, 0))],
        out_shape=[jax.ShapeDtypeStruct((PACK_SUBLANES, T), I32),
                   jax.ShapeDtypeStruct((T, LANES), F32),
                   jax.ShapeDtypeStruct((LANES, 1), I32)],
        scratch_shapes=[pltpu.VMEM((LANES, 1), F32),
                        pltpu.VMEM((n_tiles, PACK_SUBLANES, tm), I32),
                        pltpu.VMEM((LANES, 1), F32)],
        compiler_params=pltpu.CompilerParams(
            dimension_semantics=("arbitrary",), vmem_limit_bytes=VMEM_LIMIT),
        name="route",
    )(logits)


def _route_cols(lt, carry_ref):
    tm = lt.shape[1]
    sub = lax.broadcasted_iota(I32, (PACK_SUBLANES, tm), 0).astype(F32)

    def first_argmax(vals):
        mx = jnp.max(vals, axis=0, keepdims=True)
        idx = jnp.min(jnp.where(vals == mx, sub, float(PACK_SUBLANES)), axis=0, keepdims=True)
        return mx, idx

    g_logits = lt[0:N_GROUPS, :]
    g_max, g_sel = first_argmax(g_logits)
    g_w = 1.0 / jnp.sum(jnp.exp(g_logits - g_max), axis=0, keepdims=True)

    e_logits = lt[N_GROUPS:N_GROUPS + EXPERTS_PER_GROUP, :]
    for g in range(1, N_GROUPS):
        lo = N_GROUPS + g * EXPERTS_PER_GROUP
        e_logits = jnp.where(g_sel == float(g), lt[lo:lo + EXPERTS_PER_GROUP, :], e_logits)
    v1, i1 = first_argmax(e_logits)
    v2, i2 = first_argmax(jnp.where(sub == i1, -jnp.inf, e_logits))
    e1 = (g_sel * EXPERTS_PER_GROUP + i1).astype(I32)
    e2 = (g_sel * EXPERTS_PER_GROUP + i2).astype(I32)
    t = jnp.exp(v2 - v1)
    w1 = g_w / (1.0 + t)
    w2 = g_w * t / (1.0 + t)

    expert_row = lax.broadcasted_iota(I32, (LANES, tm), 0)
    hit1 = expert_row == e1
    hit2 = expert_row == e2
    onehot = jnp.where(hit1 | hit2, 1.0, 0.0)
    r = lax.broadcasted_iota(I32, (tm, tm), 0)
    c = lax.broadcasted_iota(I32, (tm, tm), 1)
    earlier = jnp.where(r < c, 1.0, 0.0).astype(BF16)
    before = (jnp.dot(onehot.astype(BF16), earlier, preferred_element_type=F32)
              + carry_ref[...])
    carry_ref[...] += jnp.sum(onehot, axis=1, keepdims=True)
    pos1 = jnp.sum(jnp.where(hit1, before, 0.0), axis=0, keepdims=True).astype(I32)
    pos2 = jnp.sum(jnp.where(hit2, before, 0.0), axis=0, keepdims=True).astype(I32)
    code1 = e1 * 65536 + pos1
    code2 = e2 * 65536 + pos2
    first, second = sub == 0.0, sub == 1.0
    return (jnp.where(first, code1, jnp.where(second, code2, 0)),
            jnp.where(first, w1, jnp.where(second, w2, 0.0)))


def _plan_kernel(dest_ref, cnt_ref, unused_hbm, slots_hbm, blke_ref, blkn_ref, blkw_ref,
                 rune_ref, slot_ref, sem):
    n_assign = dest_ref.shape[0]
    n_blocks = blke_ref.shape[0]
    load = pltpu.make_async_copy(unused_hbm, slot_ref, sem)
    load.start()

    def per_expert(e, nb_done):
        cnt = cnt_ref[e]
        nb = lax.shift_right_logical(cnt + (MOE_BLK - 1), MOE_BLK.bit_length() - 1)

        def per_block(j, _):
            blke_ref[nb_done + j] = e
            blkn_ref[nb_done + j] = jnp.minimum(cnt - j * MOE_BLK, MOE_BLK)
            return 0
        lax.fori_loop(0, nb, per_block, 0)
        return nb_done + nb
    used = lax.fori_loop(0, N_EXPERTS, per_expert, 0)

    last_e = blke_ref[jnp.maximum(used - 1, 0)]

    def tail(b, _):
        blke_ref[b] = last_e
        blkn_ref[b] = 0
        blkw_ref[b] = 0
        return 0
    lax.fori_loop(used, n_blocks, tail, 0)

    def no_run(r, _):
        rune_ref[r] = -1
        return 0
    lax.fori_loop(0, rune_ref.shape[0], no_run, 0)

    def runs_forward(b, carry):
        earlier_e, run = carry
        e = blke_ref[b]
        first = (e != earlier_e).astype(I32)
        run = run + first
        blkw_ref[b] = first + run * 2
        rune_ref[run] = e
        return e, run
    lax.fori_loop(0, used, runs_forward, (-1, -1))

    load.wait()

    def place(a, _):
        slot_ref[dest_ref[a]] = a
        return 0
    lax.fori_loop(0, n_assign, place, 0, unroll=PLAN_UNROLL)

    store = pltpu.make_async_copy(slot_ref, slots_hbm, sem)
    store.start()
    store.wait()


def _plan_call(dest_flat, counts, n_tab):
    n_assign = dest_flat.shape[0]
    n_slots = n_tab * MOE_BLK
    unused = (n_assign + (np.arange(n_slots) & (MOE_BLK - 1))).astype(np.int32)
    smem = pl.BlockSpec(memory_space=pltpu.SMEM)
    hbm = pl.BlockSpec(memory_space=pl.ANY)
    return pl.pallas_call(
        _plan_kernel,
        in_specs=[smem, smem, hbm],
        out_specs=[hbm, smem, smem, smem, smem],
        out_shape=[jax.ShapeDtypeStruct((n_slots,), I32),
                   jax.ShapeDtypeStruct((n_tab,), I32),
                   jax.ShapeDtypeStruct((n_tab,), I32),
                   jax.ShapeDtypeStruct((n_tab,), I32),
                   jax.ShapeDtypeStruct((N_EXPERTS + WEIGHT_SLOTS,), I32)],
        scratch_shapes=[pltpu.SMEM((n_slots,), I32), pltpu.SemaphoreType.DMA],
        name="plan",
    )(dest_flat, counts, unused)


def _expert_kernel(blke_ref, blkn_ref, blkw_ref, rune_ref, slot_g_ref, slot_s_ref, xn_hbm,
                   wg_hbm, wu_hbm, wd_hbm, y_hbm, x0, x1, y0, y1, wg_buf, wu_buf, wd_buf,
                   gsem, ssem, wsem):
    xbufs, ybufs = (x0, x1), (y0, y1)
    s = pl.program_id(0)
    n_tok = xn_hbm.shape[0] // PACK_SUBLANES

    def weight_copies(e, slot):
        return [pltpu.make_async_copy(hbm.at[e], buf.at[slot], wsem.at[slot])
                for hbm, buf in ((wg_hbm, wg_buf), (wu_hbm, wu_buf), (wd_hbm, wd_buf))]

    blk = jnp.maximum(s - 1, 0)
    run_info = blkw_ref[blk]
    run = run_info >> 1
    w_slot = lax.rem(run, WEIGHT_SLOTS)
    ahead_run = run + (WEIGHT_SLOTS - 1)
    ahead_expert = rune_ref[ahead_run]
    ahead_slot = lax.rem(ahead_run, WEIGHT_SLOTS)

    def tile(ref, r):
        start = r * PACK_SUBLANES
        if not isinstance(r, int):
            start = pl.multiple_of(start, PACK_SUBLANES)
        return ref.at[pl.ds(start, PACK_SUBLANES)]

    def rows(j):
        n = blkn_ref[jnp.clip(j, 0, blkn_ref.shape[0] - 1)]
        return jnp.where(j >= 0, (n + (ROW_GROUP - 1)) & ~(ROW_GROUP - 1), 0)
    rows0, rows1, rows2, rows3 = rows(s), rows(s - 1), rows(s - 2), rows(s - 3)

    def stage(nxt):
        cur = 1 - nxt
        x_nxt, x_cur, y_nxt, y_cur = xbufs[nxt], xbufs[cur], ybufs[nxt], ybufs[cur]

        def per_group(n_rows, issue_row):
            for g in range(MOE_BLK // ROW_GROUP):
                @pl.when(g * ROW_GROUP < n_rows)
                def _():
                    for r in range(g * ROW_GROUP, (g + 1) * ROW_GROUP):
                        issue_row(r)

        def gather_row(r):
            tok = slot_g_ref[0, 0, r] & (n_tok - 1)
            pltpu.make_async_copy(tile(xn_hbm, tok), tile(x_nxt, r), gsem.at[nxt]).start()

        def scatter_row(r):
            pltpu.make_async_copy(tile(y_nxt, r), tile(y_hbm, slot_s_ref[0, 0, r]),
                                  ssem.at[nxt]).start()

        def wait_rows(src, dst, sem, n_rows):
            n_words = pl.multiple_of(n_rows * PACK_SUBLANES, ROW_GROUP * PACK_SUBLANES)

            @pl.when(n_rows > 0)
            def _():
                pltpu.make_async_copy(src.at[pl.ds(0, n_words)], dst.at[pl.ds(0, n_words)],
                                      sem).wait()

        def compute():
            xb = _load_packed_rows(x_cur, 0, MOE_BLK).astype(BF16)
            a = jnp.dot(xb, wg_buf[w_slot].astype(BF16), preferred_element_type=F32)
            u = jnp.dot(xb, wu_buf[w_slot].astype(BF16), preferred_element_type=F32)
            hmid = (a * jax.nn.sigmoid(a) * u).astype(BF16)
            _store_packed_rows(y_cur, 0, jnp.dot(hmid, wd_buf[w_slot].astype(BF16),
                                                 preferred_element_type=F32))

        wait_rows(xn_hbm, x_cur, gsem.at[cur], rows1)
        wait_rows(y_cur, y_hbm, ssem.at[cur], rows3)

        @pl.when((rows1 > 0) & ((run_info & 1) == 1))
        def _():
            for copy in weight_copies(blke_ref[blk], w_slot):
                copy.wait()

            @pl.when(ahead_expert >= 0)
            def _():
                for copy in weight_copies(ahead_expert, ahead_slot):
                    copy.start(priority=WEIGHT_DMA_PRIORITY)

        if nxt == 0:
            for first_run in range(WEIGHT_SLOTS - 1):
                @pl.when((s == 0) & (rune_ref[first_run] >= 0))
                def _():
                    for copy in weight_copies(rune_ref[first_run], first_run):
                        copy.start(priority=WEIGHT_DMA_PRIORITY)

            @pl.when(s == 0)
            def _():
                x_nxt[...] = jnp.zeros(x_nxt.shape, x_nxt.dtype)
                x_cur[...] = jnp.zeros(x_cur.shape, x_cur.dtype)
                y_nxt[...] = jnp.zeros(y_nxt.shape, y_nxt.dtype)
                dump = pltpu.make_async_copy(
                    y_nxt, y_hbm.at[pl.ds(TOP_K * n_tok * PACK_SUBLANES,
                                          MOE_BLK * PACK_SUBLANES)], ssem.at[nxt])
                dump.start()
                dump.wait()

        per_group(rows0, gather_row)
        per_group(rows2, scatter_row)

        @pl.when(rows1 > 0)
        def _():
            compute()

    for parity in range(2):
        pl.when((s & 1) == parity)(functools.partial(stage, parity))


def _expert_call(blk_e, blk_n, blk_w, run_e, slots, xn, wg, wu, wd):
    T = xn.shape[0] // PACK_SUBLANES
    assert T & (T - 1) == 0, "token id is recovered from the assignment id with a mask"
    n_tab = blk_e.shape[0]
    _, D, F = wg.shape
    packed_block = pltpu.VMEM((MOE_BLK * PACK_SUBLANES, LANES), jnp.uint32)
    hbm = pl.BlockSpec(memory_space=pl.ANY)

    def used_block(j, blk_n_ref):
        j = jnp.maximum(j, 0)
        return jnp.where(blk_n_ref[j] > 0, j, 0)

    grid_spec = pltpu.PrefetchScalarGridSpec(
        num_scalar_prefetch=4,
        grid=(n_tab,),
        in_specs=[
            pl.BlockSpec((1, 1, MOE_BLK), lambda s, be, bn, *_: (used_block(s, bn), 0, 0),
                         memory_space=pltpu.SMEM),
            pl.BlockSpec((1, 1, MOE_BLK), lambda s, be, bn, *_: (used_block(s - 2, bn), 0, 0),
                         memory_space=pltpu.SMEM),
            hbm, hbm, hbm, hbm,
        ],
        out_specs=hbm,
        scratch_shapes=[packed_block] * 4
        + [pltpu.VMEM((WEIGHT_SLOTS, D, F), F32), pltpu.VMEM((WEIGHT_SLOTS, D, F), F32),
           pltpu.VMEM((WEIGHT_SLOTS, F, D), F32)]
        + [pltpu.SemaphoreType.DMA((2,)), pltpu.SemaphoreType.DMA((2,)),
           pltpu.SemaphoreType.DMA((WEIGHT_SLOTS,))],
    )
    slots3 = slots.reshape(n_tab, 1, MOE_BLK)
    return pl.pallas_call(
        _expert_kernel,
        grid_spec=grid_spec,
        out_shape=jax.ShapeDtypeStruct(((T * TOP_K + MOE_BLK) * PACK_SUBLANES, LANES),
                                       jnp.uint32),
        compiler_params=pltpu.CompilerParams(
            dimension_semantics=("arbitrary",), vmem_limit_bytes=VMEM_LIMIT,
            disable_bounds_checks=True),
        name="experts",
    )(blk_e, blk_n, blk_w, run_e, slots3, slots3, xn, wg, wu, wd)


def _final_kernel(h1_ref, y1_ref, y2_ref, wts_ref, p_ref, wpp_ref, pg_ref, g3_ref, wpg_hbm,
                  o_ref, wpg_ref, landing, sem):
    _load_weight_as_bf16(wpg_hbm, wpg_ref, landing, sem)
    for i in range(h1_ref.shape[0] // FINAL_SUB):
        rs = pl.ds(i * FINAL_SUB, FINAL_SUB)
        w = wts_ref[rs, :]
        h2 = (h1_ref[rs, :]
              + w[:, 0:1] * _load_packed_rows(y1_ref, i * FINAL_SUB, FINAL_SUB)
              + w[:, 1:2] * _load_packed_rows(y2_ref, i * FINAL_SUB, FINAL_SUB))
        e = _rms(jnp.dot(p_ref[rs, :].astype(BF16), wpp_ref[...], preferred_element_type=F32),
                 pg_ref[...])
        gate = jax.nn.sigmoid(jnp.dot(_rms(h2, g3_ref[...]).astype(BF16), wpg_ref[...],
                                      preferred_element_type=F32))
        o_ref[rs, :] = h2 + gate * e


def _final_call(h1, y, wts, p, wpp, pg, g3, wpg):
    T, D = h1.shape
    tm = TM_FINAL
    row = lambda w: pl.BlockSpec((tm, w), lambda i: (i, 0))
    first = pl.BlockSpec((tm * PACK_SUBLANES, LANES), lambda i: (i, 0))
    second = pl.BlockSpec((tm * PACK_SUBLANES, LANES), lambda i: (i + T // tm, 0))
    return pl.pallas_call(
        _final_kernel,
        grid=(T // tm,),
        in_specs=[row(D), first, second, row(LANES), row(PLE_DIM), _const_spec(wpp.shape),
                  _const_spec(pg.shape), _const_spec(g3.shape),
                  pl.BlockSpec(memory_space=pl.ANY)],
        out_specs=row(D),
        out_shape=jax.ShapeDtypeStruct((T, D), F32),
        scratch_shapes=_bf16_weight_scratch(wpg.shape),
        compiler_params=pltpu.CompilerParams(
            dimension_semantics=("arbitrary",), vmem_limit_bytes=VMEM_LIMIT),
        name="final",
    )(h1, y, y, wts, p, wpp, pg, g3, wpg)


def _rope_lanes(a):
    z = jnp.zeros(a.shape[:-1] + (ROPE_HALF,), a.dtype)
    return jnp.concatenate([a[..., :ROPE_HALF], z, a[..., ROPE_HALF:], z], axis=-1)


def _head_lanes(a):
    return jnp.concatenate([a[..., :QK_NOPE], _rope_lanes(a[..., QK_NOPE:])], axis=-1)


def _score_bound(q_gain, k_gain):
    return Q_PRESCALE * QK_DIM * jnp.max(jnp.abs(q_gain)) * jnp.max(jnp.abs(k_gain))


def kernel(x, p, positions, norm1_gain, w_in, q_a_gain, w_q_b, kv_a_gain, w_kv_b, q_norm_gain, k_norm_gain, w_pool, pool_scale, w_out, norm2_gain, w_router_group, b_router_group, w_router_expert, b_router_expert, w_exp_gate, w_exp_up, w_exp_down, norm3_gain, w_ple_gate, w_ple_proj, ple_norm_gain):
    B, S, D = x.shape
    T = B * S
    assert x.shape[2] == D_MODEL and S % TQ == 0 and S % TM_PRE == 0
    assert T % TM_POST == 0 and T % TM_FINAL == 0
    layer = 0
    row = lambda a: a[layer].reshape(1, -1)

    win = _win_call(w_in[layer].T)
    wq = _head_lanes(w_q_b[layer].reshape(Q_LORA, N_HEADS, QK_DIM)
                     ).reshape(Q_LORA, N_HEADS * HEAD_W).astype(BF16)
    wkv3 = w_kv_b[layer].reshape(KV_LORA, N_HEADS, QK_NOPE + V_DIM)
    wkv = jnp.concatenate([wkv3[..., :QK_NOPE].reshape(KV_LORA, -1),
                           wkv3[..., QK_NOPE:].reshape(KV_LORA, -1)], axis=1).astype(BF16)
    qng = _head_lanes(q_norm_gain[layer]).reshape(1, HEAD_W)
    kng = _head_lanes(k_norm_gain[layer]).reshape(1, HEAD_W)
    inv_freq = np.float32(ROPE_THETA) ** (-np.arange(ROPE_HALF, dtype=np.float32) / ROPE_HALF)
    zeros_half = np.zeros((ROPE_HALF,), np.float32)
    ones_half = np.ones((ROPE_HALF,), np.float32)
    invf = np.concatenate([inv_freq, zeros_half, inv_freq, zeros_half]).reshape(1, LANES)
    sgn = np.concatenate([-ones_half, zeros_half, ones_half, zeros_half]).reshape(1, LANES)
    spare = np.arange(LANES) == SPARE_ROPE_LANE
    kpad = spare.astype(np.float32).reshape(1, LANES)
    pad_lanes = LANES - N_GROUPS - N_EXPERTS
    wr = jnp.concatenate([w_router_group[layer], w_router_expert[layer],
                          jnp.zeros((D, pad_lanes), F32)], axis=1)
    wr_hi = wr.astype(BF16)
    wr = jnp.concatenate([wr_hi, (wr - wr_hi.astype(F32)).astype(BF16)], axis=1)
    br = jnp.concatenate([b_router_group[layer], b_router_expert[layer],
                          jnp.zeros((pad_lanes,), F32)]).reshape(1, LANES)

    bound = _score_bound(q_norm_gain[layer], k_norm_gain[layer])
    bounded = bound <= SCORE_BOUND_LIMIT
    qpad = jnp.where(bounded, -bound, 0.0) * kpad
    ypool, q, k, vt = _pre_call(
        x, positions.reshape(B, S, 1), row(norm1_gain), win, row(q_a_gain), wq,
        row(kv_a_gain), wkv, qng, kng, w_pool[layer].astype(BF16), row(pool_scale), invf, sgn,
        qpad, kpad)
    yattn = _attn_call(bounded.astype(I32).reshape(1), q, k, vt)
    h1, xn2, logits = _post_call(
        x.reshape(T, D), ypool.reshape(T, D_POOL), yattn.reshape(T, D_ATTN),
        w_out[layer], row(norm2_gain), wr, br)
    dest, wts, counts = _route_call(logits)

    n_assign = T * TOP_K
    n_blocks = (n_assign + N_EXPERTS * (MOE_BLK - 1)) // MOE_BLK
    slots, blk_e, blk_n, blk_w, run_e = _plan_call(
        dest[:TOP_K].reshape(n_assign), counts.reshape(LANES), n_blocks + MOE_DRAIN_STEPS)
    y = _expert_call(blk_e, blk_n, blk_w, run_e, slots, xn2,
                     w_exp_gate[layer], w_exp_up[layer], w_exp_down[layer])

    out = _final_call(h1, y, wts, p[layer].reshape(T, PLE_DIM),
                      w_ple_proj[layer].astype(BF16), row(ple_norm_gain), row(norm3_gain),
                      w_ple_gate[layer])
    return out.reshape(B, S, D)
```

```python
import functools
import math

import jax
import jax.numpy as jnp
import numpy as np
from jax import lax
from jax.experimental import pallas as pl
from jax.experimental.pallas import tpu as pltpu

F32 = jnp.float32
BF16 = jnp.bfloat16
I32 = jnp.int32

D_MODEL = 2048
PLE_DIM = 256
EPS = 1e-6
POOL_WINDOWS = (2, 4, 8, 16)
POOL_CH = 256
D_POOL = POOL_CH * len(POOL_WINDOWS)
N_HEADS = 8
Q_LORA = 512
KV_LORA = 512
QK_NOPE = 128
QK_ROPE = 64
QK_DIM = QK_NOPE + QK_ROPE
V_DIM = 128
D_ATTN = N_HEADS * V_DIM
ROPE_THETA = 10000.0
ATTN_SCALE = 1.0 / math.sqrt(QK_DIM)
Q_PRESCALE = ATTN_SCALE * math.log2(math.e)
N_GROUPS = 8
EXPERTS_PER_GROUP = 8
N_EXPERTS = N_GROUPS * EXPERTS_PER_GROUP
TOP_K = 2

LANES = 128
PACK_SUBLANES = 8
VMEM_LIMIT = 56 * 1024 * 1024

HEAD_W = 2 * LANES
ROPE_HALF = QK_ROPE // 2
POOL_HALO = 16
TM_PRE = 512
PRE_SUB = 256
TM_POST = 512
POST_SUB = 256
TM_FINAL = 512
FINAL_SUB = 256
WIN_ROWS = 256
CAST_ROWS = 512
TQ = 512
TK = 256
TM_ROUTE = 1024
ROUTE_SUB = 256
MOE_BLK = 256
MOE_DRAIN_STEPS = 3
ROW_GROUP = 32
WEIGHT_DMA_PRIORITY = 1
WEIGHT_SLOTS = 3
PLAN_UNROLL = 32
NEG_BIG = -1e30
V_ROWS = V_DIM + 16
SPARE_ROPE_LANE = ROPE_HALF
SCORE_BOUND_LIMIT = 50.0


def _const_spec(shape):
    nd = len(shape)
    return pl.BlockSpec(shape, lambda *_: (0,) * nd, pipeline_mode=pl.Buffered(1))


def _rms(x, gain):
    return x * lax.rsqrt(jnp.mean(x * x, axis=-1, keepdims=True) + EPS) * gain


def _load_weight_as_bf16(w_hbm, w_bf16, landing, sem):
    n_chunks = w_hbm.shape[0] // CAST_ROWS

    def chunk(c):
        return pltpu.make_async_copy(w_hbm.at[pl.ds(c * CAST_ROWS, CAST_ROWS)],
                                     landing.at[c % 2], sem.at[c % 2])

    @pl.when(pl.program_id(0) == 0)
    def _():
        chunk(0).start()
        for c in range(n_chunks):
            if c + 1 < n_chunks:
                chunk(c + 1).start()
            chunk(c).wait()
            w_bf16[pl.ds(c * CAST_ROWS, CAST_ROWS), :] = landing[c % 2].astype(BF16)


def _bf16_weight_scratch(shape):
    rows, cols = shape
    assert rows % CAST_ROWS == 0
    return [pltpu.VMEM((rows, cols), BF16), pltpu.VMEM((2, CAST_ROWS, cols), F32),
            pltpu.SemaphoreType.DMA((2,))]


def _store_packed_rows(ref, first_row, x):
    rows, d = x.shape
    half = d // 2
    assert half == PACK_SUBLANES * LANES
    for i in range(PACK_SUBLANES):
        lo = x[:, i * LANES:(i + 1) * LANES].astype(BF16).astype(F32)
        hi = x[:, half + i * LANES:half + (i + 1) * LANES].astype(BF16).astype(F32)
        word = (lax.shift_right_logical(pltpu.bitcast(lo, jnp.uint32), jnp.uint32(16))
                | (pltpu.bitcast(hi, jnp.uint32) & jnp.uint32(0xFFFF0000)))
        ref[pl.ds(first_row * PACK_SUBLANES + i, rows, stride=PACK_SUBLANES), :] = word


def _load_packed_rows(ref, first_row, rows):
    lo, hi = [], []
    for i in range(PACK_SUBLANES):
        word = ref[pl.ds(first_row * PACK_SUBLANES + i, rows, stride=PACK_SUBLANES), :]
        lo.append(pltpu.bitcast(lax.shift_left(word, jnp.uint32(16)), F32))
        hi.append(pltpu.bitcast(word & jnp.uint32(0xFFFF0000), F32))
    return jnp.concatenate(lo + hi, axis=1)


def _pre_kernel(x_ref, pos_ref, g1_ref, win_ref, qag_ref, wq_ref, kvag_ref, wkv_ref,
                qng_ref, kng_ref, wpool_ref, pscale_ref, invf_ref, sgn_ref, qpad_ref, kpad_ref,
                ypool_ref, q_ref, k_ref, vt_ref, carry_ref):
    st = pl.program_id(1)
    tm = PRE_SUB
    n_sub = x_ref.shape[0] // tm

    @pl.when(st == 0)
    def _():
        carry_ref[...] = jnp.zeros_like(carry_ref)

    zs = [_pre_project(pl.ds(i * tm, tm), x_ref, g1_ref, win_ref) for i in range(n_sub)]
    halo = carry_ref[...]
    mixed = []
    for i in range(n_sub):
        first_pos = st * x_ref.shape[0] + i * tm
        qa, kv, halo = _pre_mix(pl.ds(i * tm, tm), first_pos, zs[i], halo, qag_ref, wq_ref,
                                kvag_ref, wkv_ref, wpool_ref, pscale_ref, ypool_ref)
        mixed.append((qa, kv))
    carry_ref[...] = halo
    for i in range(n_sub):
        k_rope = zs[i][:, D_POOL + Q_LORA + KV_LORA:]
        _pre_heads(i, *mixed[i], k_rope, pos_ref, qng_ref, kng_ref, invf_ref, sgn_ref,
                   qpad_ref, kpad_ref, q_ref, k_ref, vt_ref)


def _pre_project(rs, x_ref, g1_ref, win_ref):
    hn = _rms(x_ref[rs, :], g1_ref[...])
    return jnp.dot(hn.astype(BF16), win_ref[...], preferred_element_type=F32)


def _pre_mix(rs, first_pos, z, halo, qag_ref, wq_ref, kvag_ref, wkv_ref, wpool_ref, pscale_ref,
             ypool_ref):
    tm = z.shape[0]
    u = z[:, :D_POOL]
    ext = jnp.concatenate([halo, u], axis=0)
    row = lax.broadcasted_iota(I32, (tm, 1), 0) + first_pos
    level = ext
    shift = 1
    for g, w in enumerate(POOL_WINDOWS):
        sl = slice(g * POOL_CH, (g + 1) * POOL_CH)
        while shift < w:
            level = level + pltpu.roll(level, shift, 0)
            shift *= 2
        win_sum = level[POOL_HALO:, sl]
        cnt = jnp.minimum(row + 1, w).astype(F32)
        d = win_sum / cnt - u[:, sl]
        y = jnp.dot(d.astype(BF16), wpool_ref[g], preferred_element_type=F32)
        ypool_ref[rs, sl] = (y * pscale_ref[:, sl]).astype(BF16)

    q_lat = z[:, D_POOL:D_POOL + Q_LORA]
    kv_lat = z[:, D_POOL + Q_LORA:D_POOL + Q_LORA + KV_LORA]
    qa = jnp.dot(_rms(q_lat, qag_ref[...]).astype(BF16), wq_ref[...],
                 preferred_element_type=F32)
    kv = jnp.dot(_rms(kv_lat, kvag_ref[...]).astype(BF16), wkv_ref[...],
                 preferred_element_type=F32)
    return qa, kv, u[tm - POOL_HALO:, :]


def _pre_heads(i, qa, kv, k_rope, pos_ref, qng_ref, kng_ref, invf_ref, sgn_ref, qpad_ref,
               kpad_ref, q_ref, k_ref, vt_ref):
    tm = qa.shape[0]
    rs = pl.ds(i * tm, tm)
    ang = pos_ref[rs, :].astype(F32) * invf_ref[...]
    cos = jnp.cos(ang)
    sin = jnp.sin(ang) * sgn_ref[...]

    def rot(t):
        return t * cos + pltpu.roll(t, LANES // 2, 1) * sin

    qng = qng_ref[...]
    kng = kng_ref[...]
    kr_rot = rot(k_rope * kng[:, LANES:])
    kr_ssq = jnp.sum(k_rope * k_rope, axis=-1, keepdims=True)
    for h in range(N_HEADS):
        qh = qa[:, h * HEAD_W:(h + 1) * HEAD_W]
        rq = lax.rsqrt(jnp.sum(qh * qh, axis=-1, keepdims=True) / QK_DIM + EPS) * Q_PRESCALE
        qn = qh * rq * qng
        q_ref[0, h, rs, :LANES] = qn[:, :LANES].astype(BF16)
        q_ref[0, h, rs, LANES:] = (rot(qn[:, LANES:]) + qpad_ref[...]).astype(BF16)
        kh = kv[:, h * QK_NOPE:(h + 1) * QK_NOPE]
        rk = lax.rsqrt((jnp.sum(kh * kh, axis=-1, keepdims=True) + kr_ssq) / QK_DIM + EPS)
        k_ref[0, h, rs, :LANES] = (kh * rk * kng[:, :LANES]).astype(BF16)
        k_ref[0, h, rs, LANES:] = (kr_rot * rk + kpad_ref[...]).astype(BF16)
        vh = kv[:, N_HEADS * QK_NOPE + h * V_DIM:N_HEADS * QK_NOPE + (h + 1) * V_DIM]
        key_tile, keys = (i * tm) // TK, pl.ds((i * tm) % TK, tm)
        vt_ref[0, h, key_tile, :V_DIM, keys] = vh.T.astype(BF16)
        vt_ref[0, h, key_tile, V_DIM:, keys] = jnp.ones((V_ROWS - V_DIM, tm), BF16)


def _win_kernel(wt_ref, o_ref):
    w = wt_ref[...].T
    n_lat = o_ref.shape[1] - LANES
    zeros = jnp.zeros((w.shape[0], ROPE_HALF), F32)
    rope = jnp.concatenate([w[:, n_lat:n_lat + ROPE_HALF], zeros,
                            w[:, n_lat + ROPE_HALF:], zeros], axis=1)
    o_ref[:, :n_lat] = w[:, :n_lat].astype(BF16)
    o_ref[:, n_lat:] = rope.astype(BF16)


def _win_call(w_in_t):
    n_in, D = w_in_t.shape
    n_lat = n_in - QK_ROPE
    tm = WIN_ROWS
    return pl.pallas_call(
        _win_kernel,
        grid=(D // tm,),
        in_specs=[pl.BlockSpec((n_in, tm), lambda i: (0, i))],
        out_specs=pl.BlockSpec((tm, n_lat + LANES), lambda i: (i, 0)),
        out_shape=jax.ShapeDtypeStruct((D, n_lat + LANES), BF16),
        compiler_params=pltpu.CompilerParams(
            dimension_semantics=("arbitrary",), vmem_limit_bytes=VMEM_LIMIT),
        name="win",
    )(w_in_t)


def _pre_call(x, pos, g1, win, qag, wq, kvag, wkv, qng, kng, wpool, pscale, invf, sgn,
              qpad, kpad):
    B, S, D = x.shape
    tm = TM_PRE
    grid = (B, S // tm)
    row_spec = lambda w: pl.BlockSpec((None, tm, w), lambda b, s: (b, s, 0))
    head_spec = lambda w: pl.BlockSpec((1, N_HEADS, tm, w), lambda b, s: (b, 0, s, 0))
    consts = [g1, win, qag, wq, kvag, wkv, qng, kng, wpool, pscale, invf, sgn, qpad, kpad]
    return pl.pallas_call(
        _pre_kernel,
        grid=grid,
        in_specs=[row_spec(D), row_spec(1)] + [_const_spec(c.shape) for c in consts],
        out_specs=[row_spec(D_POOL), head_spec(HEAD_W), head_spec(HEAD_W),
                   pl.BlockSpec((1, N_HEADS, tm // TK, V_ROWS, TK),
                                lambda b, s: (b, 0, s, 0, 0))],
        out_shape=[jax.ShapeDtypeStruct((B, S, D_POOL), BF16),
                   jax.ShapeDtypeStruct((B, N_HEADS, S, HEAD_W), BF16),
                   jax.ShapeDtypeStruct((B, N_HEADS, S, HEAD_W), BF16),
                   jax.ShapeDtypeStruct((B, N_HEADS, S // TK, V_ROWS, TK), BF16)],
        scratch_shapes=[pltpu.VMEM((POOL_HALO, D_POOL), F32)],
        compiler_params=pltpu.CompilerParams(
            dimension_semantics=("arbitrary", "arbitrary"), vmem_limit_bytes=VMEM_LIMIT),
        name="pre",
    )(x, pos, *consts)


def _attn_kernel(bounded_ref, q_ref, k_ref, vt_ref, o_ref, *chain_scratch):
    S = q_ref.shape[2]
    nq = S // TQ
    per_q = TQ // TK
    assert per_q == 2, "two online-softmax chains take the even / odd key tiles"
    chains = (chain_scratch[0:2], chain_scratch[2:4])
    score_bufs = (chain_scratch[4:6], chain_scratch[6:8])
    key_idx = lax.broadcasted_iota(I32, (TK, TQ), 0)
    qry_idx = lax.broadcasted_iota(I32, (TK, TQ), 1)

    diag_masks = (key_idx <= qry_idx, key_idx + TK <= qry_idx)

    def weighted_values(kt, p):
        return jnp.dot(vt_ref[0, 0, kt], p.astype(BF16), preferred_element_type=F32)

    def write_out(qi, acc):
        out = acc[:V_DIM, :] / acc[V_DIM:V_DIM + 1, :]
        start = qi * TQ if isinstance(qi, int) else pl.multiple_of(qi * TQ, TQ)
        o_ref[0, pl.ds(start, TQ), :] = out.T.astype(BF16)

    def bounded_head():
        steps = [(qi, j) for qi in range(nq) for j in range(qi + 1)]

        def pair_scores(step, buf):
            qi, j = step
            q = q_ref[0, 0, qi * TQ:(qi + 1) * TQ, :]
            for c in range(per_q):
                kt = per_q * j + c
                buf[c][...] = lax.dot_general(k_ref[0, 0, kt * TK:(kt + 1) * TK, :], q,
                                              (((1,), (1,)), ((), ())),
                                              preferred_element_type=F32)

        pair_scores(steps[0], score_bufs[0])
        acc = None
        for t, (qi, j) in enumerate(steps):
            if t + 1 < len(steps):
                pair_scores(steps[t + 1], score_bufs[(t + 1) % 2])
            for c in range(per_q):
                st = score_bufs[t % 2][c][...]
                if j == qi:
                    st = jnp.where(diag_masks[c], st, NEG_BIG)
                pv = weighted_values(per_q * j + c, jnp.exp2(st))
                acc = pv if acc is None else acc + pv
            if j == qi:
                write_out(qi, acc)
                acc = None

    def online_q_tile(qi, _):
        q = q_ref[0, 0, pl.ds(pl.multiple_of(qi * TQ, TQ), TQ), :]
        for m_ref, acc_ref in chains:
            m_ref[...] = jnp.full(m_ref.shape, NEG_BIG, F32)
            acc_ref[...] = jnp.zeros(acc_ref.shape, F32)

        def scores(kt):
            k = k_ref[0, 0, pl.ds(pl.multiple_of(kt * TK, TK), TK), :]
            return lax.dot_general(k, q, (((1,), (1,)), ((), ())), preferred_element_type=F32)

        def fold(chain, st, kt, mask):
            m_ref, acc_ref = chain
            if mask is not None:
                st = jnp.where(mask, st, NEG_BIG)
            m = m_ref[...]
            m_new = jnp.maximum(m, jnp.max(st, axis=0, keepdims=True))
            m_ref[...] = m_new
            acc_ref[...] = (jnp.exp2(m - m_new) * acc_ref[...]
                            + weighted_values(kt, jnp.exp2(st - m_new)))

        def pair_scores(j, buf):
            for c in range(per_q):
                buf[c][...] = scores(per_q * j + c)

        def fold_pair(j, buf, masks=(None, None)):
            for c in range(per_q):
                fold(chains[c], buf[c][...], per_q * j + c, masks[c])

        pair_scores(0, score_bufs[0])

        def two_pairs(i, _):
            pair_scores(2 * i + 1, score_bufs[1])
            fold_pair(2 * i, score_bufs[0])
            pair_scores(2 * i + 2, score_bufs[0])
            fold_pair(2 * i + 1, score_bufs[1])
            return 0
        lax.fori_loop(0, qi // 2, two_pairs, 0)

        @pl.when(qi % 2 == 0)
        def _():
            fold_pair(qi, score_bufs[0], diag_masks)

        @pl.when(qi % 2 == 1)
        def _():
            pair_scores(qi, score_bufs[1])
            fold_pair(qi - 1, score_bufs[0])
            fold_pair(qi, score_bufs[1], diag_masks)

        (m0, acc0), (m1, acc1) = chains
        m = jnp.maximum(m0[...], m1[...])
        write_out(qi, jnp.exp2(m0[...] - m) * acc0[...] + jnp.exp2(m1[...] - m) * acc1[...])
        return 0

    pl.when(bounded_ref[0] != 0)(bounded_head)

    @pl.when(bounded_ref[0] == 0)
    def _():
        lax.fori_loop(0, nq, online_q_tile, 0)


def _attn_call(bounded, q, k, vt):
    B, H, S, _ = q.shape
    head = lambda w: pl.BlockSpec((1, 1, S, w), lambda b, h, flag: (b, h, 0, 0))
    grid_spec = pltpu.PrefetchScalarGridSpec(
        num_scalar_prefetch=1,
        grid=(B, H),
        in_specs=[head(HEAD_W), head(HEAD_W),
                  pl.BlockSpec((1, 1) + vt.shape[2:], lambda b, h, flag: (b, h, 0, 0, 0))],
        out_specs=pl.BlockSpec((1, S, V_DIM), lambda b, h, flag: (b, 0, h)),
        scratch_shapes=[pltpu.VMEM((1, TQ), F32), pltpu.VMEM((V_ROWS, TQ), F32)] * 2
        + [pltpu.VMEM((TK, TQ), F32)] * 4,
    )
    return pl.pallas_call(
        _attn_kernel,
        grid_spec=grid_spec,
        out_shape=jax.ShapeDtypeStruct((B, S, H * V_DIM), BF16),
        compiler_params=pltpu.CompilerParams(
            dimension_semantics=("arbitrary", "arbitrary"), vmem_limit_bytes=VMEM_LIMIT),
        name="attn",
    )(bounded, q, k, vt)


def _post_kernel(x_ref, yp_ref, ya_ref, wo_hbm, g2_ref, wr_ref, br_ref,
                 h1_ref, xn_ref, lg_ref, wo_ref, landing, sem):
    _load_weight_as_bf16(wo_hbm, wo_ref, landing, sem)
    for i in range(x_ref.shape[0] // POST_SUB):
        rs = pl.ds(i * POST_SUB, POST_SUB)
        h1 = (x_ref[rs, :]
              + jnp.dot(yp_ref[rs, :], wo_ref[:D_POOL, :], preferred_element_type=F32)
              + jnp.dot(ya_ref[rs, :], wo_ref[D_POOL:, :], preferred_element_type=F32))
        h1_ref[rs, :] = h1
        xn = _rms(h1, g2_ref[...])
        _store_packed_rows(xn_ref, i * POST_SUB, xn)
        xn_hi = xn.astype(BF16)
        xn_lo = (xn - xn_hi.astype(F32)).astype(BF16)
        both = (jnp.dot(xn_hi, wr_ref[...], preferred_element_type=F32)
                + jnp.dot(xn_lo, wr_ref[...], preferred_element_type=F32))
        lg_ref[rs, :] = both[:, :LANES] + both[:, LANES:] + br_ref[...]


def _post_call(x, yp, ya, wo, g2, wr, br):
    T, D = x.shape
    tm = TM_POST
    row = lambda w: pl.BlockSpec((tm, w), lambda i: (i, 0))
    return pl.pallas_call(
        _post_kernel,
        grid=(T // tm,),
        in_specs=[row(D), row(D_POOL), row(D_ATTN), pl.BlockSpec(memory_space=pl.ANY),
                  _const_spec(g2.shape), _const_spec(wr.shape), _const_spec(br.shape)],
        out_specs=[row(D), pl.BlockSpec((tm * PACK_SUBLANES, LANES), lambda i: (i, 0)),
                   row(LANES)],
        out_shape=[jax.ShapeDtypeStruct((T, D), F32),
                   jax.ShapeDtypeStruct((T * PACK_SUBLANES, LANES), jnp.uint32),
                   jax.ShapeDtypeStruct((T, LANES), F32)],
        scratch_shapes=_bf16_weight_scratch(wo.shape),
        compiler_params=pltpu.CompilerParams(
            dimension_semantics=("arbitrary",), vmem_limit_bytes=VMEM_LIMIT),
        name="post",
    )(x, yp, ya, wo, g2, wr, br)


def _route_kernel(lg_ref, dest_ref, wts_ref, cnt_ref, carry_ref, code_ref, start_ref):
    n_tiles = code_ref.shape[0]
    g = pl.program_id(0)

    @pl.when(g == 0)
    def _():
        carry_ref[...] = jnp.zeros_like(carry_ref)

    @pl.when(g < n_tiles)
    def _():
        for i in range(lg_ref.shape[0] // ROUTE_SUB):
            rs = pl.ds(i * ROUTE_SUB, ROUTE_SUB)
            code_t, wts_t = _route_cols(lg_ref[rs, :].T, carry_ref)
            code_ref[g, :, rs] = code_t
            wts_cols = jnp.concatenate(
                [wts_t, jnp.zeros((LANES - PACK_SUBLANES, ROUTE_SUB), F32)], axis=0)
            wts_ref[rs, :] = wts_cols.T
        cnt_ref[...] = carry_ref[...].astype(I32)

    @pl.when(g == n_tiles)
    def _():
        blocks = jnp.floor((carry_ref[...] + (MOE_BLK - 1)) * (1.0 / MOE_BLK))
        r = lax.broadcasted_iota(I32, (LANES, LANES), 0)
        c = lax.broadcasted_iota(I32, (LANES, LANES), 1)
        earlier = jnp.where(c < r, 1.0, 0.0).astype(BF16)
        before = jnp.dot(earlier, jnp.broadcast_to(blocks, (LANES, LANES)).astype(BF16),
                         preferred_element_type=F32)
        start_ref[...] = before[:, 0:1] * MOE_BLK

    @pl.when(g >= n_tiles)
    def _():
        code = code_ref[g - n_tiles]
        expert = lax.shift_right_logical(code, 16)
        rank = code & 0xFFFF
        expert_row = lax.broadcasted_iota(I32, (LANES, code.shape[1]), 0)
        sub = lax.broadcasted_iota(I32, code.shape, 0)
        dest = jnp.zeros(code.shape, I32)
        for choice in range(TOP_K):
            hit = expert_row == expert[choice:choice + 1, :]
            start = jnp.sum(jnp.where(hit, start_ref[...], 0.0), axis=0, keepdims=True)
            dest = jnp.where(sub == choice, start.astype(I32) + rank, dest)
        dest_ref[...] = dest


def _route_call(logits):
    T = logits.shape[0]
    tm = TM_ROUTE
    n_tiles = T // tm
    pass0 = lambda g: jnp.minimum(g, n_tiles - 1)
    pass1 = lambda g: jnp.maximum(g - n_tiles, 0)
    return pl.pallas_call(
        _route_kernel,
        grid=(2 * n_tiles,),
        in_specs=[pl.BlockSpec((tm, LANES), lambda g: (pass0(g), 0))],
        out_specs=[pl.BlockSpec((PACK_SUBLANES, tm), lambda g: (0, pass1(g))),
                   pl.BlockSpec((tm, LANES), lambda g: (pass0(g), 0)),
                   pl.BlockSpec((LANES, 1), lambda g: (0, 0))],
        out_shape=[jax.ShapeDtypeStruct((PACK_SUBLANES, T), I32),
                   jax.ShapeDtypeStruct((T, LANES), F32),
                   jax.ShapeDtypeStruct((LANES, 1), I32)],
        scratch_shapes=[pltpu.VMEM((LANES, 1), F32),
                        pltpu.VMEM((n_tiles, PACK_SUBLANES, tm), I32),
                        pltpu.VMEM((LANES, 1), F32)],
        compiler_params=pltpu.CompilerParams(
            dimension_semantics=("arbitrary",), vmem_limit_bytes=VMEM_LIMIT),
        name="route",
    )(logits)


def _route_cols(lt, carry_ref):
    tm = lt.shape[1]
    sub = lax.broadcasted_iota(I32, (PACK_SUBLANES, tm), 0).astype(F32)

    def first_argmax(vals):
        mx = jnp.max(vals, axis=0, keepdims=True)
        idx = jnp.min(jnp.where(vals == mx, sub, float(PACK_SUBLANES)), axis=0, keepdims=True)
        return mx, idx

    g_logits = lt[0:N_GROUPS, :]
    g_max, g_sel = first_argmax(g_logits)
    g_w = 1.0 / jnp.sum(jnp.exp(g_logits - g_max), axis=0, keepdims=True)

    e_logits = lt[N_GROUPS:N_GROUPS + EXPERTS_PER_GROUP, :]
    for g in range(1, N_GROUPS):
        lo = N_GROUPS + g * EXPERTS_PER_GROUP
        e_logits = jnp.where(g_sel == float(g), lt[lo:lo + EXPERTS_PER_GROUP, :], e_logits)
    v1, i1 = first_argmax(e_logits)
    v2, i2 = first_argmax(jnp.where(sub == i1, -jnp.inf, e_logits))
    e1 = (g_sel * EXPERTS_PER_GROUP + i1).astype(I32)
    e2 = (g_sel * EXPERTS_PER_GROUP + i2).astype(I32)
    t = jnp.exp(v2 - v1)
    w1 = g_w / (1.0 + t)
    w2 = g_w * t / (1.0 + t)

    expert_row = lax.broadcasted_iota(I32, (LANES, tm), 0)
    hit1 = expert_row == e1
    hit2 = expert_row == e2
    onehot = jnp.where(hit1 | hit2, 1.0, 0.0)
    r = lax.broadcasted_iota(I32, (tm, tm), 0)
    c = lax.broadcasted_iota(I32, (tm, tm), 1)
    earlier = jnp.where(r < c, 1.0, 0.0).astype(BF16)
    before = (jnp.dot(onehot.astype(BF16), earlier, preferred_element_type=F32)
              + carry_ref[...])
    carry_ref[...] += jnp.sum(onehot, axis=1, keepdims=True)
    pos1 = jnp.sum(jnp.where(hit1, before, 0.0), axis=0, keepdims=True).astype(I32)
    pos2 = jnp.sum(jnp.where(hit2, before, 0.0), axis=0, keepdims=True).astype(I32)
    code1 = e1 * 65536 + pos1
    code2 = e2 * 65536 + pos2
    first, second = sub == 0.0, sub == 1.0
    return (jnp.where(first, code1, jnp.where(second, code2, 0)),
            jnp.where(first, w1, jnp.where(second, w2, 0.0)))


def _plan_kernel(dest_ref, cnt_ref, unused_hbm, slots_hbm, blke_ref, blkn_ref, blkw_ref,
                 rune_ref, slot_ref, sem):
    n_assign = dest_ref.shape[0]
    n_blocks = blke_ref.shape[0]
    load = pltpu.make_async_copy(unused_hbm, slot_ref, sem)
    load.start()

    def per_expert(e, nb_done):
        cnt = cnt_ref[e]
        nb = lax.shift_right_logical(cnt + (MOE_BLK - 1), MOE_BLK.bit_length() - 1)

        def per_block(j, _):
            blke_ref[nb_done + j] = e
            blkn_ref[nb_done + j] = jnp.minimum(cnt - j * MOE_BLK, MOE_BLK)
            return 0
        lax.fori_loop(0, nb, per_block, 0)
        return nb_done + nb
    used = lax.fori_loop(0, N_EXPERTS, per_expert, 0)

    last_e = blke_ref[jnp.maximum(used - 1, 0)]

    def tail(b, _):
        blke_ref[b] = last_e
        blkn_ref[b] = 0
        blkw_ref[b] = 0
        return 0
    lax.fori_loop(used, n_blocks, tail, 0)

    def no_run(r, _):
        rune_ref[r] = -1
        return 0
    lax.fori_loop(0, rune_ref.shape[0], no_run, 0)

    def runs_forward(b, carry):
        earlier_e, run = carry
        e = blke_ref[b]
        first = (e != earlier_e).astype(I32)
        run = run + first
        blkw_ref[b] = first + run * 2
        rune_ref[run] = e
        return e, run
    lax.fori_loop(0, used, runs_forward, (-1, -1))

    load.wait()

    def place(a, _):
        slot_ref[dest_ref[a]] = a
        return 0
    lax.fori_loop(0, n_assign, place, 0, unroll=PLAN_UNROLL)

    store = pltpu.make_async_copy(slot_ref, slots_hbm, sem)
    store.start()
    store.wait()


def _plan_call(dest_flat, counts, n_tab):
    n_assign = dest_flat.shape[0]
    n_slots = n_tab * MOE_BLK
    unused = (n_assign + (np.arange(n_slots) & (MOE_BLK - 1))).astype(np.int32)
    smem = pl.BlockSpec(memory_space=pltpu.SMEM)
    hbm = pl.BlockSpec(memory_space=pl.ANY)
    return pl.pallas_call(
        _plan_kernel,
        in_specs=[smem, smem, hbm],
        out_specs=[hbm, smem, smem, smem, smem],
        out_shape=[jax.ShapeDtypeStruct((n_slots,), I32),
                   jax.ShapeDtypeStruct((n_tab,), I32),
                   jax.ShapeDtypeStruct((n_tab,), I32),
                   jax.ShapeDtypeStruct((n_tab,), I32),
                   jax.ShapeDtypeStruct((N_EXPERTS + WEIGHT_SLOTS,), I32)],
        scratch_shapes=[pltpu.SMEM((n_slots,), I32), pltpu.SemaphoreType.DMA],
        name="plan",
    )(dest_flat, counts, unused)


def _expert_kernel(blke_ref, blkn_ref, blkw_ref, rune_ref, slot_ref, xn_hbm,
                   wg_hbm, wu_hbm, wd_hbm, y_hbm, x0, x1, y0, y1, wg_buf, wu_buf, wd_buf,
                   gsem, ssem, wsem):
    xbufs, ybufs = (x0, x1), (y0, y1)
    s = pl.program_id(0)
    n_tok = xn_hbm.shape[0] // PACK_SUBLANES

    def weight_copies(e, slot):
        return [pltpu.make_async_copy(hbm.at[e], buf.at[slot], wsem.at[slot])
                for hbm, buf in ((wg_hbm, wg_buf), (wu_hbm, wu_buf), (wd_hbm, wd_buf))]

    gather_base = s * MOE_BLK
    scatter_base = jnp.maximum(s - 2, 0) * MOE_BLK
    blk = jnp.maximum(s - 1, 0)
    run_info = blkw_ref[blk]
    run = run_info >> 1
    w_slot = lax.rem(run, WEIGHT_SLOTS)
    ahead_run = run + (WEIGHT_SLOTS - 1)
    ahead_expert = rune_ref[ahead_run]
    ahead_slot = lax.rem(ahead_run, WEIGHT_SLOTS)

    def tile(ref, r):
        start = r * PACK_SUBLANES
        if not isinstance(r, int):
            start = pl.multiple_of(start, PACK_SUBLANES)
        return ref.at[pl.ds(start, PACK_SUBLANES)]

    def rows(j):
        n = blkn_ref[jnp.clip(j, 0, blkn_ref.shape[0] - 1)]
        return jnp.where(j >= 0, (n + (ROW_GROUP - 1)) & ~(ROW_GROUP - 1), 0)
    rows0, rows1, rows2, rows3 = rows(s), rows(s - 1), rows(s - 2), rows(s - 3)

    def stage(nxt):
        cur = 1 - nxt
        x_nxt, x_cur, y_nxt, y_cur = xbufs[nxt], xbufs[cur], ybufs[nxt], ybufs[cur]

        def per_group(n_rows, issue_row):
            for g in range(MOE_BLK // ROW_GROUP):
                @pl.when(g * ROW_GROUP < n_rows)
                def _():
                    for r in range(g * ROW_GROUP, (g + 1) * ROW_GROUP):
                        issue_row(r)

        def gather_row(r):
            tok = slot_ref[gather_base + r] & (n_tok - 1)
            pltpu.make_async_copy(tile(xn_hbm, tok), tile(x_nxt, r), gsem.at[nxt]).start()

        def scatter_row(r):
            pltpu.make_async_copy(tile(y_nxt, r), tile(y_hbm, slot_ref[scatter_base + r]),
                                  ssem.at[nxt]).start()

        def wait_rows(src, dst, sem, n_rows):
            n_words = pl.multiple_of(n_rows * PACK_SUBLANES, ROW_GROUP * PACK_SUBLANES)

            @pl.when(n_rows > 0)
            def _():
                pltpu.make_async_copy(src.at[pl.ds(0, n_words)], dst.at[pl.ds(0, n_words)],
                                      sem).wait()

        def compute():
            xb = _load_packed_rows(x_cur, 0, MOE_BLK).astype(BF16)
            a = jnp.dot(xb, wg_buf[w_slot].astype(BF16), preferred_element_type=F32)
            u = jnp.dot(xb, wu_buf[w_slot].astype(BF16), preferred_element_type=F32)
            hmid = (a * jax.nn.sigmoid(a) * u).astype(BF16)
            _store_packed_rows(y_cur, 0, jnp.dot(hmid, wd_buf[w_slot].astype(BF16),
                                                 preferred_element_type=F32))

        wait_rows(xn_hbm, x_cur, gsem.at[cur], rows1)
        wait_rows(y_cur, y_hbm, ssem.at[cur], rows3)

        @pl.when((rows1 > 0) & ((run_info & 1) == 1))
        def _():
            for copy in weight_copies(blke_ref[blk], w_slot):
                copy.wait()

            @pl.when(ahead_expert >= 0)
            def _():
                for copy in weight_copies(ahead_expert, ahead_slot):
                    copy.start(priority=WEIGHT_DMA_PRIORITY)

        if nxt == 0:
            for first_run in range(WEIGHT_SLOTS - 1):
                @pl.when((s == 0) & (rune_ref[first_run] >= 0))
                def _():
                    for copy in weight_copies(rune_ref[first_run], first_run):
                        copy.start(priority=WEIGHT_DMA_PRIORITY)

            @pl.when(s == 0)
            def _():
                x_nxt[...] = jnp.zeros(x_nxt.shape, x_nxt.dtype)
                x_cur[...] = jnp.zeros(x_cur.shape, x_cur.dtype)
                y_nxt[...] = jnp.zeros(y_nxt.shape, y_nxt.dtype)
                dump = pltpu.make_async_copy(
                    y_nxt, y_hbm.at[pl.ds(TOP_K * n_tok * PACK_SUBLANES,
                                          MOE_BLK * PACK_SUBLANES)], ssem.at[nxt])
                dump.start()
                dump.wait()

        per_group(rows0, gather_row)
        per_group(rows2, scatter_row)

        @pl.when(rows1 > 0)
        def _():
            compute()

    for parity in range(2):
        pl.when((s & 1) == parity)(functools.partial(stage, parity))


def _expert_call(blk_e, blk_n, blk_w, run_e, slots, xn, wg, wu, wd):
    T = xn.shape[0] // PACK_SUBLANES
    assert T & (T - 1) == 0, "token id is recovered from the assignment id with a mask"
    n_tab = blk_e.shape[0]
    _, D, F = wg.shape
    packed_block = pltpu.VMEM((MOE_BLK * PACK_SUBLANES, LANES), jnp.uint32)
    hbm = pl.BlockSpec(memory_space=pl.ANY)
    grid_spec = pltpu.PrefetchScalarGridSpec(
        num_scalar_prefetch=5,
        grid=(n_tab,),
        in_specs=[hbm, hbm, hbm, hbm],
        out_specs=hbm,
        scratch_shapes=[packed_block] * 4
        + [pltpu.VMEM((WEIGHT_SLOTS, D, F), F32), pltpu.VMEM((WEIGHT_SLOTS, D, F), F32),
           pltpu.VMEM((WEIGHT_SLOTS, F, D), F32)]
        + [pltpu.SemaphoreType.DMA((2,)), pltpu.SemaphoreType.DMA((2,)),
           pltpu.SemaphoreType.DMA((WEIGHT_SLOTS,))],
    )
    return pl.pallas_call(
        _expert_kernel,
        grid_spec=grid_spec,
        out_shape=jax.ShapeDtypeStruct(((T * TOP_K + MOE_BLK) * PACK_SUBLANES, LANES),
                                       jnp.uint32),
        compiler_params=pltpu.CompilerParams(
            dimension_semantics=("arbitrary",), vmem_limit_bytes=VMEM_LIMIT,
            disable_bounds_checks=True),
        name="experts",
    )(blk_e, blk_n, blk_w, run_e, slots, xn, wg, wu, wd)


def _final_kernel(h1_ref, y1_ref, y2_ref, wts_ref, p_ref, wpp_ref, pg_ref, g3_ref, wpg_hbm,
                  o_ref, wpg_ref, landing, sem):
    _load_weight_as_bf16(wpg_hbm, wpg_ref, landing, sem)
    for i in range(h1_ref.shape[0] // FINAL_SUB):
        rs = pl.ds(i * FINAL_SUB, FINAL_SUB)
        w = wts_ref[rs, :]
        h2 = (h1_ref[rs, :]
              + w[:, 0:1] * _load_packed_rows(y1_ref, i * FINAL_SUB, FINAL_SUB)
              + w[:, 1:2] * _load_packed_rows(y2_ref, i * FINAL_SUB, FINAL_SUB))
        e = _rms(jnp.dot(p_ref[rs, :].astype(BF16), wpp_ref[...], preferred_element_type=F32),
                 pg_ref[...])
        gate = jax.nn.sigmoid(jnp.dot(_rms(h2, g3_ref[...]).astype(BF16), wpg_ref[...],
                                      preferred_element_type=F32))
        o_ref[rs, :] = h2 + gate * e


def _final_call(h1, y, wts, p, wpp, pg, g3, wpg):
    T, D = h1.shape
    tm = TM_FINAL
    row = lambda w: pl.BlockSpec((tm, w), lambda i: (i, 0))
    first = pl.BlockSpec((tm * PACK_SUBLANES, LANES), lambda i: (i, 0))
    second = pl.BlockSpec((tm * PACK_SUBLANES, LANES), lambda i: (i + T // tm, 0))
    return pl.pallas_call(
        _final_kernel,
        grid=(T // tm,),
        in_specs=[row(D), first, second, row(LANES), row(PLE_DIM), _const_spec(wpp.shape),
                  _const_spec(pg.shape), _const_spec(g3.shape),
                  pl.BlockSpec(memory_space=pl.ANY)],
        out_specs=row(D),
        out_shape=jax.ShapeDtypeStruct((T, D), F32),
        scratch_shapes=_bf16_weight_scratch(wpg.shape),
        compiler_params=pltpu.CompilerParams(
            dimension_semantics=("arbitrary",), vmem_limit_bytes=VMEM_LIMIT),
        name="final",
    )(h1, y, y, wts, p, wpp, pg, g3, wpg)


def _rope_lanes(a):
    z = jnp.zeros(a.shape[:-1] + (ROPE_HALF,), a.dtype)
    return jnp.concatenate([a[..., :ROPE_HALF], z, a[..., ROPE_HALF:], z], axis=-1)


def _head_lanes(a):
    return jnp.concatenate([a[..., :QK_NOPE], _rope_lanes(a[..., QK_NOPE:])], axis=-1)


def _score_bound(q_gain, k_gain):
    return Q_PRESCALE * QK_DIM * jnp.max(jnp.abs(q_gain)) * jnp.max(jnp.abs(k_gain))


def kernel(x, p, positions, norm1_gain, w_in, q_a_gain, w_q_b, kv_a_gain, w_kv_b, q_norm_gain, k_norm_gain, w_pool, pool_scale, w_out, norm2_gain, w_router_group, b_router_group, w_router_expert, b_router_expert, w_exp_gate, w_exp_up, w_exp_down, norm3_gain, w_ple_gate, w_ple_proj, ple_norm_gain):
    B, S, D = x.shape
    T = B * S
    assert x.shape[2] == D_MODEL and S % TQ == 0 and S % TM_PRE == 0
    assert T % TM_POST == 0 and T % TM_FINAL == 0
    layer = 0
    row = lambda a: a[layer].reshape(1, -1)

    win = _win_call(w_in[layer].T)
    wq = _head_lanes(w_q_b[layer].reshape(Q_LORA, N_HEADS, QK_DIM)
                     ).reshape(Q_LORA, N_HEADS * HEAD_W).astype(BF16)
    wkv3 = w_kv_b[layer].reshape(KV_LORA, N_HEADS, QK_NOPE + V_DIM)
    wkv = jnp.concatenate([wkv3[..., :QK_NOPE].reshape(KV_LORA, -1),
                           wkv3[..., QK_NOPE:].reshape(KV_LORA, -1)], axis=1).astype(BF16)
    qng = _head_lanes(q_norm_gain[layer]).reshape(1, HEAD_W)
    kng = _head_lanes(k_norm_gain[layer]).reshape(1, HEAD_W)
    inv_freq = np.float32(ROPE_THETA) ** (-np.arange(ROPE_HALF, dtype=np.float32) / ROPE_HALF)
    zeros_half = np.zeros((ROPE_HALF,), np.float32)
    ones_half = np.ones((ROPE_HALF,), np.float32)
    invf = np.concatenate([inv_freq, zeros_half, inv_freq, zeros_half]).reshape(1, LANES)
    sgn = np.concatenate([-ones_half, zeros_half, ones_half, zeros_half]).reshape(1, LANES)
    spare = np.arange(LANES) == SPARE_ROPE_LANE
    kpad = spare.astype(np.float32).reshape(1, LANES)
    pad_lanes = LANES - N_GROUPS - N_EXPERTS
    wr = jnp.concatenate([w_router_group[layer], w_router_expert[layer],
                          jnp.zeros((D, pad_lanes), F32)], axis=1)
    wr_hi = wr.astype(BF16)
    wr = jnp.concatenate([wr_hi, (wr - wr_hi.astype(F32)).astype(BF16)], axis=1)
    br = jnp.concatenate([b_router_group[layer], b_router_expert[layer],
                          jnp.zeros((pad_lanes,), F32)]).reshape(1, LANES)

    bound = _score_bound(q_norm_gain[layer], k_norm_gain[layer])
    bounded = bound <= SCORE_BOUND_LIMIT
    qpad = jnp.where(bounded, -bound, 0.0) * kpad
    ypool, q, k, vt = _pre_call(
        x, positions.reshape(B, S, 1), row(norm1_gain), win, row(q_a_gain), wq,
        row(kv_a_gain), wkv, qng, kng, w_pool[layer].astype(BF16), row(pool_scale), invf, sgn,
        qpad, kpad)
    yattn = _attn_call(bounded.astype(I32).reshape(1), q, k, vt)
    h1, xn2, logits = _post_call(
        x.reshape(T, D), ypool.reshape(T, D_POOL), yattn.reshape(T, D_ATTN),
        w_out[layer], row(norm2_gain), wr, br)
    dest, wts, counts = _route_call(logits)

    n_assign = T * TOP_K
    n_blocks = (n_assign + N_EXPERTS * (MOE_BLK - 1)) // MOE_BLK
    slots, blk_e, blk_n, blk_w, run_e = _plan_call(
        dest[:TOP_K].reshape(n_assign), counts.reshape(LANES), n_blocks + MOE_DRAIN_STEPS)
    y = _expert_call(blk_e, blk_n, blk_w, run_e, slots, xn2,
                     w_exp_gate[layer], w_exp_up[layer], w_exp_down[layer])

    out = _final_call(h1, y, wts, p[layer].reshape(T, PLE_DIM),
                      w_ple_proj[layer].astype(BF16), row(ple_norm_gain), row(norm3_gain),
                      w_ple_gate[layer])
    return out.reshape(B, S, D)
```

```python
import functools
import math

import jax
import jax.numpy as jnp
import numpy as np
from jax import lax
from jax.experimental import pallas as pl
from jax.experimental.pallas import tpu as pltpu

F32 = jnp.float32
BF16 = jnp.bfloat16
I32 = jnp.int32

D_MODEL = 2048
PLE_DIM = 256
EPS = 1e-6
POOL_WINDOWS = (2, 4, 8, 16)
POOL_CH = 256
D_POOL = POOL_CH * len(POOL_WINDOWS)
N_HEADS = 8
Q_LORA = 512
KV_LORA = 512
QK_NOPE = 128
QK_ROPE = 64
QK_DIM = QK_NOPE + QK_ROPE
V_DIM = 128
D_ATTN = N_HEADS * V_DIM
ROPE_THETA = 10000.0
ATTN_SCALE = 1.0 / math.sqrt(QK_DIM)
Q_PRESCALE = ATTN_SCALE * math.log2(math.e)
N_GROUPS = 8
EXPERTS_PER_GROUP = 8
N_EXPERTS = N_GROUPS * EXPERTS_PER_GROUP
TOP_K = 2

LANES = 128
PACK_SUBLANES = 8
VMEM_LIMIT = 56 * 1024 * 1024

HEAD_W = 2 * LANES
ROPE_HALF = QK_ROPE // 2
POOL_HALO = 16
TM_PRE = 512
PRE_SUB = 256
TM_POST = 512
POST_SUB = 256
TM_FINAL = 512
FINAL_SUB = 256
WIN_ROWS = 256
CAST_ROWS = 512
TQ = 512
TK = 256
TM_ROUTE = 1024
ROUTE_SUB = 256
MOE_BLK = 256
MOE_DRAIN_STEPS = 3
ROW_GROUP = 32
WEIGHT_DMA_PRIORITY = 1
WEIGHT_SLOTS = 3
PLAN_UNROLL = 32
NEG_BIG = -1e30
V_ROWS = V_DIM + 16
SPARE_ROPE_LANE = ROPE_HALF
SCORE_BOUND_LIMIT = 50.0


def _const_spec(shape):
    nd = len(shape)
    return pl.BlockSpec(shape, lambda *_: (0,) * nd, pipeline_mode=pl.Buffered(1))


def _rms(x, gain):
    return x * lax.rsqrt(jnp.mean(x * x, axis=-1, keepdims=True) + EPS) * gain


def _load_weight_as_bf16(w_hbm, w_bf16, landing, sem):
    n_chunks = w_hbm.shape[0] // CAST_ROWS

    def chunk(c):
        return pltpu.make_async_copy(w_hbm.at[pl.ds(c * CAST_ROWS, CAST_ROWS)],
                                     landing.at[c % 2], sem.at[c % 2])

    @pl.when(pl.program_id(0) == 0)
    def _():
        chunk(0).start()
        for c in range(n_chunks):
            if c + 1 < n_chunks:
                chunk(c + 1).start()
            chunk(c).wait()
            w_bf16[pl.ds(c * CAST_ROWS, CAST_ROWS), :] = landing[c % 2].astype(BF16)


def _bf16_weight_scratch(shape):
    rows, cols = shape
    assert rows % CAST_ROWS == 0
    return [pltpu.VMEM((rows, cols), BF16), pltpu.VMEM((2, CAST_ROWS, cols), F32),
            pltpu.SemaphoreType.DMA((2,))]


def _store_packed_rows(ref, first_row, x):
    rows, d = x.shape
    half = d // 2
    assert half == PACK_SUBLANES * LANES
    for i in range(PACK_SUBLANES):
        lo = x[:, i * LANES:(i + 1) * LANES].astype(BF16).astype(F32)
        hi = x[:, half + i * LANES:half + (i + 1) * LANES].astype(BF16).astype(F32)
        word = (lax.shift_right_logical(pltpu.bitcast(lo, jnp.uint32), jnp.uint32(16))
                | (pltpu.bitcast(hi, jnp.uint32) & jnp.uint32(0xFFFF0000)))
        ref[pl.ds(first_row * PACK_SUBLANES + i, rows, stride=PACK_SUBLANES), :] = word


def _load_packed_rows(ref, first_row, rows):
    lo, hi = [], []
    for i in range(PACK_SUBLANES):
        word = ref[pl.ds(first_row * PACK_SUBLANES + i, rows, stride=PACK_SUBLANES), :]
        lo.append(pltpu.bitcast(lax.shift_left(word, jnp.uint32(16)), F32))
        hi.append(pltpu.bitcast(word & jnp.uint32(0xFFFF0000), F32))
    return jnp.concatenate(lo + hi, axis=1)


def _pre_kernel(x_ref, pos_ref, g1_ref, win_ref, qag_ref, wq_ref, kvag_ref, wkv_ref,
                qng_ref, kng_ref, wpool_ref, pscale_ref, invf_ref, sgn_ref, qpad_ref, kpad_ref,
                ypool_ref, q_ref, k_ref, vt_ref, carry_ref):
    st = pl.program_id(1)
    tm = PRE_SUB
    n_sub = x_ref.shape[0] // tm

    @pl.when(st == 0)
    def _():
        carry_ref[...] = jnp.zeros_like(carry_ref)

    zs = [_pre_project(pl.ds(i * tm, tm), x_ref, g1_ref, win_ref) for i in range(n_sub)]
    halo = carry_ref[...]
    mixed = []
    for i in range(n_sub):
        first_pos = st * x_ref.shape[0] + i * tm
        qa, kv, halo = _pre_mix(pl.ds(i * tm, tm), first_pos, zs[i], halo, qag_ref, wq_ref,
                                kvag_ref, wkv_ref, wpool_ref, pscale_ref, ypool_ref)
        mixed.append((qa, kv))
    carry_ref[...] = halo
    for i in range(n_sub):
        k_rope = zs[i][:, D_POOL + Q_LORA + KV_LORA:]
        _pre_heads(i, *mixed[i], k_rope, pos_ref, qng_ref, kng_ref, invf_ref, sgn_ref,
                   qpad_ref, kpad_ref, q_ref, k_ref, vt_ref)


def _pre_project(rs, x_ref, g1_ref, win_ref):
    hn = _rms(x_ref[rs, :], g1_ref[...])
    return jnp.dot(hn.astype(BF16), win_ref[...], preferred_element_type=F32)


def _pre_mix(rs, first_pos, z, halo, qag_ref, wq_ref, kvag_ref, wkv_ref, wpool_ref, pscale_ref,
             ypool_ref):
    tm = z.shape[0]
    u = z[:, :D_POOL]
    ext = jnp.concatenate([halo, u], axis=0)
    row = lax.broadcasted_iota(I32, (tm, 1), 0) + first_pos
    level = ext
    shift = 1
    for g, w in enumerate(POOL_WINDOWS):
        sl = slice(g * POOL_CH, (g + 1) * POOL_CH)
        while shift < w:
            level = level + pltpu.roll(level, shift, 0)
            shift *= 2
        win_sum = level[POOL_HALO:, sl]
        cnt = jnp.minimum(row + 1, w).astype(F32)
        d = win_sum / cnt - u[:, sl]
        y = jnp.dot(d.astype(BF16), wpool_ref[g], preferred_element_type=F32)
        ypool_ref[rs, sl] = (y * pscale_ref[:, sl]).astype(BF16)

    q_lat = z[:, D_POOL:D_POOL + Q_LORA]
    kv_lat = z[:, D_POOL + Q_LORA:D_POOL + Q_LORA + KV_LORA]
    qa = jnp.dot(_rms(q_lat, qag_ref[...]).astype(BF16), wq_ref[...],
                 preferred_element_type=F32)
    kv = jnp.dot(_rms(kv_lat, kvag_ref[...]).astype(BF16), wkv_ref[...],
                 preferred_element_type=F32)
    return qa, kv, u[tm - POOL_HALO:, :]


def _pre_heads(i, qa, kv, k_rope, pos_ref, qng_ref, kng_ref, invf_ref, sgn_ref, qpad_ref,
               kpad_ref, q_ref, k_ref, vt_ref):
    tm = qa.shape[0]
    rs = pl.ds(i * tm, tm)
    ang = pos_ref[rs, :].astype(F32) * invf_ref[...]
    cos = jnp.cos(ang)
    sin = jnp.sin(ang) * sgn_ref[...]

    def rot(t):
        return t * cos + pltpu.roll(t, LANES // 2, 1) * sin

    qng = qng_ref[...]
    kng = kng_ref[...]
    kr_rot = rot(k_rope * kng[:, LANES:])
    kr_ssq = jnp.sum(k_rope * k_rope, axis=-1, keepdims=True)
    for h in range(N_HEADS):
        qh = qa[:, h * HEAD_W:(h + 1) * HEAD_W]
        rq = lax.rsqrt(jnp.sum(qh * qh, axis=-1, keepdims=True) / QK_DIM + EPS) * Q_PRESCALE
        qn = qh * rq * qng
        q_ref[0, h, rs, :LANES] = qn[:, :LANES].astype(BF16)
        q_ref[0, h, rs, LANES:] = (rot(qn[:, LANES:]) + qpad_ref[...]).astype(BF16)
        kh = kv[:, h * QK_NOPE:(h + 1) * QK_NOPE]
        rk = lax.rsqrt((jnp.sum(kh * kh, axis=-1, keepdims=True) + kr_ssq) / QK_DIM + EPS)
        k_ref[0, h, rs, :LANES] = (kh * rk * kng[:, :LANES]).astype(BF16)
        k_ref[0, h, rs, LANES:] = (kr_rot * rk + kpad_ref[...]).astype(BF16)
        vh = kv[:, N_HEADS * QK_NOPE + h * V_DIM:N_HEADS * QK_NOPE + (h + 1) * V_DIM]
        key_tile, keys = (i * tm) // TK, pl.ds((i * tm) % TK, tm)
        vt_ref[0, h, key_tile, :V_DIM, keys] = vh.T.astype(BF16)
        vt_ref[0, h, key_tile, V_DIM:, keys] = jnp.ones((V_ROWS - V_DIM, tm), BF16)


def _win_kernel(wt_ref, o_ref):
    w = wt_ref[...].T
    n_lat = o_ref.shape[1] - LANES
    zeros = jnp.zeros((w.shape[0], ROPE_HALF), F32)
    rope = jnp.concatenate([w[:, n_lat:n_lat + ROPE_HALF], zeros,
                            w[:, n_lat + ROPE_HALF:], zeros], axis=1)
    o_ref[:, :n_lat] = w[:, :n_lat].astype(BF16)
    o_ref[:, n_lat:] = rope.astype(BF16)


def _win_call(w_in_t):
    n_in, D = w_in_t.shape
    n_lat = n_in - QK_ROPE
    tm = WIN_ROWS
    return pl.pallas_call(
        _win_kernel,
        grid=(D // tm,),
        in_specs=[pl.BlockSpec((n_in, tm), lambda i: (0, i))],
        out_specs=pl.BlockSpec((tm, n_lat + LANES), lambda i: (i, 0)),
        out_shape=jax.ShapeDtypeStruct((D, n_lat + LANES), BF16),
        compiler_params=pltpu.CompilerParams(
            dimension_semantics=("arbitrary",), vmem_limit_bytes=VMEM_LIMIT),
        name="win",
    )(w_in_t)


def _pre_call(x, pos, g1, win, qag, wq, kvag, wkv, qng, kng, wpool, pscale, invf, sgn,
              qpad, kpad):
    B, S, D = x.shape
    tm = TM_PRE
    grid = (B, S // tm)
    row_spec = lambda w: pl.BlockSpec((None, tm, w), lambda b, s: (b, s, 0))
    head_spec = lambda w: pl.BlockSpec((1, N_HEADS, tm, w), lambda b, s: (b, 0, s, 0))
    consts = [g1, win, qag, wq, kvag, wkv, qng, kng, wpool, pscale, invf, sgn, qpad, kpad]
    return pl.pallas_call(
        _pre_kernel,
        grid=grid,
        in_specs=[row_spec(D), row_spec(1)] + [_const_spec(c.shape) for c in consts],
        out_specs=[row_spec(D_POOL), head_spec(HEAD_W), head_spec(HEAD_W),
                   pl.BlockSpec((1, N_HEADS, tm // TK, V_ROWS, TK),
                                lambda b, s: (b, 0, s, 0, 0))],
        out_shape=[jax.ShapeDtypeStruct((B, S, D_POOL), BF16),
                   jax.ShapeDtypeStruct((B, N_HEADS, S, HEAD_W), BF16),
                   jax.ShapeDtypeStruct((B, N_HEADS, S, HEAD_W), BF16),
                   jax.ShapeDtypeStruct((B, N_HEADS, S // TK, V_ROWS, TK), BF16)],
        scratch_shapes=[pltpu.VMEM((POOL_HALO, D_POOL), F32)],
        compiler_params=pltpu.CompilerParams(
            dimension_semantics=("arbitrary", "arbitrary"), vmem_limit_bytes=VMEM_LIMIT),
        name="pre",
    )(x, pos, *consts)


def _attn_kernel(bounded_ref, q_ref, k_ref, vt_ref, o_ref, *chain_scratch):
    S = q_ref.shape[2]
    nq = S // TQ
    per_q = TQ // TK
    assert per_q == 2, "two online-softmax chains take the even / odd key tiles"
    chains = (chain_scratch[0:2], chain_scratch[2:4])
    score_bufs = (chain_scratch[4:6], chain_scratch[6:8])
    key_idx = lax.broadcasted_iota(I32, (TK, TQ), 0)
    qry_idx = lax.broadcasted_iota(I32, (TK, TQ), 1)

    diag_masks = (key_idx <= qry_idx, key_idx + TK <= qry_idx)

    def weighted_values(kt, p):
        return jnp.dot(vt_ref[0, 0, kt], p.astype(BF16), preferred_element_type=F32)

    def write_out(qi, acc):
        out = acc[:V_DIM, :] / acc[V_DIM:V_DIM + 1, :]
        start = qi * TQ if isinstance(qi, int) else pl.multiple_of(qi * TQ, TQ)
        o_ref[0, pl.ds(start, TQ), :] = out.T.astype(BF16)

    def bounded_head():
        steps = [(qi, j) for qi in range(nq) for j in range(qi + 1)]

        def pair_scores(step, buf):
            qi, j = step
            q = q_ref[0, 0, qi * TQ:(qi + 1) * TQ, :]
            for c in range(per_q):
                kt = per_q * j + c
                buf[c][...] = lax.dot_general(k_ref[0, 0, kt * TK:(kt + 1) * TK, :], q,
                                              (((1,), (1,)), ((), ())),
                                              preferred_element_type=F32)

        pair_scores(steps[0], score_bufs[0])
        acc = None
        for t, (qi, j) in enumerate(steps):
            if t + 1 < len(steps):
                pair_scores(steps[t + 1], score_bufs[(t + 1) % 2])
            for c in range(per_q):
                st = score_bufs[t % 2][c][...]
                if j == qi:
                    st = jnp.where(diag_masks[c], st, NEG_BIG)
                pv = weighted_values(per_q * j + c, jnp.exp2(st))
                acc = pv if acc is None else acc + pv
            if j == qi:
                write_out(qi, acc)
                acc = None

    def online_q_tile(qi, _):
        q = q_ref[0, 0, pl.ds(pl.multiple_of(qi * TQ, TQ), TQ), :]
        for m_ref, acc_ref in chains:
            m_ref[...] = jnp.full(m_ref.shape, NEG_BIG, F32)
            acc_ref[...] = jnp.zeros(acc_ref.shape, F32)

        def scores(kt):
            k = k_ref[0, 0, pl.ds(pl.multiple_of(kt * TK, TK), TK), :]
            return lax.dot_general(k, q, (((1,), (1,)), ((), ())), preferred_element_type=F32)

        def fold(chain, st, kt, mask):
            m_ref, acc_ref = chain
            if mask is not None:
                st = jnp.where(mask, st, NEG_BIG)
            m = m_ref[...]
            m_new = jnp.maximum(m, jnp.max(st, axis=0, keepdims=True))
            m_ref[...] = m_new
            acc_ref[...] = (jnp.exp2(m - m_new) * acc_ref[...]
                            + weighted_values(kt, jnp.exp2(st - m_new)))

        def pair_scores(j, buf):
            for c in range(per_q):
                buf[c][...] = scores(per_q * j + c)

        def fold_pair(j, buf, masks=(None, None)):
            for c in range(per_q):
                fold(chains[c], buf[c][...], per_q * j + c, masks[c])

        pair_scores(0, score_bufs[0])

        def two_pairs(i, _):
            pair_scores(2 * i + 1, score_bufs[1])
            fold_pair(2 * i, score_bufs[0])
            pair_scores(2 * i + 2, score_bufs[0])
            fold_pair(2 * i + 1, score_bufs[1])
            return 0
        lax.fori_loop(0, qi // 2, two_pairs, 0)

        @pl.when(qi % 2 == 0)
        def _():
            fold_pair(qi, score_bufs[0], diag_masks)

        @pl.when(qi % 2 == 1)
        def _():
            pair_scores(qi, score_bufs[1])
            fold_pair(qi - 1, score_bufs[0])
            fold_pair(qi, score_bufs[1], diag_masks)

        (m0, acc0), (m1, acc1) = chains
        m = jnp.maximum(m0[...], m1[...])
        write_out(qi, jnp.exp2(m0[...] - m) * acc0[...] + jnp.exp2(m1[...] - m) * acc1[...])
        return 0

    pl.when(bounded_ref[0] != 0)(bounded_head)

    @pl.when(bounded_ref[0] == 0)
    def _():
        lax.fori_loop(0, nq, online_q_tile, 0)


def _attn_call(bounded, q, k, vt):
    B, H, S, _ = q.shape
    head = lambda w: pl.BlockSpec((1, 1, S, w), lambda b, h, flag: (b, h, 0, 0))
    grid_spec = pltpu.PrefetchScalarGridSpec(
        num_scalar_prefetch=1,
        grid=(B, H),
        in_specs=[head(HEAD_W), head(HEAD_W),
                  pl.BlockSpec((1, 1) + vt.shape[2:], lambda b, h, flag: (b, h, 0, 0, 0))],
        out_specs=pl.BlockSpec((1, S, V_DIM), lambda b, h, flag: (b, 0, h)),
        scratch_shapes=[pltpu.VMEM((1, TQ), F32), pltpu.VMEM((V_ROWS, TQ), F32)] * 2
        + [pltpu.VMEM((TK, TQ), F32)] * 4,
    )
    return pl.pallas_call(
        _attn_kernel,
        grid_spec=grid_spec,
        out_shape=jax.ShapeDtypeStruct((B, S, H * V_DIM), BF16),
        compiler_params=pltpu.CompilerParams(
            dimension_semantics=("arbitrary", "arbitrary"), vmem_limit_bytes=VMEM_LIMIT),
        name="attn",
    )(bounded, q, k, vt)


def _post_kernel(x_ref, yp_ref, ya_ref, wo_hbm, g2_ref, wr_ref, br_ref,
                 h1_ref, xn_ref, lg_ref, wo_ref, landing, sem):
    _load_weight_as_bf16(wo_hbm, wo_ref, landing, sem)
    for i in range(x_ref.shape[0] // POST_SUB):
        rs = pl.ds(i * POST_SUB, POST_SUB)
        h1 = (x_ref[rs, :]
              + jnp.dot(yp_ref[rs, :], wo_ref[:D_POOL, :], preferred_element_type=F32)
              + jnp.dot(ya_ref[rs, :], wo_ref[D_POOL:, :], preferred_element_type=F32))
        h1_ref[rs, :] = h1
        xn = _rms(h1, g2_ref[...])
        _store_packed_rows(xn_ref, i * POST_SUB, xn)
        xn_hi = xn.astype(BF16)
        xn_lo = (xn - xn_hi.astype(F32)).astype(BF16)
        both = (jnp.dot(xn_hi, wr_ref[...], preferred_element_type=F32)
                + jnp.dot(xn_lo, wr_ref[...], preferred_element_type=F32))
        lg_ref[rs, :] = both[:, :LANES] + both[:, LANES:] + br_ref[...]


def _post_call(x, yp, ya, wo, g2, wr, br):
    T, D = x.shape
    tm = TM_POST
    row = lambda w: pl.BlockSpec((tm, w), lambda i: (i, 0))
    return pl.pallas_call(
        _post_kernel,
        grid=(T // tm,),
        in_specs=[row(D), row(D_POOL), row(D_ATTN), pl.BlockSpec(memory_space=pl.ANY),
                  _const_spec(g2.shape), _const_spec(wr.shape), _const_spec(br.shape)],
        out_specs=[row(D), pl.BlockSpec((tm * PACK_SUBLANES, LANES), lambda i: (i, 0)),
                   row(LANES)],
        out_shape=[jax.ShapeDtypeStruct((T, D), F32),
                   jax.ShapeDtypeStruct((T * PACK_SUBLANES, LANES), jnp.uint32),
                   jax.ShapeDtypeStruct((T, LANES), F32)],
        scratch_shapes=_bf16_weight_scratch(wo.shape),
        compiler_params=pltpu.CompilerParams(
            dimension_semantics=("arbitrary",), vmem_limit_bytes=VMEM_LIMIT),
        name="post",
    )(x, yp, ya, wo, g2, wr, br)


def _route_kernel(lg_ref, dest_ref, wts_ref, cnt_ref, carry_ref, code_ref, start_ref):
    n_tiles = code_ref.shape[0]
    g = pl.program_id(0)

    @pl.when(g == 0)
    def _():
        carry_ref[...] = jnp.zeros_like(carry_ref)

    @pl.when(g < n_tiles)
    def _():
        for i in range(lg_ref.shape[0] // ROUTE_SUB):
            rs = pl.ds(i * ROUTE_SUB, ROUTE_SUB)
            code_t, wts_t = _route_cols(lg_ref[rs, :].T, carry_ref)
            code_ref[g, :, rs] = code_t
            wts_cols = jnp.concatenate(
                [wts_t, jnp.zeros((LANES - PACK_SUBLANES, ROUTE_SUB), F32)], axis=0)
            wts_ref[rs, :] = wts_cols.T
        cnt_ref[...] = carry_ref[...].astype(I32)

    @pl.when(g == n_tiles)
    def _():
        blocks = jnp.floor((carry_ref[...] + (MOE_BLK - 1)) * (1.0 / MOE_BLK))
        r = lax.broadcasted_iota(I32, (LANES, LANES), 0)
        c = lax.broadcasted_iota(I32, (LANES, LANES), 1)
        earlier = jnp.where(c < r, 1.0, 0.0).astype(BF16)
        before = jnp.dot(earlier, jnp.broadcast_to(blocks, (LANES, LANES)).astype(BF16),
                         preferred_element_type=F32)
        start_ref[...] = before[:, 0:1] * MOE_BLK

    @pl.when(g >= n_tiles)
    def _():
        code = code_ref[g - n_tiles]
        expert = lax.shift_right_logical(code, 16)
        rank = code & 0xFFFF
        expert_row = lax.broadcasted_iota(I32, (LANES, code.shape[1]), 0)
        sub = lax.broadcasted_iota(I32, code.shape, 0)
        dest = jnp.zeros(code.shape, I32)
        for choice in range(TOP_K):
            hit = expert_row == expert[choice:choice + 1, :]
            start = jnp.sum(jnp.where(hit, start_ref[...], 0.0), axis=0, keepdims=True)
            dest = jnp.where(sub == choice, start.astype(I32) + rank, dest)
        dest_ref[...] = dest


def _route_call(logits):
    T = logits.shape[0]
    tm = TM_ROUTE
    n_tiles = T // tm
    pass0 = lambda g: jnp.minimum(g, n_tiles - 1)
    pass1 = lambda g: jnp.maximum(g - n_tiles, 0)
    return pl.pallas_call(
        _route_kernel,
        grid=(2 * n_tiles,),
        in_specs=[pl.BlockSpec((tm, LANES), lambda g: (pass0(g), 0))],
        out_specs=[pl.BlockSpec((PACK_SUBLANES, tm), lambda g: (0, pass1(g))),
                   pl.BlockSpec((tm, LANES), lambda g: (pass0(g), 0)),
                   pl.BlockSpec((LANES, 1), lambda g: (0, 0))],
        out_shape=[jax.ShapeDtypeStruct((PACK_SUBLANES, T), I32),
                   jax.ShapeDtypeStruct((T, LANES), F32),
                   jax.ShapeDtypeStruct((LANES, 1), I32)],
        scratch_shapes=[pltpu.VMEM((LANES, 1), F32),
                        pltpu.VMEM((n_tiles, PACK_SUBLANES, tm), I32),
                        pltpu.VMEM((LANES, 1), F32)],
        compiler_params=pltpu.CompilerParams(
            dimension_semantics=("arbitrary",), vmem_limit_bytes=VMEM_LIMIT),
        name="route",
    )(logits)


def _route_cols(lt, carry_ref):
    tm = lt.shape[1]
    sub = lax.broadcasted_iota(I32, (PACK_SUBLANES, tm), 0).astype(F32)

    def first_argmax(vals):
        mx = jnp.max(vals, axis=0, keepdims=True)
        idx = jnp.min(jnp.where(vals == mx, sub, float(PACK_SUBLANES)), axis=0, keepdims=True)
        return mx, idx

    g_logits = lt[0:N_GROUPS, :]
    g_max, g_sel = first_argmax(g_logits)
    g_w = 1.0 / jnp.sum(jnp.exp(g_logits - g_max), axis=0, keepdims=True)

    e_logits = lt[N_GROUPS:N_GROUPS + EXPERTS_PER_GROUP, :]
    for g in range(1, N_GROUPS):
        lo = N_GROUPS + g * EXPERTS_PER_GROUP
        e_logits = jnp.where(g_sel == float(g), lt[lo:lo + EXPERTS_PER_GROUP, :], e_logits)
    v1, i1 = first_argmax(e_logits)
    v2, i2 = first_argmax(jnp.where(sub == i1, -jnp.inf, e_logits))
    e1 = (g_sel * EXPERTS_PER_GROUP + i1).astype(I32)
    e2 = (g_sel * EXPERTS_PER_GROUP + i2).astype(I32)
    t = jnp.exp(v2 - v1)
    w1 = g_w / (1.0 + t)
    w2 = g_w * t / (1.0 + t)

    expert_row = lax.broadcasted_iota(I32, (LANES, tm), 0)
    hit1 = expert_row == e1
    hit2 = expert_row == e2
    onehot = jnp.where(hit1 | hit2, 1.0, 0.0)
    r = lax.broadcasted_iota(I32, (tm, tm), 0)
    c = lax.broadcasted_iota(I32, (tm, tm), 1)
    earlier = jnp.where(r < c, 1.0, 0.0).astype(BF16)
    before = (jnp.dot(onehot.astype(BF16), earlier, preferred_element_type=F32)
              + carry_ref[...])
    carry_ref[...] += jnp.sum(onehot, axis=1, keepdims=True)
    pos1 = jnp.sum(jnp.where(hit1, before, 0.0), axis=0, keepdims=True).astype(I32)
    pos2 = jnp.sum(jnp.where(hit2, before, 0.0), axis=0, keepdims=True).astype(I32)
    code1 = e1 * 65536 + pos1
    code2 = e2 * 65536 + pos2
    first, second = sub == 0.0, sub == 1.0
    return (jnp.where(first, code1, jnp.where(second, code2, 0)),
            jnp.where(first, w1, jnp.where(second, w2, 0.0)))


def _plan_kernel(dest_ref, cnt_ref, unused_hbm, slots_hbm, blke_ref, blkn_ref, blkw_ref,
                 rune_ref, slot_ref, sem):
    n_assign = dest_ref.shape[0]
    n_blocks = blke_ref.shape[0]
    load = pltpu.make_async_copy(unused_hbm, slot_ref, sem)
    load.start()

    def per_expert(e, nb_done):
        cnt = cnt_ref[e]
        nb = lax.shift_right_logical(cnt + (MOE_BLK - 1), MOE_BLK.bit_length() - 1)

        def per_block(j, _):
            blke_ref[nb_done + j] = e
            blkn_ref[nb_done + j] = jnp.minimum(cnt - j * MOE_BLK, MOE_BLK)
            return 0
        lax.fori_loop(0, nb, per_block, 0)
        return nb_done + nb
    used = lax.fori_loop(0, N_EXPERTS, per_expert, 0)

    last_e = blke_ref[jnp.maximum(used - 1, 0)]

    def tail(b, _):
        blke_ref[b] = last_e
        blkn_ref[b] = 0
        blkw_ref[b] = 0
        return 0
    lax.fori_loop(used, n_blocks, tail, 0)

    def no_run(r, _):
        rune_ref[r] = -1
        return 0
    lax.fori_loop(0, rune_ref.shape[0], no_run, 0)

    def runs_forward(b, carry):
        earlier_e, run = carry
        e = blke_ref[b]
        first = (e != earlier_e).astype(I32)
        run = run + first
        blkw_ref[b] = first + run * 2
        rune_ref[run] = e
        return e, run
    lax.fori_loop(0, used, runs_forward, (-1, -1))

    load.wait()

    def place(a, _):
        slot_ref[dest_ref[a]] = a
        return 0
    lax.fori_loop(0, n_assign, place, 0, unroll=PLAN_UNROLL)

    store = pltpu.make_async_copy(slot_ref, slots_hbm, sem)
    store.start()
    store.wait()


def _plan_call(dest_flat, counts, n_tab):
    n_assign = dest_flat.shape[0]
    n_slots = n_tab * MOE_BLK
    unused = (n_assign + (np.arange(n_slots) & (MOE_BLK - 1))).astype(np.int32)
    smem = pl.BlockSpec(memory_space=pltpu.SMEM)
    hbm = pl.BlockSpec(memory_space=pl.ANY)
    return pl.pallas_call(
        _plan_kernel,
        in_specs=[smem, smem, hbm],
        out_specs=[hbm, smem, smem, smem, smem],
        out_shape=[jax.ShapeDtypeStruct((n_slots,), I32),
                   jax.ShapeDtypeStruct((n_tab,), I32),
                   jax.ShapeDtypeStruct((n_tab,), I32),
                   jax.ShapeDtypeStruct((n_tab,), I32),
                   jax.ShapeDtypeStruct((N_EXPERTS + WEIGHT_SLOTS,), I32)],
        scratch_shapes=[pltpu.SMEM((n_slots,), I32), pltpu.SemaphoreType.DMA],
        name="plan",
    )(dest_flat, counts, unused)


def _expert_kernel(blke_ref, blkn_ref, blkw_ref, rune_ref, slot_ref, xn_hbm,
                   wg_hbm, wu_hbm, wd_hbm, y_hbm, x0, x1, y0, y1, wg_buf, wu_buf, wd_buf,
                   gsem, ssem, wsem):
    xbufs, ybufs = (x0, x1), (y0, y1)
    s = pl.program_id(0)
    n_tok = xn_hbm.shape[0] // PACK_SUBLANES

    def weight_copies(e, slot):
        return [pltpu.make_async_copy(hbm.at[e], buf.at[slot], wsem.at[slot])
                for hbm, buf in ((wg_hbm, wg_buf), (wu_hbm, wu_buf), (wd_hbm, wd_buf))]

    gather_base = s * MOE_BLK
    scatter_base = jnp.maximum(s - 2, 0) * MOE_BLK
    blk = jnp.maximum(s - 1, 0)
    run_info = blkw_ref[blk]
    run = run_info >> 1
    w_slot = lax.rem(run, WEIGHT_SLOTS)
    ahead_run = run + (WEIGHT_SLOTS - 1)
    ahead_expert = rune_ref[ahead_run]
    ahead_slot = lax.rem(ahead_run, WEIGHT_SLOTS)

    def tile(ref, r):
        start = r * PACK_SUBLANES
        if not isinstance(r, int):
            start = pl.multiple_of(start, PACK_SUBLANES)
        return ref.at[pl.ds(start, PACK_SUBLANES)]

    def rows(j):
        n = blkn_ref[jnp.clip(j, 0, blkn_ref.shape[0] - 1)]
        return jnp.where(j >= 0, (n + (ROW_GROUP - 1)) & ~(ROW_GROUP - 1), 0)
    rows0, rows1, rows2, rows3 = rows(s), rows(s - 1), rows(s - 2), rows(s - 3)

    def stage(nxt):
        cur = 1 - nxt
        x_nxt, x_cur, y_nxt, y_cur = xbufs[nxt], xbufs[cur], ybufs[nxt], ybufs[cur]

        def per_group(n_rows, issue_row):
            for g in range(MOE_BLK // ROW_GROUP):
                @pl.when(g * ROW_GROUP < n_rows)
                def _():
                    for r in range(g * ROW_GROUP, (g + 1) * ROW_GROUP):
                        issue_row(r)

        def gather_row(r):
            tok = slot_ref[gather_base + r] & (n_tok - 1)
            pltpu.make_async_copy(tile(xn_hbm, tok), tile(x_nxt, r), gsem.at[nxt]).start()

        def scatter_row(r):
            pltpu.make_async_copy(tile(y_nxt, r), tile(y_hbm, slot_ref[scatter_base + r]),
                                  ssem.at[nxt]).start()

        def wait_rows(src, dst, sem, n_rows):
            n_words = pl.multiple_of(n_rows * PACK_SUBLANES, ROW_GROUP * PACK_SUBLANES)

            @pl.when(n_rows > 0)
            def _():
                pltpu.make_async_copy(src.at[pl.ds(0, n_words)], dst.at[pl.ds(0, n_words)],
                                      sem).wait()

        def compute(n_rows):
            xb = _load_packed_rows(x_cur, 0, n_rows).astype(BF16)
            a = jnp.dot(xb, wg_buf[w_slot].astype(BF16), preferred_element_type=F32)
            u = jnp.dot(xb, wu_buf[w_slot].astype(BF16), preferred_element_type=F32)
            hmid = (a * jax.nn.sigmoid(a) * u).astype(BF16)
            _store_packed_rows(y_cur, 0, jnp.dot(hmid, wd_buf[w_slot].astype(BF16),
                                                 preferred_element_type=F32))

        wait_rows(xn_hbm, x_cur, gsem.at[cur], rows1)
        wait_rows(y_cur, y_hbm, ssem.at[cur], rows3)

        @pl.when((rows1 > 0) & ((run_info & 1) == 1))
        def _():
            for copy in weight_copies(blke_ref[blk], w_slot):
                copy.wait()

            @pl.when(ahead_expert >= 0)
            def _():
                for copy in weight_copies(ahead_expert, ahead_slot):
                    copy.start(priority=WEIGHT_DMA_PRIORITY)

        if nxt == 0:
            for first_run in range(WEIGHT_SLOTS - 1):
                @pl.when((s == 0) & (rune_ref[first_run] >= 0))
                def _():
                    for copy in weight_copies(rune_ref[first_run], first_run):
                        copy.start(priority=WEIGHT_DMA_PRIORITY)

            @pl.when(s == 0)
            def _():
                x_nxt[...] = jnp.zeros(x_nxt.shape, x_nxt.dtype)
                x_cur[...] = jnp.zeros(x_cur.shape, x_cur.dtype)
                y_nxt[...] = jnp.zeros(y_nxt.shape, y_nxt.dtype)
                dump = pltpu.make_async_copy(
                    y_nxt, y_hbm.at[pl.ds(TOP_K * n_tok * PACK_SUBLANES,
                                          MOE_BLK * PACK_SUBLANES)], ssem.at[nxt])
                dump.start()
                dump.wait()

        per_group(rows0, gather_row)
        per_group(rows2, scatter_row)

        pl.when(rows1 > MOE_BLK // 2)(functools.partial(compute, MOE_BLK))
        pl.when((rows1 > 0) & (rows1 <= MOE_BLK // 2))(functools.partial(compute, MOE_BLK // 2))

    for parity in range(2):
        pl.when((s & 1) == parity)(functools.partial(stage, parity))


def _expert_call(blk_e, blk_n, blk_w, run_e, slots, xn, wg, wu, wd):
    T = xn.shape[0] // PACK_SUBLANES
    assert T & (T - 1) == 0, "token id is recovered from the assignment id with a mask"
    n_tab = blk_e.shape[0]
    _, D, F = wg.shape
    packed_block = pltpu.VMEM((MOE_BLK * PACK_SUBLANES, LANES), jnp.uint32)
    hbm = pl.BlockSpec(memory_space=pl.ANY)
    grid_spec = pltpu.PrefetchScalarGridSpec(
        num_scalar_prefetch=5,
        grid=(n_tab,),
        in_specs=[hbm, hbm, hbm, hbm],
        out_specs=hbm,
        scratch_shapes=[packed_block] * 4
        + [pltpu.VMEM((WEIGHT_SLOTS, D, F), F32), pltpu.VMEM((WEIGHT_SLOTS, D, F), F32),
           pltpu.VMEM((WEIGHT_SLOTS, F, D), F32)]
        + [pltpu.SemaphoreType.DMA((2,)), pltpu.SemaphoreType.DMA((2,)),
           pltpu.SemaphoreType.DMA((WEIGHT_SLOTS,))],
    )
    return pl.pallas_call(
        _expert_kernel,
        grid_spec=grid_spec,
        out_shape=jax.ShapeDtypeStruct(((T * TOP_K + MOE_BLK) * PACK_SUBLANES, LANES),
                                       jnp.uint32),
        compiler_params=pltpu.CompilerParams(
            dimension_semantics=("arbitrary",), vmem_limit_bytes=VMEM_LIMIT,
            disable_bounds_checks=True),
        name="experts",
    )(blk_e, blk_n, blk_w, run_e, slots, xn, wg, wu, wd)


def _final_kernel(h1_ref, y1_ref, y2_ref, wts_ref, p_ref, wpp_ref, pg_ref, g3_ref, wpg_hbm,
                  o_ref, wpg_ref, landing, sem):
    _load_weight_as_bf16(wpg_hbm, wpg_ref, landing, sem)
    for i in range(h1_ref.shape[0] // FINAL_SUB):
        rs = pl.ds(i * FINAL_SUB, FINAL_SUB)
        w = wts_ref[rs, :]
        h2 = (h1_ref[rs, :]
              + w[:, 0:1] * _load_packed_rows(y1_ref, i * FINAL_SUB, FINAL_SUB)
              + w[:, 1:2] * _load_packed_rows(y2_ref, i * FINAL_SUB, FINAL_SUB))
        e = _rms(jnp.dot(p_ref[rs, :].astype(BF16), wpp_ref[...], preferred_element_type=F32),
                 pg_ref[...])
        gate = jax.nn.sigmoid(jnp.dot(_rms(h2, g3_ref[...]).astype(BF16), wpg_ref[...],
                                      preferred_element_type=F32))
        o_ref[rs, :] = h2 + gate * e


def _final_call(h1, y, wts, p, wpp, pg, g3, wpg):
    T, D = h1.shape
    tm = TM_FINAL
    row = lambda w: pl.BlockSpec((tm, w), lambda i: (i, 0))
    first = pl.BlockSpec((tm * PACK_SUBLANES, LANES), lambda i: (i, 0))
    second = pl.BlockSpec((tm * PACK_SUBLANES, LANES), lambda i: (i + T // tm, 0))
    return pl.pallas_call(
        _final_kernel,
        grid=(T // tm,),
        in_specs=[row(D), first, second, row(LANES), row(PLE_DIM), _const_spec(wpp.shape),
                  _const_spec(pg.shape), _const_spec(g3.shape),
                  pl.BlockSpec(memory_space=pl.ANY)],
        out_specs=row(D),
        out_shape=jax.ShapeDtypeStruct((T, D), F32),
        scratch_shapes=_bf16_weight_scratch(wpg.shape),
        compiler_params=pltpu.CompilerParams(
            dimension_semantics=("arbitrary",), vmem_limit_bytes=VMEM_LIMIT),
        name="final",
    )(h1, y, y, wts, p, wpp, pg, g3, wpg)


def _rope_lanes(a):
    z = jnp.zeros(a.shape[:-1] + (ROPE_HALF,), a.dtype)
    return jnp.concatenate([a[..., :ROPE_HALF], z, a[..., ROPE_HALF:], z], axis=-1)


def _head_lanes(a):
    return jnp.concatenate([a[..., :QK_NOPE], _rope_lanes(a[..., QK_NOPE:])], axis=-1)


def _score_bound(q_gain, k_gain):
    return Q_PRESCALE * QK_DIM * jnp.max(jnp.abs(q_gain)) * jnp.max(jnp.abs(k_gain))


def kernel(x, p, positions, norm1_gain, w_in, q_a_gain, w_q_b, kv_a_gain, w_kv_b, q_norm_gain, k_norm_gain, w_pool, pool_scale, w_out, norm2_gain, w_router_group, b_router_group, w_router_expert, b_router_expert, w_exp_gate, w_exp_up, w_exp_down, norm3_gain, w_ple_gate, w_ple_proj, ple_norm_gain):
    B, S, D = x.shape
    T = B * S
    assert x.shape[2] == D_MODEL and S % TQ == 0 and S % TM_PRE == 0
    assert T % TM_POST == 0 and T % TM_FINAL == 0
    layer = 0
    row = lambda a: a[layer].reshape(1, -1)

    win = _win_call(w_in[layer].T)
    wq = _head_lanes(w_q_b[layer].reshape(Q_LORA, N_HEADS, QK_DIM)
                     ).reshape(Q_LORA, N_HEADS * HEAD_W).astype(BF16)
    wkv3 = w_kv_b[layer].reshape(KV_LORA, N_HEADS, QK_NOPE + V_DIM)
    wkv = jnp.concatenate([wkv3[..., :QK_NOPE].reshape(KV_LORA, -1),
                           wkv3[..., QK_NOPE:].reshape(KV_LORA, -1)], axis=1).astype(BF16)
    qng = _head_lanes(q_norm_gain[layer]).reshape(1, HEAD_W)
    kng = _head_lanes(k_norm_gain[layer]).reshape(1, HEAD_W)
    inv_freq = np.float32(ROPE_THETA) ** (-np.arange(ROPE_HALF, dtype=np.float32) / ROPE_HALF)
    zeros_half = np.zeros((ROPE_HALF,), np.float32)
    ones_half = np.ones((ROPE_HALF,), np.float32)
    invf = np.concatenate([inv_freq, zeros_half, inv_freq, zeros_half]).reshape(1, LANES)
    sgn = np.concatenate([-ones_half, zeros_half, ones_half, zeros_half]).reshape(1, LANES)
    spare = np.arange(LANES) == SPARE_ROPE_LANE
    kpad = spare.astype(np.float32).reshape(1, LANES)
    pad_lanes = LANES - N_GROUPS - N_EXPERTS
    wr = jnp.concatenate([w_router_group[layer], w_router_expert[layer],
                          jnp.zeros((D, pad_lanes), F32)], axis=1)
    wr_hi = wr.astype(BF16)
    wr = jnp.concatenate([wr_hi, (wr - wr_hi.astype(F32)).astype(BF16)], axis=1)
    br = jnp.concatenate([b_router_group[layer], b_router_expert[layer],
                          jnp.zeros((pad_lanes,), F32)]).reshape(1, LANES)

    bound = _score_bound(q_norm_gain[layer], k_norm_gain[layer])
    bounded = bound <= SCORE_BOUND_LIMIT
    qpad = jnp.where(bounded, -bound, 0.0) * kpad
    ypool, q, k, vt = _pre_call(
        x, positions.reshape(B, S, 1), row(norm1_gain), win, row(q_a_gain), wq,
        row(kv_a_gain), wkv, qng, kng, w_pool[layer].astype(BF16), row(pool_scale), invf, sgn,
        qpad, kpad)
    yattn = _attn_call(bounded.astype(I32).reshape(1), q, k, vt)
    h1, xn2, logits = _post_call(
        x.reshape(T, D), ypool.reshape(T, D_POOL), yattn.reshape(T, D_ATTN),
        w_out[layer], row(norm2_gain), wr, br)
    dest, wts, counts = _route_call(logits)

    n_assign = T * TOP_K
    n_blocks = (n_assign + N_EXPERTS * (MOE_BLK - 1)) // MOE_BLK
    slots, blk_e, blk_n, blk_w, run_e = _plan_call(
        dest[:TOP_K].reshape(n_assign), counts.reshape(LANES), n_blocks + MOE_DRAIN_STEPS)
    y = _expert_call(blk_e, blk_n, blk_w, run_e, slots, xn2,
                     w_exp_gate[layer], w_exp_up[layer], w_exp_down[layer])

    out = _final_call(h1, y, wts, p[layer].reshape(T, PLE_DIM),
                      w_ple_proj[layer].astype(BF16), row(ple_norm_gain), row(norm3_gain),
                      w_ple_gate[layer])
    return out.reshape(B, S, D)
```

```python
import functools
import math

import jax
import jax.numpy as jnp
import numpy as np
from jax import lax
from jax.experimental import pallas as pl
from jax.experimental.pallas import tpu as pltpu

F32 = jnp.float32
BF16 = jnp.bfloat16
I32 = jnp.int32

D_MODEL = 2048
PLE_DIM = 256
EPS = 1e-6
POOL_WINDOWS = (2, 4, 8, 16)
POOL_CH = 256
D_POOL = POOL_CH * len(POOL_WINDOWS)
N_HEADS = 8
Q_LORA = 512
KV_LORA = 512
QK_NOPE = 128
QK_ROPE = 64
QK_DIM = QK_NOPE + QK_ROPE
V_DIM = 128
D_ATTN = N_HEADS * V_DIM
ROPE_THETA = 10000.0
ATTN_SCALE = 1.0 / math.sqrt(QK_DIM)
Q_PRESCALE = ATTN_SCALE * math.log2(math.e)
N_GROUPS = 8
EXPERTS_PER_GROUP = 8
N_EXPERTS = N_GROUPS * EXPERTS_PER_GROUP
TOP_K = 2

LANES = 128
PACK_SUBLANES = 8
VMEM_LIMIT = 56 * 1024 * 1024

HEAD_W = 2 * LANES
ROPE_HALF = QK_ROPE // 2
POOL_HALO = 16
TM_PRE = 512
PRE_SUB = 256
TM_POST = 512
POST_SUB = 256
TM_FINAL = 512
FINAL_SUB = 256
WIN_ROWS = 256
CAST_ROWS = 512
TQ = 512
TK = 256
TM_ROUTE = 1024
ROUTE_SUB = 256
MOE_BLK = 256
MOE_DRAIN_STEPS = 3
ROW_GROUP = 32
WEIGHT_DMA_PRIORITIES = (1, 1, 0)
WEIGHT_SLOTS = 3
PLAN_UNROLL = 32
NEG_BIG = -1e30
V_ROWS = V_DIM + 16
SPARE_ROPE_LANE = ROPE_HALF
SCORE_BOUND_LIMIT = 50.0


def _const_spec(shape):
    nd = len(shape)
    return pl.BlockSpec(shape, lambda *_: (0,) * nd, pipeline_mode=pl.Buffered(1))


def _rms(x, gain):
    return x * lax.rsqrt(jnp.mean(x * x, axis=-1, keepdims=True) + EPS) * gain


def _load_weight_as_bf16(w_hbm, w_bf16, landing, sem):
    n_chunks = w_hbm.shape[0] // CAST_ROWS

    def chunk(c):
        return pltpu.make_async_copy(w_hbm.at[pl.ds(c * CAST_ROWS, CAST_ROWS)],
                                     landing.at[c % 2], sem.at[c % 2])

    @pl.when(pl.program_id(0) == 0)
    def _():
        chunk(0).start()
        for c in range(n_chunks):
            if c + 1 < n_chunks:
                chunk(c + 1).start()
            chunk(c).wait()
            w_bf16[pl.ds(c * CAST_ROWS, CAST_ROWS), :] = landing[c % 2].astype(BF16)


def _bf16_weight_scratch(shape):
    rows, cols = shape
    assert rows % CAST_ROWS == 0
    return [pltpu.VMEM((rows, cols), BF16), pltpu.VMEM((2, CAST_ROWS, cols), F32),
            pltpu.SemaphoreType.DMA((2,))]


def _store_packed_rows(ref, first_row, x):
    rows, d = x.shape
    half = d // 2
    assert half == PACK_SUBLANES * LANES
    for i in range(PACK_SUBLANES):
        lo = x[:, i * LANES:(i + 1) * LANES].astype(BF16).astype(F32)
        hi = x[:, half + i * LANES:half + (i + 1) * LANES].astype(BF16).astype(F32)
        word = (lax.shift_right_logical(pltpu.bitcast(lo, jnp.uint32), jnp.uint32(16))
                | (pltpu.bitcast(hi, jnp.uint32) & jnp.uint32(0xFFFF0000)))
        ref[pl.ds(first_row * PACK_SUBLANES + i, rows, stride=PACK_SUBLANES), :] = word


def _load_packed_rows(ref, first_row, rows):
    lo, hi = [], []
    for i in range(PACK_SUBLANES):
        word = ref[pl.ds(first_row * PACK_SUBLANES + i, rows, stride=PACK_SUBLANES), :]
        lo.append(pltpu.bitcast(lax.shift_left(word, jnp.uint32(16)), F32))
        hi.append(pltpu.bitcast(word & jnp.uint32(0xFFFF0000), F32))
    return jnp.concatenate(lo + hi, axis=1)


def _pre_kernel(x_ref, pos_ref, g1_ref, win_ref, qag_ref, wq_ref, kvag_ref, wkv_ref,
                qng_ref, kng_ref, wpool_ref, pscale_ref, invf_ref, sgn_ref, qpad_ref, kpad_ref,
                ypool_ref, q_ref, k_ref, vt_ref, carry_ref):
    st = pl.program_id(1)
    tm = PRE_SUB
    n_sub = x_ref.shape[0] // tm

    @pl.when(st == 0)
    def _():
        carry_ref[...] = jnp.zeros_like(carry_ref)

    zs = [_pre_project(pl.ds(i * tm, tm), x_ref, g1_ref, win_ref) for i in range(n_sub)]
    halo = carry_ref[...]
    mixed = []
    for i in range(n_sub):
        first_pos = st * x_ref.shape[0] + i * tm
        qa, kv, halo = _pre_mix(pl.ds(i * tm, tm), first_pos, zs[i], halo, qag_ref, wq_ref,
                                kvag_ref, wkv_ref, wpool_ref, pscale_ref, ypool_ref)
        mixed.append((qa, kv))
    carry_ref[...] = halo
    for i in range(n_sub):
        k_rope = zs[i][:, D_POOL + Q_LORA + KV_LORA:]
        _pre_heads(i, *mixed[i], k_rope, pos_ref, qng_ref, kng_ref, invf_ref, sgn_ref,
                   qpad_ref, kpad_ref, q_ref, k_ref, vt_ref)


def _pre_project(rs, x_ref, g1_ref, win_ref):
    hn = _rms(x_ref[rs, :], g1_ref[...])
    return jnp.dot(hn.astype(BF16), win_ref[...], preferred_element_type=F32)


def _pre_mix(rs, first_pos, z, halo, qag_ref, wq_ref, kvag_ref, wkv_ref, wpool_ref, pscale_ref,
             ypool_ref):
    tm = z.shape[0]
    u = z[:, :D_POOL]
    ext = jnp.concatenate([halo, u], axis=0)
    row = lax.broadcasted_iota(I32, (tm, 1), 0) + first_pos
    level = ext
    shift = 1
    for g, w in enumerate(POOL_WINDOWS):
        sl = slice(g * POOL_CH, (g + 1) * POOL_CH)
        while shift < w:
            level = level + pltpu.roll(level, shift, 0)
            shift *= 2
        win_sum = level[POOL_HALO:, sl]
        cnt = jnp.minimum(row + 1, w).astype(F32)
        d = win_sum / cnt - u[:, sl]
        y = jnp.dot(d.astype(BF16), wpool_ref[g], preferred_element_type=F32)
        ypool_ref[rs, sl] = (y * pscale_ref[:, sl]).astype(BF16)

    q_lat = z[:, D_POOL:D_POOL + Q_LORA]
    kv_lat = z[:, D_POOL + Q_LORA:D_POOL + Q_LORA + KV_LORA]
    qa = jnp.dot(_rms(q_lat, qag_ref[...]).astype(BF16), wq_ref[...],
                 preferred_element_type=F32)
    kv = jnp.dot(_rms(kv_lat, kvag_ref[...]).astype(BF16), wkv_ref[...],
                 preferred_element_type=F32)
    return qa, kv, u[tm - POOL_HALO:, :]


def _pre_heads(i, qa, kv, k_rope, pos_ref, qng_ref, kng_ref, invf_ref, sgn_ref, qpad_ref,
               kpad_ref, q_ref, k_ref, vt_ref):
    tm = qa.shape[0]
    rs = pl.ds(i * tm, tm)
    ang = pos_ref[rs, :].astype(F32) * invf_ref[...]
    cos = jnp.cos(ang)
    sin = jnp.sin(ang) * sgn_ref[...]

    def rot(t):
        return t * cos + pltpu.roll(t, LANES // 2, 1) * sin

    qng = qng_ref[...]
    kng = kng_ref[...]
    kr_rot = rot(k_rope * kng[:, LANES:])
    kr_ssq = jnp.sum(k_rope * k_rope, axis=-1, keepdims=True)
    for h in range(N_HEADS):
        qh = qa[:, h * HEAD_W:(h + 1) * HEAD_W]
        rq = lax.rsqrt(jnp.sum(qh * qh, axis=-1, keepdims=True) / QK_DIM + EPS) * Q_PRESCALE
        qn = qh * rq * qng
        q_ref[0, h, rs, :LANES] = qn[:, :LANES].astype(BF16)
        q_ref[0, h, rs, LANES:] = (rot(qn[:, LANES:]) + qpad_ref[...]).astype(BF16)
        kh = kv[:, h * QK_NOPE:(h + 1) * QK_NOPE]
        rk = lax.rsqrt((jnp.sum(kh * kh, axis=-1, keepdims=True) + kr_ssq) / QK_DIM + EPS)
        k_ref[0, h, rs, :LANES] = (kh * rk * kng[:, :LANES]).astype(BF16)
        k_ref[0, h, rs, LANES:] = (kr_rot * rk + kpad_ref[...]).astype(BF16)
        vh = kv[:, N_HEADS * QK_NOPE + h * V_DIM:N_HEADS * QK_NOPE + (h + 1) * V_DIM]
        key_tile, keys = (i * tm) // TK, pl.ds((i * tm) % TK, tm)
        vt_ref[0, h, key_tile, :V_DIM, keys] = vh.T.astype(BF16)
        vt_ref[0, h, key_tile, V_DIM:, keys] = jnp.ones((V_ROWS - V_DIM, tm), BF16)


def _win_kernel(wt_ref, o_ref):
    w = wt_ref[...].T
    n_lat = o_ref.shape[1] - LANES
    zeros = jnp.zeros((w.shape[0], ROPE_HALF), F32)
    rope = jnp.concatenate([w[:, n_lat:n_lat + ROPE_HALF], zeros,
                            w[:, n_lat + ROPE_HALF:], zeros], axis=1)
    o_ref[:, :n_lat] = w[:, :n_lat].astype(BF16)
    o_ref[:, n_lat:] = rope.astype(BF16)


def _win_call(w_in_t):
    n_in, D = w_in_t.shape
    n_lat = n_in - QK_ROPE
    tm = WIN_ROWS
    return pl.pallas_call(
        _win_kernel,
        grid=(D // tm,),
        in_specs=[pl.BlockSpec((n_in, tm), lambda i: (0, i))],
        out_specs=pl.BlockSpec((tm, n_lat + LANES), lambda i: (i, 0)),
        out_shape=jax.ShapeDtypeStruct((D, n_lat + LANES), BF16),
        compiler_params=pltpu.CompilerParams(
            dimension_semantics=("arbitrary",), vmem_limit_bytes=VMEM_LIMIT),
        name="win",
    )(w_in_t)


def _pre_call(x, pos, g1, win, qag, wq, kvag, wkv, qng, kng, wpool, pscale, invf, sgn,
              qpad, kpad):
    B, S, D = x.shape
    tm = TM_PRE
    grid = (B, S // tm)
    row_spec = lambda w: pl.BlockSpec((None, tm, w), lambda b, s: (b, s, 0))
    head_spec = lambda w: pl.BlockSpec((1, N_HEADS, tm, w), lambda b, s: (b, 0, s, 0))
    consts = [g1, win, qag, wq, kvag, wkv, qng, kng, wpool, pscale, invf, sgn, qpad, kpad]
    return pl.pallas_call(
        _pre_kernel,
        grid=grid,
        in_specs=[row_spec(D), row_spec(1)] + [_const_spec(c.shape) for c in consts],
        out_specs=[row_spec(D_POOL), head_spec(HEAD_W), head_spec(HEAD_W),
                   pl.BlockSpec((1, N_HEADS, tm // TK, V_ROWS, TK),
                                lambda b, s: (b, 0, s, 0, 0))],
        out_shape=[jax.ShapeDtypeStruct((B, S, D_POOL), BF16),
                   jax.ShapeDtypeStruct((B, N_HEADS, S, HEAD_W), BF16),
                   jax.ShapeDtypeStruct((B, N_HEADS, S, HEAD_W), BF16),
                   jax.ShapeDtypeStruct((B, N_HEADS, S // TK, V_ROWS, TK), BF16)],
        scratch_shapes=[pltpu.VMEM((POOL_HALO, D_POOL), F32)],
        compiler_params=pltpu.CompilerParams(
            dimension_semantics=("arbitrary", "arbitrary"), vmem_limit_bytes=VMEM_LIMIT),
        name="pre",
    )(x, pos, *consts)


def _attn_kernel(bounded_ref, q_ref, k_ref, vt_ref, o_ref, *chain_scratch):
    S = q_ref.shape[2]
    nq = S // TQ
    per_q = TQ // TK
    assert per_q == 2, "two online-softmax chains take the even / odd key tiles"
    chains = (chain_scratch[0:2], chain_scratch[2:4])
    score_bufs = (chain_scratch[4:6], chain_scratch[6:8])
    key_idx = lax.broadcasted_iota(I32, (TK, TQ), 0)
    qry_idx = lax.broadcasted_iota(I32, (TK, TQ), 1)

    diag_masks = (key_idx <= qry_idx, key_idx + TK <= qry_idx)

    def weighted_values(kt, p):
        return jnp.dot(vt_ref[0, 0, kt], p.astype(BF16), preferred_element_type=F32)

    def write_out(qi, acc):
        out = acc[:V_DIM, :] / acc[V_DIM:V_DIM + 1, :]
        start = qi * TQ if isinstance(qi, int) else pl.multiple_of(qi * TQ, TQ)
        o_ref[0, pl.ds(start, TQ), :] = out.T.astype(BF16)

    def bounded_head():
        steps = [(qi, j) for qi in range(nq) for j in range(qi + 1)]

        def pair_scores(step, buf):
            qi, j = step
            q = q_ref[0, 0, qi * TQ:(qi + 1) * TQ, :]
            for c in range(per_q):
                kt = per_q * j + c
                buf[c][...] = lax.dot_general(k_ref[0, 0, kt * TK:(kt + 1) * TK, :], q,
                                              (((1,), (1,)), ((), ())),
                                              preferred_element_type=F32)

        pair_scores(steps[0], score_bufs[0])
        acc = None
        for t, (qi, j) in enumerate(steps):
            if t + 1 < len(steps):
                pair_scores(steps[t + 1], score_bufs[(t + 1) % 2])
            for c in range(per_q):
                st = score_bufs[t % 2][c][...]
                if j == qi:
                    st = jnp.where(diag_masks[c], st, NEG_BIG)
                pv = weighted_values(per_q * j + c, jnp.exp2(st))
                acc = pv if acc is None else acc + pv
            if j == qi:
                write_out(qi, acc)
                acc = None

    def online_q_tile(qi, _):
        q = q_ref[0, 0, pl.ds(pl.multiple_of(qi * TQ, TQ), TQ), :]
        for m_ref, acc_ref in chains:
            m_ref[...] = jnp.full(m_ref.shape, NEG_BIG, F32)
            acc_ref[...] = jnp.zeros(acc_ref.shape, F32)

        def scores(kt):
            k = k_ref[0, 0, pl.ds(pl.multiple_of(kt * TK, TK), TK), :]
            return lax.dot_general(k, q, (((1,), (1,)), ((), ())), preferred_element_type=F32)

        def fold(chain, st, kt, mask):
            m_ref, acc_ref = chain
            if mask is not None:
                st = jnp.where(mask, st, NEG_BIG)
            m = m_ref[...]
            m_new = jnp.maximum(m, jnp.max(st, axis=0, keepdims=True))
            m_ref[...] = m_new
            acc_ref[...] = (jnp.exp2(m - m_new) * acc_ref[...]
                            + weighted_values(kt, jnp.exp2(st - m_new)))

        def pair_scores(j, buf):
            for c in range(per_q):
                buf[c][...] = scores(per_q * j + c)

        def fold_pair(j, buf, masks=(None, None)):
            for c in range(per_q):
                fold(chains[c], buf[c][...], per_q * j + c, masks[c])

        pair_scores(0, score_bufs[0])

        def two_pairs(i, _):
            pair_scores(2 * i + 1, score_bufs[1])
            fold_pair(2 * i, score_bufs[0])
            pair_scores(2 * i + 2, score_bufs[0])
            fold_pair(2 * i + 1, score_bufs[1])
            return 0
        lax.fori_loop(0, qi // 2, two_pairs, 0)

        @pl.when(qi % 2 == 0)
        def _():
            fold_pair(qi, score_bufs[0], diag_masks)

        @pl.when(qi % 2 == 1)
        def _():
            pair_scores(qi, score_bufs[1])
            fold_pair(qi - 1, score_bufs[0])
            fold_pair(qi, score_bufs[1], diag_masks)

        (m0, acc0), (m1, acc1) = chains
        m = jnp.maximum(m0[...], m1[...])
        write_out(qi, jnp.exp2(m0[...] - m) * acc0[...] + jnp.exp2(m1[...] - m) * acc1[...])
        return 0

    pl.when(bounded_ref[0] != 0)(bounded_head)

    @pl.when(bounded_ref[0] == 0)
    def _():
        lax.fori_loop(0, nq, online_q_tile, 0)


def _attn_call(bounded, q, k, vt):
    B, H, S, _ = q.shape
    head = lambda w: pl.BlockSpec((1, 1, S, w), lambda b, h, flag: (b, h, 0, 0))
    grid_spec = pltpu.PrefetchScalarGridSpec(
        num_scalar_prefetch=1,
        grid=(B, H),
        in_specs=[head(HEAD_W), head(HEAD_W),
                  pl.BlockSpec((1, 1) + vt.shape[2:], lambda b, h, flag: (b, h, 0, 0, 0))],
        out_specs=pl.BlockSpec((1, S, V_DIM), lambda b, h, flag: (b, 0, h)),
        scratch_shapes=[pltpu.VMEM((1, TQ), F32), pltpu.VMEM((V_ROWS, TQ), F32)] * 2
        + [pltpu.VMEM((TK, TQ), F32)] * 4,
    )
    return pl.pallas_call(
        _attn_kernel,
        grid_spec=grid_spec,
        out_shape=jax.ShapeDtypeStruct((B, S, H * V_DIM), BF16),
        compiler_params=pltpu.CompilerParams(
            dimension_semantics=("arbitrary", "arbitrary"), vmem_limit_bytes=VMEM_LIMIT),
        name="attn",
    )(bounded, q, k, vt)


def _post_kernel(x_ref, yp_ref, ya_ref, wo_hbm, g2_ref, wr_ref, br_ref,
                 h1_ref, xn_ref, lg_ref, wo_ref, landing, sem):
    _load_weight_as_bf16(wo_hbm, wo_ref, landing, sem)
    for i in range(x_ref.shape[0] // POST_SUB):
        rs = pl.ds(i * POST_SUB, POST_SUB)
        h1 = (x_ref[rs, :]
              + jnp.dot(yp_ref[rs, :], wo_ref[:D_POOL, :], preferred_element_type=F32)
              + jnp.dot(ya_ref[rs, :], wo_ref[D_POOL:, :], preferred_element_type=F32))
        h1_ref[rs, :] = h1
        xn = _rms(h1, g2_ref[...])
        _store_packed_rows(xn_ref, i * POST_SUB, xn)
        xn_hi = xn.astype(BF16)
        xn_lo = (xn - xn_hi.astype(F32)).astype(BF16)
        both = (jnp.dot(xn_hi, wr_ref[...], preferred_element_type=F32)
                + jnp.dot(xn_lo, wr_ref[...], preferred_element_type=F32))
        lg_ref[rs, :] = both[:, :LANES] + both[:, LANES:] + br_ref[...]


def _post_call(x, yp, ya, wo, g2, wr, br):
    T, D = x.shape
    tm = TM_POST
    row = lambda w: pl.BlockSpec((tm, w), lambda i: (i, 0))
    return pl.pallas_call(
        _post_kernel,
        grid=(T // tm,),
        in_specs=[row(D), row(D_POOL), row(D_ATTN), pl.BlockSpec(memory_space=pl.ANY),
                  _const_spec(g2.shape), _const_spec(wr.shape), _const_spec(br.shape)],
        out_specs=[row(D), pl.BlockSpec((tm * PACK_SUBLANES, LANES), lambda i: (i, 0)),
                   row(LANES)],
        out_shape=[jax.ShapeDtypeStruct((T, D), F32),
                   jax.ShapeDtypeStruct((T * PACK_SUBLANES, LANES), jnp.uint32),
                   jax.ShapeDtypeStruct((T, LANES), F32)],
        scratch_shapes=_bf16_weight_scratch(wo.shape),
        compiler_params=pltpu.CompilerParams(
            dimension_semantics=("arbitrary",), vmem_limit_bytes=VMEM_LIMIT),
        name="post",
    )(x, yp, ya, wo, g2, wr, br)


def _route_kernel(lg_ref, dest_ref, wts_ref, cnt_ref, carry_ref, code_ref, start_ref):
    n_tiles = code_ref.shape[0]
    g = pl.program_id(0)

    @pl.when(g == 0)
    def _():
        carry_ref[...] = jnp.zeros_like(carry_ref)

    @pl.when(g < n_tiles)
    def _():
        for i in range(lg_ref.shape[0] // ROUTE_SUB):
            rs = pl.ds(i * ROUTE_SUB, ROUTE_SUB)
            code_t, wts_t = _route_cols(lg_ref[rs, :].T, carry_ref)
            code_ref[g, :, rs] = code_t
            wts_cols = jnp.concatenate(
                [wts_t, jnp.zeros((LANES - PACK_SUBLANES, ROUTE_SUB), F32)], axis=0)
            wts_ref[rs, :] = wts_cols.T
        cnt_ref[...] = carry_ref[...].astype(I32)

    @pl.when(g == n_tiles)
    def _():
        blocks = jnp.floor((carry_ref[...] + (MOE_BLK - 1)) * (1.0 / MOE_BLK))
        r = lax.broadcasted_iota(I32, (LANES, LANES), 0)
        c = lax.broadcasted_iota(I32, (LANES, LANES), 1)
        earlier = jnp.where(c < r, 1.0, 0.0).astype(BF16)
        before = jnp.dot(earlier, jnp.broadcast_to(blocks, (LANES, LANES)).astype(BF16),
                         preferred_element_type=F32)
        start_ref[...] = before[:, 0:1] * MOE_BLK

    @pl.when(g >= n_tiles)
    def _():
        code = code_ref[g - n_tiles]
        expert = lax.shift_right_logical(code, 16)
        rank = code & 0xFFFF
        expert_row = lax.broadcasted_iota(I32, (LANES, code.shape[1]), 0)
        sub = lax.broadcasted_iota(I32, code.shape, 0)
        dest = jnp.zeros(code.shape, I32)
        for choice in range(TOP_K):
            hit = expert_row == expert[choice:choice + 1, :]
            start = jnp.sum(jnp.where(hit, start_ref[...], 0.0), axis=0, keepdims=True)
            dest = jnp.where(sub == choice, start.astype(I32) + rank, dest)
        dest_ref[...] = dest


def _route_call(logits):
    T = logits.shape[0]
    tm = TM_ROUTE
    n_tiles = T // tm
    pass0 = lambda g: jnp.minimum(g, n_tiles - 1)
    pass1 = lambda g: jnp.maximum(g - n_tiles, 0)
    return pl.pallas_call(
        _route_kernel,
        grid=(2 * n_tiles,),
        in_specs=[pl.BlockSpec((tm, LANES), lambda g: (pass0(g), 0))],
        out_specs=[pl.BlockSpec((PACK_SUBLANES, tm), lambda g: (0, pass1(g))),
                   pl.BlockSpec((tm, LANES), lambda g: (pass0(g), 0)),
                   pl.BlockSpec((LANES, 1), lambda g: (0, 0))],
        out_shape=[jax.ShapeDtypeStruct((PACK_SUBLANES, T), I32),
                   jax.ShapeDtypeStruct((T, LANES), F32),
                   jax.ShapeDtypeStruct((LANES, 1), I32)],
        scratch_shapes=[pltpu.VMEM((LANES, 1), F32),
                        pltpu.VMEM((n_tiles, PACK_SUBLANES, tm), I32),
                        pltpu.VMEM((LANES, 1), F32)],
        compiler_params=pltpu.CompilerParams(
            dimension_semantics=("arbitrary",), vmem_limit_bytes=VMEM_LIMIT),
        name="route",
    )(logits)


def _route_cols(lt, carry_ref):
    tm = lt.shape[1]
    sub = lax.broadcasted_iota(I32, (PACK_SUBLANES, tm), 0).astype(F32)

    def first_argmax(vals):
        mx = jnp.max(vals, axis=0, keepdims=True)
        idx = jnp.min(jnp.where(vals == mx, sub, float(PACK_SUBLANES)), axis=0, keepdims=True)
        return mx, idx

    g_logits = lt[0:N_GROUPS, :]
    g_max, g_sel = first_argmax(g_logits)
    g_w = 1.0 / jnp.sum(jnp.exp(g_logits - g_max), axis=0, keepdims=True)

    e_logits = lt[N_GROUPS:N_GROUPS + EXPERTS_PER_GROUP, :]
    for g in range(1, N_GROUPS):
        lo = N_GROUPS + g * EXPERTS_PER_GROUP
        e_logits = jnp.where(g_sel == float(g), lt[lo:lo + EXPERTS_PER_GROUP, :], e_logits)
    v1, i1 = first_argmax(e_logits)
    v2, i2 = first_argmax(jnp.where(sub == i1, -jnp.inf, e_logits))
    e1 = (g_sel * EXPERTS_PER_GROUP + i1).astype(I32)
    e2 = (g_sel * EXPERTS_PER_GROUP + i2).astype(I32)
    t = jnp.exp(v2 - v1)
    w1 = g_w / (1.0 + t)
    w2 = g_w * t / (1.0 + t)

    expert_row = lax.broadcasted_iota(I32, (LANES, tm), 0)
    hit1 = expert_row == e1
    hit2 = expert_row == e2
    onehot = jnp.where(hit1 | hit2, 1.0, 0.0)
    r = lax.broadcasted_iota(I32, (tm, tm), 0)
    c = lax.broadcasted_iota(I32, (tm, tm), 1)
    earlier = jnp.where(r < c, 1.0, 0.0).astype(BF16)
    before = (jnp.dot(onehot.astype(BF16), earlier, preferred_element_type=F32)
              + carry_ref[...])
    carry_ref[...] += jnp.sum(onehot, axis=1, keepdims=True)
    pos1 = jnp.sum(jnp.where(hit1, before, 0.0), axis=0, keepdims=True).astype(I32)
    pos2 = jnp.sum(jnp.where(hit2, before, 0.0), axis=0, keepdims=True).astype(I32)
    code1 = e1 * 65536 + pos1
    code2 = e2 * 65536 + pos2
    first, second = sub == 0.0, sub == 1.0
    return (jnp.where(first, code1, jnp.where(second, code2, 0)),
            jnp.where(first, w1, jnp.where(second, w2, 0.0)))


def _plan_kernel(dest_ref, cnt_ref, unused_hbm, slots_hbm, blke_ref, blkn_ref, blkw_ref,
                 rune_ref, slot_ref, sem):
    n_assign = dest_ref.shape[0]
    n_blocks = blke_ref.shape[0]
    load = pltpu.make_async_copy(unused_hbm, slot_ref, sem)
    load.start()

    def per_expert(e, nb_done):
        cnt = cnt_ref[e]
        nb = lax.shift_right_logical(cnt + (MOE_BLK - 1), MOE_BLK.bit_length() - 1)

        def per_block(j, _):
            blke_ref[nb_done + j] = e
            blkn_ref[nb_done + j] = jnp.minimum(cnt - j * MOE_BLK, MOE_BLK)
            return 0
        lax.fori_loop(0, nb, per_block, 0)
        return nb_done + nb
    used = lax.fori_loop(0, N_EXPERTS, per_expert, 0)

    last_e = blke_ref[jnp.maximum(used - 1, 0)]

    def tail(b, _):
        blke_ref[b] = last_e
        blkn_ref[b] = 0
        blkw_ref[b] = 0
        return 0
    lax.fori_loop(used, n_blocks, tail, 0)

    def no_run(r, _):
        rune_ref[r] = -1
        return 0
    lax.fori_loop(0, rune_ref.shape[0], no_run, 0)

    def runs_forward(b, carry):
        earlier_e, run = carry
        e = blke_ref[b]
        first = (e != earlier_e).astype(I32)
        run = run + first
        blkw_ref[b] = first + run * 2
        rune_ref[run] = e
        return e, run
    lax.fori_loop(0, used, runs_forward, (-1, -1))

    load.wait()

    def place(a, _):
        slot_ref[dest_ref[a]] = a
        return 0
    lax.fori_loop(0, n_assign, place, 0, unroll=PLAN_UNROLL)

    store = pltpu.make_async_copy(slot_ref, slots_hbm, sem)
    store.start()
    store.wait()


def _plan_call(dest_flat, counts, n_tab):
    n_assign = dest_flat.shape[0]
    n_slots = n_tab * MOE_BLK
    unused = (n_assign + (np.arange(n_slots) & (MOE_BLK - 1))).astype(np.int32)
    smem = pl.BlockSpec(memory_space=pltpu.SMEM)
    hbm = pl.BlockSpec(memory_space=pl.ANY)
    return pl.pallas_call(
        _plan_kernel,
        in_specs=[smem, smem, hbm],
        out_specs=[hbm, smem, smem, smem, smem],
        out_shape=[jax.ShapeDtypeStruct((n_slots,), I32),
                   jax.ShapeDtypeStruct((n_tab,), I32),
                   jax.ShapeDtypeStruct((n_tab,), I32),
                   jax.ShapeDtypeStruct((n_tab,), I32),
                   jax.ShapeDtypeStruct((N_EXPERTS + WEIGHT_SLOTS,), I32)],
        scratch_shapes=[pltpu.SMEM((n_slots,), I32), pltpu.SemaphoreType.DMA],
        name="plan",
    )(dest_flat, counts, unused)


def _expert_kernel(blke_ref, blkn_ref, blkw_ref, rune_ref, slot_ref, xn_hbm,
                   wg_hbm, wu_hbm, wd_hbm, y_hbm, x0, x1, y0, y1, wg_buf, wu_buf, wd_buf,
                   gsem, ssem, wsem):
    xbufs, ybufs = (x0, x1), (y0, y1)
    s = pl.program_id(0)
    n_tok = xn_hbm.shape[0] // PACK_SUBLANES

    def weight_copies(e, slot):
        return [pltpu.make_async_copy(hbm.at[e], buf.at[slot], wsem.at[slot])
                for hbm, buf in ((wg_hbm, wg_buf), (wu_hbm, wu_buf), (wd_hbm, wd_buf))]

    gather_base = s * MOE_BLK
    scatter_base = jnp.maximum(s - 2, 0) * MOE_BLK
    blk = jnp.maximum(s - 1, 0)
    run_info = blkw_ref[blk]
    run = run_info >> 1
    w_slot = lax.rem(run, WEIGHT_SLOTS)
    ahead_run = run + (WEIGHT_SLOTS - 1)
    ahead_expert = rune_ref[ahead_run]
    ahead_slot = lax.rem(ahead_run, WEIGHT_SLOTS)

    def tile(ref, r):
        start = r * PACK_SUBLANES
        if not isinstance(r, int):
            start = pl.multiple_of(start, PACK_SUBLANES)
        return ref.at[pl.ds(start, PACK_SUBLANES)]

    def rows(j):
        n = blkn_ref[jnp.clip(j, 0, blkn_ref.shape[0] - 1)]
        return jnp.where(j >= 0, (n + (ROW_GROUP - 1)) & ~(ROW_GROUP - 1), 0)
    rows0, rows1, rows2, rows3 = rows(s), rows(s - 1), rows(s - 2), rows(s - 3)

    def stage(nxt):
        cur = 1 - nxt
        x_nxt, x_cur, y_nxt, y_cur = xbufs[nxt], xbufs[cur], ybufs[nxt], ybufs[cur]

        def per_group(n_rows, issue_row):
            for g in range(MOE_BLK // ROW_GROUP):
                @pl.when(g * ROW_GROUP < n_rows)
                def _():
                    for r in range(g * ROW_GROUP, (g + 1) * ROW_GROUP):
                        issue_row(r)

        def gather_row(r):
            tok = slot_ref[gather_base + r] & (n_tok - 1)
            pltpu.make_async_copy(tile(xn_hbm, tok), tile(x_nxt, r), gsem.at[nxt]).start()

        def scatter_row(r):
            pltpu.make_async_copy(tile(y_nxt, r), tile(y_hbm, slot_ref[scatter_base + r]),
                                  ssem.at[nxt]).start()

        def wait_rows(src, dst, sem, n_rows):
            n_words = pl.multiple_of(n_rows * PACK_SUBLANES, ROW_GROUP * PACK_SUBLANES)

            @pl.when(n_rows > 0)
            def _():
                pltpu.make_async_copy(src.at[pl.ds(0, n_words)], dst.at[pl.ds(0, n_words)],
                                      sem).wait()

        def compute():
            xb = _load_packed_rows(x_cur, 0, MOE_BLK).astype(BF16)
            a = jnp.dot(xb, wg_buf[w_slot].astype(BF16), preferred_element_type=F32)
            u = jnp.dot(xb, wu_buf[w_slot].astype(BF16), preferred_element_type=F32)
            hmid = (a * jax.nn.sigmoid(a) * u).astype(BF16)
            _store_packed_rows(y_cur, 0, jnp.dot(hmid, wd_buf[w_slot].astype(BF16),
                                                 preferred_element_type=F32))

        wait_rows(xn_hbm, x_cur, gsem.at[cur], rows1)
        wait_rows(y_cur, y_hbm, ssem.at[cur], rows3)

        @pl.when((rows1 > 0) & ((run_info & 1) == 1))
        def _():
            for copy in weight_copies(blke_ref[blk], w_slot):
                copy.wait()

            @pl.when(ahead_expert >= 0)
            def _():
                for copy, priority in zip(weight_copies(ahead_expert, ahead_slot),
                                          WEIGHT_DMA_PRIORITIES):
                    copy.start(priority=priority)

        if nxt == 0:
            for first_run in range(WEIGHT_SLOTS - 1):
                @pl.when((s == 0) & (rune_ref[first_run] >= 0))
                def _():
                    for copy, priority in zip(weight_copies(rune_ref[first_run], first_run),
                                              WEIGHT_DMA_PRIORITIES):
                        copy.start(priority=priority)

            @pl.when(s == 0)
            def _():
                x_nxt[...] = jnp.zeros(x_nxt.shape, x_nxt.dtype)
                x_cur[...] = jnp.zeros(x_cur.shape, x_cur.dtype)
                y_nxt[...] = jnp.zeros(y_nxt.shape, y_nxt.dtype)
                dump = pltpu.make_async_copy(
                    y_nxt, y_hbm.at[pl.ds(TOP_K * n_tok * PACK_SUBLANES,
                                          MOE_BLK * PACK_SUBLANES)], ssem.at[nxt])
                dump.start()
                dump.wait()

        per_group(rows0, gather_row)
        per_group(rows2, scatter_row)

        @pl.when(rows1 > 0)
        def _():
            compute()

    for parity in range(2):
        pl.when((s & 1) == parity)(functools.partial(stage, parity))


def _expert_call(blk_e, blk_n, blk_w, run_e, slots, xn, wg, wu, wd):
    T = xn.shape[0] // PACK_SUBLANES
    assert T & (T - 1) == 0, "token id is recovered from the assignment id with a mask"
    n_tab = blk_e.shape[0]
    _, D, F = wg.shape
    packed_block = pltpu.VMEM((MOE_BLK * PACK_SUBLANES, LANES), jnp.uint32)
    hbm = pl.BlockSpec(memory_space=pl.ANY)
    grid_spec = pltpu.PrefetchScalarGridSpec(
        num_scalar_prefetch=5,
        grid=(n_tab,),
        in_specs=[hbm, hbm, hbm, hbm],
        out_specs=hbm,
        scratch_shapes=[packed_block] * 4
        + [pltpu.VMEM((WEIGHT_SLOTS, D, F), F32), pltpu.VMEM((WEIGHT_SLOTS, D, F), F32),
           pltpu.VMEM((WEIGHT_SLOTS, F, D), F32)]
        + [pltpu.SemaphoreType.DMA((2,)), pltpu.SemaphoreType.DMA((2,)),
           pltpu.SemaphoreType.DMA((WEIGHT_SLOTS,))],
    )
    return pl.pallas_call(
        _expert_kernel,
        grid_spec=grid_spec,
        out_shape=jax.ShapeDtypeStruct(((T * TOP_K + MOE_BLK) * PACK_SUBLANES, LANES),
                                       jnp.uint32),
        compiler_params=pltpu.CompilerParams(
            dimension_semantics=("arbitrary",), vmem_limit_bytes=VMEM_LIMIT,
            disable_bounds_checks=True),
        name="experts",
    )(blk_e, blk_n, blk_w, run_e, slots, xn, wg, wu, wd)


def _final_kernel(h1_ref, y1_ref, y2_ref, wts_ref, p_ref, wpp_ref, pg_ref, g3_ref, wpg_hbm,
                  o_ref, wpg_ref, landing, sem):
    _load_weight_as_bf16(wpg_hbm, wpg_ref, landing, sem)
    for i in range(h1_ref.shape[0] // FINAL_SUB):
        rs = pl.ds(i * FINAL_SUB, FINAL_SUB)
        w = wts_ref[rs, :]
        h2 = (h1_ref[rs, :]
              + w[:, 0:1] * _load_packed_rows(y1_ref, i * FINAL_SUB, FINAL_SUB)
              + w[:, 1:2] * _load_packed_rows(y2_ref, i * FINAL_SUB, FINAL_SUB))
        e = _rms(jnp.dot(p_ref[rs, :].astype(BF16), wpp_ref[...], preferred_element_type=F32),
                 pg_ref[...])
        gate = jax.nn.sigmoid(jnp.dot(_rms(h2, g3_ref[...]).astype(BF16), wpg_ref[...],
                                      preferred_element_type=F32))
        o_ref[rs, :] = h2 + gate * e


def _final_call(h1, y, wts, p, wpp, pg, g3, wpg):
    T, D = h1.shape
    tm = TM_FINAL
    row = lambda w: pl.BlockSpec((tm, w), lambda i: (i, 0))
    first = pl.BlockSpec((tm * PACK_SUBLANES, LANES), lambda i: (i, 0))
    second = pl.BlockSpec((tm * PACK_SUBLANES, LANES), lambda i: (i + T // tm, 0))
    return pl.pallas_call(
        _final_kernel,
        grid=(T // tm,),
        in_specs=[row(D), first, second, row(LANES), row(PLE_DIM), _const_spec(wpp.shape),
                  _const_spec(pg.shape), _const_spec(g3.shape),
                  pl.BlockSpec(memory_space=pl.ANY)],
        out_specs=row(D),
        out_shape=jax.ShapeDtypeStruct((T, D), F32),
        scratch_shapes=_bf16_weight_scratch(wpg.shape),
        compiler_params=pltpu.CompilerParams(
            dimension_semantics=("arbitrary",), vmem_limit_bytes=VMEM_LIMIT),
        name="final",
    )(h1, y, y, wts, p, wpp, pg, g3, wpg)


def _rope_lanes(a):
    z = jnp.zeros(a.shape[:-1] + (ROPE_HALF,), a.dtype)
    return jnp.concatenate([a[..., :ROPE_HALF], z, a[..., ROPE_HALF:], z], axis=-1)


def _head_lanes(a):
    return jnp.concatenate([a[..., :QK_NOPE], _rope_lanes(a[..., QK_NOPE:])], axis=-1)


def _score_bound(q_gain, k_gain):
    return Q_PRESCALE * QK_DIM * jnp.max(jnp.abs(q_gain)) * jnp.max(jnp.abs(k_gain))


def kernel(x, p, positions, norm1_gain, w_in, q_a_gain, w_q_b, kv_a_gain, w_kv_b, q_norm_gain, k_norm_gain, w_pool, pool_scale, w_out, norm2_gain, w_router_group, b_router_group, w_router_expert, b_router_expert, w_exp_gate, w_exp_up, w_exp_down, norm3_gain, w_ple_gate, w_ple_proj, ple_norm_gain):
    B, S, D = x.shape
    T = B * S
    assert x.shape[2] == D_MODEL and S % TQ == 0 and S % TM_PRE == 0
    assert T % TM_POST == 0 and T % TM_FINAL == 0
    layer = 0
    row = lambda a: a[layer].reshape(1, -1)

    win = _win_call(w_in[layer].T)
    wq = _head_lanes(w_q_b[layer].reshape(Q_LORA, N_HEADS, QK_DIM)
                     ).reshape(Q_LORA, N_HEADS * HEAD_W).astype(BF16)
    wkv3 = w_kv_b[layer].reshape(KV_LORA, N_HEADS, QK_NOPE + V_DIM)
    wkv = jnp.concatenate([wkv3[..., :QK_NOPE].reshape(KV_LORA, -1),
                           wkv3[..., QK_NOPE:].reshape(KV_LORA, -1)], axis=1).astype(BF16)
    qng = _head_lanes(q_norm_gain[layer]).reshape(1, HEAD_W)
    kng = _head_lanes(k_norm_gain[layer]).reshape(1, HEAD_W)
    inv_freq = np.float32(ROPE_THETA) ** (-np.arange(ROPE_HALF, dtype=np.float32) / ROPE_HALF)
    zeros_half = np.zeros((ROPE_HALF,), np.float32)
    ones_half = np.ones((ROPE_HALF,), np.float32)
    invf = np.concatenate([inv_freq, zeros_half, inv_freq, zeros_half]).reshape(1, LANES)
    sgn = np.concatenate([-ones_half, zeros_half, ones_half, zeros_half]).reshape(1, LANES)
    spare = np.arange(LANES) == SPARE_ROPE_LANE
    kpad = spare.astype(np.float32).reshape(1, LANES)
    pad_lanes = LANES - N_GROUPS - N_EXPERTS
    wr = jnp.concatenate([w_router_group[layer], w_router_expert[layer],
                          jnp.zeros((D, pad_lanes), F32)], axis=1)
    wr_hi = wr.astype(BF16)
    wr = jnp.concatenate([wr_hi, (wr - wr_hi.astype(F32)).astype(BF16)], axis=1)
    br = jnp.concatenate([b_router_group[layer], b_router_expert[layer],
                          jnp.zeros((pad_lanes,), F32)]).reshape(1, LANES)

    bound = _score_bound(q_norm_gain[layer], k_norm_gain[layer])
    bounded = bound <= SCORE_BOUND_LIMIT
    qpad = jnp.where(bounded, -bound, 0.0) * kpad
    ypool, q, k, vt = _pre_call(
        x, positions.reshape(B, S, 1), row(norm1_gain), win, row(q_a_gain), wq,
        row(kv_a_gain), wkv, qng, kng, w_pool[layer].astype(BF16), row(pool_scale), invf, sgn,
        qpad, kpad)
    yattn = _attn_call(bounded.astype(I32).reshape(1), q, k, vt)
    h1, xn2, logits = _post_call(
        x.reshape(T, D), ypool.reshape(T, D_POOL), yattn.reshape(T, D_ATTN),
        w_out[layer], row(norm2_gain), wr, br)
    dest, wts, counts = _route_call(logits)

    n_assign = T * TOP_K
    n_blocks = (n_assign + N_EXPERTS * (MOE_BLK - 1)) // MOE_BLK
    slots, blk_e, blk_n, blk_w, run_e = _plan_call(
        dest[:TOP_K].reshape(n_assign), counts.reshape(LANES), n_blocks + MOE_DRAIN_STEPS)
    y = _expert_call(blk_e, blk_n, blk_w, run_e, slots, xn2,
                     w_exp_gate[layer], w_exp_up[layer], w_exp_down[layer])

    out = _final_call(h1, y, wts, p[layer].reshape(T, PLE_DIM),
                      w_ple_proj[layer].astype(BF16), row(ple_norm_gain), row(norm3_gain),
                      w_ple_gate[layer])
    return out.reshape(B, S, D)
```
